```python
import jax, jax.numpy as jnp
from jax import lax
import numpy as np

D_MODEL = 1024
BATCH = 4
SEQ = 4096
DEPTH = 1

CHUNK = 64
POOL_WIDTH = 512
POOL_WINDOWS = (2, 4, 8, 16)
POOL_GROUPS = len(POOL_WINDOWS)
POOL_GROUP_DIM = POOL_WIDTH // POOL_GROUPS
MLSTM_WIDTH = 1024
MLSTM_HEADS = 4
MLSTM_HEAD_DIM = MLSTM_WIDTH // MLSTM_HEADS
CONV_WIDTH = 4
N_BRANCHES = 2
SPLITS = (POOL_WIDTH,
          POOL_WIDTH + MLSTM_WIDTH,
          POOL_WIDTH + 2 * MLSTM_WIDTH,
          POOL_WIDTH + 3 * MLSTM_WIDTH,
          POOL_WIDTH + 3 * MLSTM_WIDTH + 2 * MLSTM_HEADS)
IN_COLS = SPLITS[-1] + N_BRANCHES * D_MODEL
N_EXPERTS = 32
TOP_K = 4
D_FF = D_MODEL
SWIGLU_ALPHA = 1.702
SWIGLU_LIMIT = 7.0
EXPERT_BLOCK = 128
LN_EPS = 1e-5
DEEPNORM_ALPHA = (2.0 * DEPTH) ** 0.25
DEEPNORM_BETA = (8.0 * DEPTH) ** -0.25

kernel_name = 'hybrid_pool_mlstm_moe_encoder'


def layer_norm(x, g, b):
    xf = x.astype(jnp.float32)
    mu = xf.mean(-1, keepdims=True)
    var = jnp.square(xf - mu).mean(-1, keepdims=True)
    return ((xf - mu) * lax.rsqrt(var + LN_EPS) * g + b).astype(x.dtype)


def pool_mixer(p, w_pool, b_pool, ls_pool):
    B, S, _ = p.shape
    pf = p.astype(jnp.float32)
    cs = jnp.pad(jnp.cumsum(pf, axis=1), ((0, 0), (1, 0), (0, 0)))
    t = jnp.arange(S)
    outs = []
    for gi, w in enumerate(POOL_WINDOWS):
        sl = slice(gi * POOL_GROUP_DIM, (gi + 1) * POOL_GROUP_DIM)
        csg = cs[:, :, sl]
        start = jnp.maximum(t + 1 - w, 0)
        win_sum = csg[:, 1:] - csg[:, start]
        cnt = jnp.minimum(t + 1, w).astype(jnp.float32)[None, :, None]
        outs.append(win_sum / cnt - pf[:, :, sl])
    pooled = jnp.stack(outs, axis=2).astype(p.dtype)
    mixed = jnp.einsum('bsgc,gcd->bsgd', pooled, w_pool).reshape(B, S, POOL_WIDTH) + b_pool
    return mixed * ls_pool


def causal_depthwise_conv(u, w, b):
    C = u.shape[-1]
    y = lax.conv_general_dilated(u, w[:, None, :].astype(u.dtype), window_strides=(1,),
                                 padding=[(CONV_WIDTH - 1, 0)],
                                 dimension_numbers=('NWC', 'WIO', 'NWC'),
                                 feature_group_count=C)
    return y + b


def mlstm_chunkwise(q, k, v, i_pre, f_pre):
    B, H, S, Dh = q.shape
    nc = S // CHUNK

    def to_chunks(a):
        return jnp.moveaxis(a.reshape(B, H, nc, CHUNK, *a.shape[3:]), 2, 0)

    log_f = jax.nn.log_sigmoid(f_pre)
    xs = (to_chunks(q), to_chunks(k), to_chunks(v), to_chunks(i_pre), to_chunks(log_f))
    causal = jnp.tril(jnp.ones((CHUNK, CHUNK), dtype=bool))

    def step(carry, inp):
        C, n, m = carry
        qc, kc, vc, ic, lfc = inp
        b = jnp.cumsum(lfc, axis=-1)
        d_log = jnp.where(causal, b[..., :, None] - b[..., None, :] + ic[..., None, :], -jnp.inf)
        m_inter = b + m[..., None]
        m_t = jnp.maximum(m_inter, d_log.max(-1))
        w_intra = jnp.exp(d_log - m_t[..., None])
        s = jnp.einsum('bhtd,bhsd->bhts', qc, kc) * w_intra
        w_inter = jnp.exp(m_inter - m_t)
        num = (jnp.einsum('bhts,bhse->bhte', s, vc)
               + w_inter[..., None] * jnp.einsum('bhed,bhtd->bhte', C, qc))
        den = s.sum(-1) + w_inter * jnp.einsum('bhd,bhtd->bht', n, qc)
        h = num / jnp.maximum(jnp.abs(den), jnp.exp(-m_t))[..., None]
        g = b[..., -1]
        w_log = g[..., None] - b + ic
        m_new = jnp.maximum(g + m, w_log.max(-1))
        decay = jnp.exp(g + m - m_new)
        w_state = jnp.exp(w_log - m_new[..., None])
        C_new = decay[..., None, None] * C + jnp.einsum('bhse,bhsd->bhed', vc * w_state[..., None], kc)
        n_new = decay[..., None] * n + jnp.einsum('bhs,bhsd->bhd', w_state, kc)
        return (C_new, n_new, m_new), h

    init = (jnp.zeros((B, H, Dh, Dh), jnp.float32),
            jnp.zeros((B, H, Dh), jnp.float32),
            jnp.zeros((B, H), jnp.float32))
    _, h = lax.scan(step, init, xs)
    return jnp.moveaxis(h, 0, 2).reshape(B, H, S, Dh)


def hybrid_mixer(xn, w_in, conv_w, conv_b, w_q, w_k, b_if, mh_gain, w_pool, b_pool, ls_pool,
                 w_branch_pool, w_branch_mlstm, w_out):
    B, S, _ = xn.shape
    H, Dh = MLSTM_HEADS, MLSTM_HEAD_DIM
    proj = xn @ w_in
    p, u, v, o_pre, if_pre, gates = jnp.split(proj, list(SPLITS), axis=-1)
    y_pool = pool_mixer(p, w_pool, b_pool, ls_pool) @ w_branch_pool
    uc = jax.nn.silu(causal_depthwise_conv(u, conv_w, conv_b)).reshape(B, S, H, Dh)
    q = jnp.einsum('bshd,hde->bhse', uc, w_q) * (Dh ** -0.5)
    k = jnp.einsum('bshd,hde->bhse', uc, w_k)
    vh = v.reshape(B, S, H, Dh).transpose(0, 2, 1, 3)
    gif = (if_pre + b_if).astype(jnp.float32)
    i_pre = gif[..., :H].transpose(0, 2, 1)
    f_pre = gif[..., H:].transpose(0, 2, 1)
    h = mlstm_chunkwise(q.astype(jnp.float32), k.astype(jnp.float32), vh.astype(jnp.float32), i_pre, f_pre)
    h = h.transpose(0, 2, 1, 3)
    mu = h.mean(-1, keepdims=True)
    var = jnp.square(h - mu).mean(-1, keepdims=True)
    hn = ((h - mu) * lax.rsqrt(var + LN_EPS)).reshape(B, S, MLSTM_WIDTH) * mh_gain
    h_out = (jax.nn.sigmoid(o_pre.astype(jnp.float32)) * hn).astype(xn.dtype)
    y_mlstm = h_out @ w_branch_mlstm
    g = jax.nn.sigmoid(gates.astype(jnp.float32)).reshape(B, S, N_BRANCHES, D_MODEL).astype(xn.dtype)
    merged = g[:, :, 0] * y_pool + g[:, :, 1] * y_mlstm
    return merged @ w_out


def clamped_swiglu(h):
    x_glu, x_lin = h[..., ::2], h[..., 1::2]
    x_glu = jnp.minimum(x_glu, SWIGLU_LIMIT)
    x_lin = jnp.clip(x_lin, -SWIGLU_LIMIT, SWIGLU_LIMIT)
    return x_glu * jax.nn.sigmoid(SWIGLU_ALPHA * x_glu) * (x_lin + 1.0)


def moe_ffn(x, w_router, b_router, w_e1, b_e1, w_e2, b_e2):
    B, S, D = x.shape
    T = B * S
    xt = x.reshape(T, D)
    logits = (xt @ w_router + b_router).astype(jnp.float32)
    top_val, top_idx = lax.top_k(logits, TOP_K)
    gate = jax.nn.softmax(top_val, axis=-1).astype(x.dtype)
    A = T * TOP_K
    e_flat = top_idx.reshape(A).astype(jnp.int32)
    tok_flat = jnp.arange(A, dtype=jnp.int32) // TOP_K
    g_flat = gate.reshape(A)
    order = jnp.argsort(e_flat, stable=True)
    e_sorted = e_flat[order]
    counts = jnp.zeros((N_EXPERTS,), jnp.int32).at[e_flat].add(1)
    padded = (counts + EXPERT_BLOCK - 1) // EXPERT_BLOCK * EXPERT_BLOCK
    start = jnp.cumsum(counts) - counts
    padded_end = jnp.cumsum(padded)
    padded_start = padded_end - padded
    rank = jnp.arange(A, dtype=jnp.int32) - start[e_sorted]
    dest = padded_start[e_sorted] + rank
    n_rows = -(-(A + N_EXPERTS * (EXPERT_BLOCK - 1)) // EXPERT_BLOCK) * EXPERT_BLOCK
    n_blocks = n_rows // EXPERT_BLOCK
    row_tok = jnp.zeros((n_rows,), jnp.int32).at[dest].set(tok_flat[order])
    row_gate = jnp.zeros((n_rows,), x.dtype).at[dest].set(g_flat[order])
    block_e = jnp.minimum(
        jnp.searchsorted(padded_end, jnp.arange(n_blocks, dtype=jnp.int32) * EXPERT_BLOCK, side='right'),
        N_EXPERTS - 1).astype(jnp.int32)
    x_rows = xt[row_tok].reshape(n_blocks, EXPERT_BLOCK, D)

    def expert_block(args):
        xb, e = args
        hb = xb @ w_e1[e] + b_e1[e]
        return clamped_swiglu(hb) @ w_e2[e] + b_e2[e]

    y_rows = lax.map(expert_block, (x_rows, block_e)).reshape(n_rows, D)
    y = jnp.zeros((T, D), x.dtype).at[row_tok].add(y_rows * row_gate[:, None])
    return y.reshape(B, S, D)


def setup_inputs(seed: int = 0) -> dict:
    key = jax.random.key(seed)
    ks = jax.random.split(key, 32)
    f32 = jnp.float32
    L, D, H = DEPTH, D_MODEL, MLSTM_HEADS

    def nrm(k, shape, scale):
        return jax.random.normal(k, shape, f32) * scale

    b_if = jnp.concatenate([
        nrm(ks[8], (L, H), 0.1),
        jnp.broadcast_to(jnp.linspace(3.0, 6.0, H, dtype=f32), (L, H)) + nrm(ks[9], (L, H), 0.1)], axis=-1)
    return {
        'x': nrm(ks[0], (BATCH, SEQ, D), 1.0),
        'ln0_g': 1.0 + nrm(ks[1], (D,), 0.02),
        'ln0_b': nrm(ks[2], (D,), 0.02),
        'w_in': nrm(ks[3], (L, D, IN_COLS), D ** -0.5),
        'conv_w': nrm(ks[4], (L, CONV_WIDTH, MLSTM_WIDTH), CONV_WIDTH ** -0.5),
        'conv_b': nrm(ks[5], (L, MLSTM_WIDTH), 0.02),
        'w_q': nrm(ks[6], (L, H, MLSTM_HEAD_DIM, MLSTM_HEAD_DIM), MLSTM_HEAD_DIM ** -0.5),
        'w_k': nrm(ks[7], (L, H, MLSTM_HEAD_DIM, MLSTM_HEAD_DIM), MLSTM_HEAD_DIM ** -0.5),
        'b_if': b_if,
        'mh_gain': 1.0 + nrm(ks[10], (L, MLSTM_WIDTH), 0.02),
        'w_pool': nrm(ks[11], (L, POOL_GROUPS, POOL_GROUP_DIM, POOL_GROUP_DIM), POOL_GROUP_DIM ** -0.5),
        'b_pool': nrm(ks[12], (L, POOL_WIDTH), 0.02),
        'ls_pool': 1.0 + nrm(ks[13], (L, POOL_WIDTH), 0.02),
        'w_branch_pool': nrm(ks[14], (L, POOL_WIDTH, D), POOL_WIDTH ** -0.5),
        'w_branch_mlstm': nrm(ks[15], (L, MLSTM_WIDTH, D), MLSTM_WIDTH ** -0.5),
        'w_out': nrm(ks[16], (L, D, D), D ** -0.5 * DEEPNORM_BETA),
        'ln1_g': 1.0 + nrm(ks[17], (L, D), 0.02),
        'ln1_b': nrm(ks[18], (L, D), 0.02),
        'w_router': nrm(ks[19], (L, D, N_EXPERTS), D ** -0.5),
        'b_router': nrm(ks[20], (L, N_EXPERTS), 0.01),
        'w_e1': nrm(ks[21], (L, N_EXPERTS, D, 2 * D_FF), D ** -0.5),
        'b_e1': nrm(ks[22], (L, N_EXPERTS, 2 * D_FF), 0.02),
        'w_e2': nrm(ks[23], (L, N_EXPERTS, D_FF, D), D_FF ** -0.5 * DEEPNORM_BETA),
        'b_e2': nrm(ks[24], (L, N_EXPERTS, D), 0.02),
        'ln2_g': 1.0 + nrm(ks[25], (L, D), 0.02),
        'ln2_b': nrm(ks[26], (L, D), 0.02),
    }


def reference(x, ln0_g, ln0_b, w_in, conv_w, conv_b, w_q, w_k, b_if, mh_gain, w_pool, b_pool, ls_pool,
              w_branch_pool, w_branch_mlstm, w_out, ln1_g, ln1_b, w_router, b_router, w_e1, b_e1,
              w_e2, b_e2, ln2_g, ln2_b):
    h = layer_norm(x, ln0_g, ln0_b)
    for l in range(DEPTH):
        mix = hybrid_mixer(h, w_in[l], conv_w[l], conv_b[l], w_q[l], w_k[l], b_if[l], mh_gain[l],
                           w_pool[l], b_pool[l], ls_pool[l], w_branch_pool[l], w_branch_mlstm[l], w_out[l])
        h = layer_norm(DEEPNORM_ALPHA * h + mix, ln1_g[l], ln1_b[l])
        ffn = moe_ffn(h, w_router[l], b_router[l], w_e1[l], b_e1[l], w_e2[l], b_e2[l])
        h = layer_norm(DEEPNORM_ALPHA * h + ffn, ln2_g[l], ln2_b[l])
    return h
```

```python
import functools

import jax
import jax.numpy as jnp
from jax import lax
from jax.experimental import pallas as pl
from jax.experimental.pallas import tpu as pltpu

F32 = jnp.float32
BF16 = jnp.bfloat16

D_MODEL = 1024
N_HEADS = 4
HEAD_DIM = 256
POOL_WIDTH = 512
POOL_GROUP = 128
POOL_WINDOWS = (2, 4, 8, 16)
CONV_WIDTH = 4
N_EXPERTS = 32
TOP_K = 4
D_FF = 1024
SWIGLU_ALPHA = 1.702
SWIGLU_LIMIT = 7.0
LN_EPS = 1e-5
DEEPNORM_ALPHA = 2.0 ** 0.25

SUBLANES = 8
LANES = 128
ROW_TILES = D_MODEL // LANES

COL_P = 0
COL_U = COL_P + POOL_WIDTH
COL_V = COL_U + D_MODEL
COL_O = COL_V + D_MODEL
COL_G = COL_O + D_MODEL
COL_IF = COL_G + 2 * D_MODEL
IN_COLS_PADDED = COL_IF + LANES

TOKEN_TILE = 256
POOL_HALO = 16
CONV_HALO = 8
EXPERT_ROWS = 256
COMBINE_TILE = 256
VMEM_LIMIT = 56 * 1024 * 1024


def _layer_norm(x, g, b):
    mu = jnp.mean(x, axis=-1, keepdims=True)
    xc = x - mu
    var = jnp.mean(xc * xc, axis=-1, keepdims=True)
    return xc * lax.rsqrt(var + LN_EPS) * g + b


def _sigmoid(x):
    return 1.0 / (1.0 + jnp.exp(-x))


def _log_sigmoid(x):
    return -(jnp.maximum(-x, 0.0) + jnp.log(1.0 + jnp.exp(-jnp.abs(x))))


def _split3(x):
    hi = x.astype(BF16)
    r1 = x - hi.astype(F32)
    mid = r1.astype(BF16)
    lo = (r1 - mid.astype(F32)).astype(BF16)
    return hi, mid, lo


def _dot(a, b):
    return jnp.dot(a, b, preferred_element_type=F32)


def _dot_nt(a, b):
    return lax.dot_general(a, b, (((1,), (1,)), ((), ())), preferred_element_type=F32)


def _mixer_kernel(x_ref, ln0g_ref, ln0b_ref, win_ref, bif_ref, convw_ref, convb_ref, wqk_ref,
                  gain_ref, wpool_ref, bpool_ref, lspool_ref, wbp_ref, wbm_ref, wout_ref,
                  ln1g_ref, ln1b_ref, wr_ref, br_ref,
                  h1_ref, idx_ref, rank_ref, gcol_ref, cnt_ref,
                  pext_ref, uext_ref, ct_ref, n_ref, m_ref, run_ref):
    tm = TOKEN_TILE
    b = pl.program_id(0)
    s = pl.program_id(1)

    @pl.when(s == 0)
    def _():
        pext_ref[0:POOL_HALO, :] = jnp.zeros((POOL_HALO, POOL_WIDTH), F32)
        uext_ref[0:CONV_HALO, :] = jnp.zeros((CONV_HALO, D_MODEL), F32)
        ct_ref[...] = jnp.zeros_like(ct_ref)
        n_ref[...] = jnp.zeros_like(n_ref)
        m_ref[...] = jnp.zeros_like(m_ref)

    @pl.when(jnp.logical_and(b == 0, s == 0))
    def _():
        run_ref[...] = jnp.zeros_like(run_ref)

    h0 = _layer_norm(x_ref[...], ln0g_ref[...], ln0b_ref[...])
    h0b = h0.astype(BF16)

    def proj(lo, hi):
        return _dot(h0b, win_ref[:, lo:hi])

    pext_ref[POOL_HALO:POOL_HALO + tm, :] = proj(COL_P, COL_U)
    tpos = s * tm + lax.broadcasted_iota(jnp.int32, (tm, 1), 0)
    groups = []
    for gi, w in enumerate(POOL_WINDOWS):
        c0 = gi * POOL_GROUP
        cur = pext_ref[POOL_HALO:POOL_HALO + tm, c0:c0 + POOL_GROUP]
        acc = cur
        for j in range(1, w):
            acc = acc + pext_ref[POOL_HALO - j:POOL_HALO - j + tm, c0:c0 + POOL_GROUP]
        inv_cnt = 1.0 / jnp.minimum(tpos + 1, w).astype(F32)
        groups.append(acc * inv_cnt - cur)
    pooled = jnp.concatenate(groups, axis=1)
    pext_ref[0:POOL_HALO, :] = pext_ref[tm:tm + POOL_HALO, :]
    mixed = (_dot(pooled.astype(BF16), wpool_ref[...]) + bpool_ref[...]) * lspool_ref[...]
    y_pool = _dot(mixed.astype(BF16), wbp_ref[...])

    uext_ref[CONV_HALO:CONV_HALO + tm, :] = proj(COL_U, COL_V)
    conv = convb_ref[...]
    for j in range(CONV_WIDTH):
        off = CONV_HALO - (CONV_WIDTH - 1) + j
        conv = conv + convw_ref[j:j + 1, :] * uext_ref[off:off + tm, :]
    uext_ref[0:CONV_HALO, :] = uext_ref[tm:tm + CONV_HALO, :]
    ucb = (conv * _sigmoid(conv)).astype(BF16)
    vb = proj(COL_V, COL_O).astype(BF16)

    slab = proj(COL_IF, IN_COLS_PADDED) + bif_ref[...]
    lane = lax.broadcasted_iota(jnp.int32, (tm, LANES), 1)
    is_f = jnp.logical_and(lane >= N_HEADS, lane < 2 * N_HEADS)
    slab = jnp.where(is_f, _log_sigmoid(slab), slab)
    row_i = lax.broadcasted_iota(jnp.int32, (tm, tm), 0)
    col_i = lax.broadcasted_iota(jnp.int32, (tm, tm), 1)
    causal = row_i >= col_i
    tri = jnp.where(causal, 1.0, 0.0).astype(BF16)
    hi, mid, lo = _split3(slab)
    bcol = _dot(tri, hi) + _dot(tri, mid) + _dot(tri, lo)
    slab_t = slab.T
    bcol_t = bcol.T

    heads = []
    for h in range(N_HEADS):
        hs = slice(h * HEAD_DIM, (h + 1) * HEAD_DIM)
        qk = _dot(ucb[:, hs], wqk_ref[h])
        q = qk[:, :HEAD_DIM]
        k = qk[:, HEAD_DIM:]
        qb = q.astype(BF16)
        kb = k.astype(BF16)
        vh = vb[:, hs]

        i_c = slab[:, h:h + 1]
        b_c = bcol[:, N_HEADS + h:N_HEADS + h + 1]
        i_r = slab_t[h:h + 1, :]
        b_r = bcol_t[N_HEADS + h:N_HEADS + h + 1, :]
        m_prev = m_ref[:, h:h + 1]

        d_log = jnp.where(causal, b_c - (b_r - i_r), -jnp.inf)
        m_inter = b_c + m_prev
        m_t = jnp.maximum(m_inter, jnp.max(d_log, axis=1, keepdims=True))
        w_intra = jnp.exp(d_log - m_t)
        sc = _dot_nt(qb, kb) * w_intra
        w_inter = jnp.exp(m_inter - m_t)
        ctb = ct_ref[h].astype(BF16)
        num = _dot(sc.astype(BF16), vh) + w_inter * _dot(qb, ctb)
        qn = jnp.sum(q * n_ref[h], axis=1, keepdims=True)
        den = jnp.sum(sc, axis=1, keepdims=True) + w_inter * qn
        hh = num * (1.0 / jnp.maximum(jnp.abs(den), jnp.exp(-m_t)))
        mu = jnp.mean(hh, axis=1, keepdims=True)
        hc = hh - mu
        var = jnp.mean(hc * hc, axis=1, keepdims=True)
        heads.append(hc * lax.rsqrt(var + LN_EPS))

        g = b_r[:, tm - 1:tm]
        m_new = jnp.maximum(g + m_prev, jnp.max(g - b_r + i_r, axis=1, keepdims=True))
        decay = jnp.exp(g + m_prev - m_new)
        w_state = jnp.exp(g - b_c + i_c - m_new)
        kw = k * w_state
        ct_ref[h] = decay * ct_ref[h] + _dot(kw.T.astype(BF16), vh)
        n_ref[h] = decay * n_ref[h] + jnp.sum(kw, axis=0, keepdims=True)
        m_ref[:, h:h + 1] = m_new

    hn = jnp.concatenate(heads, axis=1) * gain_ref[...]
    h_out = _sigmoid(proj(COL_O, COL_G)) * hn
    y_mlstm = _dot(h_out.astype(BF16), wbm_ref[...])

    merged = (_sigmoid(proj(COL_G, COL_G + D_MODEL)) * y_pool
              + _sigmoid(proj(COL_G + D_MODEL, COL_IF)) * y_mlstm)
    mix = _dot(merged.astype(BF16), wout_ref[...])
    h1 = _layer_norm(DEEPNORM_ALPHA * h0 + mix, ln1g_ref[...], ln1b_ref[...])
    for c in range(ROW_TILES):
        h1_ref[pl.ds(c, tm, stride=ROW_TILES), :] = h1[:, c * LANES:(c + 1) * LANES]

    h1_hi = h1.astype(BF16)
    h1_lo = (h1 - h1_hi.astype(F32)).astype(BF16)
    la = _dot_nt(wr_ref[...], h1_hi)
    lb = _dot_nt(wr_ref[0:N_EXPERTS, :], h1_lo)
    logits = la[0:N_EXPERTS] + la[N_EXPERTS:2 * N_EXPERTS] + lb + br_ref[...]
    e_iota = lax.broadcasted_iota(jnp.int32, (N_EXPERTS, tm), 0)
    sels, vals, onehots = [], [], []
    lg = logits
    for _ in range(TOP_K):
        mx = jnp.max(lg, axis=0, keepdims=True)
        sel = jnp.min(jnp.where(lg == mx, e_iota, N_EXPERTS), axis=0, keepdims=True)
        oh = e_iota == sel
        lg = jnp.where(oh, -jnp.inf, lg)
        sels.append(sel)
        vals.append(mx)
        onehots.append(oh)
    exps = [jnp.exp(v - vals[0]) for v in vals]
    inv_den = 1.0 / (exps[0] + exps[1] + exps[2] + exps[3])
    gates = [e * inv_den for e in exps]

    oh_all = jnp.where(onehots[0], 1.0, 0.0)
    for oh in onehots[1:]:
        oh_all = oh_all + jnp.where(oh, 1.0, 0.0)
    strict = jnp.where(row_i < col_i, 1.0, 0.0).astype(BF16)
    before = _dot(oh_all.astype(BF16), strict) + run_ref[:, 0:1]
    ranks = [jnp.sum(jnp.where(oh, before, 0.0), axis=0, keepdims=True) for oh in onehots]
    run_ref[...] = run_ref[...] + jnp.sum(oh_all, axis=1, keepdims=True)
    cnt_ref[...] = run_ref[...]

    r8 = lax.broadcasted_iota(jnp.int32, (SUBLANES, tm), 0)
    idx_out = jnp.zeros((SUBLANES, tm), jnp.int32)
    rank_out = jnp.zeros((SUBLANES, tm), jnp.int32)
    r128 = lax.broadcasted_iota(jnp.int32, (LANES, tm), 0)
    gate_rows = jnp.zeros((LANES, tm), F32)
    for kk in range(TOP_K):
        idx_out = jnp.where(r8 == kk, sels[kk], idx_out)
        rank_out = jnp.where(r8 == kk, ranks[kk].astype(jnp.int32), rank_out)
        gate_rows = jnp.where(r128 == kk, gates[kk], gate_rows)
    idx_ref[...] = idx_out
    rank_ref[...] = rank_out
    gcol_ref[...] = gate_rows.T


def _const_spec(shape):
    zeros = (0,) * len(shape)
    return pl.BlockSpec(shape, lambda b, s: zeros, pipeline_mode=pl.Buffered(1))


def _mixer_call(x, weights):
    bsz, seq, _ = x.shape
    tm = TOKEN_TILE
    n_s = seq // tm
    t_total = bsz * seq
    tile = lambda b, s: (b * n_s + s)
    in_specs = [pl.BlockSpec((None, tm, D_MODEL), lambda b, s: (b, s, 0))]
    in_specs += [_const_spec(w.shape) for w in weights]
    out_shape = (
        jax.ShapeDtypeStruct((t_total * ROW_TILES, LANES), F32),
        jax.ShapeDtypeStruct((SUBLANES, t_total), jnp.int32),
        jax.ShapeDtypeStruct((SUBLANES, t_total), jnp.int32),
        jax.ShapeDtypeStruct((t_total, LANES), F32),
        jax.ShapeDtypeStruct((N_EXPERTS, LANES), F32),
    )
    out_specs = (
        pl.BlockSpec((tm * ROW_TILES, LANES), lambda b, s: (tile(b, s), 0)),
        pl.BlockSpec((SUBLANES, tm), lambda b, s: (0, tile(b, s))),
        pl.BlockSpec((SUBLANES, tm), lambda b, s: (0, tile(b, s))),
        pl.BlockSpec((tm, LANES), lambda b, s: (tile(b, s), 0)),
        pl.BlockSpec((N_EXPERTS, LANES), lambda b, s: (0, 0)),
    )
    scratch = [
        pltpu.VMEM((POOL_HALO + tm, POOL_WIDTH), F32),
        pltpu.VMEM((CONV_HALO + tm, D_MODEL), F32),
        pltpu.VMEM((N_HEADS, HEAD_DIM, HEAD_DIM), F32),
        pltpu.VMEM((N_HEADS, 1, HEAD_DIM), F32),
        pltpu.VMEM((1, LANES), F32),
        pltpu.VMEM((N_EXPERTS, LANES), F32),
    ]
    return pl.pallas_call(
        _mixer_kernel,
        grid=(bsz, n_s),
        in_specs=in_specs,
        out_specs=out_specs,
        out_shape=out_shape,
        scratch_shapes=scratch,
        compiler_params=pltpu.CompilerParams(
            dimension_semantics=("arbitrary", "arbitrary"), vmem_limit_bytes=VMEM_LIMIT),
        name="mixer",
    )(x, *weights)


def _experts_kernel(be_ref, slotp_ref, slot_ref, slotn_ref, h1_hbm, w1_ref, b1_ref, w2_ref, b2_ref,
                    y_hbm, xg_ref, yb_ref, w1b_ref, w2z_ref, w2b_ref, gsem, ssem, *, n_tokens):
    rows = EXPERT_ROWS
    j = pl.program_id(0)
    last = pl.num_programs(0) - 1
    cur = j % 2
    nxt = 1 - cur

    def gather(slots, r, buf):
        tok = slots[0, 0, r] & (n_tokens - 1)
        return pltpu.make_async_copy(
            h1_hbm.at[tok], xg_ref.at[buf, pl.ds(r * ROW_TILES, ROW_TILES)], gsem.at[buf])

    def scatter(slots, r, buf):
        return pltpu.make_async_copy(
            yb_ref.at[buf, pl.ds(r * ROW_TILES, ROW_TILES)], y_hbm.at[slots[0, 0, r]], ssem.at[buf])

    def wait_gathers(buf):
        pltpu.make_async_copy(xg_ref.at[buf], xg_ref.at[buf], gsem.at[buf]).wait()

    def wait_scatters(buf):
        pltpu.make_async_copy(yb_ref.at[buf], yb_ref.at[buf], ssem.at[buf]).wait()

    @pl.when(j == 0)
    def _():
        yb_ref[1] = jnp.zeros((rows * ROW_TILES, LANES), F32)
        for r in range(rows):
            gather(slot_ref, r, 0).start()

    wait_gathers(cur)

    @pl.when(j >= 1)
    def _():
        wait_scatters(cur)

    e = be_ref[j]
    e_prev = be_ref[jnp.maximum(j - 1, 0)]

    @pl.when(jnp.logical_or(j == 0, e != e_prev))
    def _():
        step = 128
        for c in range(D_MODEL // step):
            w1b_ref[c * step:(c + 1) * step, :] = w1_ref[0, c * step:(c + 1) * step, :].astype(BF16)
        half = D_FF // 2
        for c in range(ROW_TILES):
            cs = slice(c * LANES, (c + 1) * LANES)
            w2z_ref[c, pl.ds(0, half, stride=2), :] = w2_ref[0, 0:half, cs]
            w2z_ref[c, pl.ds(1, half, stride=2), :] = w2_ref[0, half:D_FF, cs]
            w2b_ref[:, cs] = w2z_ref[c].astype(BF16)

    for r in range(rows):
        gather(slotn_ref, r, nxt).start()
    for r in range(rows):
        scatter(slotp_ref, r, nxt).start()

    x = jnp.concatenate(
        [xg_ref[cur, pl.ds(c, rows, stride=ROW_TILES), :] for c in range(ROW_TILES)], axis=1)
    hb = _dot(x.astype(BF16), w1b_ref[...]) + b1_ref[0]
    glu = jnp.minimum(hb, SWIGLU_LIMIT)
    sg = glu * _sigmoid(SWIGLU_ALPHA * glu)
    lin = jnp.clip(hb, -SWIGLU_LIMIT, SWIGLU_LIMIT) + 1.0
    act_even = sg[:, :D_FF] * pltpu.roll(lin[:, :D_FF], D_FF - 1, 1)
    act_odd = pltpu.roll(sg[:, D_FF:], 1, 1) * lin[:, D_FF:]
    lane = lax.broadcasted_iota(jnp.int32, (rows, D_FF), 1)
    act = jnp.where((lane & 1) == 0, act_even, act_odd)
    y = _dot(act.astype(BF16), w2b_ref[...]) + b2_ref[0]
    for c in range(ROW_TILES):
        yb_ref[cur, pl.ds(c, rows, stride=ROW_TILES), :] = y[:, c * LANES:(c + 1) * LANES]

    @pl.when(j == last)
    def _():
        wait_scatters(nxt)
        for r in range(rows):
            scatter(slot_ref, r, cur).start()
        wait_scatters(cur)
        wait_gathers(nxt)


def _experts_call(block_e, row_slot_ext, h1_tiles, w_e1, b_e1, w_e2, b_e2, n_tokens):
    rows = EXPERT_ROWS
    n_blocks = block_e.shape[0]
    n_slots = TOP_K * n_tokens + rows
    smem_block = lambda shift: pl.BlockSpec(
        (1, 1, rows), lambda j, be: (j + shift, 0, 0), memory_space=pltpu.SMEM)
    grid_spec = pltpu.PrefetchScalarGridSpec(
        num_scalar_prefetch=1,
        grid=(n_blocks,),
        in_specs=[
            smem_block(0), smem_block(1), smem_block(2),
            pl.BlockSpec(memory_space=pl.ANY),
            pl.BlockSpec((1, D_MODEL, 2 * D_FF), lambda j, be: (be[j], 0, 0)),
            pl.BlockSpec((1, 1, 2 * D_FF), lambda j, be: (be[j], 0, 0)),
            pl.BlockSpec((1, D_FF, D_MODEL), lambda j, be: (be[j], 0, 0)),
            pl.BlockSpec((1, 1, D_MODEL), lambda j, be: (be[j], 0, 0)),
        ],
        out_specs=pl.BlockSpec(memory_space=pl.ANY),
        scratch_shapes=[
            pltpu.VMEM((2, rows * ROW_TILES, LANES), F32),
            pltpu.VMEM((2, rows * ROW_TILES, LANES), F32),
            pltpu.VMEM((D_MODEL, 2 * D_FF), BF16),
            pltpu.VMEM((ROW_TILES, D_FF, LANES), F32),
            pltpu.VMEM((D_FF, D_MODEL), BF16),
            pltpu.SemaphoreType.DMA((2,)),
            pltpu.SemaphoreType.DMA((2,)),
        ],
    )
    return pl.pallas_call(
        functools.partial(_experts_kernel, n_tokens=n_tokens),
        grid_spec=grid_spec,
        out_shape=jax.ShapeDtypeStruct((n_slots, ROW_TILES, LANES), F32),
        compiler_params=pltpu.CompilerParams(
            dimension_semantics=("arbitrary",), vmem_limit_bytes=VMEM_LIMIT),
        name="experts",
    )(block_e, row_slot_ext, row_slot_ext, row_slot_ext, h1_tiles, w_e1, b_e1, w_e2, b_e2)


def _combine_kernel(h1_ref, y0_ref, y1_ref, y2_ref, y3_ref, gcol_ref, g_ref, b_ref, out_ref):
    tc = COMBINE_TILE
    gcol = gcol_ref[...]
    y_refs = (y0_ref, y1_ref, y2_ref, y3_ref)
    chunks = []
    for c in range(ROW_TILES):
        z = DEEPNORM_ALPHA * h1_ref[pl.ds(c, tc, stride=ROW_TILES), :]
        for kk in range(TOP_K):
            z = z + gcol[:, kk:kk + 1] * y_refs[kk][pl.ds(c, tc, stride=ROW_TILES), :]
        chunks.append(z)
    total = chunks[0].sum(axis=1, keepdims=True)
    for z in chunks[1:]:
        total = total + z.sum(axis=1, keepdims=True)
    mu = total * (1.0 / D_MODEL)
    sq = None
    for z in chunks:
        zc = z - mu
        part = (zc * zc).sum(axis=1, keepdims=True)
        sq = part if sq is None else sq + part
    inv = lax.rsqrt(sq * (1.0 / D_MODEL) + LN_EPS)
    for c, z in enumerate(chunks):
        cs = slice(c * LANES, (c + 1) * LANES)
        out_ref[:, cs] = (z - mu) * inv * g_ref[:, cs] + b_ref[:, cs]


def _combine_call(h1_2d, y_2d, gcol, ln_g, ln_b, n_tokens):
    tc = COMBINE_TILE
    n_t = n_tokens // tc
    blk = tc * ROW_TILES
    y_spec = lambda kk: pl.BlockSpec((blk, LANES), lambda i: (kk * n_t + i, 0))
    return pl.pallas_call(
        _combine_kernel,
        grid=(n_t,),
        in_specs=[
            pl.BlockSpec((blk, LANES), lambda i: (i, 0)),
            y_spec(0), y_spec(1), y_spec(2), y_spec(3),
            pl.BlockSpec((tc, LANES), lambda i: (i, 0)),
            pl.BlockSpec((1, D_MODEL), lambda i: (0, 0)),
            pl.BlockSpec((1, D_MODEL), lambda i: (0, 0)),
        ],
        out_specs=pl.BlockSpec((tc, D_MODEL), lambda i: (i, 0)),
        out_shape=jax.ShapeDtypeStruct((n_tokens, D_MODEL), F32),
        compiler_params=pltpu.CompilerParams(
            dimension_semantics=("arbitrary",), vmem_limit_bytes=VMEM_LIMIT),
        name="combine",
    )(h1_2d, y_2d, y_2d, y_2d, y_2d, gcol, ln_g, ln_b)


def _prepare_mixer_weights(ln0_g, ln0_b, w_in, conv_w, conv_b, w_q, w_k, b_if, mh_gain, w_pool,
                           b_pool, ls_pool, w_branch_pool, w_branch_mlstm, w_out, ln1_g, ln1_b,
                           w_router, b_router):
    s_p = POOL_WIDTH
    s_u, s_v, s_o = s_p + D_MODEL, s_p + 2 * D_MODEL, s_p + 3 * D_MODEL
    s_if = s_o + 2 * N_HEADS
    pad = jnp.zeros((D_MODEL, LANES - 2 * N_HEADS), w_in.dtype)
    w_in_r = jnp.concatenate(
        [w_in[:, :s_o], w_in[:, s_if:], w_in[:, s_o:s_if], pad], axis=1).astype(BF16)
    bif = jnp.concatenate([b_if, jnp.zeros((LANES - 2 * N_HEADS,), F32)]).reshape(1, LANES)
    wqk = jnp.concatenate([w_q * (HEAD_DIM ** -0.5), w_k], axis=-1).astype(BF16)
    wpool_bd = jax.scipy.linalg.block_diag(*[w_pool[g] for g in range(len(POOL_WINDOWS))]).astype(BF16)
    wr_t = w_router.T
    wr_hi = wr_t.astype(BF16)
    wr_lo = (wr_t - wr_hi.astype(F32)).astype(BF16)
    row = lambda v: v.reshape(1, -1)
    return (row(ln0_g), row(ln0_b), w_in_r, bif, conv_w, row(conv_b), wqk, row(mh_gain), wpool_bd,
            row(b_pool), row(ls_pool), w_branch_pool.astype(BF16), w_branch_mlstm.astype(BF16),
            w_out.astype(BF16), row(ln1_g), row(ln1_b), jnp.concatenate([wr_hi, wr_lo], axis=0),
            b_router.reshape(N_EXPERTS, 1))


def _route(idx, rank, counts, n_tokens):
    rows = EXPERT_ROWS
    n_assign = TOP_K * n_tokens
    n_blocks = (n_assign + N_EXPERTS * (rows - 1) + rows - 1) // rows
    padded = (counts + rows - 1) // rows * rows
    pend = jnp.cumsum(padded)
    pstart = pend - padded
    onehot = idx[..., None] == jnp.arange(N_EXPERTS, dtype=jnp.int32)
    pos = jnp.sum(jnp.where(onehot, pstart, 0), axis=-1) + rank
    slots = jnp.arange(n_assign, dtype=jnp.int32)
    pad_slot = n_assign + jnp.arange(n_blocks * rows, dtype=jnp.int32) % rows
    row_slot = pad_slot.at[pos.reshape(-1)].set(slots, unique_indices=True)
    block_e = jnp.minimum(
        jnp.searchsorted(pend, jnp.arange(n_blocks, dtype=jnp.int32) * rows, side="right"),
        N_EXPERTS - 1).astype(jnp.int32)
    filler = (n_assign + jnp.arange(rows, dtype=jnp.int32)).reshape(1, rows)
    row_slot_ext = jnp.concatenate([filler, row_slot.reshape(n_blocks, rows), filler], axis=0)
    return block_e, row_slot_ext.reshape(n_blocks + 2, 1, rows)


def kernel(x, ln0_g, ln0_b, w_in, conv_w, conv_b, w_q, w_k, b_if, mh_gain, w_pool, b_pool, ls_pool,
           w_branch_pool, w_branch_mlstm, w_out, ln1_g, ln1_b, w_router, b_router, w_e1, b_e1,
           w_e2, b_e2, ln2_g, ln2_b):
    bsz, seq, _ = x.shape
    n_tokens = bsz * seq
    assert w_in.shape[0] == 1, "single-layer trunk"
    assert seq % TOKEN_TILE == 0 and n_tokens % COMBINE_TILE == 0
    assert n_tokens & (n_tokens - 1) == 0, "slot -> token uses a power-of-two mask"
    weights = _prepare_mixer_weights(
        ln0_g, ln0_b, w_in[0], conv_w[0], conv_b[0], w_q[0], w_k[0], b_if[0], mh_gain[0], w_pool[0],
        b_pool[0], ls_pool[0], w_branch_pool[0], w_branch_mlstm[0], w_out[0], ln1_g[0], ln1_b[0],
        w_router[0], b_router[0])
    h1_2d, idx8, rank8, gcol, cnt = _mixer_call(x, weights)
    counts = cnt[:, 0].astype(jnp.int32)
    block_e, row_slot_ext = _route(idx8[:TOP_K], rank8[:TOP_K], counts, n_tokens)
    y_slots = _experts_call(
        block_e, row_slot_ext, h1_2d.reshape(n_tokens, ROW_TILES, LANES), w_e1[0],
        b_e1[0].reshape(N_EXPERTS, 1, 2 * D_FF), w_e2[0], b_e2[0].reshape(N_EXPERTS, 1, D_MODEL),
        n_tokens)
    out = _combine_call(h1_2d, y_slots.reshape(-1, LANES), gcol, ln2_g[0].reshape(1, D_MODEL),
                        ln2_b[0].reshape(1, D_MODEL), n_tokens)
    return out.reshape(bsz, seq, D_MODEL)
```

```python
import functools

import jax
import jax.numpy as jnp
from jax import lax
from jax.experimental import pallas as pl
from jax.experimental.pallas import tpu as pltpu

F32 = jnp.float32
BF16 = jnp.bfloat16

D_MODEL = 1024
N_HEADS = 4
HEAD_DIM = 256
POOL_WIDTH = 512
POOL_GROUP = 128
POOL_WINDOWS = (2, 4, 8, 16)
CONV_WIDTH = 4
N_EXPERTS = 32
TOP_K = 4
D_FF = 1024
SWIGLU_ALPHA = 1.702
SWIGLU_LIMIT = 7.0
LN_EPS = 1e-5
DEEPNORM_ALPHA = 2.0 ** 0.25

SUBLANES = 8
LANES = 128
ROW_TILES = D_MODEL // LANES

COL_P = 0
COL_U = COL_P + POOL_WIDTH
COL_V = COL_U + D_MODEL
COL_O = COL_V + D_MODEL
COL_G = COL_O + D_MODEL
COL_IF = COL_G + 2 * D_MODEL
IN_COLS_PADDED = COL_IF + LANES

TOKEN_TILE = 256
POOL_HALO = 16
CONV_HALO = 8
EXPERT_ROWS = 256
COMBINE_TILE = 256
VMEM_LIMIT = 56 * 1024 * 1024


def _layer_norm(x, g, b):
    mu = jnp.mean(x, axis=-1, keepdims=True)
    xc = x - mu
    var = jnp.mean(xc * xc, axis=-1, keepdims=True)
    return xc * lax.rsqrt(var + LN_EPS) * g + b


def _sigmoid(x):
    return 1.0 / (1.0 + jnp.exp(-x))


def _log_sigmoid(x):
    return -(jnp.maximum(-x, 0.0) + jnp.log(1.0 + jnp.exp(-jnp.abs(x))))


def _split3(x):
    hi = x.astype(BF16)
    r1 = x - hi.astype(F32)
    mid = r1.astype(BF16)
    lo = (r1 - mid.astype(F32)).astype(BF16)
    return hi, mid, lo


def _dot(a, b):
    return jnp.dot(a, b, preferred_element_type=F32)


def _dot_nt(a, b):
    return lax.dot_general(a, b, (((1,), (1,)), ((), ())), preferred_element_type=F32)


def _mixer_kernel(x_ref, ln0g_ref, ln0b_ref, win_ref, bif_ref, convw_ref, convb_ref, wqk_ref,
                  gain_ref, wpool_ref, bpool_ref, lspool_ref, wbp_ref, wbm_ref, wout_ref,
                  ln1g_ref, ln1b_ref, wr_ref, br_ref,
                  h1_ref, idx_ref, rank_ref, gcol_ref, cnt_ref,
                  pext_ref, uext_ref, ct_ref, n_ref, m_ref, run_ref):
    tm = TOKEN_TILE
    b = pl.program_id(0)
    s = pl.program_id(1)

    @pl.when(s == 0)
    def _():
        pext_ref[0:POOL_HALO, :] = jnp.zeros((POOL_HALO, POOL_WIDTH), F32)
        uext_ref[0:CONV_HALO, :] = jnp.zeros((CONV_HALO, D_MODEL), F32)
        ct_ref[...] = jnp.zeros_like(ct_ref)
        n_ref[...] = jnp.zeros_like(n_ref)
        m_ref[...] = jnp.zeros_like(m_ref)

    @pl.when(jnp.logical_and(b == 0, s == 0))
    def _():
        run_ref[...] = jnp.zeros_like(run_ref)

    h0 = _layer_norm(x_ref[...], ln0g_ref[...], ln0b_ref[...])
    h0b = h0.astype(BF16)

    def proj(lo, hi):
        return _dot(h0b, win_ref[:, lo:hi])

    pext_ref[POOL_HALO:POOL_HALO + tm, :] = proj(COL_P, COL_U)
    tpos = s * tm + lax.broadcasted_iota(jnp.int32, (tm, 1), 0)
    groups = []
    for gi, w in enumerate(POOL_WINDOWS):
        c0 = gi * POOL_GROUP
        cur = pext_ref[POOL_HALO:POOL_HALO + tm, c0:c0 + POOL_GROUP]
        acc = cur
        for j in range(1, w):
            acc = acc + pext_ref[POOL_HALO - j:POOL_HALO - j + tm, c0:c0 + POOL_GROUP]
        inv_cnt = 1.0 / jnp.minimum(tpos + 1, w).astype(F32)
        groups.append(acc * inv_cnt - cur)
    pooled = jnp.concatenate(groups, axis=1)
    pext_ref[0:POOL_HALO, :] = pext_ref[tm:tm + POOL_HALO, :]
    mixed = (_dot(pooled.astype(BF16), wpool_ref[...]) + bpool_ref[...]) * lspool_ref[...]
    y_pool = _dot(mixed.astype(BF16), wbp_ref[...])

    uext_ref[CONV_HALO:CONV_HALO + tm, :] = proj(COL_U, COL_V)
    conv = convb_ref[...]
    for j in range(CONV_WIDTH):
        off = CONV_HALO - (CONV_WIDTH - 1) + j
        conv = conv + convw_ref[j:j + 1, :] * uext_ref[off:off + tm, :]
    uext_ref[0:CONV_HALO, :] = uext_ref[tm:tm + CONV_HALO, :]
    ucb = (conv * _sigmoid(conv)).astype(BF16)
    vb = proj(COL_V, COL_O).astype(BF16)

    slab = proj(COL_IF, IN_COLS_PADDED) + bif_ref[...]
    lane = lax.broadcasted_iota(jnp.int32, (tm, LANES), 1)
    is_f = jnp.logical_and(lane >= N_HEADS, lane < 2 * N_HEADS)
    slab = jnp.where(is_f, _log_sigmoid(slab), slab)
    row_i = lax.broadcasted_iota(jnp.int32, (tm, tm), 0)
    col_i = lax.broadcasted_iota(jnp.int32, (tm, tm), 1)
    causal = row_i >= col_i
    tri = jnp.where(causal, 1.0, 0.0).astype(BF16)
    hi, mid, lo = _split3(slab)
    bcol = _dot(tri, hi) + _dot(tri, mid) + _dot(tri, lo)
    slab_t = slab.T
    bcol_t = bcol.T

    heads = []
    for h in range(N_HEADS):
        hs = slice(h * HEAD_DIM, (h + 1) * HEAD_DIM)
        qk = _dot(ucb[:, hs], wqk_ref[h])
        q = qk[:, :HEAD_DIM]
        k = qk[:, HEAD_DIM:]
        qb = q.astype(BF16)
        kb = k.astype(BF16)
        vh = vb[:, hs]

        i_c = slab[:, h:h + 1]
        b_c = bcol[:, N_HEADS + h:N_HEADS + h + 1]
        i_r = slab_t[h:h + 1, :]
        b_r = bcol_t[N_HEADS + h:N_HEADS + h + 1, :]
        m_prev = m_ref[:, h:h + 1]

        d_log = jnp.where(causal, b_c - (b_r - i_r), -jnp.inf)
        m_inter = b_c + m_prev
        m_t = jnp.maximum(m_inter, jnp.max(d_log, axis=1, keepdims=True))
        w_intra = jnp.exp(d_log - m_t)
        sc = _dot_nt(qb, kb) * w_intra
        w_inter = jnp.exp(m_inter - m_t)
        ctb = ct_ref[h].astype(BF16)
        num = _dot(sc.astype(BF16), vh) + w_inter * _dot(qb, ctb)
        qn = jnp.sum(q * n_ref[h], axis=1, keepdims=True)
        den = jnp.sum(sc, axis=1, keepdims=True) + w_inter * qn
        hh = num * (1.0 / jnp.maximum(jnp.abs(den), jnp.exp(-m_t)))
        mu = jnp.mean(hh, axis=1, keepdims=True)
        hc = hh - mu
        var = jnp.mean(hc * hc, axis=1, keepdims=True)
        heads.append(hc * lax.rsqrt(var + LN_EPS))

        g = b_r[:, tm - 1:tm]
        m_new = jnp.maximum(g + m_prev, jnp.max(g - b_r + i_r, axis=1, keepdims=True))
        decay = jnp.exp(g + m_prev - m_new)
        w_state = jnp.exp(g - b_c + i_c - m_new)
        kw = k * w_state
        ct_ref[h] = decay * ct_ref[h] + _dot(kw.T.astype(BF16), vh)
        n_ref[h] = decay * n_ref[h] + jnp.sum(kw, axis=0, keepdims=True)
        m_ref[:, h:h + 1] = m_new

    hn = jnp.concatenate(heads, axis=1) * gain_ref[...]
    h_out = _sigmoid(proj(COL_O, COL_G)) * hn
    y_mlstm = _dot(h_out.astype(BF16), wbm_ref[...])

    merged = (_sigmoid(proj(COL_G, COL_G + D_MODEL)) * y_pool
              + _sigmoid(proj(COL_G + D_MODEL, COL_IF)) * y_mlstm)
    mix = _dot(merged.astype(BF16), wout_ref[...])
    h1 = _layer_norm(DEEPNORM_ALPHA * h0 + mix, ln1g_ref[...], ln1b_ref[...])
    for c in range(ROW_TILES):
        h1_ref[pl.ds(c, tm, stride=ROW_TILES), :] = h1[:, c * LANES:(c + 1) * LANES]

    h1_hi = h1.astype(BF16)
    h1_lo = (h1 - h1_hi.astype(F32)).astype(BF16)
    la = _dot_nt(wr_ref[...], h1_hi)
    lb = _dot_nt(wr_ref[0:N_EXPERTS, :], h1_lo)
    logits = la[0:N_EXPERTS] + la[N_EXPERTS:2 * N_EXPERTS] + lb + br_ref[...]
    e_iota = lax.broadcasted_iota(jnp.int32, (N_EXPERTS, tm), 0)
    sels, vals, onehots = [], [], []
    lg = logits
    for _ in range(TOP_K):
        mx = jnp.max(lg, axis=0, keepdims=True)
        sel = jnp.min(jnp.where(lg == mx, e_iota, N_EXPERTS), axis=0, keepdims=True)
        oh = e_iota == sel
        lg = jnp.where(oh, -jnp.inf, lg)
        sels.append(sel)
        vals.append(mx)
        onehots.append(oh)
    exps = [jnp.exp(v - vals[0]) for v in vals]
    inv_den = 1.0 / (exps[0] + exps[1] + exps[2] + exps[3])
    gates = [e * inv_den for e in exps]

    oh_all = jnp.where(onehots[0], 1.0, 0.0)
    for oh in onehots[1:]:
        oh_all = oh_all + jnp.where(oh, 1.0, 0.0)
    strict = jnp.where(row_i < col_i, 1.0, 0.0).astype(BF16)
    before = _dot(oh_all.astype(BF16), strict) + run_ref[:, 0:1]
    ranks = [jnp.sum(jnp.where(oh, before, 0.0), axis=0, keepdims=True) for oh in onehots]
    run_ref[...] = run_ref[...] + jnp.sum(oh_all, axis=1, keepdims=True)
    cnt_ref[...] = run_ref[...]

    r8 = lax.broadcasted_iota(jnp.int32, (SUBLANES, tm), 0)
    idx_out = jnp.zeros((SUBLANES, tm), jnp.int32)
    rank_out = jnp.zeros((SUBLANES, tm), jnp.int32)
    r128 = lax.broadcasted_iota(jnp.int32, (LANES, tm), 0)
    gate_rows = jnp.zeros((LANES, tm), F32)
    for kk in range(TOP_K):
        idx_out = jnp.where(r8 == kk, sels[kk], idx_out)
        rank_out = jnp.where(r8 == kk, ranks[kk].astype(jnp.int32), rank_out)
        gate_rows = jnp.where(r128 == kk, gates[kk], gate_rows)
    idx_ref[...] = idx_out
    rank_ref[...] = rank_out
    gcol_ref[...] = gate_rows.T


def _const_spec(shape):
    zeros = (0,) * len(shape)
    return pl.BlockSpec(shape, lambda b, s: zeros, pipeline_mode=pl.Buffered(1))


def _mixer_call(x, weights):
    bsz, seq, _ = x.shape
    tm = TOKEN_TILE
    n_s = seq // tm
    t_total = bsz * seq
    tile = lambda b, s: (b * n_s + s)
    in_specs = [pl.BlockSpec((None, tm, D_MODEL), lambda b, s: (b, s, 0))]
    in_specs += [_const_spec(w.shape) for w in weights]
    out_shape = (
        jax.ShapeDtypeStruct((t_total * ROW_TILES, LANES), F32),
        jax.ShapeDtypeStruct((SUBLANES, t_total), jnp.int32),
        jax.ShapeDtypeStruct((SUBLANES, t_total), jnp.int32),
        jax.ShapeDtypeStruct((t_total, LANES), F32),
        jax.ShapeDtypeStruct((N_EXPERTS, LANES), F32),
    )
    out_specs = (
        pl.BlockSpec((tm * ROW_TILES, LANES), lambda b, s: (tile(b, s), 0)),
        pl.BlockSpec((SUBLANES, tm), lambda b, s: (0, tile(b, s))),
        pl.BlockSpec((SUBLANES, tm), lambda b, s: (0, tile(b, s))),
        pl.BlockSpec((tm, LANES), lambda b, s: (tile(b, s), 0)),
        pl.BlockSpec((N_EXPERTS, LANES), lambda b, s: (0, 0)),
    )
    scratch = [
        pltpu.VMEM((POOL_HALO + tm, POOL_WIDTH), F32),
        pltpu.VMEM((CONV_HALO + tm, D_MODEL), F32),
        pltpu.VMEM((N_HEADS, HEAD_DIM, HEAD_DIM), F32),
        pltpu.VMEM((N_HEADS, 1, HEAD_DIM), F32),
        pltpu.VMEM((1, LANES), F32),
        pltpu.VMEM((N_EXPERTS, LANES), F32),
    ]
    return pl.pallas_call(
        _mixer_kernel,
        grid=(bsz, n_s),
        in_specs=in_specs,
        out_specs=out_specs,
        out_shape=out_shape,
        scratch_shapes=scratch,
        compiler_params=pltpu.CompilerParams(
            dimension_semantics=("arbitrary", "arbitrary"), vmem_limit_bytes=VMEM_LIMIT),
        name="mixer",
    )(x, *weights)


def _experts_kernel(be_ref, first_ref, nexte_ref, nused_ref, slot_lo_ref, slot_hi_ref,
                    h1_hbm, w1_hbm, b1_ref, w2_hbm, b2_ref, y_hbm,
                    buf0_ref, buf1_ref, w1s_ref, w2s_ref, w1b_ref, w2z_ref, w2b_ref,
                    gsem, ssem, wsem, *, n_tokens):
    rows = EXPERT_ROWS
    tile = rows * ROW_TILES
    i = pl.program_id(0)
    last = pl.num_programs(0) - 1
    n_used = nused_ref[0]
    buf = (buf0_ref, buf1_ref)
    xg = tuple(b.at[pl.ds(0, tile)] for b in buf)
    yb = tuple(b.at[pl.ds(tile, tile)] for b in buf)
    dyn_zero = lax.shift_right_arithmetic(n_used, 31)

    def gather(slot, r, p):
        tok = slot & (n_tokens - 1)
        return pltpu.make_async_copy(
            h1_hbm.at[tok], xg[p].at[pl.ds(r * ROW_TILES, ROW_TILES)], gsem.at[p])

    def scatter(slot, r, p):
        return pltpu.make_async_copy(
            yb[p].at[pl.ds(r * ROW_TILES, ROW_TILES)], y_hbm.at[slot], ssem.at[p])

    def issue_rows(r0, r1, g_slots, g_q, s_slots, s_q, q):
        for r in range(r0, r1):
            gather(g_slots[g_q, 0, r], r, q).start()
            scatter(s_slots[s_q, 0, r], r, q).start()
        spare = pl.multiple_of(2 * tile + dyn_zero * SUBLANES, SUBLANES)
        buf[q][pl.ds(spare, SUBLANES), :] = jnp.full((SUBLANES, LANES), dyn_zero.astype(F32))
        return buf[q][pl.ds(spare, SUBLANES), :][0:1, :]

    def wait_gathers(p):
        pltpu.make_async_copy(xg[p], xg[p], gsem.at[p]).wait()

    def wait_scatters(p):
        pltpu.make_async_copy(yb[p], yb[p], ssem.at[p]).wait()

    def weight_copies(e):
        return (pltpu.make_async_copy(w1_hbm.at[e], w1s_ref, wsem.at[0]),
                pltpu.make_async_copy(w2_hbm.at[e], w2s_ref, wsem.at[1]))

    def switch_weights(blk):
        @pl.when(first_ref[blk] == 1)
        def _():
            for cp in weight_copies(0):
                cp.wait()
            step = 128
            for c in range(D_MODEL // step):
                w1b_ref[c * step:(c + 1) * step, :] = w1s_ref[c * step:(c + 1) * step, :].astype(BF16)
            half = D_FF // 2
            for c in range(ROW_TILES):
                cs = slice(c * LANES, (c + 1) * LANES)
                w2z_ref[c, pl.ds(0, half, stride=2), :] = w2s_ref[0:half, cs]
                w2z_ref[c, pl.ds(1, half, stride=2), :] = w2s_ref[half:D_FF, cs]
                w2b_ref[:, cs] = w2z_ref[c].astype(BF16)
            nxt = nexte_ref[blk]

            @pl.when(nxt >= 0)
            def _():
                for cp in weight_copies(nxt):
                    cp.start()

    n_chunks = D_FF // (2 * LANES)
    n_groups = n_chunks + 1
    bounds = [rows * g // n_groups for g in range(n_groups + 1)]

    def compute(p, e, issue):
        x = jnp.concatenate(
            [xg[p][pl.ds(c, rows, stride=ROW_TILES), :] for c in range(ROW_TILES)], axis=1).astype(BF16)
        b1 = b1_ref[pl.ds(e, 1), :]
        width = 2 * LANES
        even = (lax.broadcasted_iota(jnp.int32, (rows, width), 1) & 1) == 0
        zs = []
        for c in range(n_chunks):
            lo = c * width
            hi = D_FF + lo
            zero = issue(bounds[c], bounds[c + 1])
            zero = jnp.concatenate([zero, zero], axis=1)
            ha = _dot(x, w1b_ref[:, lo:lo + width]) + (b1[:, lo:lo + width] + zero)
            hb = _dot(x, w1b_ref[:, hi:hi + width]) + (b1[:, hi:hi + width] + zero)
            glu = jnp.where(even, ha, pltpu.roll(hb, 1, 1))
            lin = jnp.where(even, pltpu.roll(ha, width - 1, 1), hb)
            glu = jnp.minimum(glu, SWIGLU_LIMIT)
            lin = jnp.clip(lin, -SWIGLU_LIMIT, SWIGLU_LIMIT) + 1.0
            zs.append((glu * _sigmoid(SWIGLU_ALPHA * glu) * lin).astype(BF16))
        zero = issue(bounds[n_chunks], bounds[n_groups])
        b2 = b2_ref[pl.ds(e, 1), :] + jnp.concatenate([zero] * ROW_TILES, axis=1)
        y = _dot(jnp.concatenate(zs, axis=1), w2b_ref[...]) + b2
        for c in range(ROW_TILES):
            yb[p][pl.ds(c, rows, stride=ROW_TILES), :] = y[:, c * LANES:(c + 1) * LANES]

    def phase(blk, p, issue):
        switch_weights(blk)

        @pl.when(blk < n_used)
        def _():
            compute(p, be_ref[blk], issue)

        @pl.when(blk >= n_used)
        def _():
            issue(0, rows)

    @pl.when(i == 0)
    def _():
        buf1_ref[...] = jnp.zeros_like(buf1_ref)
        for cp in weight_copies(be_ref[0]):
            cp.start()
        for r in range(rows):
            gather(slot_lo_ref[1, 0, r], r, 0).start()

    wait_gathers(0)

    @pl.when(i >= 1)
    def _():
        wait_scatters(0)

    phase(2 * i, 0, functools.partial(
        issue_rows, g_slots=slot_hi_ref, g_q=0, s_slots=slot_lo_ref, s_q=0, q=1))

    wait_gathers(1)
    wait_scatters(1)

    phase(2 * i + 1, 1, functools.partial(
        issue_rows, g_slots=slot_hi_ref, g_q=1, s_slots=slot_lo_ref, s_q=1, q=0))

    @pl.when(i == last)
    def _():
        wait_scatters(0)
        for r in range(rows):
            scatter(slot_hi_ref[0, 0, r], r, 1).start()
        wait_scatters(1)
        wait_gathers(0)


def _experts_call(plan, h1_tiles, w_e1, b_e1, w_e2, b_e2, n_tokens):
    block_e, first, next_e, n_used, row_slot_ext = plan
    rows = EXPERT_ROWS
    n_blocks = block_e.shape[0]
    assert n_blocks % 2 == 0
    n_slots = TOP_K * n_tokens + rows
    tile_rows = rows * ROW_TILES
    smem_pair = lambda shift: pl.BlockSpec(
        (2, 1, rows), lambda i, *_: (i + shift, 0, 0), memory_space=pltpu.SMEM)
    whole = lambda shape: pl.BlockSpec(shape, lambda i, *_: (0,) * len(shape))
    grid_spec = pltpu.PrefetchScalarGridSpec(
        num_scalar_prefetch=4,
        grid=(n_blocks // 2,),
        in_specs=[
            smem_pair(0), smem_pair(1),
            pl.BlockSpec(memory_space=pl.ANY),
            pl.BlockSpec(memory_space=pl.ANY),
            whole((N_EXPERTS, 2 * D_FF)),
            pl.BlockSpec(memory_space=pl.ANY),
            whole((N_EXPERTS, D_MODEL)),
        ],
        out_specs=pl.BlockSpec(memory_space=pl.ANY),
        scratch_shapes=[
            pltpu.VMEM((2 * tile_rows + SUBLANES, LANES), F32),
            pltpu.VMEM((2 * tile_rows + SUBLANES, LANES), F32),
            pltpu.VMEM((D_MODEL, 2 * D_FF), F32),
            pltpu.VMEM((D_FF, D_MODEL), F32),
            pltpu.VMEM((D_MODEL, 2 * D_FF), BF16),
            pltpu.VMEM((ROW_TILES, D_FF, LANES), F32),
            pltpu.VMEM((D_FF, D_MODEL), BF16),
            pltpu.SemaphoreType.DMA((2,)),
            pltpu.SemaphoreType.DMA((2,)),
            pltpu.SemaphoreType.DMA((2,)),
        ],
    )
    return pl.pallas_call(
        functools.partial(_experts_kernel, n_tokens=n_tokens),
        grid_spec=grid_spec,
        out_shape=jax.ShapeDtypeStruct((n_slots, ROW_TILES, LANES), F32),
        compiler_params=pltpu.CompilerParams(
            dimension_semantics=("arbitrary",), vmem_limit_bytes=VMEM_LIMIT),
        name="experts",
    )(block_e, first, next_e, n_used, row_slot_ext, row_slot_ext, h1_tiles, w_e1, b_e1, w_e2, b_e2)


def _combine_kernel(h1_ref, y0_ref, y1_ref, y2_ref, y3_ref, gcol_ref, g_ref, b_ref, out_ref):
    tc = COMBINE_TILE
    gcol = gcol_ref[...]
    y_refs = (y0_ref, y1_ref, y2_ref, y3_ref)
    chunks = []
    for c in range(ROW_TILES):
        z = DEEPNORM_ALPHA * h1_ref[pl.ds(c, tc, stride=ROW_TILES), :]
        for kk in range(TOP_K):
            z = z + gcol[:, kk:kk + 1] * y_refs[kk][pl.ds(c, tc, stride=ROW_TILES), :]
        chunks.append(z)
    total = chunks[0].sum(axis=1, keepdims=True)
    for z in chunks[1:]:
        total = total + z.sum(axis=1, keepdims=True)
    mu = total * (1.0 / D_MODEL)
    sq = None
    for z in chunks:
        zc = z - mu
        part = (zc * zc).sum(axis=1, keepdims=True)
        sq = part if sq is None else sq + part
    inv = lax.rsqrt(sq * (1.0 / D_MODEL) + LN_EPS)
    for c, z in enumerate(chunks):
        cs = slice(c * LANES, (c + 1) * LANES)
        out_ref[:, cs] = (z - mu) * inv * g_ref[:, cs] + b_ref[:, cs]


def _combine_call(h1_2d, y_2d, gcol, ln_g, ln_b, n_tokens):
    tc = COMBINE_TILE
    n_t = n_tokens // tc
    blk = tc * ROW_TILES
    y_spec = lambda kk: pl.BlockSpec((blk, LANES), lambda i: (kk * n_t + i, 0))
    return pl.pallas_call(
        _combine_kernel,
        grid=(n_t,),
        in_specs=[
            pl.BlockSpec((blk, LANES), lambda i: (i, 0)),
            y_spec(0), y_spec(1), y_spec(2), y_spec(3),
            pl.BlockSpec((tc, LANES), lambda i: (i, 0)),
            pl.BlockSpec((1, D_MODEL), lambda i: (0, 0)),
            pl.BlockSpec((1, D_MODEL), lambda i: (0, 0)),
        ],
        out_specs=pl.BlockSpec((tc, D_MODEL), lambda i: (i, 0)),
        out_shape=jax.ShapeDtypeStruct((n_tokens, D_MODEL), F32),
        compiler_params=pltpu.CompilerParams(
            dimension_semantics=("arbitrary",), vmem_limit_bytes=VMEM_LIMIT),
        name="combine",
    )(h1_2d, y_2d, y_2d, y_2d, y_2d, gcol, ln_g, ln_b)


def _prepare_mixer_weights(ln0_g, ln0_b, w_in, conv_w, conv_b, w_q, w_k, b_if, mh_gain, w_pool,
                           b_pool, ls_pool, w_branch_pool, w_branch_mlstm, w_out, ln1_g, ln1_b,
                           w_router, b_router):
    s_p = POOL_WIDTH
    s_u, s_v, s_o = s_p + D_MODEL, s_p + 2 * D_MODEL, s_p + 3 * D_MODEL
    s_if = s_o + 2 * N_HEADS
    pad = jnp.zeros((D_MODEL, LANES - 2 * N_HEADS), w_in.dtype)
    w_in_r = jnp.concatenate(
        [w_in[:, :s_o], w_in[:, s_if:], w_in[:, s_o:s_if], pad], axis=1).astype(BF16)
    bif = jnp.concatenate([b_if, jnp.zeros((LANES - 2 * N_HEADS,), F32)]).reshape(1, LANES)
    wqk = jnp.concatenate([w_q * (HEAD_DIM ** -0.5), w_k], axis=-1).astype(BF16)
    wpool_bd = jax.scipy.linalg.block_diag(*[w_pool[g] for g in range(len(POOL_WINDOWS))]).astype(BF16)
    wr_t = w_router.T
    wr_hi = wr_t.astype(BF16)
    wr_lo = (wr_t - wr_hi.astype(F32)).astype(BF16)
    row = lambda v: v.reshape(1, -1)
    return (row(ln0_g), row(ln0_b), w_in_r, bif, conv_w, row(conv_b), wqk, row(mh_gain), wpool_bd,
            row(b_pool), row(ls_pool), w_branch_pool.astype(BF16), w_branch_mlstm.astype(BF16),
            w_out.astype(BF16), row(ln1_g), row(ln1_b), jnp.concatenate([wr_hi, wr_lo], axis=0),
            b_router.reshape(N_EXPERTS, 1))


def _route(idx, rank, counts, n_tokens):
    rows = EXPERT_ROWS
    n_assign = TOP_K * n_tokens
    n_blocks = (n_assign + N_EXPERTS * (rows - 1) + rows - 1) // rows
    padded = (counts + rows - 1) // rows * rows
    pend = jnp.cumsum(padded)
    pstart = pend - padded
    onehot = idx[..., None] == jnp.arange(N_EXPERTS, dtype=jnp.int32)
    pos = jnp.sum(jnp.where(onehot, pstart, 0), axis=-1) + rank
    slots = jnp.arange(n_assign, dtype=jnp.int32)
    hit = jnp.zeros((n_blocks * rows,), jnp.int32).at[pos.reshape(-1)].add(slots + 1)
    pad_slot = n_assign + jnp.arange(n_blocks * rows, dtype=jnp.int32) % rows
    row_slot = jnp.where(hit == 0, pad_slot, hit - 1)
    filler = (n_assign + jnp.arange(rows, dtype=jnp.int32)).reshape(1, rows)
    row_slot_ext = jnp.concatenate([filler, row_slot.reshape(n_blocks, rows), filler], axis=0)

    blk = jnp.arange(n_blocks, dtype=jnp.int32)
    n_used = pend[-1] // rows
    valid = blk < n_used
    block_e = jnp.sum((pend[None, :] <= (blk * rows)[:, None]).astype(jnp.int32), axis=1)
    block_e = jnp.minimum(block_e, N_EXPERTS - 1)
    prev_e = jnp.concatenate([jnp.full((1,), -1, jnp.int32), block_e[:-1]])
    first = jnp.logical_and(valid, block_e != prev_e).astype(jnp.int32)
    e_iota = jnp.arange(N_EXPERTS, dtype=jnp.int32)
    seg_end_blk = jnp.sum(jnp.where(block_e[:, None] == e_iota[None, :], pend[None, :], 0), axis=1) // rows
    e_at_end = jnp.sum(jnp.where(blk[None, :] == seg_end_blk[:, None], block_e[None, :], 0), axis=1)
    next_e = jnp.where(seg_end_blk < n_used, e_at_end, -1).astype(jnp.int32)
    return (block_e, first, next_e, n_used.reshape(1).astype(jnp.int32),
            row_slot_ext.reshape(n_blocks + 2, 1, rows))


def kernel(x, ln0_g, ln0_b, w_in, conv_w, conv_b, w_q, w_k, b_if, mh_gain, w_pool, b_pool, ls_pool,
           w_branch_pool, w_branch_mlstm, w_out, ln1_g, ln1_b, w_router, b_router, w_e1, b_e1,
           w_e2, b_e2, ln2_g, ln2_b):
    bsz, seq, _ = x.shape
    n_tokens = bsz * seq
    assert w_in.shape[0] == 1, "single-layer trunk"
    assert seq % TOKEN_TILE == 0 and n_tokens % COMBINE_TILE == 0
    assert n_tokens & (n_tokens - 1) == 0, "slot -> token uses a power-of-two mask"
    weights = _prepare_mixer_weights(
        ln0_g, ln0_b, w_in[0], conv_w[0], conv_b[0], w_q[0], w_k[0], b_if[0], mh_gain[0], w_pool[0],
        b_pool[0], ls_pool[0], w_branch_pool[0], w_branch_mlstm[0], w_out[0], ln1_g[0], ln1_b[0],
        w_router[0], b_router[0])
    h1_2d, idx8, rank8, gcol, cnt = _mixer_call(x, weights)
    counts = cnt[:, 0].astype(jnp.int32)
    plan = _route(idx8[:TOP_K], rank8[:TOP_K], counts, n_tokens)
    y_slots = _experts_call(
        plan, h1_2d.reshape(n_tokens, ROW_TILES, LANES), w_e1[0], b_e1[0], w_e2[0], b_e2[0], n_tokens)
    out = _combine_call(h1_2d, y_slots.reshape(-1, LANES), gcol, ln2_g[0].reshape(1, D_MODEL),
                        ln2_b[0].reshape(1, D_MODEL), n_tokens)
    return out.reshape(bsz, seq, D_MODEL)
```

```python
import functools

import jax
import jax.numpy as jnp
from jax import lax
from jax.experimental import pallas as pl
from jax.experimental.pallas import tpu as pltpu

F32 = jnp.float32
BF16 = jnp.bfloat16

D_MODEL = 1024
N_HEADS = 4
HEAD_DIM = 256
POOL_WIDTH = 512
POOL_GROUP = 128
POOL_WINDOWS = (2, 4, 8, 16)
CONV_WIDTH = 4
N_EXPERTS = 32
TOP_K = 4
D_FF = 1024
SWIGLU_ALPHA = 1.702
SWIGLU_LIMIT = 7.0
LN_EPS = 1e-5
DEEPNORM_ALPHA = 2.0 ** 0.25

SUBLANES = 8
LANES = 128
ROW_TILES = D_MODEL // LANES

COL_P = 0
COL_U = COL_P + POOL_WIDTH
COL_V = COL_U + D_MODEL
COL_O = COL_V + D_MODEL
COL_G = COL_O + D_MODEL
COL_IF = COL_G + 2 * D_MODEL
IN_COLS_PADDED = COL_IF + LANES

TOKEN_TILE = 256
POOL_HALO = 16
CONV_HALO = 8
EXPERT_ROWS = 256
GATHER_STAGES = 2
SCATTER_STAGES = 3
COMBINE_TILE = 256
VMEM_LIMIT = 56 * 1024 * 1024


def _layer_norm(x, g, b):
    mu = jnp.mean(x, axis=-1, keepdims=True)
    xc = x - mu
    var = jnp.mean(xc * xc, axis=-1, keepdims=True)
    return xc * lax.rsqrt(var + LN_EPS) * g + b


def _sigmoid(x):
    return 1.0 / (1.0 + jnp.exp(-x))


def _log_sigmoid(x):
    return -(jnp.maximum(-x, 0.0) + jnp.log(1.0 + jnp.exp(-jnp.abs(x))))


def _split3(x):
    hi = x.astype(BF16)
    r1 = x - hi.astype(F32)
    mid = r1.astype(BF16)
    lo = (r1 - mid.astype(F32)).astype(BF16)
    return hi, mid, lo


def _dot(a, b):
    return jnp.dot(a, b, preferred_element_type=F32)


def _dot_nt(a, b):
    return lax.dot_general(a, b, (((1,), (1,)), ((), ())), preferred_element_type=F32)


def _mixer_kernel(x_ref, ln0g_ref, ln0b_ref, win_ref, bif_ref, convw_ref, convb_ref, wqk_ref,
                  gain_ref, wpool_ref, bpool_ref, lspool_ref, wbp_ref, wbm_ref, wout_ref,
                  ln1g_ref, ln1b_ref, wr_ref, br_ref,
                  h1_ref, idx_ref, rank_ref, gcol_ref, cnt_ref,
                  pext_ref, uext_ref, ct_ref, n_ref, m_ref, run_ref):
    tm = TOKEN_TILE
    b = pl.program_id(0)
    s = pl.program_id(1)

    @pl.when(s == 0)
    def _():
        pext_ref[0:POOL_HALO, :] = jnp.zeros((POOL_HALO, POOL_WIDTH), F32)
        uext_ref[0:CONV_HALO, :] = jnp.zeros((CONV_HALO, D_MODEL), F32)
        ct_ref[...] = jnp.zeros_like(ct_ref)
        n_ref[...] = jnp.zeros_like(n_ref)
        m_ref[...] = jnp.zeros_like(m_ref)

    @pl.when(jnp.logical_and(b == 0, s == 0))
    def _():
        run_ref[...] = jnp.zeros_like(run_ref)

    h0 = _layer_norm(x_ref[...], ln0g_ref[...], ln0b_ref[...])
    h0b = h0.astype(BF16)

    def proj(lo, hi):
        return _dot(h0b, win_ref[:, lo:hi])

    pext_ref[POOL_HALO:POOL_HALO + tm, :] = proj(COL_P, COL_U)
    tpos = s * tm + lax.broadcasted_iota(jnp.int32, (tm, 1), 0)
    groups = []
    for gi, w in enumerate(POOL_WINDOWS):
        c0 = gi * POOL_GROUP
        cur = pext_ref[POOL_HALO:POOL_HALO + tm, c0:c0 + POOL_GROUP]
        acc = cur
        for j in range(1, w):
            acc = acc + pext_ref[POOL_HALO - j:POOL_HALO - j + tm, c0:c0 + POOL_GROUP]
        inv_cnt = 1.0 / jnp.minimum(tpos + 1, w).astype(F32)
        groups.append(acc * inv_cnt - cur)
    pooled = jnp.concatenate(groups, axis=1)
    pext_ref[0:POOL_HALO, :] = pext_ref[tm:tm + POOL_HALO, :]
    mixed = (_dot(pooled.astype(BF16), wpool_ref[...]) + bpool_ref[...]) * lspool_ref[...]
    y_pool = _dot(mixed.astype(BF16), wbp_ref[...])

    uext_ref[CONV_HALO:CONV_HALO + tm, :] = proj(COL_U, COL_V)
    conv = convb_ref[...]
    for j in range(CONV_WIDTH):
        off = CONV_HALO - (CONV_WIDTH - 1) + j
        conv = conv + convw_ref[j:j + 1, :] * uext_ref[off:off + tm, :]
    uext_ref[0:CONV_HALO, :] = uext_ref[tm:tm + CONV_HALO, :]
    ucb = (conv * _sigmoid(conv)).astype(BF16)
    vb = proj(COL_V, COL_O).astype(BF16)

    slab = proj(COL_IF, IN_COLS_PADDED) + bif_ref[...]
    lane = lax.broadcasted_iota(jnp.int32, (tm, LANES), 1)
    is_f = jnp.logical_and(lane >= N_HEADS, lane < 2 * N_HEADS)
    slab = jnp.where(is_f, _log_sigmoid(slab), slab)
    row_i = lax.broadcasted_iota(jnp.int32, (tm, tm), 0)
    col_i = lax.broadcasted_iota(jnp.int32, (tm, tm), 1)
    causal = row_i >= col_i
    tri = jnp.where(causal, 1.0, 0.0).astype(BF16)
    hi, mid, lo = _split3(slab)
    bcol = _dot(tri, hi) + _dot(tri, mid) + _dot(tri, lo)
    slab_t = slab.T
    bcol_t = bcol.T

    heads = []
    for h in range(N_HEADS):
        hs = slice(h * HEAD_DIM, (h + 1) * HEAD_DIM)
        qk = _dot(ucb[:, hs], wqk_ref[h])
        q = qk[:, :HEAD_DIM]
        k = qk[:, HEAD_DIM:]
        qb = q.astype(BF16)
        kb = k.astype(BF16)
        vh = vb[:, hs]

        i_c = slab[:, h:h + 1]
        b_c = bcol[:, N_HEADS + h:N_HEADS + h + 1]
        i_r = slab_t[h:h + 1, :]
        b_r = bcol_t[N_HEADS + h:N_HEADS + h + 1, :]
        m_prev = m_ref[:, h:h + 1]

        d_log = jnp.where(causal, b_c - (b_r - i_r), -jnp.inf)
        m_inter = b_c + m_prev
        m_t = jnp.maximum(m_inter, jnp.max(d_log, axis=1, keepdims=True))
        w_intra = jnp.exp(d_log - m_t)
        sc = _dot_nt(qb, kb) * w_intra
        w_inter = jnp.exp(m_inter - m_t)
        ctb = ct_ref[h].astype(BF16)
        num = _dot(sc.astype(BF16), vh) + w_inter * _dot(qb, ctb)
        qn = jnp.sum(q * n_ref[h], axis=1, keepdims=True)
        den = jnp.sum(sc, axis=1, keepdims=True) + w_inter * qn
        hh = num * (1.0 / jnp.maximum(jnp.abs(den), jnp.exp(-m_t)))
        mu = jnp.mean(hh, axis=1, keepdims=True)
        hc = hh - mu
        var = jnp.mean(hc * hc, axis=1, keepdims=True)
        heads.append(hc * lax.rsqrt(var + LN_EPS))

        g = b_r[:, tm - 1:tm]
        m_new = jnp.maximum(g + m_prev, jnp.max(g - b_r + i_r, axis=1, keepdims=True))
        decay = jnp.exp(g + m_prev - m_new)
        w_state = jnp.exp(g - b_c + i_c - m_new)
        kw = k * w_state
        ct_ref[h] = decay * ct_ref[h] + _dot(kw.T.astype(BF16), vh)
        n_ref[h] = decay * n_ref[h] + jnp.sum(kw, axis=0, keepdims=True)
        m_ref[:, h:h + 1] = m_new

    hn = jnp.concatenate(heads, axis=1) * gain_ref[...]
    h_out = _sigmoid(proj(COL_O, COL_G)) * hn
    y_mlstm = _dot(h_out.astype(BF16), wbm_ref[...])

    merged = (_sigmoid(proj(COL_G, COL_G + D_MODEL)) * y_pool
              + _sigmoid(proj(COL_G + D_MODEL, COL_IF)) * y_mlstm)
    mix = _dot(merged.astype(BF16), wout_ref[...])
    h1 = _layer_norm(DEEPNORM_ALPHA * h0 + mix, ln1g_ref[...], ln1b_ref[...])
    for c in range(ROW_TILES):
        h1_ref[pl.ds(c, tm, stride=ROW_TILES), :] = h1[:, c * LANES:(c + 1) * LANES]

    h1_hi = h1.astype(BF16)
    h1_lo = (h1 - h1_hi.astype(F32)).astype(BF16)
    la = _dot_nt(wr_ref[...], h1_hi)
    lb = _dot_nt(wr_ref[0:N_EXPERTS, :], h1_lo)
    logits = la[0:N_EXPERTS] + la[N_EXPERTS:2 * N_EXPERTS] + lb + br_ref[...]
    e_iota = lax.broadcasted_iota(jnp.int32, (N_EXPERTS, tm), 0)
    sels, vals, onehots = [], [], []
    lg = logits
    for _ in range(TOP_K):
        mx = jnp.max(lg, axis=0, keepdims=True)
        sel = jnp.min(jnp.where(lg == mx, e_iota, N_EXPERTS), axis=0, keepdims=True)
        oh = e_iota == sel
        lg = jnp.where(oh, -jnp.inf, lg)
        sels.append(sel)
        vals.append(mx)
        onehots.append(oh)
    exps = [jnp.exp(v - vals[0]) for v in vals]
    inv_den = 1.0 / (exps[0] + exps[1] + exps[2] + exps[3])
    gates = [e * inv_den for e in exps]

    oh_all = jnp.where(onehots[0], 1.0, 0.0)
    for oh in onehots[1:]:
        oh_all = oh_all + jnp.where(oh, 1.0, 0.0)
    strict = jnp.where(row_i < col_i, 1.0, 0.0).astype(BF16)
    before = _dot(oh_all.astype(BF16), strict) + run_ref[:, 0:1]
    ranks = [jnp.sum(jnp.where(oh, before, 0.0), axis=0, keepdims=True) for oh in onehots]
    run_ref[...] = run_ref[...] + jnp.sum(oh_all, axis=1, keepdims=True)
    cnt_ref[...] = run_ref[...]

    r8 = lax.broadcasted_iota(jnp.int32, (SUBLANES, tm), 0)
    idx_out = jnp.zeros((SUBLANES, tm), jnp.int32)
    rank_out = jnp.zeros((SUBLANES, tm), jnp.int32)
    r128 = lax.broadcasted_iota(jnp.int32, (LANES, tm), 0)
    gate_rows = jnp.zeros((LANES, tm), F32)
    for kk in range(TOP_K):
        idx_out = jnp.where(r8 == kk, sels[kk], idx_out)
        rank_out = jnp.where(r8 == kk, ranks[kk].astype(jnp.int32), rank_out)
        gate_rows = jnp.where(r128 == kk, gates[kk], gate_rows)
    idx_ref[...] = idx_out
    rank_ref[...] = rank_out
    gcol_ref[...] = gate_rows.T


def _const_spec(shape):
    zeros = (0,) * len(shape)
    return pl.BlockSpec(shape, lambda b, s: zeros, pipeline_mode=pl.Buffered(1))


def _mixer_call(x, weights):
    bsz, seq, _ = x.shape
    tm = TOKEN_TILE
    n_s = seq // tm
    t_total = bsz * seq
    tile = lambda b, s: (b * n_s + s)
    in_specs = [pl.BlockSpec((None, tm, D_MODEL), lambda b, s: (b, s, 0))]
    in_specs += [_const_spec(w.shape) for w in weights]
    out_shape = (
        jax.ShapeDtypeStruct((t_total * ROW_TILES, LANES), F32),
        jax.ShapeDtypeStruct((SUBLANES, t_total), jnp.int32),
        jax.ShapeDtypeStruct((SUBLANES, t_total), jnp.int32),
        jax.ShapeDtypeStruct((t_total, LANES), F32),
        jax.ShapeDtypeStruct((N_EXPERTS, LANES), F32),
    )
    out_specs = (
        pl.BlockSpec((tm * ROW_TILES, LANES), lambda b, s: (tile(b, s), 0)),
        pl.BlockSpec((SUBLANES, tm), lambda b, s: (0, tile(b, s))),
        pl.BlockSpec((SUBLANES, tm), lambda b, s: (0, tile(b, s))),
        pl.BlockSpec((tm, LANES), lambda b, s: (tile(b, s), 0)),
        pl.BlockSpec((N_EXPERTS, LANES), lambda b, s: (0, 0)),
    )
    scratch = [
        pltpu.VMEM((POOL_HALO + tm, POOL_WIDTH), F32),
        pltpu.VMEM((CONV_HALO + tm, D_MODEL), F32),
        pltpu.VMEM((N_HEADS, HEAD_DIM, HEAD_DIM), F32),
        pltpu.VMEM((N_HEADS, 1, HEAD_DIM), F32),
        pltpu.VMEM((1, LANES), F32),
        pltpu.VMEM((N_EXPERTS, LANES), F32),
    ]
    return pl.pallas_call(
        _mixer_kernel,
        grid=(bsz, n_s),
        in_specs=in_specs,
        out_specs=out_specs,
        out_shape=out_shape,
        scratch_shapes=scratch,
        compiler_params=pltpu.CompilerParams(
            dimension_semantics=("arbitrary", "arbitrary"), vmem_limit_bytes=VMEM_LIMIT),
        name="mixer",
    )(x, *weights)


def _experts_kernel(be_ref, first_ref, nexte_ref, nused_ref, slot_lo_ref, slot_hi_ref,
                    h1_hbm, w1_hbm, b1_ref, w2_hbm, b2_ref, y_hbm,
                    buf0_ref, buf1_ref, w1s_ref, w2s_ref, w1b_ref, w2z_ref, w2b_ref,
                    gsem, ssem, wsem, *, n_tokens):
    rows = EXPERT_ROWS
    tile = rows * ROW_TILES
    i = pl.program_id(0)
    last = pl.num_programs(0) - 1
    n_used = nused_ref[0]
    buf = (buf0_ref, buf1_ref)
    xg = tuple(b.at[pl.ds(0, tile)] for b in buf)
    yb = tuple(b.at[pl.ds(tile, tile)] for b in buf)
    dyn_zero = lax.shift_right_arithmetic(n_used, 31)

    def gather(slot, r, p):
        tok = slot & (n_tokens - 1)
        return pltpu.make_async_copy(
            h1_hbm.at[tok], xg[p].at[pl.ds(r * ROW_TILES, ROW_TILES)], gsem.at[p])

    def scatter(slot, r, p):
        return pltpu.make_async_copy(
            yb[p].at[pl.ds(r * ROW_TILES, ROW_TILES)], y_hbm.at[slot], ssem.at[p])

    def issue_rows(stage, g_slots, g_q, s_slots, s_q, q):
        share = lambda k, n: range(rows * k // n, rows * (k + 1) // n)
        if stage is None:
            g_rows, s_rows = range(rows), range(rows)
        elif stage < GATHER_STAGES:
            g_rows, s_rows = share(stage, GATHER_STAGES), range(0)
        else:
            g_rows, s_rows = range(0), share(stage - GATHER_STAGES, SCATTER_STAGES)
        for r in g_rows:
            gather(g_slots[g_q, 0, r], r, q).start()
        for r in s_rows:
            scatter(s_slots[s_q, 0, r], r, q).start()
        spare = pl.multiple_of(2 * tile + dyn_zero * SUBLANES, SUBLANES)
        buf[q][pl.ds(spare, SUBLANES), :] = jnp.full((SUBLANES, LANES), dyn_zero.astype(F32))
        return buf[q][pl.ds(spare, SUBLANES), :][0:1, :]

    def wait_gathers(p):
        pltpu.make_async_copy(xg[p], xg[p], gsem.at[p]).wait()

    def wait_scatters(p):
        pltpu.make_async_copy(yb[p], yb[p], ssem.at[p]).wait()

    def weight_copies(e):
        return (pltpu.make_async_copy(w1_hbm.at[e], w1s_ref, wsem.at[0]),
                pltpu.make_async_copy(w2_hbm.at[e], w2s_ref, wsem.at[1]))

    def switch_weights(blk):
        @pl.when(first_ref[blk] == 1)
        def _():
            for cp in weight_copies(0):
                cp.wait()
            step = 128
            for c in range(D_MODEL // step):
                w1b_ref[c * step:(c + 1) * step, :] = w1s_ref[c * step:(c + 1) * step, :].astype(BF16)
            half = D_FF // 2
            for c in range(ROW_TILES):
                cs = slice(c * LANES, (c + 1) * LANES)
                w2z_ref[c, pl.ds(0, half, stride=2), :] = w2s_ref[0:half, cs]
                w2z_ref[c, pl.ds(1, half, stride=2), :] = w2s_ref[half:D_FF, cs]
                w2b_ref[:, cs] = w2z_ref[c].astype(BF16)
            nxt = nexte_ref[blk]

            @pl.when(nxt >= 0)
            def _():
                for cp in weight_copies(nxt):
                    cp.start()

    n_chunks = D_FF // (2 * LANES)
    assert n_chunks + 1 == GATHER_STAGES + SCATTER_STAGES

    def compute(p, e, issue):
        x = jnp.concatenate(
            [xg[p][pl.ds(c, rows, stride=ROW_TILES), :] for c in range(ROW_TILES)], axis=1).astype(BF16)
        b1 = b1_ref[pl.ds(e, 1), :]
        width = 2 * LANES
        even = (lax.broadcasted_iota(jnp.int32, (rows, width), 1) & 1) == 0
        zs = []
        for c in range(n_chunks):
            lo = c * width
            hi = D_FF + lo
            zero = issue(c)
            zero = jnp.concatenate([zero, zero], axis=1)
            ha = _dot(x, w1b_ref[:, lo:lo + width]) + (b1[:, lo:lo + width] + zero)
            hb = _dot(x, w1b_ref[:, hi:hi + width]) + (b1[:, hi:hi + width] + zero)
            glu = jnp.where(even, ha, pltpu.roll(hb, 1, 1))
            lin = jnp.where(even, pltpu.roll(ha, width - 1, 1), hb)
            glu = jnp.minimum(glu, SWIGLU_LIMIT)
            lin = jnp.clip(lin, -SWIGLU_LIMIT, SWIGLU_LIMIT) + 1.0
            zs.append((glu * _sigmoid(SWIGLU_ALPHA * glu) * lin).astype(BF16))
        zero = issue(n_chunks)
        b2 = b2_ref[pl.ds(e, 1), :] + jnp.concatenate([zero] * ROW_TILES, axis=1)
        y = _dot(jnp.concatenate(zs, axis=1), w2b_ref[...]) + b2
        for c in range(ROW_TILES):
            yb[p][pl.ds(c, rows, stride=ROW_TILES), :] = y[:, c * LANES:(c + 1) * LANES]

    def phase(blk, p, issue):
        switch_weights(blk)

        @pl.when(blk < n_used)
        def _():
            compute(p, be_ref[blk], issue)

        @pl.when(blk >= n_used)
        def _():
            issue(None)

    @pl.when(i == 0)
    def _():
        buf1_ref[...] = jnp.zeros_like(buf1_ref)
        for cp in weight_copies(be_ref[0]):
            cp.start()
        for r in range(rows):
            gather(slot_lo_ref[1, 0, r], r, 0).start()

    wait_gathers(0)

    @pl.when(i >= 1)
    def _():
        wait_scatters(0)

    phase(2 * i, 0, functools.partial(
        issue_rows, g_slots=slot_hi_ref, g_q=0, s_slots=slot_lo_ref, s_q=0, q=1))

    wait_gathers(1)
    wait_scatters(1)

    phase(2 * i + 1, 1, functools.partial(
        issue_rows, g_slots=slot_hi_ref, g_q=1, s_slots=slot_lo_ref, s_q=1, q=0))

    @pl.when(i == last)
    def _():
        wait_scatters(0)
        for r in range(rows):
            scatter(slot_hi_ref[0, 0, r], r, 1).start()
        wait_scatters(1)
        wait_gathers(0)


def _experts_call(plan, h1_tiles, w_e1, b_e1, w_e2, b_e2, n_tokens):
    block_e, first, next_e, n_used, row_slot_ext = plan
    rows = EXPERT_ROWS
    n_blocks = block_e.shape[0]
    assert n_blocks % 2 == 0
    n_slots = TOP_K * n_tokens + rows
    tile_rows = rows * ROW_TILES
    smem_pair = lambda shift: pl.BlockSpec(
        (2, 1, rows), lambda i, *_: (i + shift, 0, 0), memory_space=pltpu.SMEM)
    whole = lambda shape: pl.BlockSpec(shape, lambda i, *_: (0,) * len(shape))
    grid_spec = pltpu.PrefetchScalarGridSpec(
        num_scalar_prefetch=4,
        grid=(n_blocks // 2,),
        in_specs=[
            smem_pair(0), smem_pair(1),
            pl.BlockSpec(memory_space=pl.ANY),
            pl.BlockSpec(memory_space=pl.ANY),
            whole((N_EXPERTS, 2 * D_FF)),
            pl.BlockSpec(memory_space=pl.ANY),
            whole((N_EXPERTS, D_MODEL)),
        ],
        out_specs=pl.BlockSpec(memory_space=pl.ANY),
        scratch_shapes=[
            pltpu.VMEM((2 * tile_rows + SUBLANES, LANES), F32),
            pltpu.VMEM((2 * tile_rows + SUBLANES, LANES), F32),
            pltpu.VMEM((D_MODEL, 2 * D_FF), F32),
            pltpu.VMEM((D_FF, D_MODEL), F32),
            pltpu.VMEM((D_MODEL, 2 * D_FF), BF16),
            pltpu.VMEM((ROW_TILES, D_FF, LANES), F32),
            pltpu.VMEM((D_FF, D_MODEL), BF16),
            pltpu.SemaphoreType.DMA((2,)),
            pltpu.SemaphoreType.DMA((2,)),
            pltpu.SemaphoreType.DMA((2,)),
        ],
    )
    return pl.pallas_call(
        functools.partial(_experts_kernel, n_tokens=n_tokens),
        grid_spec=grid_spec,
        out_shape=jax.ShapeDtypeStruct((n_slots, ROW_TILES, LANES), F32),
        compiler_params=pltpu.CompilerParams(
            dimension_semantics=("arbitrary",), vmem_limit_bytes=VMEM_LIMIT),
        name="experts",
    )(block_e, first, next_e, n_used, row_slot_ext, row_slot_ext, h1_tiles, w_e1, b_e1, w_e2, b_e2)


def _combine_kernel(h1_ref, y0_ref, y1_ref, y2_ref, y3_ref, gcol_ref, g_ref, b_ref, out_ref):
    tc = COMBINE_TILE
    gcol = gcol_ref[...]
    y_refs = (y0_ref, y1_ref, y2_ref, y3_ref)
    chunks = []
    for c in range(ROW_TILES):
        z = DEEPNORM_ALPHA * h1_ref[pl.ds(c, tc, stride=ROW_TILES), :]
        for kk in range(TOP_K):
            z = z + gcol[:, kk:kk + 1] * y_refs[kk][pl.ds(c, tc, stride=ROW_TILES), :]
        chunks.append(z)
    total = chunks[0].sum(axis=1, keepdims=True)
    for z in chunks[1:]:
        total = total + z.sum(axis=1, keepdims=True)
    mu = total * (1.0 / D_MODEL)
    sq = None
    for z in chunks:
        zc = z - mu
        part = (zc * zc).sum(axis=1, keepdims=True)
        sq = part if sq is None else sq + part
    inv = lax.rsqrt(sq * (1.0 / D_MODEL) + LN_EPS)
    for c, z in enumerate(chunks):
        cs = slice(c * LANES, (c + 1) * LANES)
        out_ref[:, cs] = (z - mu) * inv * g_ref[:, cs] + b_ref[:, cs]


def _combine_call(h1_2d, y_2d, gcol, ln_g, ln_b, n_tokens):
    tc = COMBINE_TILE
    n_t = n_tokens // tc
    blk = tc * ROW_TILES
    y_spec = lambda kk: pl.BlockSpec((blk, LANES), lambda i: (kk * n_t + i, 0))
    return pl.pallas_call(
        _combine_kernel,
        grid=(n_t,),
        in_specs=[
            pl.BlockSpec((blk, LANES), lambda i: (i, 0)),
            y_spec(0), y_spec(1), y_spec(2), y_spec(3),
            pl.BlockSpec((tc, LANES), lambda i: (i, 0)),
            pl.BlockSpec((1, D_MODEL), lambda i: (0, 0)),
            pl.BlockSpec((1, D_MODEL), lambda i: (0, 0)),
        ],
        out_specs=pl.BlockSpec((tc, D_MODEL), lambda i: (i, 0)),
        out_shape=jax.ShapeDtypeStruct((n_tokens, D_MODEL), F32),
        compiler_params=pltpu.CompilerParams(
            dimension_semantics=("arbitrary",), vmem_limit_bytes=VMEM_LIMIT),
        name="combine",
    )(h1_2d, y_2d, y_2d, y_2d, y_2d, gcol, ln_g, ln_b)


def _prepare_mixer_weights(ln0_g, ln0_b, w_in, conv_w, conv_b, w_q, w_k, b_if, mh_gain, w_pool,
                           b_pool, ls_pool, w_branch_pool, w_branch_mlstm, w_out, ln1_g, ln1_b,
                           w_router, b_router):
    s_p = POOL_WIDTH
    s_u, s_v, s_o = s_p + D_MODEL, s_p + 2 * D_MODEL, s_p + 3 * D_MODEL
    s_if = s_o + 2 * N_HEADS
    pad = jnp.zeros((D_MODEL, LANES - 2 * N_HEADS), w_in.dtype)
    w_in_r = jnp.concatenate(
        [w_in[:, :s_o], w_in[:, s_if:], w_in[:, s_o:s_if], pad], axis=1).astype(BF16)
    bif = jnp.concatenate([b_if, jnp.zeros((LANES - 2 * N_HEADS,), F32)]).reshape(1, LANES)
    wqk = jnp.concatenate([w_q * (HEAD_DIM ** -0.5), w_k], axis=-1).astype(BF16)
    wpool_bd = jax.scipy.linalg.block_diag(*[w_pool[g] for g in range(len(POOL_WINDOWS))]).astype(BF16)
    wr_t = w_router.T
    wr_hi = wr_t.astype(BF16)
    wr_lo = (wr_t - wr_hi.astype(F32)).astype(BF16)
    row = lambda v: v.reshape(1, -1)
    return (row(ln0_g), row(ln0_b), w_in_r, bif, conv_w, row(conv_b), wqk, row(mh_gain), wpool_bd,
            row(b_pool), row(ls_pool), w_branch_pool.astype(BF16), w_branch_mlstm.astype(BF16),
            w_out.astype(BF16), row(ln1_g), row(ln1_b), jnp.concatenate([wr_hi, wr_lo], axis=0),
            b_router.reshape(N_EXPERTS, 1))


def _route(idx, rank, counts, n_tokens):
    rows = EXPERT_ROWS
    n_assign = TOP_K * n_tokens
    n_blocks = (n_assign + N_EXPERTS * (rows - 1) + rows - 1) // rows
    padded = (counts + rows - 1) // rows * rows
    pend = jnp.cumsum(padded)
    pstart = pend - padded
    onehot = idx[..., None] == jnp.arange(N_EXPERTS, dtype=jnp.int32)
    pos = jnp.sum(jnp.where(onehot, pstart, 0), axis=-1) + rank
    slots = jnp.arange(n_assign, dtype=jnp.int32)
    hit = jnp.zeros((n_blocks * rows,), jnp.int32).at[pos.reshape(-1)].add(slots + 1)
    pad_slot = n_assign + jnp.arange(n_blocks * rows, dtype=jnp.int32) % rows
    row_slot = jnp.where(hit == 0, pad_slot, hit - 1)
    filler = (n_assign + jnp.arange(rows, dtype=jnp.int32)).reshape(1, rows)
    row_slot_ext = jnp.concatenate([filler, row_slot.reshape(n_blocks, rows), filler], axis=0)

    blk = jnp.arange(n_blocks, dtype=jnp.int32)
    n_used = pend[-1] // rows
    valid = blk < n_used
    block_e = jnp.sum((pend[None, :] <= (blk * rows)[:, None]).astype(jnp.int32), axis=1)
    block_e = jnp.minimum(block_e, N_EXPERTS - 1)
    prev_e = jnp.concatenate([jnp.full((1,), -1, jnp.int32), block_e[:-1]])
    first = jnp.logical_and(valid, block_e != prev_e).astype(jnp.int32)
    e_iota = jnp.arange(N_EXPERTS, dtype=jnp.int32)
    seg_end_blk = jnp.sum(jnp.where(block_e[:, None] == e_iota[None, :], pend[None, :], 0), axis=1) // rows
    e_at_end = jnp.sum(jnp.where(blk[None, :] == seg_end_blk[:, None], block_e[None, :], 0), axis=1)
    next_e = jnp.where(seg_end_blk < n_used, e_at_end, -1).astype(jnp.int32)
    return (block_e, first, next_e, n_used.reshape(1).astype(jnp.int32),
            row_slot_ext.reshape(n_blocks + 2, 1, rows))


def kernel(x, ln0_g, ln0_b, w_in, conv_w, conv_b, w_q, w_k, b_if, mh_gain, w_pool, b_pool, ls_pool,
           w_branch_pool, w_branch_mlstm, w_out, ln1_g, ln1_b, w_router, b_router, w_e1, b_e1,
           w_e2, b_e2, ln2_g, ln2_b):
    bsz, seq, _ = x.shape
    n_tokens = bsz * seq
    assert w_in.shape[0] == 1, "single-layer trunk"
    assert seq % TOKEN_TILE == 0 and n_tokens % COMBINE_TILE == 0
    assert n_tokens & (n_tokens - 1) == 0, "slot -> token uses a power-of-two mask"
    weights = _prepare_mixer_weights(
        ln0_g, ln0_b, w_in[0], conv_w[0], conv_b[0], w_q[0], w_k[0], b_if[0], mh_gain[0], w_pool[0],
        b_pool[0], ls_pool[0], w_branch_pool[0], w_branch_mlstm[0], w_out[0], ln1_g[0], ln1_b[0],
        w_router[0], b_router[0])
    h1_2d, idx8, rank8, gcol, cnt = _mixer_call(x, weights)
    counts = cnt[:, 0].astype(jnp.int32)
    plan = _route(idx8[:TOP_K], rank8[:TOP_K], counts, n_tokens)
    y_slots = _experts_call(
        plan, h1_2d.reshape(n_tokens, ROW_TILES, LANES), w_e1[0], b_e1[0], w_e2[0], b_e2[0], n_tokens)
    out = _combine_call(h1_2d, y_slots.reshape(-1, LANES), gcol, ln2_g[0].reshape(1, D_MODEL),
                        ln2_b[0].reshape(1, D_MODEL), n_tokens)
    return out.reshape(bsz, seq, D_MODEL)
```

```python
import functools

import jax
import jax.numpy as jnp
from jax import lax
from jax.experimental import pallas as pl
from jax.experimental.pallas import tpu as pltpu

F32 = jnp.float32
BF16 = jnp.bfloat16

D_MODEL = 1024
N_HEADS = 4
HEAD_DIM = 256
POOL_WIDTH = 512
POOL_GROUP = 128
POOL_WINDOWS = (2, 4, 8, 16)
CONV_WIDTH = 4
N_EXPERTS = 32
TOP_K = 4
D_FF = 1024
SWIGLU_ALPHA = 1.702
SWIGLU_LIMIT = 7.0
LN_EPS = 1e-5
DEEPNORM_ALPHA = 2.0 ** 0.25

SUBLANES = 8
LANES = 128
ROW_TILES = D_MODEL // LANES

COL_P = 0
COL_U = COL_P + POOL_WIDTH
COL_V = COL_U + D_MODEL
COL_O = COL_V + D_MODEL
COL_G = COL_O + D_MODEL
COL_IF = COL_G + 2 * D_MODEL
IN_COLS_PADDED = COL_IF + LANES

TOKEN_TILE = 256
POOL_HALO = 16
CONV_HALO = 8
EXPERT_ROWS = 256
GATHER_STAGES = 2
SCATTER_STAGES = 3
DMA_PRIORITIES = 2
COMBINE_TILE = 256
VMEM_LIMIT = 56 * 1024 * 1024


def _layer_norm(x, g, b):
    mu = jnp.mean(x, axis=-1, keepdims=True)
    xc = x - mu
    var = jnp.mean(xc * xc, axis=-1, keepdims=True)
    return xc * lax.rsqrt(var + LN_EPS) * g + b


def _sigmoid(x):
    return 1.0 / (1.0 + jnp.exp(-x))


def _log_sigmoid(x):
    return -(jnp.maximum(-x, 0.0) + jnp.log(1.0 + jnp.exp(-jnp.abs(x))))


def _split3(x):
    hi = x.astype(BF16)
    r1 = x - hi.astype(F32)
    mid = r1.astype(BF16)
    lo = (r1 - mid.astype(F32)).astype(BF16)
    return hi, mid, lo


def _dot(a, b):
    return jnp.dot(a, b, preferred_element_type=F32)


def _dot_nt(a, b):
    return lax.dot_general(a, b, (((1,), (1,)), ((), ())), preferred_element_type=F32)


def _mixer_kernel(x_ref, ln0g_ref, ln0b_ref, win_ref, bif_ref, convw_ref, convb_ref, wqk_ref,
                  gain_ref, wpool_ref, bpool_ref, lspool_ref, wbp_ref, wbm_ref, wout_ref,
                  ln1g_ref, ln1b_ref, wr_ref, br_ref,
                  h1_ref, idx_ref, rank_ref, gcol_ref, cnt_ref,
                  pext_ref, uext_ref, ct_ref, n_ref, m_ref, run_ref):
    tm = TOKEN_TILE
    b = pl.program_id(0)
    s = pl.program_id(1)

    @pl.when(s == 0)
    def _():
        pext_ref[0:POOL_HALO, :] = jnp.zeros((POOL_HALO, POOL_WIDTH), F32)
        uext_ref[0:CONV_HALO, :] = jnp.zeros((CONV_HALO, D_MODEL), F32)
        ct_ref[...] = jnp.zeros_like(ct_ref)
        n_ref[...] = jnp.zeros_like(n_ref)
        m_ref[...] = jnp.zeros_like(m_ref)

    @pl.when(jnp.logical_and(b == 0, s == 0))
    def _():
        run_ref[...] = jnp.zeros_like(run_ref)

    h0 = _layer_norm(x_ref[...], ln0g_ref[...], ln0b_ref[...])
    h0b = h0.astype(BF16)

    def proj(lo, hi):
        return _dot(h0b, win_ref[:, lo:hi])

    pext_ref[POOL_HALO:POOL_HALO + tm, :] = proj(COL_P, COL_U)
    tpos = s * tm + lax.broadcasted_iota(jnp.int32, (tm, 1), 0)
    groups = []
    for gi, w in enumerate(POOL_WINDOWS):
        c0 = gi * POOL_GROUP
        cur = pext_ref[POOL_HALO:POOL_HALO + tm, c0:c0 + POOL_GROUP]
        acc = cur
        for j in range(1, w):
            acc = acc + pext_ref[POOL_HALO - j:POOL_HALO - j + tm, c0:c0 + POOL_GROUP]
        inv_cnt = 1.0 / jnp.minimum(tpos + 1, w).astype(F32)
        groups.append(acc * inv_cnt - cur)
    pooled = jnp.concatenate(groups, axis=1)
    pext_ref[0:POOL_HALO, :] = pext_ref[tm:tm + POOL_HALO, :]
    mixed = (_dot(pooled.astype(BF16), wpool_ref[...]) + bpool_ref[...]) * lspool_ref[...]
    y_pool = _dot(mixed.astype(BF16), wbp_ref[...])

    uext_ref[CONV_HALO:CONV_HALO + tm, :] = proj(COL_U, COL_V)
    conv = convb_ref[...]
    for j in range(CONV_WIDTH):
        off = CONV_HALO - (CONV_WIDTH - 1) + j
        conv = conv + convw_ref[j:j + 1, :] * uext_ref[off:off + tm, :]
    uext_ref[0:CONV_HALO, :] = uext_ref[tm:tm + CONV_HALO, :]
    ucb = (conv * _sigmoid(conv)).astype(BF16)
    vb = proj(COL_V, COL_O).astype(BF16)

    slab = proj(COL_IF, IN_COLS_PADDED) + bif_ref[...]
    lane = lax.broadcasted_iota(jnp.int32, (tm, LANES), 1)
    is_f = jnp.logical_and(lane >= N_HEADS, lane < 2 * N_HEADS)
    slab = jnp.where(is_f, _log_sigmoid(slab), slab)
    row_i = lax.broadcasted_iota(jnp.int32, (tm, tm), 0)
    col_i = lax.broadcasted_iota(jnp.int32, (tm, tm), 1)
    causal = row_i >= col_i
    tri = jnp.where(causal, 1.0, 0.0).astype(BF16)
    hi, mid, lo = _split3(slab)
    bcol = _dot(tri, hi) + _dot(tri, mid) + _dot(tri, lo)
    slab_t = slab.T
    bcol_t = bcol.T

    heads = []
    for h in range(N_HEADS):
        hs = slice(h * HEAD_DIM, (h + 1) * HEAD_DIM)
        qk = _dot(ucb[:, hs], wqk_ref[h])
        q = qk[:, :HEAD_DIM]
        k = qk[:, HEAD_DIM:]
        qb = q.astype(BF16)
        kb = k.astype(BF16)
        vh = vb[:, hs]

        i_c = slab[:, h:h + 1]
        b_c = bcol[:, N_HEADS + h:N_HEADS + h + 1]
        i_r = slab_t[h:h + 1, :]
        b_r = bcol_t[N_HEADS + h:N_HEADS + h + 1, :]
        m_prev = m_ref[:, h:h + 1]

        d_log = jnp.where(causal, b_c - (b_r - i_r), -jnp.inf)
        m_inter = b_c + m_prev
        m_t = jnp.maximum(m_inter, jnp.max(d_log, axis=1, keepdims=True))
        w_intra = jnp.exp(d_log - m_t)
        sc = _dot_nt(qb, kb) * w_intra
        w_inter = jnp.exp(m_inter - m_t)
        ctb = ct_ref[h].astype(BF16)
        num = _dot(sc.astype(BF16), vh) + w_inter * _dot(qb, ctb)
        qn = jnp.sum(q * n_ref[h], axis=1, keepdims=True)
        den = jnp.sum(sc, axis=1, keepdims=True) + w_inter * qn
        hh = num * (1.0 / jnp.maximum(jnp.abs(den), jnp.exp(-m_t)))
        mu = jnp.mean(hh, axis=1, keepdims=True)
        hc = hh - mu
        var = jnp.mean(hc * hc, axis=1, keepdims=True)
        heads.append(hc * lax.rsqrt(var + LN_EPS))

        g = b_r[:, tm - 1:tm]
        m_new = jnp.maximum(g + m_prev, jnp.max(g - b_r + i_r, axis=1, keepdims=True))
        decay = jnp.exp(g + m_prev - m_new)
        w_state = jnp.exp(g - b_c + i_c - m_new)
        kw = k * w_state
        ct_ref[h] = decay * ct_ref[h] + _dot(kw.T.astype(BF16), vh)
        n_ref[h] = decay * n_ref[h] + jnp.sum(kw, axis=0, keepdims=True)
        m_ref[:, h:h + 1] = m_new

    hn = jnp.concatenate(heads, axis=1) * gain_ref[...]
    h_out = _sigmoid(proj(COL_O, COL_G)) * hn
    y_mlstm = _dot(h_out.astype(BF16), wbm_ref[...])

    merged = (_sigmoid(proj(COL_G, COL_G + D_MODEL)) * y_pool
              + _sigmoid(proj(COL_G + D_MODEL, COL_IF)) * y_mlstm)
    mix = _dot(merged.astype(BF16), wout_ref[...])
    h1 = _layer_norm(DEEPNORM_ALPHA * h0 + mix, ln1g_ref[...], ln1b_ref[...])
    for c in range(ROW_TILES):
        h1_ref[pl.ds(c, tm, stride=ROW_TILES), :] = h1[:, c * LANES:(c + 1) * LANES]

    h1_hi = h1.astype(BF16)
    h1_lo = (h1 - h1_hi.astype(F32)).astype(BF16)
    la = _dot_nt(wr_ref[...], h1_hi)
    lb = _dot_nt(wr_ref[0:N_EXPERTS, :], h1_lo)
    logits = la[0:N_EXPERTS] + la[N_EXPERTS:2 * N_EXPERTS] + lb + br_ref[...]
    e_iota = lax.broadcasted_iota(jnp.int32, (N_EXPERTS, tm), 0)
    sels, vals, onehots = [], [], []
    lg = logits
    for _ in range(TOP_K):
        mx = jnp.max(lg, axis=0, keepdims=True)
        sel = jnp.min(jnp.where(lg == mx, e_iota, N_EXPERTS), axis=0, keepdims=True)
        oh = e_iota == sel
        lg = jnp.where(oh, -jnp.inf, lg)
        sels.append(sel)
        vals.append(mx)
        onehots.append(oh)
    exps = [jnp.exp(v - vals[0]) for v in vals]
    inv_den = 1.0 / (exps[0] + exps[1] + exps[2] + exps[3])
    gates = [e * inv_den for e in exps]

    oh_all = jnp.where(onehots[0], 1.0, 0.0)
    for oh in onehots[1:]:
        oh_all = oh_all + jnp.where(oh, 1.0, 0.0)
    strict = jnp.where(row_i < col_i, 1.0, 0.0).astype(BF16)
    before = _dot(oh_all.astype(BF16), strict) + run_ref[:, 0:1]
    ranks = [jnp.sum(jnp.where(oh, before, 0.0), axis=0, keepdims=True) for oh in onehots]
    run_ref[...] = run_ref[...] + jnp.sum(oh_all, axis=1, keepdims=True)
    cnt_ref[...] = run_ref[...]

    r8 = lax.broadcasted_iota(jnp.int32, (SUBLANES, tm), 0)
    idx_out = jnp.zeros((SUBLANES, tm), jnp.int32)
    rank_out = jnp.zeros((SUBLANES, tm), jnp.int32)
    r128 = lax.broadcasted_iota(jnp.int32, (LANES, tm), 0)
    gate_rows = jnp.zeros((LANES, tm), F32)
    for kk in range(TOP_K):
        idx_out = jnp.where(r8 == kk, sels[kk], idx_out)
        rank_out = jnp.where(r8 == kk, ranks[kk].astype(jnp.int32), rank_out)
        gate_rows = jnp.where(r128 == kk, gates[kk], gate_rows)
    idx_ref[...] = idx_out
    rank_ref[...] = rank_out
    gcol_ref[...] = gate_rows.T


def _const_spec(shape):
    zeros = (0,) * len(shape)
    return pl.BlockSpec(shape, lambda b, s: zeros, pipeline_mode=pl.Buffered(1))


def _mixer_call(x, weights):
    bsz, seq, _ = x.shape
    tm = TOKEN_TILE
    n_s = seq // tm
    t_total = bsz * seq
    tile = lambda b, s: (b * n_s + s)
    in_specs = [pl.BlockSpec((None, tm, D_MODEL), lambda b, s: (b, s, 0))]
    in_specs += [_const_spec(w.shape) for w in weights]
    out_shape = (
        jax.ShapeDtypeStruct((t_total * ROW_TILES, LANES), F32),
        jax.ShapeDtypeStruct((SUBLANES, t_total), jnp.int32),
        jax.ShapeDtypeStruct((SUBLANES, t_total), jnp.int32),
        jax.ShapeDtypeStruct((t_total, LANES), F32),
        jax.ShapeDtypeStruct((N_EXPERTS, LANES), F32),
    )
    out_specs = (
        pl.BlockSpec((tm * ROW_TILES, LANES), lambda b, s: (tile(b, s), 0)),
        pl.BlockSpec((SUBLANES, tm), lambda b, s: (0, tile(b, s))),
        pl.BlockSpec((SUBLANES, tm), lambda b, s: (0, tile(b, s))),
        pl.BlockSpec((tm, LANES), lambda b, s: (tile(b, s), 0)),
        pl.BlockSpec((N_EXPERTS, LANES), lambda b, s: (0, 0)),
    )
    scratch = [
        pltpu.VMEM((POOL_HALO + tm, POOL_WIDTH), F32),
        pltpu.VMEM((CONV_HALO + tm, D_MODEL), F32),
        pltpu.VMEM((N_HEADS, HEAD_DIM, HEAD_DIM), F32),
        pltpu.VMEM((N_HEADS, 1, HEAD_DIM), F32),
        pltpu.VMEM((1, LANES), F32),
        pltpu.VMEM((N_EXPERTS, LANES), F32),
    ]
    return pl.pallas_call(
        _mixer_kernel,
        grid=(bsz, n_s),
        in_specs=in_specs,
        out_specs=out_specs,
        out_shape=out_shape,
        scratch_shapes=scratch,
        compiler_params=pltpu.CompilerParams(
            dimension_semantics=("arbitrary", "arbitrary"), vmem_limit_bytes=VMEM_LIMIT),
        name="mixer",
    )(x, *weights)


def _experts_kernel(be_ref, first_ref, nexte_ref, nused_ref, slot_lo_ref, slot_hi_ref,
                    h1_hbm, w1_hbm, b1_ref, w2_hbm, b2_ref, y_hbm,
                    buf0_ref, buf1_ref, w1s_ref, w2s_ref, w1b_ref, w2z_ref, w2b_ref,
                    gsem, ssem, wsem, *, n_tokens):
    rows = EXPERT_ROWS
    tile = rows * ROW_TILES
    i = pl.program_id(0)
    last = pl.num_programs(0) - 1
    n_used = nused_ref[0]
    buf = (buf0_ref, buf1_ref)
    xg = tuple(b.at[pl.ds(0, tile)] for b in buf)
    yb = tuple(b.at[pl.ds(tile, tile)] for b in buf)
    dyn_zero = lax.shift_right_arithmetic(n_used, 31)

    def gather(slot, r, p):
        tok = slot & (n_tokens - 1)
        return pltpu.make_async_copy(
            h1_hbm.at[tok], xg[p].at[pl.ds(r * ROW_TILES, ROW_TILES)], gsem.at[p])

    def scatter(slot, r, p):
        return pltpu.make_async_copy(
            yb[p].at[pl.ds(r * ROW_TILES, ROW_TILES)], y_hbm.at[slot], ssem.at[p])

    def issue_rows(stage, g_slots, g_q, s_slots, s_q, q):
        share = lambda k, n: range(rows * k // n, rows * (k + 1) // n)
        if stage is None:
            g_rows, s_rows = range(rows), range(rows)
        elif stage < GATHER_STAGES:
            g_rows, s_rows = share(stage, GATHER_STAGES), range(0)
        else:
            g_rows, s_rows = range(0), share(stage - GATHER_STAGES, SCATTER_STAGES)
        for r in g_rows:
            gather(g_slots[g_q, 0, r], r, q).start(priority=r % DMA_PRIORITIES)
        for r in s_rows:
            scatter(s_slots[s_q, 0, r], r, q).start(priority=r % DMA_PRIORITIES)
        spare = pl.multiple_of(2 * tile + dyn_zero * SUBLANES, SUBLANES)
        buf[q][pl.ds(spare, SUBLANES), :] = jnp.full((SUBLANES, LANES), dyn_zero.astype(F32))
        return buf[q][pl.ds(spare, SUBLANES), :][0:1, :]

    def wait_gathers(p):
        pltpu.make_async_copy(xg[p], xg[p], gsem.at[p]).wait()

    def wait_scatters(p):
        pltpu.make_async_copy(yb[p], yb[p], ssem.at[p]).wait()

    def weight_copies(e):
        return (pltpu.make_async_copy(w1_hbm.at[e], w1s_ref, wsem.at[0]),
                pltpu.make_async_copy(w2_hbm.at[e], w2s_ref, wsem.at[1]))

    def switch_weights(blk):
        @pl.when(first_ref[blk] == 1)
        def _():
            for cp in weight_copies(0):
                cp.wait()
            step = 128
            for c in range(D_MODEL // step):
                w1b_ref[c * step:(c + 1) * step, :] = w1s_ref[c * step:(c + 1) * step, :].astype(BF16)
            half = D_FF // 2
            for c in range(ROW_TILES):
                cs = slice(c * LANES, (c + 1) * LANES)
                w2z_ref[c, pl.ds(0, half, stride=2), :] = w2s_ref[0:half, cs]
                w2z_ref[c, pl.ds(1, half, stride=2), :] = w2s_ref[half:D_FF, cs]
                w2b_ref[:, cs] = w2z_ref[c].astype(BF16)
            nxt = nexte_ref[blk]

            @pl.when(nxt >= 0)
            def _():
                for cp in weight_copies(nxt):
                    cp.start()

    n_chunks = D_FF // (2 * LANES)
    assert n_chunks + 1 == GATHER_STAGES + SCATTER_STAGES

    def compute(p, e, issue):
        x = jnp.concatenate(
            [xg[p][pl.ds(c, rows, stride=ROW_TILES), :] for c in range(ROW_TILES)], axis=1).astype(BF16)
        b1 = b1_ref[pl.ds(e, 1), :]
        width = 2 * LANES
        even = (lax.broadcasted_iota(jnp.int32, (rows, width), 1) & 1) == 0
        zs = []
        for c in range(n_chunks):
            lo = c * width
            hi = D_FF + lo
            zero = issue(c)
            zero = jnp.concatenate([zero, zero], axis=1)
            ha = _dot(x, w1b_ref[:, lo:lo + width]) + (b1[:, lo:lo + width] + zero)
            hb = _dot(x, w1b_ref[:, hi:hi + width]) + (b1[:, hi:hi + width] + zero)
            glu = jnp.where(even, ha, pltpu.roll(hb, 1, 1))
            lin = jnp.where(even, pltpu.roll(ha, width - 1, 1), hb)
            glu = jnp.minimum(glu, SWIGLU_LIMIT)
            lin = jnp.clip(lin, -SWIGLU_LIMIT, SWIGLU_LIMIT) + 1.0
            zs.append((glu * _sigmoid(SWIGLU_ALPHA * glu) * lin).astype(BF16))
        zero = issue(n_chunks)
        b2 = b2_ref[pl.ds(e, 1), :] + jnp.concatenate([zero] * ROW_TILES, axis=1)
        y = _dot(jnp.concatenate(zs, axis=1), w2b_ref[...]) + b2
        for c in range(ROW_TILES):
            yb[p][pl.ds(c, rows, stride=ROW_TILES), :] = y[:, c * LANES:(c + 1) * LANES]

    def phase(blk, p, issue):
        switch_weights(blk)

        @pl.when(blk < n_used)
        def _():
            compute(p, be_ref[blk], issue)

        @pl.when(blk >= n_used)
        def _():
            issue(None)

    @pl.when(i == 0)
    def _():
        buf1_ref[...] = jnp.zeros_like(buf1_ref)
        for cp in weight_copies(be_ref[0]):
            cp.start()
        for r in range(rows):
            gather(slot_lo_ref[1, 0, r], r, 0).start()

    wait_gathers(0)

    @pl.when(i >= 1)
    def _():
        wait_scatters(0)

    phase(2 * i, 0, functools.partial(
        issue_rows, g_slots=slot_hi_ref, g_q=0, s_slots=slot_lo_ref, s_q=0, q=1))

    wait_gathers(1)
    wait_scatters(1)

    phase(2 * i + 1, 1, functools.partial(
        issue_rows, g_slots=slot_hi_ref, g_q=1, s_slots=slot_lo_ref, s_q=1, q=0))

    @pl.when(i == last)
    def _():
        wait_scatters(0)
        for r in range(rows):
            scatter(slot_hi_ref[0, 0, r], r, 1).start()
        wait_scatters(1)
        wait_gathers(0)


def _experts_call(plan, h1_tiles, w_e1, b_e1, w_e2, b_e2, n_tokens):
    block_e, first, next_e, n_used, row_slot_ext = plan
    rows = EXPERT_ROWS
    n_blocks = block_e.shape[0]
    assert n_blocks % 2 == 0
    n_slots = TOP_K * n_tokens + rows
    tile_rows = rows * ROW_TILES
    smem_pair = lambda shift: pl.BlockSpec(
        (2, 1, rows), lambda i, *_: (i + shift, 0, 0), memory_space=pltpu.SMEM)
    whole = lambda shape: pl.BlockSpec(shape, lambda i, *_: (0,) * len(shape))
    grid_spec = pltpu.PrefetchScalarGridSpec(
        num_scalar_prefetch=4,
        grid=(n_blocks // 2,),
        in_specs=[
            smem_pair(0), smem_pair(1),
            pl.BlockSpec(memory_space=pl.ANY),
            pl.BlockSpec(memory_space=pl.ANY),
            whole((N_EXPERTS, 2 * D_FF)),
            pl.BlockSpec(memory_space=pl.ANY),
            whole((N_EXPERTS, D_MODEL)),
        ],
        out_specs=pl.BlockSpec(memory_space=pl.ANY),
        scratch_shapes=[
            pltpu.VMEM((2 * tile_rows + SUBLANES, LANES), F32),
            pltpu.VMEM((2 * tile_rows + SUBLANES, LANES), F32),
            pltpu.VMEM((D_MODEL, 2 * D_FF), F32),
            pltpu.VMEM((D_FF, D_MODEL), F32),
            pltpu.VMEM((D_MODEL, 2 * D_FF), BF16),
            pltpu.VMEM((ROW_TILES, D_FF, LANES), F32),
            pltpu.VMEM((D_FF, D_MODEL), BF16),
            pltpu.SemaphoreType.DMA((2,)),
            pltpu.SemaphoreType.DMA((2,)),
            pltpu.SemaphoreType.DMA((2,)),
        ],
    )
    return pl.pallas_call(
        functools.partial(_experts_kernel, n_tokens=n_tokens),
        grid_spec=grid_spec,
        out_shape=jax.ShapeDtypeStruct((n_slots, ROW_TILES, LANES), F32),
        compiler_params=pltpu.CompilerParams(
            dimension_semantics=("arbitrary",), vmem_limit_bytes=VMEM_LIMIT),
        name="experts",
    )(block_e, first, next_e, n_used, row_slot_ext, row_slot_ext, h1_tiles, w_e1, b_e1, w_e2, b_e2)


def _combine_kernel(h1_ref, y0_ref, y1_ref, y2_ref, y3_ref, gcol_ref, g_ref, b_ref, out_ref):
    tc = COMBINE_TILE
    gcol = gcol_ref[...]
    y_refs = (y0_ref, y1_ref, y2_ref, y3_ref)
    chunks = []
    for c in range(ROW_TILES):
        z = DEEPNORM_ALPHA * h1_ref[pl.ds(c, tc, stride=ROW_TILES), :]
        for kk in range(TOP_K):
            z = z + gcol[:, kk:kk + 1] * y_refs[kk][pl.ds(c, tc, stride=ROW_TILES), :]
        chunks.append(z)
    total = chunks[0].sum(axis=1, keepdims=True)
    for z in chunks[1:]:
        total = total + z.sum(axis=1, keepdims=True)
    mu = total * (1.0 / D_MODEL)
    sq = None
    for z in chunks:
        zc = z - mu
        part = (zc * zc).sum(axis=1, keepdims=True)
        sq = part if sq is None else sq + part
    inv = lax.rsqrt(sq * (1.0 / D_MODEL) + LN_EPS)
    for c, z in enumerate(chunks):
        cs = slice(c * LANES, (c + 1) * LANES)
        out_ref[:, cs] = (z - mu) * inv * g_ref[:, cs] + b_ref[:, cs]


def _combine_call(h1_2d, y_2d, gcol, ln_g, ln_b, n_tokens):
    tc = COMBINE_TILE
    n_t = n_tokens // tc
    blk = tc * ROW_TILES
    y_spec = lambda kk: pl.BlockSpec((blk, LANES), lambda i: (kk * n_t + i, 0))
    return pl.pallas_call(
        _combine_kernel,
        grid=(n_t,),
        in_specs=[
            pl.BlockSpec((blk, LANES), lambda i: (i, 0)),
            y_spec(0), y_spec(1), y_spec(2), y_spec(3),
            pl.BlockSpec((tc, LANES), lambda i: (i, 0)),
            pl.BlockSpec((1, D_MODEL), lambda i: (0, 0)),
            pl.BlockSpec((1, D_MODEL), lambda i: (0, 0)),
        ],
        out_specs=pl.BlockSpec((tc, D_MODEL), lambda i: (i, 0)),
        out_shape=jax.ShapeDtypeStruct((n_tokens, D_MODEL), F32),
        compiler_params=pltpu.CompilerParams(
            dimension_semantics=("arbitrary",), vmem_limit_bytes=VMEM_LIMIT),
        name="combine",
    )(h1_2d, y_2d, y_2d, y_2d, y_2d, gcol, ln_g, ln_b)


def _prepare_mixer_weights(ln0_g, ln0_b, w_in, conv_w, conv_b, w_q, w_k, b_if, mh_gain, w_pool,
                           b_pool, ls_pool, w_branch_pool, w_branch_mlstm, w_out, ln1_g, ln1_b,
                           w_router, b_router):
    s_p = POOL_WIDTH
    s_u, s_v, s_o = s_p + D_MODEL, s_p + 2 * D_MODEL, s_p + 3 * D_MODEL
    s_if = s_o + 2 * N_HEADS
    pad = jnp.zeros((D_MODEL, LANES - 2 * N_HEADS), w_in.dtype)
    w_in_r = jnp.concatenate(
        [w_in[:, :s_o], w_in[:, s_if:], w_in[:, s_o:s_if], pad], axis=1).astype(BF16)
    bif = jnp.concatenate([b_if, jnp.zeros((LANES - 2 * N_HEADS,), F32)]).reshape(1, LANES)
    wqk = jnp.concatenate([w_q * (HEAD_DIM ** -0.5), w_k], axis=-1).astype(BF16)
    wpool_bd = jax.scipy.linalg.block_diag(*[w_pool[g] for g in range(len(POOL_WINDOWS))]).astype(BF16)
    wr_t = w_router.T
    wr_hi = wr_t.astype(BF16)
    wr_lo = (wr_t - wr_hi.astype(F32)).astype(BF16)
    row = lambda v: v.reshape(1, -1)
    return (row(ln0_g), row(ln0_b), w_in_r, bif, conv_w, row(conv_b), wqk, row(mh_gain), wpool_bd,
            row(b_pool), row(ls_pool), w_branch_pool.astype(BF16), w_branch_mlstm.astype(BF16),
            w_out.astype(BF16), row(ln1_g), row(ln1_b), jnp.concatenate([wr_hi, wr_lo], axis=0),
            b_router.reshape(N_EXPERTS, 1))


def _route(idx, rank, counts, n_tokens):
    rows = EXPERT_ROWS
    n_assign = TOP_K * n_tokens
    n_blocks = (n_assign + N_EXPERTS * (rows - 1) + rows - 1) // rows
    padded = (counts + rows - 1) // rows * rows
    pend = jnp.cumsum(padded)
    pstart = pend - padded
    onehot = idx[..., None] == jnp.arange(N_EXPERTS, dtype=jnp.int32)
    pos = jnp.sum(jnp.where(onehot, pstart, 0), axis=-1) + rank
    slots = jnp.arange(n_assign, dtype=jnp.int32)
    hit = jnp.zeros((n_blocks * rows,), jnp.int32).at[pos.reshape(-1)].add(slots + 1)
    pad_slot = n_assign + jnp.arange(n_blocks * rows, dtype=jnp.int32) % rows
    row_slot = jnp.where(hit == 0, pad_slot, hit - 1)
    filler = (n_assign + jnp.arange(rows, dtype=jnp.int32)).reshape(1, rows)
    row_slot_ext = jnp.concatenate([filler, row_slot.reshape(n_blocks, rows), filler], axis=0)

    blk = jnp.arange(n_blocks, dtype=jnp.int32)
    n_used = pend[-1] // rows
    valid = blk < n_used
    block_e = jnp.sum((pend[None, :] <= (blk * rows)[:, None]).astype(jnp.int32), axis=1)
    block_e = jnp.minimum(block_e, N_EXPERTS - 1)
    prev_e = jnp.concatenate([jnp.full((1,), -1, jnp.int32), block_e[:-1]])
    first = jnp.logical_and(valid, block_e != prev_e).astype(jnp.int32)
    e_iota = jnp.arange(N_EXPERTS, dtype=jnp.int32)
    seg_end_blk = jnp.sum(jnp.where(block_e[:, None] == e_iota[None, :], pend[None, :], 0), axis=1) // rows
    e_at_end = jnp.sum(jnp.where(blk[None, :] == seg_end_blk[:, None], block_e[None, :], 0), axis=1)
    next_e = jnp.where(seg_end_blk < n_used, e_at_end, -1).astype(jnp.int32)
    return (block_e, first, next_e, n_used.reshape(1).astype(jnp.int32),
            row_slot_ext.reshape(n_blocks + 2, 1, rows))


def kernel(x, ln0_g, ln0_b, w_in, conv_w, conv_b, w_q, w_k, b_if, mh_gain, w_pool, b_pool, ls_pool,
           w_branch_pool, w_branch_mlstm, w_out, ln1_g, ln1_b, w_router, b_router, w_e1, b_e1,
           w_e2, b_e2, ln2_g, ln2_b):
    bsz, seq, _ = x.shape
    n_tokens = bsz * seq
    assert w_in.shape[0] == 1, "single-layer trunk"
    assert seq % TOKEN_TILE == 0 and n_tokens % COMBINE_TILE == 0
    assert n_tokens & (n_tokens - 1) == 0, "slot -> token uses a power-of-two mask"
    weights = _prepare_mixer_weights(
        ln0_g, ln0_b, w_in[0], conv_w[0], conv_b[0], w_q[0], w_k[0], b_if[0], mh_gain[0], w_pool[0],
        b_pool[0], ls_pool[0], w_branch_pool[0], w_branch_mlstm[0], w_out[0], ln1_g[0], ln1_b[0],
        w_router[0], b_router[0])
    h1_2d, idx8, rank8, gcol, cnt = _mixer_call(x, weights)
    counts = cnt[:, 0].astype(jnp.int32)
    plan = _route(idx8[:TOP_K], rank8[:TOP_K], counts, n_tokens)
    y_slots = _experts_call(
        plan, h1_2d.reshape(n_tokens, ROW_TILES, LANES), w_e1[0], b_e1[0], w_e2[0], b_e2[0], n_tokens)
    out = _combine_call(h1_2d, y_slots.reshape(-1, LANES), gcol, ln2_g[0].reshape(1, D_MODEL),
                        ln2_b[0].reshape(1, D_MODEL), n_tokens)
    return out.reshape(bsz, seq, D_MODEL)
```

```python
import functools

import jax
import jax.numpy as jnp
from jax import lax
from jax.experimental import pallas as pl
from jax.experimental.pallas import tpu as pltpu

F32 = jnp.float32
BF16 = jnp.bfloat16

D_MODEL = 1024
N_HEADS = 4
HEAD_DIM = 256
POOL_WIDTH = 512
POOL_GROUP = 128
POOL_WINDOWS = (2, 4, 8, 16)
CONV_WIDTH = 4
N_EXPERTS = 32
TOP_K = 4
D_FF = 1024
SWIGLU_ALPHA = 1.702
SWIGLU_LIMIT = 7.0
LN_EPS = 1e-5
DEEPNORM_ALPHA = 2.0 ** 0.25

SUBLANES = 8
LANES = 128
ROW_TILES = D_MODEL // LANES

COL_P = 0
COL_U = COL_P + POOL_WIDTH
COL_V = COL_U + D_MODEL
COL_O = COL_V + D_MODEL
COL_G = COL_O + D_MODEL
COL_IF = COL_G + 2 * D_MODEL
IN_COLS_PADDED = COL_IF + LANES

TOKEN_TILE = 256
POOL_HALO = 16
CONV_HALO = 8
EXPERT_ROWS = 256
ROW_DMA_GROUPS = 8
COMBINE_TILE = 256
DMA_PRIORITIES = 2
VMEM_LIMIT = 56 * 1024 * 1024

assert TOKEN_TILE <= EXPERT_ROWS
assert (TOP_K * TOKEN_TILE) % EXPERT_ROWS == 0


def _layer_norm(x, g, b):
    mu = jnp.mean(x, axis=-1, keepdims=True)
    xc = x - mu
    var = jnp.mean(xc * xc, axis=-1, keepdims=True)
    return xc * lax.rsqrt(var + LN_EPS) * g + b


def _sigmoid(x):
    return 0.5 * jnp.tanh(0.5 * x) + 0.5


def _log_sigmoid(x):
    return -(jnp.maximum(-x, 0.0) + jnp.log(1.0 + jnp.exp(-jnp.abs(x))))


def _split3(x):
    hi = x.astype(BF16)
    r1 = x - hi.astype(F32)
    mid = r1.astype(BF16)
    lo = (r1 - mid.astype(F32)).astype(BF16)
    return hi, mid, lo


def _dot(a, b):
    return jnp.dot(a, b, preferred_element_type=F32)


def _dot_nt(a, b):
    return lax.dot_general(a, b, (((1,), (1,)), ((), ())), preferred_element_type=F32)


def _tile_lanes(row, n):
    return jnp.concatenate([row] * n, axis=1)


def _mixer_kernel(x_ref, ln0g_ref, ln0b_ref, win_ref, bif_ref, convw_ref, convb_ref, wqk_ref,
                  gain_ref, wpool_ref, bpool_ref, lspool_ref, wbp_ref, wbm_ref, wout_ref,
                  ln1g_ref, ln1b_ref, wr_ref, br_ref,
                  h1_hbm, xp_hbm, pos_ref, gcol_ref, cnt_ref, pe_ref,
                  pext_ref, uext_ref, ct_ref, n_ref, m_ref, run_ref, page_ref, free_ref,
                  hx_ref, posv_ref, poss_ref, zb_ref, stv_ref, sts_ref, psem, rsem, hsem, zsem,
                  *, n_pages):
    tm = TOKEN_TILE
    tile_rows = tm * ROW_TILES
    b = pl.program_id(0)
    s = pl.program_id(1)
    n_s = pl.num_programs(1)
    g = b * n_s + s
    last = pl.num_programs(0) * n_s - 1
    par = g % 2
    q = 1 - par
    dyn_zero = lax.shift_right_arithmetic(g, 31)

    def hx_tile(p):
        return hx_ref.at[p, pl.ds(0, tile_rows)]

    def row_copy(p, t, k):
        return pltpu.make_async_copy(
            hx_ref.at[p, pl.ds(t * ROW_TILES, ROW_TILES)], xp_hbm.at[poss_ref[p, k, t]], rsem.at[p])

    def wait_row_copies(p):
        for _ in range(TOP_K):
            pltpu.make_async_copy(hx_tile(p), hx_tile(p), rsem.at[p]).wait()

    def h1_write(p, tile_index):
        off = pl.multiple_of(tile_index * tile_rows, tile_rows)
        return pltpu.make_async_copy(hx_tile(p), h1_hbm.at[pl.ds(off, tile_rows)], hsem.at[p])

    def pos_to_smem(p):
        return pltpu.make_async_copy(posv_ref.at[p], poss_ref.at[p], psem.at[p])

    @pl.when(s == 0)
    def _():
        pext_ref[0:POOL_HALO, :] = jnp.zeros((POOL_HALO, POOL_WIDTH), F32)
        uext_ref[0:CONV_HALO, :] = jnp.zeros((CONV_HALO, D_MODEL), F32)
        ct_ref[...] = jnp.zeros_like(ct_ref)
        n_ref[...] = jnp.zeros_like(n_ref)
        m_ref[...] = jnp.zeros_like(m_ref)

    @pl.when(g == 0)
    def _():
        run_ref[...] = jnp.zeros_like(run_ref)
        page_ref[...] = jnp.zeros_like(page_ref)
        free_ref[...] = jnp.zeros_like(free_ref)
        pe_ref[...] = jnp.zeros_like(pe_ref)
        zb_ref[...] = jnp.zeros_like(zb_ref)
        hx_ref[1] = jnp.zeros((tile_rows + SUBLANES, LANES), F32)
        for k in range(TOP_K):
            for t in range(tm):
                poss_ref[1, k, t] = n_pages * EXPERT_ROWS + k * tm + t

    @pl.when(g >= 1)
    def _():
        pos_to_smem(q).wait()
        wait_row_copies(par)

    @pl.when(g >= 2)
    def _():
        h1_write(par, 0).wait()

    group = tm // ROW_DMA_GROUPS

    def start_row_copies(grp):
        for t in range(grp * group, (grp + 1) * group):
            for k in range(TOP_K):
                row_copy(q, t, k).start(priority=k % DMA_PRIORITIES)
        spare = pl.multiple_of(tile_rows + dyn_zero * SUBLANES, SUBLANES)
        hx_ref[q, pl.ds(spare, SUBLANES), :] = jnp.full((SUBLANES, LANES), dyn_zero.astype(F32))
        return hx_ref[q, pl.ds(spare, SUBLANES), :][0:1, :]

    ln0b = ln0b_ref[...] + _tile_lanes(start_row_copies(0), ROW_TILES)
    h0 = _layer_norm(x_ref[...], ln0g_ref[...], ln0b)
    h0b = h0.astype(BF16)

    def proj(lo, hi):
        return _dot(h0b, win_ref[:, lo:hi])

    pext_ref[POOL_HALO:POOL_HALO + tm, :] = proj(COL_P, COL_U)
    tpos = s * tm + lax.broadcasted_iota(jnp.int32, (tm, 1), 0)
    groups = []
    for gi, w in enumerate(POOL_WINDOWS):
        c0 = gi * POOL_GROUP
        cur = pext_ref[POOL_HALO:POOL_HALO + tm, c0:c0 + POOL_GROUP]
        acc = cur
        for j in range(1, w):
            acc = acc + pext_ref[POOL_HALO - j:POOL_HALO - j + tm, c0:c0 + POOL_GROUP]
        inv_cnt = 1.0 / jnp.minimum(tpos + 1, w).astype(F32)
        groups.append(acc * inv_cnt - cur)
    pooled = jnp.concatenate(groups, axis=1)
    pext_ref[0:POOL_HALO, :] = pext_ref[tm:tm + POOL_HALO, :]
    bpool = bpool_ref[...] + _tile_lanes(start_row_copies(1), POOL_WIDTH // LANES)
    mixed = (_dot(pooled.astype(BF16), wpool_ref[...]) + bpool) * lspool_ref[...]
    y_pool = _dot(mixed.astype(BF16), wbp_ref[...])

    uext_ref[CONV_HALO:CONV_HALO + tm, :] = proj(COL_U, COL_V)
    conv = convb_ref[...] + _tile_lanes(start_row_copies(2), ROW_TILES)
    for j in range(CONV_WIDTH):
        off = CONV_HALO - (CONV_WIDTH - 1) + j
        conv = conv + convw_ref[j:j + 1, :] * uext_ref[off:off + tm, :]
    uext_ref[0:CONV_HALO, :] = uext_ref[tm:tm + CONV_HALO, :]
    ucb = (conv * _sigmoid(conv)).astype(BF16)
    vb = proj(COL_V, COL_O).astype(BF16)

    slab = proj(COL_IF, IN_COLS_PADDED) + (bif_ref[...] + start_row_copies(3))
    lane = lax.broadcasted_iota(jnp.int32, (tm, LANES), 1)
    is_f = jnp.logical_and(lane >= N_HEADS, lane < 2 * N_HEADS)
    slab = jnp.where(is_f, _log_sigmoid(slab), slab)
    row_i = lax.broadcasted_iota(jnp.int32, (tm, tm), 0)
    col_i = lax.broadcasted_iota(jnp.int32, (tm, tm), 1)
    causal = row_i >= col_i
    tri = jnp.where(causal, 1.0, 0.0).astype(BF16)
    hi, mid, lo = _split3(slab)
    bcol = _dot(tri, hi) + _dot(tri, mid) + _dot(tri, lo)
    slab_t = slab.T
    bcol_t = bcol.T

    heads = []
    for h in range(N_HEADS):
        hs = slice(h * HEAD_DIM, (h + 1) * HEAD_DIM)
        qk = _dot(ucb[:, hs], wqk_ref[h])
        q_h = qk[:, :HEAD_DIM]
        k_h = qk[:, HEAD_DIM:]
        qb = q_h.astype(BF16)
        kb = k_h.astype(BF16)
        vh = vb[:, hs]

        i_c = slab[:, h:h + 1]
        b_c = bcol[:, N_HEADS + h:N_HEADS + h + 1]
        i_r = slab_t[h:h + 1, :]
        b_r = bcol_t[N_HEADS + h:N_HEADS + h + 1, :]
        m_prev = m_ref[:, h:h + 1] + start_row_copies(4 + h)[:, 0:1]

        d_log = jnp.where(causal, b_c - (b_r - i_r), -jnp.inf)
        m_inter = b_c + m_prev
        m_t = jnp.maximum(m_inter, jnp.max(d_log, axis=1, keepdims=True))
        w_intra = jnp.exp(d_log - m_t)
        sc = _dot_nt(qb, kb) * w_intra
        w_inter = jnp.exp(m_inter - m_t)
        ctb = ct_ref[h].astype(BF16)
        num = _dot(sc.astype(BF16), vh) + w_inter * _dot(qb, ctb)
        qn = jnp.sum(q_h * n_ref[h], axis=1, keepdims=True)
        den = jnp.sum(sc, axis=1, keepdims=True) + w_inter * qn
        hh = num * (1.0 / jnp.maximum(jnp.abs(den), jnp.exp(-m_t)))
        mu = jnp.mean(hh, axis=1, keepdims=True)
        hc = hh - mu
        var = jnp.mean(hc * hc, axis=1, keepdims=True)
        heads.append(hc * lax.rsqrt(var + LN_EPS))

        g_last = b_r[:, tm - 1:tm]
        m_new = jnp.maximum(g_last + m_prev, jnp.max(g_last - b_r + i_r, axis=1, keepdims=True))
        decay = jnp.exp(g_last + m_prev - m_new)
        w_state = jnp.exp(g_last - b_c + i_c - m_new)
        kw = k_h * w_state
        ct_ref[h] = decay * ct_ref[h] + _dot(kw.T.astype(BF16), vh)
        n_ref[h] = decay * n_ref[h] + jnp.sum(kw, axis=0, keepdims=True)
        m_ref[:, h:h + 1] = m_new

    hn = jnp.concatenate(heads, axis=1) * gain_ref[...]
    h_out = _sigmoid(proj(COL_O, COL_G)) * hn
    y_mlstm = _dot(h_out.astype(BF16), wbm_ref[...])

    merged = (_sigmoid(proj(COL_G, COL_G + D_MODEL)) * y_pool
              + _sigmoid(proj(COL_G + D_MODEL, COL_IF)) * y_mlstm)
    mix = _dot(merged.astype(BF16), wout_ref[...])
    h1 = _layer_norm(DEEPNORM_ALPHA * h0 + mix, ln1g_ref[...], ln1b_ref[...])
    for c in range(ROW_TILES):
        hx_ref[par, pl.ds(c, tm, stride=ROW_TILES), :] = h1[:, c * LANES:(c + 1) * LANES]
    h1_write(par, g).start()

    h1_hi = h1.astype(BF16)
    h1_lo = (h1 - h1_hi.astype(F32)).astype(BF16)
    la = _dot_nt(wr_ref[...], h1_hi)
    lb = _dot_nt(wr_ref[0:N_EXPERTS, :], h1_lo)
    logits = la[0:N_EXPERTS] + la[N_EXPERTS:2 * N_EXPERTS] + lb + br_ref[...]
    e_iota = lax.broadcasted_iota(jnp.int32, (N_EXPERTS, tm), 0)
    vals, onehots = [], []
    lg = logits
    for _ in range(TOP_K):
        mx = jnp.max(lg, axis=0, keepdims=True)
        sel = jnp.min(jnp.where(lg == mx, e_iota, N_EXPERTS), axis=0, keepdims=True)
        oh = e_iota == sel
        lg = jnp.where(oh, -jnp.inf, lg)
        vals.append(mx)
        onehots.append(oh)
    exps = [jnp.exp(v - vals[0]) for v in vals]
    inv_den = 1.0 / (exps[0] + exps[1] + exps[2] + exps[3])
    gates = [e * inv_den for e in exps]

    oh_all = jnp.where(onehots[0], 1.0, 0.0)
    for oh in onehots[1:]:
        oh_all = oh_all + jnp.where(oh, 1.0, 0.0)
    strict = jnp.where(row_i < col_i, 1.0, 0.0).astype(BF16)
    run = run_ref[:, 0:1]
    rank = _dot(oh_all.astype(BF16), strict) + run
    count = jnp.sum(oh_all, axis=1, keepdims=True)
    inv_rows = 1.0 / EXPERT_ROWS
    pages_before = jnp.ceil(run * inv_rows)
    need = jnp.ceil((run + count) * inv_rows) - pages_before
    er = lax.broadcasted_iota(jnp.int32, (N_EXPERTS, N_EXPERTS), 0)
    ec = lax.broadcasted_iota(jnp.int32, (N_EXPERTS, N_EXPERTS), 1)
    earlier = jnp.where(er > ec, 1.0, 0.0).astype(BF16)
    need_b = jnp.broadcast_to(need, (N_EXPERTS, LANES)).astype(BF16)
    new_page = free_ref[0:1, 0:1] + _dot(earlier, need_b)[:, 0:1]
    page_seq = jnp.floor(rank * inv_rows)
    page = jnp.where(page_seq < pages_before, page_ref[:, 0:1], new_page)
    pos_all = page * EXPERT_ROWS + (rank - page_seq * EXPERT_ROWS)
    positions = [jnp.sum(jnp.where(oh, pos_all, 0.0), axis=0, keepdims=True) for oh in onehots]
    page_ref[...] = jnp.where(need > 0.0, new_page, page_ref[...])
    free_ref[...] = free_ref[...] + jnp.sum(need, axis=0, keepdims=True)
    run_ref[...] = run_ref[...] + count
    cnt_ref[...] = run_ref[...]
    p_lane = lax.broadcasted_iota(jnp.int32, (N_EXPERTS, pe_ref.shape[1]), 1).astype(F32)
    e_plus1 = (lax.broadcasted_iota(jnp.int32, (N_EXPERTS, 1), 0) + 1).astype(F32)
    taken = jnp.logical_and(p_lane == new_page, need > 0.0)
    pe_ref[...] = pe_ref[...] + jnp.sum(jnp.where(taken, e_plus1, 0.0), axis=0, keepdims=True)

    r8 = lax.broadcasted_iota(jnp.int32, (SUBLANES, tm), 0)
    pos_out = jnp.zeros((SUBLANES, tm), jnp.int32)
    r128 = lax.broadcasted_iota(jnp.int32, (LANES, tm), 0)
    gate_rows = jnp.zeros((LANES, tm), F32)
    for kk in range(TOP_K):
        pos_out = jnp.where(r8 == kk, positions[kk].astype(jnp.int32), pos_out)
        gate_rows = jnp.where(r128 == kk, gates[kk], gate_rows)
    pos_ref[...] = pos_out
    gcol_ref[...] = gate_rows.T
    posv_ref[par] = pos_out
    pos_to_smem(par).start()

    @pl.when(g == last)
    def _():
        pos_to_smem(par).wait()
        for t in range(tm):
            for k in range(TOP_K):
                row_copy(par, t, k).start(priority=k % DMA_PRIORITIES)
        wait_row_copies(q)
        wait_row_copies(par)
        h1_write(q, 0).wait()
        h1_write(par, 0).wait()

        st_lane = lax.broadcasted_iota(jnp.int32, (N_EXPERTS, LANES), 1)
        state = jnp.where(st_lane == 0, page_ref[...], jnp.where(st_lane == 1, run_ref[...], free_ref[...]))
        stv_ref[...] = state.astype(jnp.int32)
        state_copy = pltpu.make_async_copy(stv_ref, sts_ref, psem.at[par])
        state_copy.start()
        state_copy.wait()

        def zero_fill(wait):
            def act(rows_dst, n):
                cp = pltpu.make_async_copy(zb_ref.at[pl.ds(0, n)], xp_hbm.at[pl.ds(rows_dst, n)], zsem)
                if wait:
                    cp.wait()
                else:
                    cp.start()

            def unused_page(p, carry):
                act(pl.multiple_of(p * EXPERT_ROWS, EXPERT_ROWS), EXPERT_ROWS)
                return carry

            lax.fori_loop(sts_ref[0, 2], n_pages, unused_page, 0)

            def page_tail(e, carry):
                filled = sts_ref[e, 1] & (EXPERT_ROWS - 1)
                pad = jnp.where(filled == 0, 0, EXPERT_ROWS - filled)
                dst = sts_ref[e, 0] * EXPERT_ROWS + filled
                size = 1
                while size < EXPERT_ROWS:
                    @pl.when((pad & size) != 0)
                    def _(dst=dst, size=size):
                        act(dst, size)
                    dst = dst + (pad & size)
                    size *= 2
                return carry

            lax.fori_loop(0, N_EXPERTS, page_tail, 0)

        zero_fill(wait=False)
        zero_fill(wait=True)


def _const_spec(shape):
    zeros = (0,) * len(shape)
    return pl.BlockSpec(shape, lambda b, s: zeros, pipeline_mode=pl.Buffered(1))


def _mixer_call(x, weights, n_pages):
    bsz, seq, _ = x.shape
    tm = TOKEN_TILE
    n_s = seq // tm
    t_total = bsz * seq
    assert bsz * n_s >= 2
    tile = lambda b, s: (b * n_s + s)
    spare_pages = TOP_K * tm // EXPERT_ROWS
    pe_lanes = -(-n_pages // LANES) * LANES
    in_specs = [pl.BlockSpec((None, tm, D_MODEL), lambda b, s: (b, s, 0))]
    in_specs += [_const_spec(w.shape) for w in weights]
    out_shape = (
        jax.ShapeDtypeStruct((t_total * ROW_TILES, LANES), F32),
        jax.ShapeDtypeStruct(((n_pages + spare_pages) * EXPERT_ROWS, ROW_TILES, LANES), F32),
        jax.ShapeDtypeStruct((SUBLANES, t_total), jnp.int32),
        jax.ShapeDtypeStruct((t_total, LANES), F32),
        jax.ShapeDtypeStruct((N_EXPERTS, LANES), F32),
        jax.ShapeDtypeStruct((SUBLANES, pe_lanes), F32),
    )
    out_specs = (
        pl.BlockSpec(memory_space=pl.ANY),
        pl.BlockSpec(memory_space=pl.ANY),
        pl.BlockSpec((SUBLANES, tm), lambda b, s: (0, tile(b, s))),
        pl.BlockSpec((tm, LANES), lambda b, s: (tile(b, s), 0)),
        pl.BlockSpec((N_EXPERTS, LANES), lambda b, s: (0, 0)),
        pl.BlockSpec((SUBLANES, pe_lanes), lambda b, s: (0, 0)),
    )
    scratch = [
        pltpu.VMEM((POOL_HALO + tm, POOL_WIDTH), F32),
        pltpu.VMEM((CONV_HALO + tm, D_MODEL), F32),
        pltpu.VMEM((N_HEADS, HEAD_DIM, HEAD_DIM), F32),
        pltpu.VMEM((N_HEADS, 1, HEAD_DIM), F32),
        pltpu.VMEM((1, LANES), F32),
        pltpu.VMEM((N_EXPERTS, LANES), F32),
        pltpu.VMEM((N_EXPERTS, LANES), F32),
        pltpu.VMEM((1, LANES), F32),
        pltpu.VMEM((2, tm * ROW_TILES + SUBLANES, LANES), F32),
        pltpu.VMEM((2, SUBLANES, tm), jnp.int32),
        pltpu.SMEM((2, SUBLANES, tm), jnp.int32),
        pltpu.VMEM((EXPERT_ROWS, ROW_TILES, LANES), F32),
        pltpu.VMEM((N_EXPERTS, LANES), jnp.int32),
        pltpu.SMEM((N_EXPERTS, LANES), jnp.int32),
        pltpu.SemaphoreType.DMA((2,)),
        pltpu.SemaphoreType.DMA((2,)),
        pltpu.SemaphoreType.DMA((2,)),
        pltpu.SemaphoreType.DMA(()),
    ]
    return pl.pallas_call(
        functools.partial(_mixer_kernel, n_pages=n_pages),
        grid=(bsz, n_s),
        in_specs=in_specs,
        out_specs=out_specs,
        out_shape=out_shape,
        scratch_shapes=scratch,
        compiler_params=pltpu.CompilerParams(
            dimension_semantics=("arbitrary", "arbitrary"), vmem_limit_bytes=VMEM_LIMIT),
        name="mixer",
    )(x, *weights)


def _experts_kernel(xpage_ref, spage_ref, be_ref, first_ref, nexte_ref, nused_ref,
                    slot_p_ref, slot_a_ref, slot_b_ref, xa_ref, xb_ref,
                    w1_hbm, b1_ref, w2_hbm, b2_ref, y_hbm,
                    yb0_ref, yb1_ref, w1s_ref, w2s_ref, w1b_ref, w2z_ref, w2b_ref, ssem, wsem):
    del xpage_ref, spage_ref
    rows = EXPERT_ROWS
    tile = rows * ROW_TILES
    i = pl.program_id(0)
    last = pl.num_programs(0) - 1
    n_used = nused_ref[0]
    yb = (yb0_ref, yb1_ref)
    x_in = (xa_ref, xb_ref)
    dyn_zero = lax.shift_right_arithmetic(n_used, 31)
    n_chunks = D_FF // (2 * LANES)
    n_stages = n_chunks + 1

    def scatter(slots, r, p):
        return pltpu.make_async_copy(
            yb[p].at[pl.ds(r * ROW_TILES, ROW_TILES)], y_hbm.at[slots[0, 0, r]], ssem.at[p])

    def start_scatters(stage, slots, p):
        r_range = range(rows) if stage is None else range(
            rows * stage // n_stages, rows * (stage + 1) // n_stages)
        for r in r_range:
            scatter(slots, r, p).start(priority=r % DMA_PRIORITIES)
        spare = pl.multiple_of(tile + dyn_zero * SUBLANES, SUBLANES)
        yb[p][pl.ds(spare, SUBLANES), :] = jnp.full((SUBLANES, LANES), dyn_zero.astype(F32))
        return yb[p][pl.ds(spare, SUBLANES), :][0:1, :]

    def wait_scatters(p):
        view = yb[p].at[pl.ds(0, tile)]
        pltpu.make_async_copy(view, view, ssem.at[p]).wait()

    def weight_copies(e):
        return (pltpu.make_async_copy(w1_hbm.at[e], w1s_ref, wsem.at[0]),
                pltpu.make_async_copy(w2_hbm.at[e], w2s_ref, wsem.at[1]))

    def switch_weights(blk):
        @pl.when(first_ref[blk] == 1)
        def _():
            for cp in weight_copies(0):
                cp.wait()
            step = 128
            for c in range(D_MODEL // step):
                w1b_ref[c * step:(c + 1) * step, :] = w1s_ref[c * step:(c + 1) * step, :].astype(BF16)
            half = D_FF // 2
            for c in range(ROW_TILES):
                cs = slice(c * LANES, (c + 1) * LANES)
                w2z_ref[c, pl.ds(0, half, stride=2), :] = w2s_ref[0:half, cs]
                w2z_ref[c, pl.ds(1, half, stride=2), :] = w2s_ref[half:D_FF, cs]
                w2b_ref[:, cs] = w2z_ref[c].astype(BF16)
            nxt = nexte_ref[blk]

            @pl.when(nxt >= 0)
            def _():
                for cp in weight_copies(nxt):
                    cp.start()

    def compute(p, blk, start_stage):
        e = be_ref[blk]
        x = jnp.concatenate(
            [x_in[p][pl.ds(c, rows, stride=ROW_TILES), :] for c in range(ROW_TILES)], axis=1).astype(BF16)
        b1 = b1_ref[pl.ds(e, 1), :]
        width = 2 * LANES
        even = (lax.broadcasted_iota(jnp.int32, (rows, width), 1) & 1) == 0
        zs = []
        for c in range(n_chunks):
            lo = c * width
            hi = D_FF + lo
            zero = _tile_lanes(start_stage(c), 2)
            ha = _dot(x, w1b_ref[:, lo:lo + width]) + (b1[:, lo:lo + width] + zero)
            hb = _dot(x, w1b_ref[:, hi:hi + width]) + (b1[:, hi:hi + width] + zero)
            glu = jnp.where(even, ha, pltpu.roll(hb, 1, 1))
            lin = jnp.where(even, pltpu.roll(ha, width - 1, 1), hb)
            glu = jnp.minimum(glu, SWIGLU_LIMIT)
            lin = jnp.clip(lin, -SWIGLU_LIMIT, SWIGLU_LIMIT) + 1.0
            zs.append((glu * _sigmoid(SWIGLU_ALPHA * glu) * lin).astype(BF16))
        b2 = b2_ref[pl.ds(e, 1), :] + _tile_lanes(start_stage(n_chunks), ROW_TILES)
        y = _dot(jnp.concatenate(zs, axis=1), w2b_ref[...]) + b2
        for c in range(ROW_TILES):
            yb[p][pl.ds(c, rows, stride=ROW_TILES), :] = y[:, c * LANES:(c + 1) * LANES]

    def phase(blk, p, start_stage):
        switch_weights(blk)

        @pl.when(blk < n_used)
        def _():
            compute(p, blk, start_stage)

        @pl.when(blk >= n_used)
        def _():
            start_stage(None)

    @pl.when(i == 0)
    def _():
        yb1_ref[...] = jnp.zeros_like(yb1_ref)
        for cp in weight_copies(be_ref[0]):
            cp.start()

    @pl.when(i >= 1)
    def _():
        wait_scatters(0)

    phase(2 * i, 0, functools.partial(start_scatters, slots=slot_p_ref, p=1))

    wait_scatters(1)
    phase(2 * i + 1, 1, functools.partial(start_scatters, slots=slot_a_ref, p=0))

    @pl.when(i == last)
    def _():
        wait_scatters(0)
        for r in range(rows):
            scatter(slot_b_ref, r, 1).start(priority=r % DMA_PRIORITIES)
        wait_scatters(1)


def _experts_call(plan, row_slot, x_pages_2d, w_e1, b_e1, w_e2, b_e2, n_tokens):
    xpage, spage, block_e, first, next_e, n_used = plan
    rows = EXPERT_ROWS
    n_blocks = block_e.shape[0]
    assert n_blocks % 2 == 0
    n_slots = TOP_K * n_tokens + rows
    tile_rows = rows * ROW_TILES
    filler = row_slot.shape[0] - 1
    smem_rows = lambda index: pl.BlockSpec((1, 1, rows), index, memory_space=pltpu.SMEM)
    whole = lambda shape: pl.BlockSpec(shape, lambda i, *_: (0,) * len(shape))
    grid_spec = pltpu.PrefetchScalarGridSpec(
        num_scalar_prefetch=6,
        grid=(n_blocks // 2,),
        in_specs=[
            smem_rows(lambda i, xp, sp, *_: (jnp.where(i == 0, filler, sp[jnp.maximum(2 * i - 1, 0)]), 0, 0)),
            smem_rows(lambda i, xp, sp, *_: (sp[2 * i], 0, 0)),
            smem_rows(lambda i, xp, sp, *_: (sp[2 * i + 1], 0, 0)),
            pl.BlockSpec((tile_rows, LANES), lambda i, xp, *_: (xp[2 * i], 0)),
            pl.BlockSpec((tile_rows, LANES), lambda i, xp, *_: (xp[2 * i + 1], 0)),
            pl.BlockSpec(memory_space=pl.ANY),
            whole((N_EXPERTS, 2 * D_FF)),
            pl.BlockSpec(memory_space=pl.ANY),
            whole((N_EXPERTS, D_MODEL)),
        ],
        out_specs=pl.BlockSpec(memory_space=pl.ANY),
        scratch_shapes=[
            pltpu.VMEM((tile_rows + SUBLANES, LANES), F32),
            pltpu.VMEM((tile_rows + SUBLANES, LANES), F32),
            pltpu.VMEM((D_MODEL, 2 * D_FF), F32),
            pltpu.VMEM((D_FF, D_MODEL), F32),
            pltpu.VMEM((D_MODEL, 2 * D_FF), BF16),
            pltpu.VMEM((ROW_TILES, D_FF, LANES), F32),
            pltpu.VMEM((D_FF, D_MODEL), BF16),
            pltpu.SemaphoreType.DMA((2,)),
            pltpu.SemaphoreType.DMA((2,)),
        ],
    )
    return pl.pallas_call(
        _experts_kernel,
        grid_spec=grid_spec,
        out_shape=jax.ShapeDtypeStruct((n_slots, ROW_TILES, LANES), F32),
        compiler_params=pltpu.CompilerParams(
            dimension_semantics=("arbitrary",), vmem_limit_bytes=VMEM_LIMIT),
        name="experts",
    )(xpage, spage, block_e, first, next_e, n_used,
      row_slot, row_slot, row_slot, x_pages_2d, x_pages_2d, w_e1, b_e1, w_e2, b_e2)


def _combine_kernel(h1_ref, y0_ref, y1_ref, y2_ref, y3_ref, gcol_ref, g_ref, b_ref, out_ref):
    tc = COMBINE_TILE
    gcol = gcol_ref[...]
    y_refs = (y0_ref, y1_ref, y2_ref, y3_ref)
    chunks = []
    for c in range(ROW_TILES):
        z = DEEPNORM_ALPHA * h1_ref[pl.ds(c, tc, stride=ROW_TILES), :]
        for kk in range(TOP_K):
            z = z + gcol[:, kk:kk + 1] * y_refs[kk][pl.ds(c, tc, stride=ROW_TILES), :]
        chunks.append(z)
    total = chunks[0].sum(axis=1, keepdims=True)
    for z in chunks[1:]:
        total = total + z.sum(axis=1, keepdims=True)
    mu = total * (1.0 / D_MODEL)
    sq = None
    for z in chunks:
        zc = z - mu
        part = (zc * zc).sum(axis=1, keepdims=True)
        sq = part if sq is None else sq + part
    inv = lax.rsqrt(sq * (1.0 / D_MODEL) + LN_EPS)
    for c, z in enumerate(chunks):
        cs = slice(c * LANES, (c + 1) * LANES)
        out_ref[:, cs] = (z - mu) * inv * g_ref[:, cs] + b_ref[:, cs]


def _combine_call(h1_2d, y_2d, gcol, ln_g, ln_b, n_tokens):
    tc = COMBINE_TILE
    n_t = n_tokens // tc
    blk = tc * ROW_TILES
    y_spec = lambda kk: pl.BlockSpec((blk, LANES), lambda i: (kk * n_t + i, 0))
    return pl.pallas_call(
        _combine_kernel,
        grid=(n_t,),
        in_specs=[
            pl.BlockSpec((blk, LANES), lambda i: (i, 0)),
            y_spec(0), y_spec(1), y_spec(2), y_spec(3),
            pl.BlockSpec((tc, LANES), lambda i: (i, 0)),
            pl.BlockSpec((1, D_MODEL), lambda i: (0, 0)),
            pl.BlockSpec((1, D_MODEL), lambda i: (0, 0)),
        ],
        out_specs=pl.BlockSpec((tc, D_MODEL), lambda i: (i, 0)),
        out_shape=jax.ShapeDtypeStruct((n_tokens, D_MODEL), F32),
        compiler_params=pltpu.CompilerParams(
            dimension_semantics=("arbitrary",), vmem_limit_bytes=VMEM_LIMIT),
        name="combine",
    )(h1_2d, y_2d, y_2d, y_2d, y_2d, gcol, ln_g, ln_b)


def _prepare_mixer_weights(ln0_g, ln0_b, w_in, conv_w, conv_b, w_q, w_k, b_if, mh_gain, w_pool,
                           b_pool, ls_pool, w_branch_pool, w_branch_mlstm, w_out, ln1_g, ln1_b,
                           w_router, b_router):
    s_p = POOL_WIDTH
    s_o = s_p + 3 * D_MODEL
    s_if = s_o + 2 * N_HEADS
    pad = jnp.zeros((D_MODEL, LANES - 2 * N_HEADS), w_in.dtype)
    w_in_r = jnp.concatenate(
        [w_in[:, :s_o], w_in[:, s_if:], w_in[:, s_o:s_if], pad], axis=1).astype(BF16)
    bif = jnp.concatenate([b_if, jnp.zeros((LANES - 2 * N_HEADS,), F32)]).reshape(1, LANES)
    wqk = jnp.concatenate([w_q * (HEAD_DIM ** -0.5), w_k], axis=-1).astype(BF16)
    wpool_bd = jax.scipy.linalg.block_diag(*[w_pool[g] for g in range(len(POOL_WINDOWS))]).astype(BF16)
    wr_t = w_router.T
    wr_hi = wr_t.astype(BF16)
    wr_lo = (wr_t - wr_hi.astype(F32)).astype(BF16)
    row = lambda v: v.reshape(1, -1)
    return (row(ln0_g), row(ln0_b), w_in_r, bif, conv_w, row(conv_b), wqk, row(mh_gain), wpool_bd,
            row(b_pool), row(ls_pool), w_branch_pool.astype(BF16), w_branch_mlstm.astype(BF16),
            w_out.astype(BF16), row(ln1_g), row(ln1_b), jnp.concatenate([wr_hi, wr_lo], axis=0),
            b_router.reshape(N_EXPERTS, 1))


def _plan(page_expert1, counts, pos, n_tokens, n_pages, n_rows_total):
    rows = EXPERT_ROWS
    n_assign = TOP_K * n_tokens
    i32 = jnp.int32
    slots = jnp.arange(n_assign, dtype=i32)
    hit = jnp.zeros((n_rows_total,), i32).at[pos.reshape(-1)].add(slots + 1)
    pad_slot = n_assign + jnp.arange(n_rows_total + rows, dtype=i32) % rows
    row_slot = jnp.where(jnp.concatenate([hit, jnp.zeros((rows,), i32)]) == 0, pad_slot,
                         jnp.concatenate([hit, jnp.zeros((rows,), i32)]) - 1)
    row_slot = row_slot.reshape(n_rows_total // rows + 1, 1, rows)
    filler = n_rows_total // rows

    pidx = jnp.arange(n_pages, dtype=i32)
    used = page_expert1 > 0
    n_used = jnp.sum(used.astype(i32))
    page_e = page_expert1 - 1
    key = jnp.where(used, page_e, N_EXPERTS) * n_pages + pidx
    place = jnp.sum((key[None, :] < key[:, None]).astype(i32), axis=1)
    at_block = place[None, :] == pidx[:, None]
    order = jnp.sum(jnp.where(at_block, pidx[None, :], 0), axis=1)
    block_e = jnp.clip(jnp.sum(jnp.where(at_block, page_e[None, :], 0), axis=1), 0, N_EXPERTS - 1)
    valid = pidx < n_used
    xpage = jnp.where(valid, order, 0)
    spage = jnp.where(valid, order, filler)
    prev_e = jnp.concatenate([jnp.full((1,), -1, i32), block_e[:-1]])
    first = jnp.logical_and(valid, block_e != prev_e).astype(i32)
    e_iota = jnp.arange(N_EXPERTS, dtype=i32)
    pages_per_e = (counts + rows - 1) // rows
    seg_end = jnp.cumsum(pages_per_e)
    mine = block_e[:, None] == e_iota[None, :]
    end_blk = jnp.sum(jnp.where(mine, seg_end[None, :], 0), axis=1)
    e_at_end = jnp.sum(jnp.where(pidx[None, :] == end_blk[:, None], block_e[None, :], 0), axis=1)
    next_e = jnp.where(end_blk < n_used, e_at_end, -1).astype(i32)
    plan = (xpage.astype(i32), spage.astype(i32), block_e.astype(i32), first, next_e,
            n_used.reshape(1).astype(i32))
    return plan, row_slot


def kernel(x, ln0_g, ln0_b, w_in, conv_w, conv_b, w_q, w_k, b_if, mh_gain, w_pool, b_pool, ls_pool,
           w_branch_pool, w_branch_mlstm, w_out, ln1_g, ln1_b, w_router, b_router, w_e1, b_e1,
           w_e2, b_e2, ln2_g, ln2_b):
    bsz, seq, _ = x.shape
    n_tokens = bsz * seq
    rows = EXPERT_ROWS
    assert w_in.shape[0] == 1, "single-layer trunk"
    assert seq % TOKEN_TILE == 0 and n_tokens % COMBINE_TILE == 0
    n_pages = (TOP_K * n_tokens + N_EXPERTS * (rows - 1) + rows - 1) // rows
    n_pages += n_pages % 2
    weights = _prepare_mixer_weights(
        ln0_g, ln0_b, w_in[0], conv_w[0], conv_b[0], w_q[0], w_k[0], b_if[0], mh_gain[0], w_pool[0],
        b_pool[0], ls_pool[0], w_branch_pool[0], w_branch_mlstm[0], w_out[0], ln1_g[0], ln1_b[0],
        w_router[0], b_router[0])
    h1_2d, x_pages, pos8, gcol, cnt, pe = _mixer_call(x, weights, n_pages)
    counts = cnt[:, 0].astype(jnp.int32)
    page_expert1 = pe[0, :n_pages].astype(jnp.int32)
    plan, row_slot = _plan(page_expert1, counts, pos8[:TOP_K], n_tokens, n_pages, x_pages.shape[0])
    y_slots = _experts_call(
        plan, row_slot, x_pages.reshape(-1, LANES), w_e1[0], b_e1[0], w_e2[0], b_e2[0], n_tokens)
    out = _combine_call(h1_2d, y_slots.reshape(-1, LANES), gcol, ln2_g[0].reshape(1, D_MODEL),
                        ln2_b[0].reshape(1, D_MODEL), n_tokens)
    return out.reshape(bsz, seq, D_MODEL)
```

```python
import functools

import jax
import jax.numpy as jnp
from jax import lax
from jax.experimental import pallas as pl
from jax.experimental.pallas import tpu as pltpu

F32 = jnp.float32
BF16 = jnp.bfloat16

D_MODEL = 1024
N_HEADS = 4
HEAD_DIM = 256
POOL_WIDTH = 512
POOL_GROUP = 128
POOL_WINDOWS = (2, 4, 8, 16)
CONV_WIDTH = 4
N_EXPERTS = 32
TOP_K = 4
D_FF = 1024
SWIGLU_ALPHA = 1.702
SWIGLU_LIMIT = 7.0
LN_EPS = 1e-5
DEEPNORM_ALPHA = 2.0 ** 0.25

SUBLANES = 8
LANES = 128
ROW_TILES = D_MODEL // LANES

COL_P = 0
COL_U = COL_P + POOL_WIDTH
COL_V = COL_U + D_MODEL
COL_O = COL_V + D_MODEL
COL_IF = COL_O + D_MODEL
COL_IF_END = COL_IF + LANES
COL_GATES = COL_IF + 2 * N_HEADS

TOKEN_TILE = 256
POOL_HALO = 16
CONV_HALO = 8
EXPERT_ROWS = 256
ROW_DMA_GROUPS = 8
SCATTER_STAGES = 3
COMBINE_TILE = 512
DMA_PRIORITIES = 2
VMEM_LIMIT = 56 * 1024 * 1024

assert TOKEN_TILE <= EXPERT_ROWS
assert (TOP_K * TOKEN_TILE) % EXPERT_ROWS == 0


def _layer_norm(x, g, b):
    mu = jnp.mean(x, axis=-1, keepdims=True)
    xc = x - mu
    var = jnp.mean(xc * xc, axis=-1, keepdims=True)
    return xc * lax.rsqrt(var + LN_EPS) * g + b


def _sigmoid(x):
    return 0.5 * jnp.tanh(0.5 * x) + 0.5


def _log_sigmoid(x):
    return -(jnp.maximum(-x, 0.0) + jnp.log(1.0 + jnp.exp(-jnp.abs(x))))


def _split3(x):
    hi = x.astype(BF16)
    r1 = x - hi.astype(F32)
    mid = r1.astype(BF16)
    lo = (r1 - mid.astype(F32)).astype(BF16)
    return hi, mid, lo


def _dot(a, b):
    return jnp.dot(a, b, preferred_element_type=F32)


def _dot_nt(a, b):
    return lax.dot_general(a, b, (((1,), (1,)), ((), ())), preferred_element_type=F32)


def _tile_lanes(row, n):
    return jnp.concatenate([row] * n, axis=1)


def _mixer_kernel(x_ref, ln0g_ref, ln0b_ref, win_ref, wgate_ref, bif_ref, convw_ref, convb_ref, wqk_ref,
                  gain_ref, wpool_ref, bpool_ref, lspool_ref, wbp_ref, wbm_ref, wout_ref,
                  ln1g_ref, ln1b_ref, wr_ref, br_ref,
                  h1_hbm, xp_hbm, pos_ref, gcol_ref, cnt_ref, pe_ref,
                  pext_ref, uext_ref, ct_ref, n_ref, m_ref, run_ref, page_ref, free_ref,
                  hx_ref, posv_ref, poss_ref, zb_ref, stv_ref, sts_ref, psem, rsem, hsem, zsem,
                  *, n_pages):
    tm = TOKEN_TILE
    tile_rows = tm * ROW_TILES
    b = pl.program_id(0)
    s = pl.program_id(1)
    n_s = pl.num_programs(1)
    g = b * n_s + s
    last = pl.num_programs(0) * n_s - 1
    par = g % 2
    q = 1 - par
    dyn_zero = lax.shift_right_arithmetic(g, 31)

    def hx_tile(p):
        return hx_ref.at[p, pl.ds(0, tile_rows)]

    def row_copy(p, t, k):
        return pltpu.make_async_copy(
            hx_ref.at[p, pl.ds(t * ROW_TILES, ROW_TILES)], xp_hbm.at[poss_ref[p, k, t]], rsem.at[p])

    def wait_row_copies(p):
        for _ in range(TOP_K):
            pltpu.make_async_copy(hx_tile(p), hx_tile(p), rsem.at[p]).wait()

    def h1_write(p, tile_index):
        off = pl.multiple_of(tile_index * tile_rows, tile_rows)
        return pltpu.make_async_copy(hx_tile(p), h1_hbm.at[pl.ds(off, tile_rows)], hsem.at[p])

    def pos_to_smem(p):
        return pltpu.make_async_copy(posv_ref.at[p], poss_ref.at[p], psem.at[p])

    @pl.when(s == 0)
    def _():
        pext_ref[0:POOL_HALO, :] = jnp.zeros((POOL_HALO, POOL_WIDTH), F32)
        uext_ref[0:CONV_HALO, :] = jnp.zeros((CONV_HALO, D_MODEL), F32)
        ct_ref[...] = jnp.zeros_like(ct_ref)
        n_ref[...] = jnp.zeros_like(n_ref)
        m_ref[...] = jnp.zeros_like(m_ref)

    @pl.when(g == 0)
    def _():
        run_ref[...] = jnp.zeros_like(run_ref)
        page_ref[...] = jnp.zeros_like(page_ref)
        free_ref[...] = jnp.zeros_like(free_ref)
        pe_ref[...] = jnp.zeros_like(pe_ref)
        zb_ref[...] = jnp.zeros_like(zb_ref)
        hx_ref[1] = jnp.zeros((tile_rows + SUBLANES, LANES), F32)
        for k in range(TOP_K):
            for t in range(tm):
                poss_ref[1, k, t] = n_pages * EXPERT_ROWS + k * tm + t

    @pl.when(g >= 1)
    def _():
        pos_to_smem(q).wait()
        wait_row_copies(par)

    @pl.when(g >= 2)
    def _():
        h1_write(par, 0).wait()

    group = tm // ROW_DMA_GROUPS

    def start_row_copies(grp):
        for t in range(grp * group, (grp + 1) * group):
            for k in range(TOP_K):
                row_copy(q, t, k).start(priority=k % DMA_PRIORITIES)
        spare = pl.multiple_of(tile_rows + dyn_zero * SUBLANES, SUBLANES)
        hx_ref[q, pl.ds(spare, SUBLANES), :] = jnp.full((SUBLANES, LANES), dyn_zero.astype(F32))
        return hx_ref[q, pl.ds(spare, SUBLANES), :][0:1, :]

    ln0b = ln0b_ref[...] + _tile_lanes(start_row_copies(0), ROW_TILES)
    h0 = _layer_norm(x_ref[...], ln0g_ref[...], ln0b)
    h0b = h0.astype(BF16)

    def proj(lo, hi):
        return _dot(h0b, win_ref[:, lo:hi])

    pext_ref[POOL_HALO:POOL_HALO + tm, :] = proj(COL_P, COL_U)
    tpos = s * tm + lax.broadcasted_iota(jnp.int32, (tm, 1), 0)
    groups = []
    for gi, w in enumerate(POOL_WINDOWS):
        c0 = gi * POOL_GROUP
        cur = pext_ref[POOL_HALO:POOL_HALO + tm, c0:c0 + POOL_GROUP]
        acc = cur
        for j in range(1, w):
            acc = acc + pext_ref[POOL_HALO - j:POOL_HALO - j + tm, c0:c0 + POOL_GROUP]
        inv_cnt = 1.0 / jnp.minimum(tpos + 1, w).astype(F32)
        groups.append(acc * inv_cnt - cur)
    pooled = jnp.concatenate(groups, axis=1)
    pext_ref[0:POOL_HALO, :] = pext_ref[tm:tm + POOL_HALO, :]
    bpool = bpool_ref[...] + _tile_lanes(start_row_copies(1), POOL_WIDTH // LANES)
    mixed = (_dot(pooled.astype(BF16), wpool_ref[...]) + bpool) * lspool_ref[...]
    y_pool = _dot(mixed.astype(BF16), wbp_ref[...])

    uext_ref[CONV_HALO:CONV_HALO + tm, :] = proj(COL_U, COL_V)
    conv = convb_ref[...] + _tile_lanes(start_row_copies(2), ROW_TILES)
    for j in range(CONV_WIDTH):
        off = CONV_HALO - (CONV_WIDTH - 1) + j
        conv = conv + convw_ref[j:j + 1, :] * uext_ref[off:off + tm, :]
    uext_ref[0:CONV_HALO, :] = uext_ref[tm:tm + CONV_HALO, :]
    ucb = (conv * _sigmoid(conv)).astype(BF16)
    vb = proj(COL_V, COL_O).astype(BF16)

    slab = proj(COL_IF, COL_IF_END) + (bif_ref[...] + start_row_copies(3))
    lane = lax.broadcasted_iota(jnp.int32, (tm, LANES), 1)
    is_f = jnp.logical_and(lane >= N_HEADS, lane < 2 * N_HEADS)
    slab = jnp.where(is_f, _log_sigmoid(slab), slab)
    row_i = lax.broadcasted_iota(jnp.int32, (tm, tm), 0)
    col_i = lax.broadcasted_iota(jnp.int32, (tm, tm), 1)
    causal = row_i >= col_i
    tri = jnp.where(causal, 1.0, 0.0).astype(BF16)
    hi, mid, lo = _split3(slab)
    bcol = _dot(tri, hi) + _dot(tri, mid) + _dot(tri, lo)
    slab_t = slab.T
    bcol_t = bcol.T

    heads = []
    for h in range(N_HEADS):
        hs = slice(h * HEAD_DIM, (h + 1) * HEAD_DIM)
        qk = _dot(ucb[:, hs], wqk_ref[h])
        q_h = qk[:, :HEAD_DIM]
        k_h = qk[:, HEAD_DIM:]
        qb = q_h.astype(BF16)
        kb = k_h.astype(BF16)
        vh = vb[:, hs]

        i_c = slab[:, h:h + 1]
        b_c = bcol[:, N_HEADS + h:N_HEADS + h + 1]
        i_r = slab_t[h:h + 1, :]
        b_r = bcol_t[N_HEADS + h:N_HEADS + h + 1, :]
        m_prev = m_ref[:, h:h + 1] + start_row_copies(4 + h)[:, 0:1]

        d_log = jnp.where(causal, b_c - (b_r - i_r), -jnp.inf)
        m_inter = b_c + m_prev
        m_t = jnp.maximum(m_inter, jnp.max(d_log, axis=1, keepdims=True))
        w_intra = jnp.exp(d_log - m_t)
        sc = _dot_nt(qb, kb) * w_intra
        w_inter = jnp.exp(m_inter - m_t)
        ctb = ct_ref[h].astype(BF16)
        num = _dot(sc.astype(BF16), vh) + w_inter * _dot(qb, ctb)
        qn = jnp.sum(q_h * n_ref[h], axis=1, keepdims=True)
        den = jnp.sum(sc, axis=1, keepdims=True) + w_inter * qn
        hh = num * (1.0 / jnp.maximum(jnp.abs(den), jnp.exp(-m_t)))
        mu = jnp.mean(hh, axis=1, keepdims=True)
        hc = hh - mu
        var = jnp.mean(hc * hc, axis=1, keepdims=True)
        heads.append(hc * lax.rsqrt(var + LN_EPS))

        g_last = b_r[:, tm - 1:tm]
        m_new = jnp.maximum(g_last + m_prev, jnp.max(g_last - b_r + i_r, axis=1, keepdims=True))
        decay = jnp.exp(g_last + m_prev - m_new)
        w_state = jnp.exp(g_last - b_c + i_c - m_new)
        kw = k_h * w_state
        ct_ref[h] = decay * ct_ref[h] + _dot(kw.T.astype(BF16), vh)
        n_ref[h] = decay * n_ref[h] + jnp.sum(kw, axis=0, keepdims=True)
        m_ref[:, h:h + 1] = m_new

    hn = jnp.concatenate(heads, axis=1) * gain_ref[...]
    h_out = _sigmoid(proj(COL_O, COL_IF)) * hn
    y_mlstm = _dot(h_out.astype(BF16), wbm_ref[...])

    merged = (_sigmoid(_dot(h0b, wgate_ref[:, 0:D_MODEL])) * y_pool
              + _sigmoid(_dot(h0b, wgate_ref[:, D_MODEL:2 * D_MODEL])) * y_mlstm)
    mix = _dot(merged.astype(BF16), wout_ref[...])
    h1 = _layer_norm(DEEPNORM_ALPHA * h0 + mix, ln1g_ref[...], ln1b_ref[...])
    for c in range(ROW_TILES):
        hx_ref[par, pl.ds(c, tm, stride=ROW_TILES), :] = h1[:, c * LANES:(c + 1) * LANES]
    h1_write(par, g).start()

    h1_hi = h1.astype(BF16)
    h1_lo = (h1 - h1_hi.astype(F32)).astype(BF16)
    la = _dot_nt(wr_ref[...], h1_hi)
    lb = _dot_nt(wr_ref[0:N_EXPERTS, :], h1_lo)
    logits = la[0:N_EXPERTS] + la[N_EXPERTS:2 * N_EXPERTS] + lb + br_ref[...]
    e_iota = lax.broadcasted_iota(jnp.int32, (N_EXPERTS, tm), 0)
    vals, onehots = [], []
    lg = logits
    for _ in range(TOP_K):
        mx = jnp.max(lg, axis=0, keepdims=True)
        sel = jnp.min(jnp.where(lg == mx, e_iota, N_EXPERTS), axis=0, keepdims=True)
        oh = e_iota == sel
        lg = jnp.where(oh, -jnp.inf, lg)
        vals.append(mx)
        onehots.append(oh)
    exps = [jnp.exp(v - vals[0]) for v in vals]
    inv_den = 1.0 / (exps[0] + exps[1] + exps[2] + exps[3])
    gates = [e * inv_den for e in exps]

    oh_all = jnp.where(onehots[0], 1.0, 0.0)
    for oh in onehots[1:]:
        oh_all = oh_all + jnp.where(oh, 1.0, 0.0)
    strict = jnp.where(row_i < col_i, 1.0, 0.0).astype(BF16)
    run = run_ref[:, 0:1]
    rank = _dot(oh_all.astype(BF16), strict) + run
    count = jnp.sum(oh_all, axis=1, keepdims=True)
    inv_rows = 1.0 / EXPERT_ROWS
    pages_before = jnp.ceil(run * inv_rows)
    need = jnp.ceil((run + count) * inv_rows) - pages_before
    er = lax.broadcasted_iota(jnp.int32, (N_EXPERTS, N_EXPERTS), 0)
    ec = lax.broadcasted_iota(jnp.int32, (N_EXPERTS, N_EXPERTS), 1)
    earlier = jnp.where(er > ec, 1.0, 0.0).astype(BF16)
    need_b = jnp.broadcast_to(need, (N_EXPERTS, LANES)).astype(BF16)
    new_page = free_ref[0:1, 0:1] + _dot(earlier, need_b)[:, 0:1]
    page_seq = jnp.floor(rank * inv_rows)
    page = jnp.where(page_seq < pages_before, page_ref[:, 0:1], new_page)
    pos_all = page * EXPERT_ROWS + (rank - page_seq * EXPERT_ROWS)
    positions = [jnp.sum(jnp.where(oh, pos_all, 0.0), axis=0, keepdims=True) for oh in onehots]
    page_ref[...] = jnp.where(need > 0.0, new_page, page_ref[...])
    free_ref[...] = free_ref[...] + jnp.sum(need, axis=0, keepdims=True)
    run_ref[...] = run_ref[...] + count
    cnt_ref[...] = run_ref[...]
    p_lane = lax.broadcasted_iota(jnp.int32, (N_EXPERTS, pe_ref.shape[1]), 1).astype(F32)
    e_plus1 = (lax.broadcasted_iota(jnp.int32, (N_EXPERTS, 1), 0) + 1).astype(F32)
    taken = jnp.logical_and(p_lane == new_page, need > 0.0)
    pe_ref[...] = pe_ref[...] + jnp.sum(jnp.where(taken, e_plus1, 0.0), axis=0, keepdims=True)

    r8 = lax.broadcasted_iota(jnp.int32, (SUBLANES, tm), 0)
    pos_out = jnp.zeros((SUBLANES, tm), jnp.int32)
    r128 = lax.broadcasted_iota(jnp.int32, (LANES, tm), 0)
    gate_rows = jnp.zeros((LANES, tm), F32)
    for kk in range(TOP_K):
        pos_out = jnp.where(r8 == kk, positions[kk].astype(jnp.int32), pos_out)
        gate_rows = jnp.where(r128 == kk, gates[kk], gate_rows)
    pos_ref[...] = pos_out
    gcol_ref[...] = gate_rows.T
    posv_ref[par] = pos_out
    pos_to_smem(par).start()

    @pl.when(g == last)
    def _():
        pos_to_smem(par).wait()
        for t in range(tm):
            for k in range(TOP_K):
                row_copy(par, t, k).start(priority=k % DMA_PRIORITIES)
        wait_row_copies(q)
        wait_row_copies(par)
        h1_write(q, 0).wait()
        h1_write(par, 0).wait()

        st_lane = lax.broadcasted_iota(jnp.int32, (N_EXPERTS, LANES), 1)
        state = jnp.where(st_lane == 0, page_ref[...], jnp.where(st_lane == 1, run_ref[...], free_ref[...]))
        stv_ref[...] = state.astype(jnp.int32)
        state_copy = pltpu.make_async_copy(stv_ref, sts_ref, psem.at[par])
        state_copy.start()
        state_copy.wait()

        def zero_fill(wait):
            def act(rows_dst, n):
                cp = pltpu.make_async_copy(zb_ref.at[pl.ds(0, n)], xp_hbm.at[pl.ds(rows_dst, n)], zsem)
                if wait:
                    cp.wait()
                else:
                    cp.start()

            def unused_page(p, carry):
                act(pl.multiple_of(p * EXPERT_ROWS, EXPERT_ROWS), EXPERT_ROWS)
                return carry

            lax.fori_loop(sts_ref[0, 2], n_pages, unused_page, 0)

            def page_tail(e, carry):
                filled = sts_ref[e, 1] & (EXPERT_ROWS - 1)
                pad = jnp.where(filled == 0, 0, EXPERT_ROWS - filled)
                dst = sts_ref[e, 0] * EXPERT_ROWS + filled
                size = 1
                while size < EXPERT_ROWS:
                    @pl.when((pad & size) != 0)
                    def _(dst=dst, size=size):
                        act(dst, size)
                    dst = dst + (pad & size)
                    size *= 2
                return carry

            lax.fori_loop(0, N_EXPERTS, page_tail, 0)

        zero_fill(wait=False)
        zero_fill(wait=True)


def _const_spec(shape):
    zeros = (0,) * len(shape)
    return pl.BlockSpec(shape, lambda b, s: zeros, pipeline_mode=pl.Buffered(1))


def _mixer_call(x, weights, n_pages):
    bsz, seq, _ = x.shape
    tm = TOKEN_TILE
    n_s = seq // tm
    t_total = bsz * seq
    assert bsz * n_s >= 2
    tile = lambda b, s: (b * n_s + s)
    spare_pages = TOP_K * tm // EXPERT_ROWS
    pe_lanes = -(-n_pages // LANES) * LANES
    in_specs = [pl.BlockSpec((None, tm, D_MODEL), lambda b, s: (b, s, 0))]
    in_specs += [_const_spec(w.shape) for w in weights]
    out_shape = (
        jax.ShapeDtypeStruct((t_total * ROW_TILES, LANES), F32),
        jax.ShapeDtypeStruct(((n_pages + spare_pages) * EXPERT_ROWS, ROW_TILES, LANES), F32),
        jax.ShapeDtypeStruct((SUBLANES, t_total), jnp.int32),
        jax.ShapeDtypeStruct((t_total, LANES), F32),
        jax.ShapeDtypeStruct((N_EXPERTS, LANES), F32),
        jax.ShapeDtypeStruct((SUBLANES, pe_lanes), F32),
    )
    out_specs = (
        pl.BlockSpec(memory_space=pl.ANY),
        pl.BlockSpec(memory_space=pl.ANY),
        pl.BlockSpec((SUBLANES, tm), lambda b, s: (0, tile(b, s))),
        pl.BlockSpec((tm, LANES), lambda b, s: (tile(b, s), 0)),
        pl.BlockSpec((N_EXPERTS, LANES), lambda b, s: (0, 0)),
        pl.BlockSpec((SUBLANES, pe_lanes), lambda b, s: (0, 0)),
    )
    scratch = [
        pltpu.VMEM((POOL_HALO + tm, POOL_WIDTH), F32),
        pltpu.VMEM((CONV_HALO + tm, D_MODEL), F32),
        pltpu.VMEM((N_HEADS, HEAD_DIM, HEAD_DIM), F32),
        pltpu.VMEM((N_HEADS, 1, HEAD_DIM), F32),
        pltpu.VMEM((1, LANES), F32),
        pltpu.VMEM((N_EXPERTS, LANES), F32),
        pltpu.VMEM((N_EXPERTS, LANES), F32),
        pltpu.VMEM((1, LANES), F32),
        pltpu.VMEM((2, tm * ROW_TILES + SUBLANES, LANES), F32),
        pltpu.VMEM((2, SUBLANES, tm), jnp.int32),
        pltpu.SMEM((2, SUBLANES, tm), jnp.int32),
        pltpu.VMEM((EXPERT_ROWS, ROW_TILES, LANES), F32),
        pltpu.VMEM((N_EXPERTS, LANES), jnp.int32),
        pltpu.SMEM((N_EXPERTS, LANES), jnp.int32),
        pltpu.SemaphoreType.DMA((2,)),
        pltpu.SemaphoreType.DMA((2,)),
        pltpu.SemaphoreType.DMA((2,)),
        pltpu.SemaphoreType.DMA(()),
    ]
    return pl.pallas_call(
        functools.partial(_mixer_kernel, n_pages=n_pages),
        grid=(bsz, n_s),
        in_specs=in_specs,
        out_specs=out_specs,
        out_shape=out_shape,
        scratch_shapes=scratch,
        compiler_params=pltpu.CompilerParams(
            dimension_semantics=("arbitrary", "arbitrary"), vmem_limit_bytes=VMEM_LIMIT),
        name="mixer",
    )(x, *weights)


def _experts_kernel(xpage_ref, spage_ref, be_ref, first_ref, nexte_ref, nused_ref,
                    slot_p_ref, slot_a_ref, slot_b_ref, xa_ref, xb_ref,
                    w1_hbm, b1_ref, w2_hbm, b2_ref, y_hbm,
                    yb0_ref, yb1_ref, w1s_ref, w2s_ref, w1b_ref, w2z_ref, w2b_ref, ssem, wsem):
    del xpage_ref, spage_ref
    rows = EXPERT_ROWS
    tile = rows * ROW_TILES
    i = pl.program_id(0)
    last = pl.num_programs(0) - 1
    n_used = nused_ref[0]
    yb = (yb0_ref, yb1_ref)
    x_in = (xa_ref, xb_ref)
    dyn_zero = lax.shift_right_arithmetic(n_used, 31)
    n_chunks = D_FF // (2 * LANES)
    assert SCATTER_STAGES <= n_chunks + 1

    def scatter(slots, r, p):
        return pltpu.make_async_copy(
            yb[p].at[pl.ds(r * ROW_TILES, ROW_TILES)], y_hbm.at[slots[0, 0, r]], ssem.at[p])

    def start_scatters(stage, slots, p):
        if stage is None:
            r_range = range(rows)
        elif stage < SCATTER_STAGES:
            r_range = range(rows * stage // SCATTER_STAGES, rows * (stage + 1) // SCATTER_STAGES)
        else:
            return jnp.zeros((1, LANES), F32)
        for r in r_range:
            scatter(slots, r, p).start(priority=r % DMA_PRIORITIES)
        spare = pl.multiple_of(tile + dyn_zero * SUBLANES, SUBLANES)
        yb[p][pl.ds(spare, SUBLANES), :] = jnp.full((SUBLANES, LANES), dyn_zero.astype(F32))
        return yb[p][pl.ds(spare, SUBLANES), :][0:1, :]

    def wait_scatters(p):
        view = yb[p].at[pl.ds(0, tile)]
        pltpu.make_async_copy(view, view, ssem.at[p]).wait()

    def weight_copies(e):
        return (pltpu.make_async_copy(w1_hbm.at[e], w1s_ref, wsem.at[0]),
                pltpu.make_async_copy(w2_hbm.at[e], w2s_ref, wsem.at[1]))

    def switch_weights(blk):
        @pl.when(first_ref[blk] == 1)
        def _():
            for cp in weight_copies(0):
                cp.wait()
            step = 128
            for c in range(D_MODEL // step):
                w1b_ref[c * step:(c + 1) * step, :] = w1s_ref[c * step:(c + 1) * step, :].astype(BF16)
            half = D_FF // 2
            for c in range(ROW_TILES):
                cs = slice(c * LANES, (c + 1) * LANES)
                w2z_ref[c, pl.ds(0, half, stride=2), :] = w2s_ref[0:half, cs]
                w2z_ref[c, pl.ds(1, half, stride=2), :] = w2s_ref[half:D_FF, cs]
                w2b_ref[:, cs] = w2z_ref[c].astype(BF16)
            nxt = nexte_ref[blk]

            @pl.when(nxt >= 0)
            def _():
                for cp in weight_copies(nxt):
                    cp.start()

    def compute(p, blk, start_stage):
        e = be_ref[blk]
        x = jnp.concatenate(
            [x_in[p][pl.ds(c, rows, stride=ROW_TILES), :] for c in range(ROW_TILES)], axis=1).astype(BF16)
        b1 = b1_ref[pl.ds(e, 1), :]
        width = 2 * LANES
        even = (lax.broadcasted_iota(jnp.int32, (rows, width), 1) & 1) == 0
        zs = []
        for c in range(n_chunks):
            lo = c * width
            hi = D_FF + lo
            zero = _tile_lanes(start_stage(c), 2)
            ha = _dot(x, w1b_ref[:, lo:lo + width]) + (b1[:, lo:lo + width] + zero)
            hb = _dot(x, w1b_ref[:, hi:hi + width]) + (b1[:, hi:hi + width] + zero)
            glu = jnp.where(even, ha, pltpu.roll(hb, 1, 1))
            lin = jnp.where(even, pltpu.roll(ha, width - 1, 1), hb)
            glu = jnp.minimum(glu, SWIGLU_LIMIT)
            lin = jnp.clip(lin, -SWIGLU_LIMIT, SWIGLU_LIMIT) + 1.0
            zs.append((glu * _sigmoid(SWIGLU_ALPHA * glu) * lin).astype(BF16))
        b2 = b2_ref[pl.ds(e, 1), :] + _tile_lanes(start_stage(n_chunks), ROW_TILES)
        y = _dot(jnp.concatenate(zs, axis=1), w2b_ref[...]) + b2
        for c in range(ROW_TILES):
            yb[p][pl.ds(c, rows, stride=ROW_TILES), :] = y[:, c * LANES:(c + 1) * LANES]

    def phase(blk, p, start_stage):
        switch_weights(blk)

        @pl.when(blk < n_used)
        def _():
            compute(p, blk, start_stage)

        @pl.when(blk >= n_used)
        def _():
            start_stage(None)

    @pl.when(i == 0)
    def _():
        yb1_ref[...] = jnp.zeros_like(yb1_ref)
        for cp in weight_copies(be_ref[0]):
            cp.start()

    @pl.when(i >= 1)
    def _():
        wait_scatters(0)

    phase(2 * i, 0, functools.partial(start_scatters, slots=slot_p_ref, p=1))

    wait_scatters(1)
    phase(2 * i + 1, 1, functools.partial(start_scatters, slots=slot_a_ref, p=0))

    @pl.when(i == last)
    def _():
        wait_scatters(0)
        for r in range(rows):
            scatter(slot_b_ref, r, 1).start(priority=r % DMA_PRIORITIES)
        wait_scatters(1)


def _experts_call(plan, row_slot, x_pages_2d, w_e1, b_e1, w_e2, b_e2, n_tokens):
    xpage, spage, block_e, first, next_e, n_used = plan
    rows = EXPERT_ROWS
    n_blocks = block_e.shape[0]
    assert n_blocks % 2 == 0
    n_slots = TOP_K * n_tokens + rows
    tile_rows = rows * ROW_TILES
    filler = row_slot.shape[0] - 1
    smem_rows = lambda index: pl.BlockSpec((1, 1, rows), index, memory_space=pltpu.SMEM)
    whole = lambda shape: pl.BlockSpec(shape, lambda i, *_: (0,) * len(shape))
    grid_spec = pltpu.PrefetchScalarGridSpec(
        num_scalar_prefetch=6,
        grid=(n_blocks // 2,),
        in_specs=[
            smem_rows(lambda i, xp, sp, *_: (jnp.where(i == 0, filler, sp[jnp.maximum(2 * i - 1, 0)]), 0, 0)),
            smem_rows(lambda i, xp, sp, *_: (sp[2 * i], 0, 0)),
            smem_rows(lambda i, xp, sp, *_: (sp[2 * i + 1], 0, 0)),
            pl.BlockSpec((tile_rows, LANES), lambda i, xp, *_: (xp[2 * i], 0)),
            pl.BlockSpec((tile_rows, LANES), lambda i, xp, *_: (xp[2 * i + 1], 0)),
            pl.BlockSpec(memory_space=pl.ANY),
            whole((N_EXPERTS, 2 * D_FF)),
            pl.BlockSpec(memory_space=pl.ANY),
            whole((N_EXPERTS, D_MODEL)),
        ],
        out_specs=pl.BlockSpec(memory_space=pl.ANY),
        scratch_shapes=[
            pltpu.VMEM((tile_rows + SUBLANES, LANES), F32),
            pltpu.VMEM((tile_rows + SUBLANES, LANES), F32),
            pltpu.VMEM((D_MODEL, 2 * D_FF), F32),
            pltpu.VMEM((D_FF, D_MODEL), F32),
            pltpu.VMEM((D_MODEL, 2 * D_FF), BF16),
            pltpu.VMEM((ROW_TILES, D_FF, LANES), F32),
            pltpu.VMEM((D_FF, D_MODEL), BF16),
            pltpu.SemaphoreType.DMA((2,)),
            pltpu.SemaphoreType.DMA((2,)),
        ],
    )
    return pl.pallas_call(
        _experts_kernel,
        grid_spec=grid_spec,
        out_shape=jax.ShapeDtypeStruct((n_slots, ROW_TILES, LANES), F32),
        compiler_params=pltpu.CompilerParams(
            dimension_semantics=("arbitrary",), vmem_limit_bytes=VMEM_LIMIT),
        name="experts",
    )(xpage, spage, block_e, first, next_e, n_used,
      row_slot, row_slot, row_slot, x_pages_2d, x_pages_2d, w_e1, b_e1, w_e2, b_e2)


def _combine_kernel(h1_ref, y0_ref, y1_ref, y2_ref, y3_ref, gcol_ref, g_ref, b_ref, out_ref):
    tc = COMBINE_TILE
    gcol = gcol_ref[...]
    y_refs = (y0_ref, y1_ref, y2_ref, y3_ref)
    chunks = []
    for c in range(ROW_TILES):
        z = DEEPNORM_ALPHA * h1_ref[pl.ds(c, tc, stride=ROW_TILES), :]
        for kk in range(TOP_K):
            z = z + gcol[:, kk:kk + 1] * y_refs[kk][pl.ds(c, tc, stride=ROW_TILES), :]
        chunks.append(z)
    total = chunks[0].sum(axis=1, keepdims=True)
    for z in chunks[1:]:
        total = total + z.sum(axis=1, keepdims=True)
    mu = total * (1.0 / D_MODEL)
    sq = None
    for z in chunks:
        zc = z - mu
        part = (zc * zc).sum(axis=1, keepdims=True)
        sq = part if sq is None else sq + part
    inv = lax.rsqrt(sq * (1.0 / D_MODEL) + LN_EPS)
    for c, z in enumerate(chunks):
        cs = slice(c * LANES, (c + 1) * LANES)
        out_ref[:, cs] = (z - mu) * inv * g_ref[:, cs] + b_ref[:, cs]


def _combine_call(h1_2d, y_2d, gcol, ln_g, ln_b, n_tokens):
    tc = COMBINE_TILE
    n_t = n_tokens // tc
    blk = tc * ROW_TILES
    y_spec = lambda kk: pl.BlockSpec((blk, LANES), lambda i: (kk * n_t + i, 0))
    return pl.pallas_call(
        _combine_kernel,
        grid=(n_t,),
        in_specs=[
            pl.BlockSpec((blk, LANES), lambda i: (i, 0)),
            y_spec(0), y_spec(1), y_spec(2), y_spec(3),
            pl.BlockSpec((tc, LANES), lambda i: (i, 0)),
            pl.BlockSpec((1, D_MODEL), lambda i: (0, 0)),
            pl.BlockSpec((1, D_MODEL), lambda i: (0, 0)),
        ],
        out_specs=pl.BlockSpec((tc, D_MODEL), lambda i: (i, 0)),
        out_shape=jax.ShapeDtypeStruct((n_tokens, D_MODEL), F32),
        compiler_params=pltpu.CompilerParams(
            dimension_semantics=("arbitrary",), vmem_limit_bytes=VMEM_LIMIT),
        name="combine",
    )(h1_2d, y_2d, y_2d, y_2d, y_2d, gcol, ln_g, ln_b)


def _prepare_mixer_weights(ln0_g, ln0_b, w_in, conv_w, conv_b, w_q, w_k, b_if, mh_gain, w_pool,
                           b_pool, ls_pool, w_branch_pool, w_branch_mlstm, w_out, ln1_g, ln1_b,
                           w_router, b_router):
    w_main = w_in[:, :COL_IF_END].astype(BF16)
    w_gates = w_in[:, COL_GATES:COL_GATES + 2 * D_MODEL].astype(BF16)
    bif = jnp.concatenate([b_if, jnp.zeros((LANES - 2 * N_HEADS,), F32)]).reshape(1, LANES)
    wqk = jnp.concatenate([w_q * (HEAD_DIM ** -0.5), w_k], axis=-1).astype(BF16)
    wpool_bd = jax.scipy.linalg.block_diag(*[w_pool[g] for g in range(len(POOL_WINDOWS))]).astype(BF16)
    wr_t = w_router.T
    wr_hi = wr_t.astype(BF16)
    wr_lo = (wr_t - wr_hi.astype(F32)).astype(BF16)
    row = lambda v: v.reshape(1, -1)
    return (row(ln0_g), row(ln0_b), w_main, w_gates, bif, conv_w, row(conv_b), wqk, row(mh_gain), wpool_bd,
            row(b_pool), row(ls_pool), w_branch_pool.astype(BF16), w_branch_mlstm.astype(BF16),
            w_out.astype(BF16), row(ln1_g), row(ln1_b), jnp.concatenate([wr_hi, wr_lo], axis=0),
            b_router.reshape(N_EXPERTS, 1))


def _plan(page_expert1, counts, pos, n_tokens, n_pages, n_rows_total):
    rows = EXPERT_ROWS
    n_assign = TOP_K * n_tokens
    i32 = jnp.int32
    slots = jnp.arange(n_assign, dtype=i32)
    hit = jnp.zeros((n_rows_total,), i32).at[pos.reshape(-1)].add(slots + 1)
    pad_slot = n_assign + jnp.arange(n_rows_total + rows, dtype=i32) % rows
    row_slot = jnp.where(jnp.concatenate([hit, jnp.zeros((rows,), i32)]) == 0, pad_slot,
                         jnp.concatenate([hit, jnp.zeros((rows,), i32)]) - 1)
    row_slot = row_slot.reshape(n_rows_total // rows + 1, 1, rows)
    filler = n_rows_total // rows

    pidx = jnp.arange(n_pages, dtype=i32)
    used = page_expert1 > 0
    n_used = jnp.sum(used.astype(i32))
    page_e = page_expert1 - 1
    key = jnp.where(used, page_e, N_EXPERTS) * n_pages + pidx
    place = jnp.sum((key[None, :] < key[:, None]).astype(i32), axis=1)
    at_block = place[None, :] == pidx[:, None]
    order = jnp.sum(jnp.where(at_block, pidx[None, :], 0), axis=1)
    block_e = jnp.clip(jnp.sum(jnp.where(at_block, page_e[None, :], 0), axis=1), 0, N_EXPERTS - 1)
    valid = pidx < n_used
    xpage = jnp.where(valid, order, 0)
    spage = jnp.where(valid, order, filler)
    prev_e = jnp.concatenate([jnp.full((1,), -1, i32), block_e[:-1]])
    first = jnp.logical_and(valid, block_e != prev_e).astype(i32)
    e_iota = jnp.arange(N_EXPERTS, dtype=i32)
    pages_per_e = (counts + rows - 1) // rows
    seg_end = jnp.cumsum(pages_per_e)
    mine = block_e[:, None] == e_iota[None, :]
    end_blk = jnp.sum(jnp.where(mine, seg_end[None, :], 0), axis=1)
    e_at_end = jnp.sum(jnp.where(pidx[None, :] == end_blk[:, None], block_e[None, :], 0), axis=1)
    next_e = jnp.where(end_blk < n_used, e_at_end, -1).astype(i32)
    plan = (xpage.astype(i32), spage.astype(i32), block_e.astype(i32), first, next_e,
            n_used.reshape(1).astype(i32))
    return plan, row_slot


def kernel(x, ln0_g, ln0_b, w_in, conv_w, conv_b, w_q, w_k, b_if, mh_gain, w_pool, b_pool, ls_pool,
           w_branch_pool, w_branch_mlstm, w_out, ln1_g, ln1_b, w_router, b_router, w_e1, b_e1,
           w_e2, b_e2, ln2_g, ln2_b):
    bsz, seq, _ = x.shape
    n_tokens = bsz * seq
    rows = EXPERT_ROWS
    assert w_in.shape[0] == 1, "single-layer trunk"
    assert seq % TOKEN_TILE == 0 and n_tokens % COMBINE_TILE == 0
    n_pages = (TOP_K * n_tokens + N_EXPERTS * (rows - 1) + rows - 1) // rows
    n_pages += n_pages % 2
    weights = _prepare_mixer_weights(
        ln0_g, ln0_b, w_in[0], conv_w[0], conv_b[0], w_q[0], w_k[0], b_if[0], mh_gain[0], w_pool[0],
        b_pool[0], ls_pool[0], w_branch_pool[0], w_branch_mlstm[0], w_out[0], ln1_g[0], ln1_b[0],
        w_router[0], b_router[0])
    h1_2d, x_pages, pos8, gcol, cnt, pe = _mixer_call(x, weights, n_pages)
    counts = cnt[:, 0].astype(jnp.int32)
    page_expert1 = pe[0, :n_pages].astype(jnp.int32)
    plan, row_slot = _plan(page_expert1, counts, pos8[:TOP_K], n_tokens, n_pages, x_pages.shape[0])
    y_slots = _experts_call(
        plan, row_slot, x_pages.reshape(-1, LANES), w_e1[0], b_e1[0], w_e2[0], b_e2[0], n_tokens)
    out = _combine_call(h1_2d, y_slots.reshape(-1, LANES), gcol, ln2_g[0].reshape(1, D_MODEL),
                        ln2_b[0].reshape(1, D_MODEL), n_tokens)
    return out.reshape(bsz, seq, D_MODEL)
```

```python
import functools

import jax
import jax.numpy as jnp
from jax import lax
from jax.experimental import pallas as pl
from jax.experimental.pallas import tpu as pltpu

F32 = jnp.float32
BF16 = jnp.bfloat16

D_MODEL = 1024
N_HEADS = 4
HEAD_DIM = 256
POOL_WIDTH = 512
POOL_GROUP = 128
POOL_WINDOWS = (2, 4, 8, 16)
CONV_WIDTH = 4
N_EXPERTS = 32
TOP_K = 4
D_FF = 1024
SWIGLU_ALPHA = 1.702
SWIGLU_LIMIT = 7.0
LN_EPS = 1e-5
DEEPNORM_ALPHA = 2.0 ** 0.25

SUBLANES = 8
LANES = 128
ROW_TILES = D_MODEL // LANES

COL_P = 0
COL_U = COL_P + POOL_WIDTH
COL_V = COL_U + D_MODEL
COL_O = COL_V + D_MODEL
COL_IF = COL_O + D_MODEL
COL_IF_END = COL_IF + LANES
COL_GATES = COL_IF + 2 * N_HEADS

TOKEN_TILE = 256
POOL_HALO = 16
CONV_HALO = 8
EXPERT_ROWS = 256
ROW_DMA_GROUPS = 8
SCATTER_STAGES = 3
COMBINE_TILE = 512
DMA_PRIORITIES = 2
VMEM_LIMIT = 56 * 1024 * 1024

assert TOKEN_TILE <= EXPERT_ROWS
assert (TOP_K * TOKEN_TILE) % EXPERT_ROWS == 0


def _layer_norm(x, g, b):
    mu = jnp.mean(x, axis=-1, keepdims=True)
    xc = x - mu
    var = jnp.mean(xc * xc, axis=-1, keepdims=True)
    return xc * lax.rsqrt(var + LN_EPS) * g + b


def _sigmoid(x):
    return 0.5 * jnp.tanh(0.5 * x) + 0.5


def _log_sigmoid(x):
    return -(jnp.maximum(-x, 0.0) + jnp.log(1.0 + jnp.exp(-jnp.abs(x))))


def _split3(x):
    hi = x.astype(BF16)
    r1 = x - hi.astype(F32)
    mid = r1.astype(BF16)
    lo = (r1 - mid.astype(F32)).astype(BF16)
    return hi, mid, lo


def _dot(a, b):
    return jnp.dot(a, b, preferred_element_type=F32)


def _dot_nt(a, b):
    return lax.dot_general(a, b, (((1,), (1,)), ((), ())), preferred_element_type=F32)


def _tile_lanes(row, n):
    return jnp.concatenate([row] * n, axis=1)


def _mixer_kernel(x_ref, ln0g_ref, ln0b_ref, win_ref, wgate_ref, bif_ref, convw_ref, convb_ref, wqk_ref,
                  gain_ref, wpool_ref, bpool_ref, lspool_ref, wbp_ref, wbm_ref, wout_ref,
                  ln1g_ref, ln1b_ref, wr_ref, br_ref,
                  h1_hbm, xp_hbm, pos_ref, gcol_ref, cnt_ref, pe_ref,
                  pext_ref, uext_ref, ct_ref, n_ref, m_ref, run_ref, page_ref, free_ref,
                  hx_ref, posv_ref, poss_ref, zb_ref, stv_ref, sts_ref, hob_ref, psem, rsem, hsem, zsem,
                  *, n_pages):
    tm = TOKEN_TILE
    tile_rows = tm * ROW_TILES
    b = pl.program_id(0)
    s = pl.program_id(1)
    n_s = pl.num_programs(1)
    g = b * n_s + s
    last = pl.num_programs(0) * n_s - 1
    par = g % 2
    q = 1 - par
    dyn_zero = lax.shift_right_arithmetic(g, 31)

    def hx_tile(p):
        return hx_ref.at[p, pl.ds(0, tile_rows)]

    def row_copy(p, t, k):
        return pltpu.make_async_copy(
            hx_ref.at[p, pl.ds(t * ROW_TILES, ROW_TILES)], xp_hbm.at[poss_ref[p, k, t]], rsem.at[p])

    def wait_row_copies(p):
        for _ in range(TOP_K):
            pltpu.make_async_copy(hx_tile(p), hx_tile(p), rsem.at[p]).wait()

    def h1_write(p, tile_index):
        off = pl.multiple_of(tile_index * tile_rows, tile_rows)
        return pltpu.make_async_copy(hx_tile(p), h1_hbm.at[pl.ds(off, tile_rows)], hsem.at[p])

    def pos_to_smem(p):
        return pltpu.make_async_copy(posv_ref.at[p], poss_ref.at[p], psem.at[p])

    @pl.when(s == 0)
    def _():
        pext_ref[0:POOL_HALO, :] = jnp.zeros((POOL_HALO, POOL_WIDTH), F32)
        uext_ref[0:CONV_HALO, :] = jnp.zeros((CONV_HALO, D_MODEL), F32)
        ct_ref[...] = jnp.zeros_like(ct_ref)
        n_ref[...] = jnp.zeros_like(n_ref)
        m_ref[...] = jnp.zeros_like(m_ref)

    @pl.when(g == 0)
    def _():
        run_ref[...] = jnp.zeros_like(run_ref)
        page_ref[...] = jnp.zeros_like(page_ref)
        free_ref[...] = jnp.zeros_like(free_ref)
        pe_ref[...] = jnp.zeros_like(pe_ref)
        zb_ref[...] = jnp.zeros_like(zb_ref)
        hx_ref[1] = jnp.zeros((tile_rows + SUBLANES, LANES), F32)
        for k in range(TOP_K):
            for t in range(tm):
                poss_ref[1, k, t] = n_pages * EXPERT_ROWS + k * tm + t

    @pl.when(g >= 1)
    def _():
        pos_to_smem(q).wait()
        wait_row_copies(par)

    @pl.when(g >= 2)
    def _():
        h1_write(par, 0).wait()

    group = tm // ROW_DMA_GROUPS

    def start_row_copies(grp):
        for t in range(grp * group, (grp + 1) * group):
            for k in range(TOP_K):
                row_copy(q, t, k).start(priority=k % DMA_PRIORITIES)
        spare = pl.multiple_of(tile_rows + dyn_zero * SUBLANES, SUBLANES)
        hx_ref[q, pl.ds(spare, SUBLANES), :] = jnp.full((SUBLANES, LANES), dyn_zero.astype(F32))
        return hx_ref[q, pl.ds(spare, SUBLANES), :][0:1, :]

    ln0b = ln0b_ref[...] + _tile_lanes(start_row_copies(0), ROW_TILES)
    h0 = _layer_norm(x_ref[...], ln0g_ref[...], ln0b)
    h0b = h0.astype(BF16)

    def proj(lo, hi):
        return _dot(h0b, win_ref[:, lo:hi])

    pext_ref[POOL_HALO:POOL_HALO + tm, :] = proj(COL_P, COL_U)
    tpos = s * tm + lax.broadcasted_iota(jnp.int32, (tm, 1), 0)
    groups = []
    for gi, w in enumerate(POOL_WINDOWS):
        c0 = gi * POOL_GROUP
        cur = pext_ref[POOL_HALO:POOL_HALO + tm, c0:c0 + POOL_GROUP]
        acc = cur
        for j in range(1, w):
            acc = acc + pext_ref[POOL_HALO - j:POOL_HALO - j + tm, c0:c0 + POOL_GROUP]
        inv_cnt = 1.0 / jnp.minimum(tpos + 1, w).astype(F32)
        groups.append(acc * inv_cnt - cur)
    pooled = jnp.concatenate(groups, axis=1)
    pext_ref[0:POOL_HALO, :] = pext_ref[tm:tm + POOL_HALO, :]
    bpool = bpool_ref[...] + _tile_lanes(start_row_copies(1), POOL_WIDTH // LANES)
    mixed = (_dot(pooled.astype(BF16), wpool_ref[...]) + bpool) * lspool_ref[...]
    y_pool = _dot(mixed.astype(BF16), wbp_ref[...])

    uext_ref[CONV_HALO:CONV_HALO + tm, :] = proj(COL_U, COL_V)
    conv = convb_ref[...] + _tile_lanes(start_row_copies(2), ROW_TILES)
    for j in range(CONV_WIDTH):
        off = CONV_HALO - (CONV_WIDTH - 1) + j
        conv = conv + convw_ref[j:j + 1, :] * uext_ref[off:off + tm, :]
    uext_ref[0:CONV_HALO, :] = uext_ref[tm:tm + CONV_HALO, :]
    ucb = (conv * (jnp.tanh(0.5 * conv) + 1.0)).astype(BF16)
    vb = proj(COL_V, COL_O).astype(BF16)

    slab = proj(COL_IF, COL_IF_END) + (bif_ref[...] + start_row_copies(3))
    lane = lax.broadcasted_iota(jnp.int32, (tm, LANES), 1)
    is_f = jnp.logical_and(lane >= N_HEADS, lane < 2 * N_HEADS)
    slab = jnp.where(is_f, _log_sigmoid(slab), slab)
    row_i = lax.broadcasted_iota(jnp.int32, (tm, tm), 0)
    col_i = lax.broadcasted_iota(jnp.int32, (tm, tm), 1)
    causal = row_i >= col_i
    tri = jnp.where(causal, 1.0, 0.0).astype(BF16)
    hi, mid, lo = _split3(slab)
    bcol = _dot(tri, hi) + _dot(tri, mid) + _dot(tri, lo)
    slab_t = slab.T
    bcol_t = bcol.T

    for h in range(N_HEADS):
        hs = slice(h * HEAD_DIM, (h + 1) * HEAD_DIM)
        qk = _dot(ucb[:, hs], wqk_ref[h])
        q_h = qk[:, :HEAD_DIM]
        k_h = qk[:, HEAD_DIM:]
        qb = q_h.astype(BF16)
        kb = k_h.astype(BF16)
        vh = vb[:, hs]

        i_c = slab[:, h:h + 1]
        b_c = bcol[:, N_HEADS + h:N_HEADS + h + 1]
        i_r = slab_t[h:h + 1, :]
        b_r = bcol_t[N_HEADS + h:N_HEADS + h + 1, :]
        m_prev = m_ref[:, h:h + 1] + start_row_copies(4 + h)[:, 0:1]

        d_log = jnp.where(causal, b_c - (b_r - i_r), -jnp.inf)
        m_inter = b_c + m_prev
        m_t = jnp.maximum(m_inter, jnp.max(d_log, axis=1, keepdims=True))
        w_intra = jnp.exp(d_log - m_t)
        sc = _dot_nt(qb, kb) * w_intra
        w_inter = jnp.exp(m_inter - m_t)
        ctb = ct_ref[h].astype(BF16)
        num = _dot(sc.astype(BF16), vh) + w_inter * _dot(qb, ctb)
        qn = jnp.sum(q_h * n_ref[h], axis=1, keepdims=True)
        den = jnp.sum(sc, axis=1, keepdims=True) + w_inter * qn
        hh = num * (1.0 / jnp.maximum(jnp.abs(den), jnp.exp(-m_t)))
        mu = jnp.mean(hh, axis=1, keepdims=True)
        hc = hh - mu
        var = jnp.mean(hc * hc, axis=1, keepdims=True)
        o_half = proj(COL_O + h * HEAD_DIM, COL_O + (h + 1) * HEAD_DIM)
        h_out = (jnp.tanh(o_half) + 1.0) * (hc * lax.rsqrt(var + LN_EPS)) * gain_ref[:, hs]
        hob_ref[:, hs] = h_out.astype(BF16)

        g_last = b_r[:, tm - 1:tm]
        m_new = jnp.maximum(g_last + m_prev, jnp.max(g_last - b_r + i_r, axis=1, keepdims=True))
        decay = jnp.exp(g_last + m_prev - m_new)
        w_state = jnp.exp(g_last - b_c + i_c - m_new)
        kw = k_h * w_state
        ct_ref[h] = decay * ct_ref[h] + _dot(kw.T.astype(BF16), vh)
        n_ref[h] = decay * n_ref[h] + jnp.sum(kw, axis=0, keepdims=True)
        m_ref[:, h:h + 1] = m_new

    y_mlstm = _dot(hob_ref[...], wbm_ref[...])

    merged2 = ((jnp.tanh(_dot(h0b, wgate_ref[:, 0:D_MODEL])) + 1.0) * y_pool
               + (jnp.tanh(_dot(h0b, wgate_ref[:, D_MODEL:2 * D_MODEL])) + 1.0) * y_mlstm)
    mix = _dot(merged2.astype(BF16), wout_ref[...])
    h1 = _layer_norm(DEEPNORM_ALPHA * h0 + mix, ln1g_ref[...], ln1b_ref[...])
    for c in range(ROW_TILES):
        hx_ref[par, pl.ds(c, tm, stride=ROW_TILES), :] = h1[:, c * LANES:(c + 1) * LANES]
    h1_write(par, g).start()

    h1_hi = h1.astype(BF16)
    h1_lo = (h1 - h1_hi.astype(F32)).astype(BF16)
    la = _dot_nt(wr_ref[...], h1_hi)
    lb = _dot_nt(wr_ref[0:N_EXPERTS, :], h1_lo)
    logits = la[0:N_EXPERTS] + la[N_EXPERTS:2 * N_EXPERTS] + lb + br_ref[...]
    e_iota = lax.broadcasted_iota(jnp.int32, (N_EXPERTS, tm), 0)
    vals, onehots = [], []
    lg = logits
    for _ in range(TOP_K):
        mx = jnp.max(lg, axis=0, keepdims=True)
        sel = jnp.min(jnp.where(lg == mx, e_iota, N_EXPERTS), axis=0, keepdims=True)
        oh = e_iota == sel
        lg = jnp.where(oh, -jnp.inf, lg)
        vals.append(mx)
        onehots.append(oh)
    exps = [jnp.exp(v - vals[0]) for v in vals]
    inv_den = 1.0 / (exps[0] + exps[1] + exps[2] + exps[3])
    gates = [e * inv_den for e in exps]

    oh_all = jnp.where(onehots[0], 1.0, 0.0)
    for oh in onehots[1:]:
        oh_all = oh_all + jnp.where(oh, 1.0, 0.0)
    strict = jnp.where(row_i < col_i, 1.0, 0.0).astype(BF16)
    run = run_ref[:, 0:1]
    rank = _dot(oh_all.astype(BF16), strict) + run
    count = jnp.sum(oh_all, axis=1, keepdims=True)
    inv_rows = 1.0 / EXPERT_ROWS
    pages_before = jnp.ceil(run * inv_rows)
    need = jnp.ceil((run + count) * inv_rows) - pages_before
    er = lax.broadcasted_iota(jnp.int32, (N_EXPERTS, N_EXPERTS), 0)
    ec = lax.broadcasted_iota(jnp.int32, (N_EXPERTS, N_EXPERTS), 1)
    earlier = jnp.where(er > ec, 1.0, 0.0).astype(BF16)
    need_b = jnp.broadcast_to(need, (N_EXPERTS, LANES)).astype(BF16)
    new_page = free_ref[0:1, 0:1] + _dot(earlier, need_b)[:, 0:1]
    page_seq = jnp.floor(rank * inv_rows)
    page = jnp.where(page_seq < pages_before, page_ref[:, 0:1], new_page)
    pos_all = page * EXPERT_ROWS + (rank - page_seq * EXPERT_ROWS)
    positions = [jnp.sum(jnp.where(oh, pos_all, 0.0), axis=0, keepdims=True) for oh in onehots]
    page_ref[...] = jnp.where(need > 0.0, new_page, page_ref[...])
    free_ref[...] = free_ref[...] + jnp.sum(need, axis=0, keepdims=True)
    run_ref[...] = run_ref[...] + count
    cnt_ref[...] = run_ref[...]
    p_lane = lax.broadcasted_iota(jnp.int32, (N_EXPERTS, pe_ref.shape[1]), 1).astype(F32)
    e_plus1 = (lax.broadcasted_iota(jnp.int32, (N_EXPERTS, 1), 0) + 1).astype(F32)
    taken = jnp.logical_and(p_lane == new_page, need > 0.0)
    pe_ref[...] = pe_ref[...] + jnp.sum(jnp.where(taken, e_plus1, 0.0), axis=0, keepdims=True)

    r8 = lax.broadcasted_iota(jnp.int32, (SUBLANES, tm), 0)
    pos_out = jnp.zeros((SUBLANES, tm), jnp.int32)
    r128 = lax.broadcasted_iota(jnp.int32, (LANES, tm), 0)
    gate_rows = jnp.zeros((LANES, tm), F32)
    for kk in range(TOP_K):
        pos_out = jnp.where(r8 == kk, positions[kk].astype(jnp.int32), pos_out)
        gate_rows = jnp.where(r128 == kk, gates[kk], gate_rows)
    pos_ref[...] = pos_out
    gcol_ref[...] = gate_rows.T
    posv_ref[par] = pos_out
    pos_to_smem(par).start()

    @pl.when(g == last)
    def _():
        pos_to_smem(par).wait()
        for t in range(tm):
            for k in range(TOP_K):
                row_copy(par, t, k).start(priority=k % DMA_PRIORITIES)
        wait_row_copies(q)
        wait_row_copies(par)
        h1_write(q, 0).wait()
        h1_write(par, 0).wait()

        st_lane = lax.broadcasted_iota(jnp.int32, (N_EXPERTS, LANES), 1)
        state = jnp.where(st_lane == 0, page_ref[...], jnp.where(st_lane == 1, run_ref[...], free_ref[...]))
        stv_ref[...] = state.astype(jnp.int32)
        state_copy = pltpu.make_async_copy(stv_ref, sts_ref, psem.at[par])
        state_copy.start()
        state_copy.wait()

        def zero_fill(wait):
            def act(rows_dst, n):
                cp = pltpu.make_async_copy(zb_ref.at[pl.ds(0, n)], xp_hbm.at[pl.ds(rows_dst, n)], zsem)
                if wait:
                    cp.wait()
                else:
                    cp.start()

            def unused_page(p, carry):
                act(pl.multiple_of(p * EXPERT_ROWS, EXPERT_ROWS), EXPERT_ROWS)
                return carry

            lax.fori_loop(sts_ref[0, 2], n_pages, unused_page, 0)

            def page_tail(e, carry):
                filled = sts_ref[e, 1] & (EXPERT_ROWS - 1)
                pad = jnp.where(filled == 0, 0, EXPERT_ROWS - filled)
                dst = sts_ref[e, 0] * EXPERT_ROWS + filled
                size = 1
                while size < EXPERT_ROWS:
                    @pl.when((pad & size) != 0)
                    def _(dst=dst, size=size):
                        act(dst, size)
                    dst = dst + (pad & size)
                    size *= 2
                return carry

            lax.fori_loop(0, N_EXPERTS, page_tail, 0)

        zero_fill(wait=False)
        zero_fill(wait=True)


def _const_spec(shape):
    zeros = (0,) * len(shape)
    return pl.BlockSpec(shape, lambda b, s: zeros, pipeline_mode=pl.Buffered(1))


def _mixer_call(x, weights, n_pages):
    bsz, seq, _ = x.shape
    tm = TOKEN_TILE
    n_s = seq // tm
    t_total = bsz * seq
    assert bsz * n_s >= 2
    tile = lambda b, s: (b * n_s + s)
    spare_pages = TOP_K * tm // EXPERT_ROWS
    pe_lanes = -(-n_pages // LANES) * LANES
    in_specs = [pl.BlockSpec((None, tm, D_MODEL), lambda b, s: (b, s, 0))]
    in_specs += [_const_spec(w.shape) for w in weights]
    out_shape = (
        jax.ShapeDtypeStruct((t_total * ROW_TILES, LANES), F32),
        jax.ShapeDtypeStruct(((n_pages + spare_pages) * EXPERT_ROWS, ROW_TILES, LANES), F32),
        jax.ShapeDtypeStruct((SUBLANES, t_total), jnp.int32),
        jax.ShapeDtypeStruct((t_total, LANES), F32),
        jax.ShapeDtypeStruct((N_EXPERTS, LANES), F32),
        jax.ShapeDtypeStruct((SUBLANES, pe_lanes), F32),
    )
    out_specs = (
        pl.BlockSpec(memory_space=pl.ANY),
        pl.BlockSpec(memory_space=pl.ANY),
        pl.BlockSpec((SUBLANES, tm), lambda b, s: (0, tile(b, s))),
        pl.BlockSpec((tm, LANES), lambda b, s: (tile(b, s), 0)),
        pl.BlockSpec((N_EXPERTS, LANES), lambda b, s: (0, 0)),
        pl.BlockSpec((SUBLANES, pe_lanes), lambda b, s: (0, 0)),
    )
    scratch = [
        pltpu.VMEM((POOL_HALO + tm, POOL_WIDTH), F32),
        pltpu.VMEM((CONV_HALO + tm, D_MODEL), F32),
        pltpu.VMEM((N_HEADS, HEAD_DIM, HEAD_DIM), F32),
        pltpu.VMEM((N_HEADS, 1, HEAD_DIM), F32),
        pltpu.VMEM((1, LANES), F32),
        pltpu.VMEM((N_EXPERTS, LANES), F32),
        pltpu.VMEM((N_EXPERTS, LANES), F32),
        pltpu.VMEM((1, LANES), F32),
        pltpu.VMEM((2, tm * ROW_TILES + SUBLANES, LANES), F32),
        pltpu.VMEM((2, SUBLANES, tm), jnp.int32),
        pltpu.SMEM((2, SUBLANES, tm), jnp.int32),
        pltpu.VMEM((EXPERT_ROWS, ROW_TILES, LANES), F32),
        pltpu.VMEM((N_EXPERTS, LANES), jnp.int32),
        pltpu.SMEM((N_EXPERTS, LANES), jnp.int32),
        pltpu.VMEM((tm, D_MODEL), BF16),
        pltpu.SemaphoreType.DMA((2,)),
        pltpu.SemaphoreType.DMA((2,)),
        pltpu.SemaphoreType.DMA((2,)),
        pltpu.SemaphoreType.DMA(()),
    ]
    return pl.pallas_call(
        functools.partial(_mixer_kernel, n_pages=n_pages),
        grid=(bsz, n_s),
        in_specs=in_specs,
        out_specs=out_specs,
        out_shape=out_shape,
        scratch_shapes=scratch,
        compiler_params=pltpu.CompilerParams(
            dimension_semantics=("arbitrary", "arbitrary"), vmem_limit_bytes=VMEM_LIMIT),
        name="mixer",
    )(x, *weights)


def _experts_kernel(xpage_ref, spage_ref, be_ref, first_ref, nexte_ref, nused_ref,
                    slot_p_ref, slot_a_ref, slot_b_ref, xa_ref, xb_ref,
                    w1_hbm, b1_ref, w2_hbm, b2_ref, y_hbm,
                    yb0_ref, yb1_ref, w1s_ref, w2s_ref, w1b_ref, w2z_ref, w2b_ref, ssem, wsem):
    del xpage_ref, spage_ref
    rows = EXPERT_ROWS
    tile = rows * ROW_TILES
    i = pl.program_id(0)
    last = pl.num_programs(0) - 1
    n_used = nused_ref[0]
    yb = (yb0_ref, yb1_ref)
    x_in = (xa_ref, xb_ref)
    dyn_zero = lax.shift_right_arithmetic(n_used, 31)
    n_chunks = D_FF // (2 * LANES)
    assert SCATTER_STAGES <= n_chunks + 1

    def scatter(slots, r, p):
        return pltpu.make_async_copy(
            yb[p].at[pl.ds(r * ROW_TILES, ROW_TILES)], y_hbm.at[slots[0, 0, r]], ssem.at[p])

    def start_scatters(stage, slots, p):
        if stage is None:
            r_range = range(rows)
        elif stage < SCATTER_STAGES:
            r_range = range(rows * stage // SCATTER_STAGES, rows * (stage + 1) // SCATTER_STAGES)
        else:
            return jnp.zeros((1, LANES), F32)
        for r in r_range:
            scatter(slots, r, p).start(priority=r % DMA_PRIORITIES)
        spare = pl.multiple_of(tile + dyn_zero * SUBLANES, SUBLANES)
        yb[p][pl.ds(spare, SUBLANES), :] = jnp.full((SUBLANES, LANES), dyn_zero.astype(F32))
        return yb[p][pl.ds(spare, SUBLANES), :][0:1, :]

    def wait_scatters(p):
        view = yb[p].at[pl.ds(0, tile)]
        pltpu.make_async_copy(view, view, ssem.at[p]).wait()

    def weight_copies(e):
        return (pltpu.make_async_copy(w1_hbm.at[e], w1s_ref, wsem.at[0]),
                pltpu.make_async_copy(w2_hbm.at[e], w2s_ref, wsem.at[1]))

    def switch_weights(blk):
        @pl.when(first_ref[blk] == 1)
        def _():
            for cp in weight_copies(0):
                cp.wait()
            step = 128
            for c in range(D_MODEL // step):
                w1b_ref[c * step:(c + 1) * step, :] = w1s_ref[c * step:(c + 1) * step, :].astype(BF16)
            half = D_FF // 2
            for c in range(ROW_TILES):
                cs = slice(c * LANES, (c + 1) * LANES)
                w2z_ref[c, pl.ds(0, half, stride=2), :] = w2s_ref[0:half, cs]
                w2z_ref[c, pl.ds(1, half, stride=2), :] = w2s_ref[half:D_FF, cs]
                w2b_ref[:, cs] = w2z_ref[c].astype(BF16)
            nxt = nexte_ref[blk]

            @pl.when(nxt >= 0)
            def _():
                for cp in weight_copies(nxt):
                    cp.start()

    def compute(p, blk, start_stage):
        e = be_ref[blk]
        x = jnp.concatenate(
            [x_in[p][pl.ds(c, rows, stride=ROW_TILES), :] for c in range(ROW_TILES)], axis=1).astype(BF16)
        b1 = b1_ref[pl.ds(e, 1), :]
        width = 2 * LANES
        even = (lax.broadcasted_iota(jnp.int32, (rows, width), 1) & 1) == 0
        zs = []
        for c in range(n_chunks):
            lo = c * width
            hi = D_FF + lo
            zero = _tile_lanes(start_stage(c), 2)
            ha = _dot(x, w1b_ref[:, lo:lo + width]) + (b1[:, lo:lo + width] + zero)
            hb = _dot(x, w1b_ref[:, hi:hi + width]) + (b1[:, hi:hi + width] + zero)
            glu = jnp.where(even, ha, pltpu.roll(hb, 1, 1))
            lin = jnp.where(even, pltpu.roll(ha, width - 1, 1), hb)
            glu = jnp.minimum(glu, SWIGLU_LIMIT)
            lin = jnp.clip(lin, -SWIGLU_LIMIT, SWIGLU_LIMIT) + 1.0
            zs.append((glu * _sigmoid(SWIGLU_ALPHA * glu) * lin).astype(BF16))
        b2 = b2_ref[pl.ds(e, 1), :] + _tile_lanes(start_stage(n_chunks), ROW_TILES)
        y = _dot(jnp.concatenate(zs, axis=1), w2b_ref[...]) + b2
        for c in range(ROW_TILES):
            yb[p][pl.ds(c, rows, stride=ROW_TILES), :] = y[:, c * LANES:(c + 1) * LANES]

    def phase(blk, p, start_stage):
        switch_weights(blk)

        @pl.when(blk < n_used)
        def _():
            compute(p, blk, start_stage)

        @pl.when(blk >= n_used)
        def _():
            start_stage(None)

    @pl.when(i == 0)
    def _():
        yb1_ref[...] = jnp.zeros_like(yb1_ref)
        for cp in weight_copies(be_ref[0]):
            cp.start()

    @pl.when(i >= 1)
    def _():
        wait_scatters(0)

    phase(2 * i, 0, functools.partial(start_scatters, slots=slot_p_ref, p=1))

    wait_scatters(1)
    phase(2 * i + 1, 1, functools.partial(start_scatters, slots=slot_a_ref, p=0))

    @pl.when(i == last)
    def _():
        wait_scatters(0)
        for r in range(rows):
            scatter(slot_b_ref, r, 1).start(priority=r % DMA_PRIORITIES)
        wait_scatters(1)


def _experts_call(plan, row_slot, x_pages_2d, w_e1, b_e1, w_e2, b_e2, n_tokens):
    xpage, spage, block_e, first, next_e, n_used = plan
    rows = EXPERT_ROWS
    n_blocks = block_e.shape[0]
    assert n_blocks % 2 == 0
    n_slots = TOP_K * n_tokens + rows
    tile_rows = rows * ROW_TILES
    filler = row_slot.shape[0] - 1
    smem_rows = lambda index: pl.BlockSpec((1, 1, rows), index, memory_space=pltpu.SMEM)
    whole = lambda shape: pl.BlockSpec(shape, lambda i, *_: (0,) * len(shape))
    grid_spec = pltpu.PrefetchScalarGridSpec(
        num_scalar_prefetch=6,
        grid=(n_blocks // 2,),
        in_specs=[
            smem_rows(lambda i, xp, sp, *_: (jnp.where(i == 0, filler, sp[jnp.maximum(2 * i - 1, 0)]), 0, 0)),
            smem_rows(lambda i, xp, sp, *_: (sp[2 * i], 0, 0)),
            smem_rows(lambda i, xp, sp, *_: (sp[2 * i + 1], 0, 0)),
            pl.BlockSpec((tile_rows, LANES), lambda i, xp, *_: (xp[2 * i], 0)),
            pl.BlockSpec((tile_rows, LANES), lambda i, xp, *_: (xp[2 * i + 1], 0)),
            pl.BlockSpec(memory_space=pl.ANY),
            whole((N_EXPERTS, 2 * D_FF)),
            pl.BlockSpec(memory_space=pl.ANY),
            whole((N_EXPERTS, D_MODEL)),
        ],
        out_specs=pl.BlockSpec(memory_space=pl.ANY),
        scratch_shapes=[
            pltpu.VMEM((tile_rows + SUBLANES, LANES), F32),
            pltpu.VMEM((tile_rows + SUBLANES, LANES), F32),
            pltpu.VMEM((D_MODEL, 2 * D_FF), F32),
            pltpu.VMEM((D_FF, D_MODEL), F32),
            pltpu.VMEM((D_MODEL, 2 * D_FF), BF16),
            pltpu.VMEM((ROW_TILES, D_FF, LANES), F32),
            pltpu.VMEM((D_FF, D_MODEL), BF16),
            pltpu.SemaphoreType.DMA((2,)),
            pltpu.SemaphoreType.DMA((2,)),
        ],
    )
    return pl.pallas_call(
        _experts_kernel,
        grid_spec=grid_spec,
        out_shape=jax.ShapeDtypeStruct((n_slots, ROW_TILES, LANES), F32),
        compiler_params=pltpu.CompilerParams(
            dimension_semantics=("arbitrary",), vmem_limit_bytes=VMEM_LIMIT),
        name="experts",
    )(xpage, spage, block_e, first, next_e, n_used,
      row_slot, row_slot, row_slot, x_pages_2d, x_pages_2d, w_e1, b_e1, w_e2, b_e2)


def _combine_kernel(h1_ref, y0_ref, y1_ref, y2_ref, y3_ref, gcol_ref, g_ref, b_ref, out_ref):
    tc = COMBINE_TILE
    gcol = gcol_ref[...]
    y_refs = (y0_ref, y1_ref, y2_ref, y3_ref)
    chunks = []
    for c in range(ROW_TILES):
        z = DEEPNORM_ALPHA * h1_ref[pl.ds(c, tc, stride=ROW_TILES), :]
        for kk in range(TOP_K):
            z = z + gcol[:, kk:kk + 1] * y_refs[kk][pl.ds(c, tc, stride=ROW_TILES), :]
        chunks.append(z)
    total = chunks[0].sum(axis=1, keepdims=True)
    for z in chunks[1:]:
        total = total + z.sum(axis=1, keepdims=True)
    mu = total * (1.0 / D_MODEL)
    sq = None
    for z in chunks:
        zc = z - mu
        part = (zc * zc).sum(axis=1, keepdims=True)
        sq = part if sq is None else sq + part
    inv = lax.rsqrt(sq * (1.0 / D_MODEL) + LN_EPS)
    for c, z in enumerate(chunks):
        cs = slice(c * LANES, (c + 1) * LANES)
        out_ref[:, cs] = (z - mu) * inv * g_ref[:, cs] + b_ref[:, cs]


def _combine_call(h1_2d, y_2d, gcol, ln_g, ln_b, n_tokens):
    tc = COMBINE_TILE
    n_t = n_tokens // tc
    blk = tc * ROW_TILES
    y_spec = lambda kk: pl.BlockSpec((blk, LANES), lambda i: (kk * n_t + i, 0))
    return pl.pallas_call(
        _combine_kernel,
        grid=(n_t,),
        in_specs=[
            pl.BlockSpec((blk, LANES), lambda i: (i, 0)),
            y_spec(0), y_spec(1), y_spec(2), y_spec(3),
            pl.BlockSpec((tc, LANES), lambda i: (i, 0)),
            pl.BlockSpec((1, D_MODEL), lambda i: (0, 0)),
            pl.BlockSpec((1, D_MODEL), lambda i: (0, 0)),
        ],
        out_specs=pl.BlockSpec((tc, D_MODEL), lambda i: (i, 0)),
        out_shape=jax.ShapeDtypeStruct((n_tokens, D_MODEL), F32),
        compiler_params=pltpu.CompilerParams(
            dimension_semantics=("arbitrary",), vmem_limit_bytes=VMEM_LIMIT),
        name="combine",
    )(h1_2d, y_2d, y_2d, y_2d, y_2d, gcol, ln_g, ln_b)


def _prepare_mixer_weights(ln0_g, ln0_b, w_in, conv_w, conv_b, w_q, w_k, b_if, mh_gain, w_pool,
                           b_pool, ls_pool, w_branch_pool, w_branch_mlstm, w_out, ln1_g, ln1_b,
                           w_router, b_router):
    col = jnp.arange(COL_IF_END)
    col_scale = jnp.where(jnp.logical_and(col >= COL_O, col < COL_IF), 0.5, 1.0).astype(F32)
    w_main = (w_in[:, :COL_IF_END] * col_scale).astype(BF16)
    w_gates = (w_in[:, COL_GATES:COL_GATES + 2 * D_MODEL] * 0.5).astype(BF16)
    bif = jnp.concatenate([b_if, jnp.zeros((LANES - 2 * N_HEADS,), F32)]).reshape(1, LANES)
    wqk = jnp.concatenate([w_q * (0.5 * HEAD_DIM ** -0.5), w_k * 0.5], axis=-1).astype(BF16)
    wpool_bd = jax.scipy.linalg.block_diag(*[w_pool[g] for g in range(len(POOL_WINDOWS))]).astype(BF16)
    wr_t = w_router.T
    wr_hi = wr_t.astype(BF16)
    wr_lo = (wr_t - wr_hi.astype(F32)).astype(BF16)
    row = lambda v: v.reshape(1, -1)
    return (row(ln0_g), row(ln0_b), w_main, w_gates, bif, conv_w, row(conv_b), wqk, row(0.5 * mh_gain),
            wpool_bd, row(b_pool), row(ls_pool), w_branch_pool.astype(BF16),
            w_branch_mlstm.astype(BF16), (0.5 * w_out).astype(BF16), row(ln1_g), row(ln1_b), jnp.concatenate([wr_hi, wr_lo], axis=0),
            b_router.reshape(N_EXPERTS, 1))


def _plan(page_expert1, counts, pos, n_tokens, n_pages, n_rows_total):
    rows = EXPERT_ROWS
    n_assign = TOP_K * n_tokens
    i32 = jnp.int32
    slots = jnp.arange(n_assign, dtype=i32)
    hit = jnp.zeros((n_rows_total,), i32).at[pos.reshape(-1)].add(slots + 1)
    pad_slot = n_assign + jnp.arange(n_rows_total + rows, dtype=i32) % rows
    row_slot = jnp.where(jnp.concatenate([hit, jnp.zeros((rows,), i32)]) == 0, pad_slot,
                         jnp.concatenate([hit, jnp.zeros((rows,), i32)]) - 1)
    row_slot = row_slot.reshape(n_rows_total // rows + 1, 1, rows)
    filler = n_rows_total // rows

    pidx = jnp.arange(n_pages, dtype=i32)
    used = page_expert1 > 0
    n_used = jnp.sum(used.astype(i32))
    page_e = page_expert1 - 1
    key = jnp.where(used, page_e, N_EXPERTS) * n_pages + pidx
    place = jnp.sum((key[None, :] < key[:, None]).astype(i32), axis=1)
    at_block = place[None, :] == pidx[:, None]
    order = jnp.sum(jnp.where(at_block, pidx[None, :], 0), axis=1)
    block_e = jnp.clip(jnp.sum(jnp.where(at_block, page_e[None, :], 0), axis=1), 0, N_EXPERTS - 1)
    valid = pidx < n_used
    xpage = jnp.where(valid, order, 0)
    spage = jnp.where(valid, order, filler)
    prev_e = jnp.concatenate([jnp.full((1,), -1, i32), block_e[:-1]])
    first = jnp.logical_and(valid, block_e != prev_e).astype(i32)
    e_iota = jnp.arange(N_EXPERTS, dtype=i32)
    pages_per_e = (counts + rows - 1) // rows
    seg_end = jnp.cumsum(pages_per_e)
    mine = block_e[:, None] == e_iota[None, :]
    end_blk = jnp.sum(jnp.where(mine, seg_end[None, :], 0), axis=1)
    e_at_end = jnp.sum(jnp.where(pidx[None, :] == end_blk[:, None], block_e[None, :], 0), axis=1)
    next_e = jnp.where(end_blk < n_used, e_at_end, -1).astype(i32)
    plan = (xpage.astype(i32), spage.astype(i32), block_e.astype(i32), first, next_e,
            n_used.reshape(1).astype(i32))
    return plan, row_slot


def kernel(x, ln0_g, ln0_b, w_in, conv_w, conv_b, w_q, w_k, b_if, mh_gain, w_pool, b_pool, ls_pool,
           w_branch_pool, w_branch_mlstm, w_out, ln1_g, ln1_b, w_router, b_router, w_e1, b_e1,
           w_e2, b_e2, ln2_g, ln2_b):
    bsz, seq, _ = x.shape
    n_tokens = bsz * seq
    rows = EXPERT_ROWS
    assert w_in.shape[0] == 1, "single-layer trunk"
    assert seq % TOKEN_TILE == 0 and n_tokens % COMBINE_TILE == 0
    n_pages = (TOP_K * n_tokens + N_EXPERTS * (rows - 1) + rows - 1) // rows
    n_pages += n_pages % 2
    weights = _prepare_mixer_weights(
        ln0_g, ln0_b, w_in[0], conv_w[0], conv_b[0], w_q[0], w_k[0], b_if[0], mh_gain[0], w_pool[0],
        b_pool[0], ls_pool[0], w_branch_pool[0], w_branch_mlstm[0], w_out[0], ln1_g[0], ln1_b[0],
        w_router[0], b_router[0])
    h1_2d, x_pages, pos8, gcol, cnt, pe = _mixer_call(x, weights, n_pages)
    counts = cnt[:, 0].astype(jnp.int32)
    page_expert1 = pe[0, :n_pages].astype(jnp.int32)
    plan, row_slot = _plan(page_expert1, counts, pos8[:TOP_K], n_tokens, n_pages, x_pages.shape[0])
    y_slots = _experts_call(
        plan, row_slot, x_pages.reshape(-1, LANES), w_e1[0], b_e1[0], w_e2[0], b_e2[0], n_tokens)
    out = _combine_call(h1_2d, y_slots.reshape(-1, LANES), gcol, ln2_g[0].reshape(1, D_MODEL),
                        ln2_b[0].reshape(1, D_MODEL), n_tokens)
    return out.reshape(bsz, seq, D_MODEL)
```

```python
import functools

import jax
import jax.numpy as jnp
from jax import lax
from jax.experimental import pallas as pl
from jax.experimental.pallas import tpu as pltpu

F32 = jnp.float32
BF16 = jnp.bfloat16

D_MODEL = 1024
N_HEADS = 4
HEAD_DIM = 256
POOL_WIDTH = 512
POOL_GROUP = 128
POOL_WINDOWS = (2, 4, 8, 16)
CONV_WIDTH = 4
N_EXPERTS = 32
TOP_K = 4
D_FF = 1024
SWIGLU_ALPHA = 1.702
SWIGLU_LIMIT = 7.0
LN_EPS = 1e-5
DEEPNORM_ALPHA = 2.0 ** 0.25

SUBLANES = 8
LANES = 128
ROW_TILES = D_MODEL // LANES

COL_P = 0
COL_U = COL_P + POOL_WIDTH
COL_V = COL_U + D_MODEL
COL_O = COL_V + D_MODEL
COL_IF = COL_O + D_MODEL
COL_IF_END = COL_IF + LANES
COL_GATES = COL_IF + 2 * N_HEADS

TOKEN_TILE = 256
POOL_HALO = 24
CONV_HALO = 8
EXPERT_ROWS = 256
ROW_DMA_GROUPS = 8
SCATTER_STAGES = 3
COMBINE_TILE = 512
DMA_PRIORITIES = 2
VMEM_LIMIT = 56 * 1024 * 1024

assert TOKEN_TILE <= EXPERT_ROWS
assert (TOP_K * TOKEN_TILE) % EXPERT_ROWS == 0


def _layer_norm(x, g, b):
    mu = jnp.mean(x, axis=-1, keepdims=True)
    xc = x - mu
    var = jnp.mean(xc * xc, axis=-1, keepdims=True)
    return xc * lax.rsqrt(var + LN_EPS) * g + b


def _sigmoid(x):
    return 0.5 * jnp.tanh(0.5 * x) + 0.5


def _log_sigmoid(x):
    return -(jnp.maximum(-x, 0.0) + jnp.log(1.0 + jnp.exp(-jnp.abs(x))))


def _split3(x):
    hi = x.astype(BF16)
    r1 = x - hi.astype(F32)
    mid = r1.astype(BF16)
    lo = (r1 - mid.astype(F32)).astype(BF16)
    return hi, mid, lo


def _dot(a, b):
    return jnp.dot(a, b, preferred_element_type=F32)


def _dot_nt(a, b):
    return lax.dot_general(a, b, (((1,), (1,)), ((), ())), preferred_element_type=F32)


def _tile_lanes(row, n):
    return jnp.concatenate([row] * n, axis=1)


def _mixer_kernel(x_ref, ln0g_ref, ln0b_ref, win_ref, wgate_ref, bif_ref, convw_ref, convb_ref, wqk_ref,
                  gain_ref, wpool_ref, bpool_ref, lspool_ref, wbp_ref, wbm_ref, wout_ref,
                  ln1g_ref, ln1b_ref, wr_ref, br_ref,
                  h1_hbm, xp_hbm, pos_ref, gcol_ref, cnt_ref, pe_ref,
                  pext_ref, pw_ref, uext_ref, ct_ref, n_ref, m_ref, run_ref, page_ref, free_ref,
                  hx_ref, posv_ref, poss_ref, zb_ref, stv_ref, sts_ref, psem, rsem, hsem, zsem,
                  *, n_pages):
    tm = TOKEN_TILE
    tile_rows = tm * ROW_TILES
    b = pl.program_id(0)
    s = pl.program_id(1)
    n_s = pl.num_programs(1)
    g = b * n_s + s
    last = pl.num_programs(0) * n_s - 1
    par = g % 2
    q = 1 - par
    dyn_zero = lax.shift_right_arithmetic(g, 31)

    def hx_tile(p):
        return hx_ref.at[p, pl.ds(0, tile_rows)]

    def row_copy(p, t, k):
        return pltpu.make_async_copy(
            hx_ref.at[p, pl.ds(t * ROW_TILES, ROW_TILES)], xp_hbm.at[poss_ref[p, k, t]], rsem.at[p])

    def wait_row_copies(p):
        for _ in range(TOP_K):
            pltpu.make_async_copy(hx_tile(p), hx_tile(p), rsem.at[p]).wait()

    def h1_write(p, tile_index):
        off = pl.multiple_of(tile_index * tile_rows, tile_rows)
        return pltpu.make_async_copy(hx_tile(p), h1_hbm.at[pl.ds(off, tile_rows)], hsem.at[p])

    def pos_to_smem(p):
        return pltpu.make_async_copy(posv_ref.at[p], poss_ref.at[p], psem.at[p])

    @pl.when(s == 0)
    def _():
        pext_ref[0:POOL_HALO, :] = jnp.zeros((POOL_HALO, POOL_WIDTH), F32)
        uext_ref[0:CONV_HALO, :] = jnp.zeros((CONV_HALO, D_MODEL), F32)
        ct_ref[...] = jnp.zeros_like(ct_ref)
        n_ref[...] = jnp.zeros_like(n_ref)
        m_ref[...] = jnp.zeros_like(m_ref)
        pw_ref[:, 0:SUBLANES, :] = jnp.zeros((2, SUBLANES, POOL_GROUP), F32)

    @pl.when(g == 0)
    def _():
        run_ref[...] = jnp.zeros_like(run_ref)
        page_ref[...] = jnp.zeros_like(page_ref)
        free_ref[...] = jnp.zeros_like(free_ref)
        pe_ref[...] = jnp.zeros_like(pe_ref)
        zb_ref[...] = jnp.zeros_like(zb_ref)
        hx_ref[1] = jnp.zeros((tile_rows + SUBLANES, LANES), F32)
        for k in range(TOP_K):
            for t in range(tm):
                poss_ref[1, k, t] = n_pages * EXPERT_ROWS + k * tm + t

    @pl.when(g >= 1)
    def _():
        pos_to_smem(q).wait()
        wait_row_copies(par)

    @pl.when(g >= 2)
    def _():
        h1_write(par, 0).wait()

    group = tm // ROW_DMA_GROUPS

    def start_row_copies(grp):
        for t in range(grp * group, (grp + 1) * group):
            for k in range(TOP_K):
                row_copy(q, t, k).start(priority=k % DMA_PRIORITIES)
        spare = pl.multiple_of(tile_rows + dyn_zero * SUBLANES, SUBLANES)
        hx_ref[q, pl.ds(spare, SUBLANES), :] = jnp.full((SUBLANES, LANES), dyn_zero.astype(F32))
        return hx_ref[q, pl.ds(spare, SUBLANES), :][0:1, :]

    ln0b = ln0b_ref[...] + _tile_lanes(start_row_copies(0), ROW_TILES)
    h0 = _layer_norm(x_ref[...], ln0g_ref[...], ln0b)
    h0b = h0.astype(BF16)

    def proj(lo, hi):
        return _dot(h0b, win_ref[:, lo:hi])

    pext_ref[POOL_HALO:POOL_HALO + tm, :] = proj(COL_P, COL_U)
    tpos = s * tm + lax.broadcasted_iota(jnp.int32, (tm, 1), 0)
    groups = []
    end = POOL_HALO + tm
    for gi, w in enumerate(POOL_WINDOWS):
        cols = slice(gi * POOL_GROUP, (gi + 1) * POOL_GROUP)
        cur = pext_ref[POOL_HALO:end, cols]
        src, src_cols, k, nxt = pext_ref, cols, 1, 0
        while 2 * k < w:
            pw_ref[nxt, SUBLANES:end, :] = (src[SUBLANES:end, src_cols]
                                            + src[SUBLANES - k:end - k, src_cols])
            src, src_cols, k, nxt = pw_ref.at[nxt], slice(None), 2 * k, 1 - nxt
        acc = src[POOL_HALO:end, src_cols] + src[POOL_HALO - k:end - k, src_cols]
        inv_cnt = 1.0 / jnp.minimum(tpos + 1, w).astype(F32)
        groups.append(acc * inv_cnt - cur)
    pooled = jnp.concatenate(groups, axis=1)
    pext_ref[0:POOL_HALO, :] = pext_ref[tm:tm + POOL_HALO, :]
    bpool = bpool_ref[...] + _tile_lanes(start_row_copies(1), POOL_WIDTH // LANES)
    mixed = (_dot(pooled.astype(BF16), wpool_ref[...]) + bpool) * lspool_ref[...]
    y_pool = _dot(mixed.astype(BF16), wbp_ref[...])

    uext_ref[CONV_HALO:CONV_HALO + tm, :] = proj(COL_U, COL_V)
    conv = convb_ref[...] + _tile_lanes(start_row_copies(2), ROW_TILES)
    for j in range(CONV_WIDTH):
        off = CONV_HALO - (CONV_WIDTH - 1) + j
        conv = conv + convw_ref[j:j + 1, :] * uext_ref[off:off + tm, :]
    uext_ref[0:CONV_HALO, :] = uext_ref[tm:tm + CONV_HALO, :]
    ucb = (conv * _sigmoid(conv)).astype(BF16)
    vb = proj(COL_V, COL_O).astype(BF16)

    slab = proj(COL_IF, COL_IF_END) + (bif_ref[...] + start_row_copies(3))
    lane = lax.broadcasted_iota(jnp.int32, (tm, LANES), 1)
    is_f = jnp.logical_and(lane >= N_HEADS, lane < 2 * N_HEADS)
    slab = jnp.where(is_f, _log_sigmoid(slab), slab)
    row_i = lax.broadcasted_iota(jnp.int32, (tm, tm), 0)
    col_i = lax.broadcasted_iota(jnp.int32, (tm, tm), 1)
    causal = row_i >= col_i
    tri = jnp.where(causal, 1.0, 0.0).astype(BF16)
    hi, mid, lo = _split3(slab)
    bcol = _dot(tri, hi) + _dot(tri, mid) + _dot(tri, lo)
    slab_t = slab.T
    bcol_t = bcol.T

    heads = []
    for h in range(N_HEADS):
        hs = slice(h * HEAD_DIM, (h + 1) * HEAD_DIM)
        qk = _dot(ucb[:, hs], wqk_ref[h])
        q_h = qk[:, :HEAD_DIM]
        k_h = qk[:, HEAD_DIM:]
        qb = q_h.astype(BF16)
        kb = k_h.astype(BF16)
        vh = vb[:, hs]

        i_c = slab[:, h:h + 1]
        b_c = bcol[:, N_HEADS + h:N_HEADS + h + 1]
        i_r = slab_t[h:h + 1, :]
        b_r = bcol_t[N_HEADS + h:N_HEADS + h + 1, :]
        m_prev = m_ref[:, h:h + 1] + start_row_copies(4 + h)[:, 0:1]

        d_log = jnp.where(causal, b_c - (b_r - i_r), -jnp.inf)
        m_inter = b_c + m_prev
        m_t = jnp.maximum(m_inter, jnp.max(d_log, axis=1, keepdims=True))
        w_intra = jnp.exp(d_log - m_t)
        sc = _dot_nt(qb, kb) * w_intra
        w_inter = jnp.exp(m_inter - m_t)
        ctb = ct_ref[h].astype(BF16)
        num = _dot(sc.astype(BF16), vh) + w_inter * _dot(qb, ctb)
        qn = jnp.sum(q_h * n_ref[h], axis=1, keepdims=True)
        den = jnp.sum(sc, axis=1, keepdims=True) + w_inter * qn
        hh = num * (1.0 / jnp.maximum(jnp.abs(den), jnp.exp(-m_t)))
        mu = jnp.mean(hh, axis=1, keepdims=True)
        hc = hh - mu
        var = jnp.mean(hc * hc, axis=1, keepdims=True)
        heads.append(hc * lax.rsqrt(var + LN_EPS))

        g_last = b_r[:, tm - 1:tm]
        m_new = jnp.maximum(g_last + m_prev, jnp.max(g_last - b_r + i_r, axis=1, keepdims=True))
        decay = jnp.exp(g_last + m_prev - m_new)
        w_state = jnp.exp(g_last - b_c + i_c - m_new)
        kw = k_h * w_state
        ct_ref[h] = decay * ct_ref[h] + _dot(kw.T.astype(BF16), vh)
        n_ref[h] = decay * n_ref[h] + jnp.sum(kw, axis=0, keepdims=True)
        m_ref[:, h:h + 1] = m_new

    hn = jnp.concatenate(heads, axis=1) * gain_ref[...]
    h_out = _sigmoid(proj(COL_O, COL_IF)) * hn
    y_mlstm = _dot(h_out.astype(BF16), wbm_ref[...])

    merged = (_sigmoid(_dot(h0b, wgate_ref[:, 0:D_MODEL])) * y_pool
              + _sigmoid(_dot(h0b, wgate_ref[:, D_MODEL:2 * D_MODEL])) * y_mlstm)
    mix = _dot(merged.astype(BF16), wout_ref[...])
    h1 = _layer_norm(DEEPNORM_ALPHA * h0 + mix, ln1g_ref[...], ln1b_ref[...])
    for c in range(ROW_TILES):
        hx_ref[par, pl.ds(c, tm, stride=ROW_TILES), :] = h1[:, c * LANES:(c + 1) * LANES]
    h1_write(par, g).start()

    h1_hi = h1.astype(BF16)
    h1_lo = (h1 - h1_hi.astype(F32)).astype(BF16)
    la = _dot(h1_hi, wr_ref[...])
    lb = _dot(h1_lo, wr_ref[:, 0:LANES])
    logits = (la[:, 0:LANES] + la[:, LANES:2 * LANES] + lb).T[0:N_EXPERTS] + br_ref[...]
    e_iota = lax.broadcasted_iota(jnp.int32, (N_EXPERTS, tm), 0)
    vals, onehots = [], []
    lg = logits
    for _ in range(TOP_K):
        mx = jnp.max(lg, axis=0, keepdims=True)
        sel = jnp.min(jnp.where(lg == mx, e_iota, N_EXPERTS), axis=0, keepdims=True)
        oh = e_iota == sel
        lg = jnp.where(oh, -jnp.inf, lg)
        vals.append(mx)
        onehots.append(oh)
    exps = [jnp.exp(v - vals[0]) for v in vals]
    inv_den = 1.0 / (exps[0] + exps[1] + exps[2] + exps[3])
    gates = [e * inv_den for e in exps]

    oh_all = jnp.where(onehots[0], 1.0, 0.0)
    for oh in onehots[1:]:
        oh_all = oh_all + jnp.where(oh, 1.0, 0.0)
    strict = jnp.where(row_i < col_i, 1.0, 0.0).astype(BF16)
    run = run_ref[:, 0:1]
    rank = _dot(oh_all.astype(BF16), strict) + run
    count = jnp.sum(oh_all, axis=1, keepdims=True)
    inv_rows = 1.0 / EXPERT_ROWS
    pages_before = jnp.ceil(run * inv_rows)
    need = jnp.ceil((run + count) * inv_rows) - pages_before
    er = lax.broadcasted_iota(jnp.int32, (N_EXPERTS, N_EXPERTS), 0)
    ec = lax.broadcasted_iota(jnp.int32, (N_EXPERTS, N_EXPERTS), 1)
    earlier = jnp.where(er > ec, 1.0, 0.0).astype(BF16)
    need_b = jnp.broadcast_to(need, (N_EXPERTS, LANES)).astype(BF16)
    new_page = free_ref[0:1, 0:1] + _dot(earlier, need_b)[:, 0:1]
    page_seq = jnp.floor(rank * inv_rows)
    page = jnp.where(page_seq < pages_before, page_ref[:, 0:1], new_page)
    pos_all = page * EXPERT_ROWS + (rank - page_seq * EXPERT_ROWS)
    positions = [jnp.sum(jnp.where(oh, pos_all, 0.0), axis=0, keepdims=True) for oh in onehots]
    page_ref[...] = jnp.where(need > 0.0, new_page, page_ref[...])
    free_ref[...] = free_ref[...] + jnp.sum(need, axis=0, keepdims=True)
    run_ref[...] = run_ref[...] + count
    cnt_ref[...] = run_ref[...]
    p_lane = lax.broadcasted_iota(jnp.int32, (N_EXPERTS, pe_ref.shape[1]), 1).astype(F32)
    e_plus1 = (lax.broadcasted_iota(jnp.int32, (N_EXPERTS, 1), 0) + 1).astype(F32)
    taken = jnp.logical_and(p_lane == new_page, need > 0.0)
    pe_ref[...] = pe_ref[...] + jnp.sum(jnp.where(taken, e_plus1, 0.0), axis=0, keepdims=True)

    r8 = lax.broadcasted_iota(jnp.int32, (SUBLANES, tm), 0)
    pos_out = jnp.zeros((SUBLANES, tm), jnp.int32)
    r128 = lax.broadcasted_iota(jnp.int32, (LANES, tm), 0)
    gate_rows = jnp.zeros((LANES, tm), F32)
    for kk in range(TOP_K):
        pos_out = jnp.where(r8 == kk, positions[kk].astype(jnp.int32), pos_out)
        gate_rows = jnp.where(r128 == kk, gates[kk], gate_rows)
    pos_ref[...] = pos_out
    gcol_ref[...] = gate_rows.T
    posv_ref[par] = pos_out
    pos_to_smem(par).start()

    @pl.when(g == last)
    def _():
        pos_to_smem(par).wait()
        for t in range(tm):
            for k in range(TOP_K):
                row_copy(par, t, k).start(priority=k % DMA_PRIORITIES)
        wait_row_copies(q)
        wait_row_copies(par)
        h1_write(q, 0).wait()
        h1_write(par, 0).wait()

        st_lane = lax.broadcasted_iota(jnp.int32, (N_EXPERTS, LANES), 1)
        state = jnp.where(st_lane == 0, page_ref[...], jnp.where(st_lane == 1, run_ref[...], free_ref[...]))
        stv_ref[...] = state.astype(jnp.int32)
        state_copy = pltpu.make_async_copy(stv_ref, sts_ref, psem.at[par])
        state_copy.start()
        state_copy.wait()

        def zero_fill(wait):
            def act(rows_dst, n):
                cp = pltpu.make_async_copy(zb_ref.at[pl.ds(0, n)], xp_hbm.at[pl.ds(rows_dst, n)], zsem)
                if wait:
                    cp.wait()
                else:
                    cp.start()

            def unused_page(p, carry):
                act(pl.multiple_of(p * EXPERT_ROWS, EXPERT_ROWS), EXPERT_ROWS)
                return carry

            lax.fori_loop(sts_ref[0, 2], n_pages, unused_page, 0)

            def page_tail(e, carry):
                filled = sts_ref[e, 1] & (EXPERT_ROWS - 1)
                pad = jnp.where(filled == 0, 0, EXPERT_ROWS - filled)
                dst = sts_ref[e, 0] * EXPERT_ROWS + filled
                size = 1
                while size < EXPERT_ROWS:
                    @pl.when((pad & size) != 0)
                    def _(dst=dst, size=size):
                        act(dst, size)
                    dst = dst + (pad & size)
                    size *= 2
                return carry

            lax.fori_loop(0, N_EXPERTS, page_tail, 0)

        zero_fill(wait=False)
        zero_fill(wait=True)


def _const_spec(shape):
    zeros = (0,) * len(shape)
    return pl.BlockSpec(shape, lambda b, s: zeros, pipeline_mode=pl.Buffered(1))


def _mixer_call(x, weights, n_pages):
    bsz, seq, _ = x.shape
    tm = TOKEN_TILE
    n_s = seq // tm
    t_total = bsz * seq
    assert bsz * n_s >= 2
    tile = lambda b, s: (b * n_s + s)
    spare_pages = TOP_K * tm // EXPERT_ROWS
    pe_lanes = -(-n_pages // LANES) * LANES
    in_specs = [pl.BlockSpec((None, tm, D_MODEL), lambda b, s: (b, s, 0))]
    in_specs += [_const_spec(w.shape) for w in weights]
    out_shape = (
        jax.ShapeDtypeStruct((t_total * ROW_TILES, LANES), F32),
        jax.ShapeDtypeStruct(((n_pages + spare_pages) * EXPERT_ROWS, ROW_TILES, LANES), F32),
        jax.ShapeDtypeStruct((SUBLANES, t_total), jnp.int32),
        jax.ShapeDtypeStruct((t_total, LANES), F32),
        jax.ShapeDtypeStruct((N_EXPERTS, LANES), F32),
        jax.ShapeDtypeStruct((SUBLANES, pe_lanes), F32),
    )
    out_specs = (
        pl.BlockSpec(memory_space=pl.ANY),
        pl.BlockSpec(memory_space=pl.ANY),
        pl.BlockSpec((SUBLANES, tm), lambda b, s: (0, tile(b, s))),
        pl.BlockSpec((tm, LANES), lambda b, s: (tile(b, s), 0)),
        pl.BlockSpec((N_EXPERTS, LANES), lambda b, s: (0, 0)),
        pl.BlockSpec((SUBLANES, pe_lanes), lambda b, s: (0, 0)),
    )
    scratch = [
        pltpu.VMEM((POOL_HALO + tm, POOL_WIDTH), F32),
        pltpu.VMEM((2, POOL_HALO + tm, POOL_GROUP), F32),
        pltpu.VMEM((CONV_HALO + tm, D_MODEL), F32),
        pltpu.VMEM((N_HEADS, HEAD_DIM, HEAD_DIM), F32),
        pltpu.VMEM((N_HEADS, 1, HEAD_DIM), F32),
        pltpu.VMEM((1, LANES), F32),
        pltpu.VMEM((N_EXPERTS, LANES), F32),
        pltpu.VMEM((N_EXPERTS, LANES), F32),
        pltpu.VMEM((1, LANES), F32),
        pltpu.VMEM((2, tm * ROW_TILES + SUBLANES, LANES), F32),
        pltpu.VMEM((2, SUBLANES, tm), jnp.int32),
        pltpu.SMEM((2, SUBLANES, tm), jnp.int32),
        pltpu.VMEM((EXPERT_ROWS, ROW_TILES, LANES), F32),
        pltpu.VMEM((N_EXPERTS, LANES), jnp.int32),
        pltpu.SMEM((N_EXPERTS, LANES), jnp.int32),
        pltpu.SemaphoreType.DMA((2,)),
        pltpu.SemaphoreType.DMA((2,)),
        pltpu.SemaphoreType.DMA((2,)),
        pltpu.SemaphoreType.DMA(()),
    ]
    return pl.pallas_call(
        functools.partial(_mixer_kernel, n_pages=n_pages),
        grid=(bsz, n_s),
        in_specs=in_specs,
        out_specs=out_specs,
        out_shape=out_shape,
        scratch_shapes=scratch,
        compiler_params=pltpu.CompilerParams(
            dimension_semantics=("arbitrary", "arbitrary"), vmem_limit_bytes=VMEM_LIMIT),
        name="mixer",
    )(x, *weights)


def _experts_kernel(xpage_ref, spage_ref, be_ref, first_ref, nexte_ref, nused_ref,
                    slot_p_ref, slot_a_ref, slot_b_ref, xa_ref, xb_ref,
                    w1_hbm, b1_ref, w2_hbm, b2_ref, y_hbm,
                    yb0_ref, yb1_ref, w1s_ref, w2s_ref, w1b_ref, w2z_ref, w2b_ref, ssem, wsem):
    del xpage_ref, spage_ref
    rows = EXPERT_ROWS
    tile = rows * ROW_TILES
    i = pl.program_id(0)
    last = pl.num_programs(0) - 1
    n_used = nused_ref[0]
    yb = (yb0_ref, yb1_ref)
    x_in = (xa_ref, xb_ref)
    dyn_zero = lax.shift_right_arithmetic(n_used, 31)
    n_chunks = D_FF // (2 * LANES)
    assert SCATTER_STAGES <= n_chunks + 1

    def scatter(slots, r, p):
        return pltpu.make_async_copy(
            yb[p].at[pl.ds(r * ROW_TILES, ROW_TILES)], y_hbm.at[slots[0, 0, r]], ssem.at[p])

    def start_scatters(stage, slots, p):
        if stage is None:
            r_range = range(rows)
        elif stage < SCATTER_STAGES:
            r_range = range(rows * stage // SCATTER_STAGES, rows * (stage + 1) // SCATTER_STAGES)
        else:
            return jnp.zeros((1, LANES), F32)
        for r in r_range:
            scatter(slots, r, p).start(priority=r % DMA_PRIORITIES)
        spare = pl.multiple_of(tile + dyn_zero * SUBLANES, SUBLANES)
        yb[p][pl.ds(spare, SUBLANES), :] = jnp.full((SUBLANES, LANES), dyn_zero.astype(F32))
        return yb[p][pl.ds(spare, SUBLANES), :][0:1, :]

    def wait_scatters(p):
        view = yb[p].at[pl.ds(0, tile)]
        pltpu.make_async_copy(view, view, ssem.at[p]).wait()

    def weight_copies(e):
        return (pltpu.make_async_copy(w1_hbm.at[e], w1s_ref, wsem.at[0]),
                pltpu.make_async_copy(w2_hbm.at[e], w2s_ref, wsem.at[1]))

    def switch_weights(blk):
        @pl.when(first_ref[blk] == 1)
        def _():
            for cp in weight_copies(0):
                cp.wait()
            step = 128
            for c in range(D_MODEL // step):
                w1b_ref[c * step:(c + 1) * step, :] = w1s_ref[c * step:(c + 1) * step, :].astype(BF16)
            half = D_FF // 2
            for c in range(ROW_TILES):
                cs = slice(c * LANES, (c + 1) * LANES)
                w2z_ref[c, pl.ds(0, half, stride=2), :] = w2s_ref[0:half, cs]
                w2z_ref[c, pl.ds(1, half, stride=2), :] = w2s_ref[half:D_FF, cs]
                w2b_ref[:, cs] = w2z_ref[c].astype(BF16)
            nxt = nexte_ref[blk]

            @pl.when(nxt >= 0)
            def _():
                for cp in weight_copies(nxt):
                    cp.start()

    def compute(p, blk, start_stage):
        e = be_ref[blk]
        x = jnp.concatenate(
            [x_in[p][pl.ds(c, rows, stride=ROW_TILES), :] for c in range(ROW_TILES)], axis=1).astype(BF16)
        b1 = b1_ref[pl.ds(e, 1), :]
        width = 2 * LANES
        even = (lax.broadcasted_iota(jnp.int32, (rows, width), 1) & 1) == 0
        zs = []
        for c in range(n_chunks):
            lo = c * width
            hi = D_FF + lo
            zero = _tile_lanes(start_stage(c), 2)
            ha = _dot(x, w1b_ref[:, lo:lo + width]) + (b1[:, lo:lo + width] + zero)
            hb = _dot(x, w1b_ref[:, hi:hi + width]) + (b1[:, hi:hi + width] + zero)
            glu = jnp.where(even, ha, pltpu.roll(hb, 1, 1))
            lin = jnp.where(even, pltpu.roll(ha, width - 1, 1), hb)
            glu = jnp.minimum(glu, SWIGLU_LIMIT)
            lin = jnp.clip(lin, -SWIGLU_LIMIT, SWIGLU_LIMIT) + 1.0
            zs.append((glu * _sigmoid(SWIGLU_ALPHA * glu) * lin).astype(BF16))
        b2 = b2_ref[pl.ds(e, 1), :] + _tile_lanes(start_stage(n_chunks), ROW_TILES)
        y = _dot(jnp.concatenate(zs, axis=1), w2b_ref[...]) + b2
        for c in range(ROW_TILES):
            yb[p][pl.ds(c, rows, stride=ROW_TILES), :] = y[:, c * LANES:(c + 1) * LANES]

    def phase(blk, p, start_stage):
        switch_weights(blk)

        @pl.when(blk < n_used)
        def _():
            compute(p, blk, start_stage)

        @pl.when(blk >= n_used)
        def _():
            start_stage(None)

    @pl.when(i == 0)
    def _():
        yb1_ref[...] = jnp.zeros_like(yb1_ref)
        for cp in weight_copies(be_ref[0]):
            cp.start()

    @pl.when(i >= 1)
    def _():
        wait_scatters(0)

    phase(2 * i, 0, functools.partial(start_scatters, slots=slot_p_ref, p=1))

    wait_scatters(1)
    phase(2 * i + 1, 1, functools.partial(start_scatters, slots=slot_a_ref, p=0))

    @pl.when(i == last)
    def _():
        wait_scatters(0)
        for r in range(rows):
            scatter(slot_b_ref, r, 1).start(priority=r % DMA_PRIORITIES)
        wait_scatters(1)


def _experts_call(plan, row_slot, x_pages_2d, w_e1, b_e1, w_e2, b_e2, n_tokens):
    xpage, spage, block_e, first, next_e, n_used = plan
    rows = EXPERT_ROWS
    n_blocks = block_e.shape[0]
    assert n_blocks % 2 == 0
    n_slots = TOP_K * n_tokens + rows
    tile_rows = rows * ROW_TILES
    filler = row_slot.shape[0] - 1
    smem_rows = lambda index: pl.BlockSpec((1, 1, rows), index, memory_space=pltpu.SMEM)
    whole = lambda shape: pl.BlockSpec(shape, lambda i, *_: (0,) * len(shape))
    grid_spec = pltpu.PrefetchScalarGridSpec(
        num_scalar_prefetch=6,
        grid=(n_blocks // 2,),
        in_specs=[
            smem_rows(lambda i, xp, sp, *_: (jnp.where(i == 0, filler, sp[jnp.maximum(2 * i - 1, 0)]), 0, 0)),
            smem_rows(lambda i, xp, sp, *_: (sp[2 * i], 0, 0)),
            smem_rows(lambda i, xp, sp, *_: (sp[2 * i + 1], 0, 0)),
            pl.BlockSpec((tile_rows, LANES), lambda i, xp, *_: (xp[2 * i], 0)),
            pl.BlockSpec((tile_rows, LANES), lambda i, xp, *_: (xp[2 * i + 1], 0)),
            pl.BlockSpec(memory_space=pl.ANY),
            whole((N_EXPERTS, 2 * D_FF)),
            pl.BlockSpec(memory_space=pl.ANY),
            whole((N_EXPERTS, D_MODEL)),
        ],
        out_specs=pl.BlockSpec(memory_space=pl.ANY),
        scratch_shapes=[
            pltpu.VMEM((tile_rows + SUBLANES, LANES), F32),
            pltpu.VMEM((tile_rows + SUBLANES, LANES), F32),
            pltpu.VMEM((D_MODEL, 2 * D_FF), F32),
            pltpu.VMEM((D_FF, D_MODEL), F32),
            pltpu.VMEM((D_MODEL, 2 * D_FF), BF16),
            pltpu.VMEM((ROW_TILES, D_FF, LANES), F32),
            pltpu.VMEM((D_FF, D_MODEL), BF16),
            pltpu.SemaphoreType.DMA((2,)),
            pltpu.SemaphoreType.DMA((2,)),
        ],
    )
    return pl.pallas_call(
        _experts_kernel,
        grid_spec=grid_spec,
        out_shape=jax.ShapeDtypeStruct((n_slots, ROW_TILES, LANES), F32),
        compiler_params=pltpu.CompilerParams(
            dimension_semantics=("arbitrary",), vmem_limit_bytes=VMEM_LIMIT),
        name="experts",
    )(xpage, spage, block_e, first, next_e, n_used,
      row_slot, row_slot, row_slot, x_pages_2d, x_pages_2d, w_e1, b_e1, w_e2, b_e2)


def _combine_kernel(h1_ref, y0_ref, y1_ref, y2_ref, y3_ref, gcol_ref, g_ref, b_ref, out_ref):
    tc = COMBINE_TILE
    gcol = gcol_ref[...]
    y_refs = (y0_ref, y1_ref, y2_ref, y3_ref)
    chunks = []
    for c in range(ROW_TILES):
        z = DEEPNORM_ALPHA * h1_ref[pl.ds(c, tc, stride=ROW_TILES), :]
        for kk in range(TOP_K):
            z = z + gcol[:, kk:kk + 1] * y_refs[kk][pl.ds(c, tc, stride=ROW_TILES), :]
        chunks.append(z)
    total = chunks[0].sum(axis=1, keepdims=True)
    for z in chunks[1:]:
        total = total + z.sum(axis=1, keepdims=True)
    mu = total * (1.0 / D_MODEL)
    sq = None
    for z in chunks:
        zc = z - mu
        part = (zc * zc).sum(axis=1, keepdims=True)
        sq = part if sq is None else sq + part
    inv = lax.rsqrt(sq * (1.0 / D_MODEL) + LN_EPS)
    for c, z in enumerate(chunks):
        cs = slice(c * LANES, (c + 1) * LANES)
        out_ref[:, cs] = (z - mu) * inv * g_ref[:, cs] + b_ref[:, cs]


def _combine_call(h1_2d, y_2d, gcol, ln_g, ln_b, n_tokens):
    tc = COMBINE_TILE
    n_t = n_tokens // tc
    blk = tc * ROW_TILES
    y_spec = lambda kk: pl.BlockSpec((blk, LANES), lambda i: (kk * n_t + i, 0))
    return pl.pallas_call(
        _combine_kernel,
        grid=(n_t,),
        in_specs=[
            pl.BlockSpec((blk, LANES), lambda i: (i, 0)),
            y_spec(0), y_spec(1), y_spec(2), y_spec(3),
            pl.BlockSpec((tc, LANES), lambda i: (i, 0)),
            pl.BlockSpec((1, D_MODEL), lambda i: (0, 0)),
            pl.BlockSpec((1, D_MODEL), lambda i: (0, 0)),
        ],
        out_specs=pl.BlockSpec((tc, D_MODEL), lambda i: (i, 0)),
        out_shape=jax.ShapeDtypeStruct((n_tokens, D_MODEL), F32),
        compiler_params=pltpu.CompilerParams(
            dimension_semantics=("arbitrary",), vmem_limit_bytes=VMEM_LIMIT),
        name="combine",
    )(h1_2d, y_2d, y_2d, y_2d, y_2d, gcol, ln_g, ln_b)


def _prepare_mixer_weights(ln0_g, ln0_b, w_in, conv_w, conv_b, w_q, w_k, b_if, mh_gain, w_pool,
                           b_pool, ls_pool, w_branch_pool, w_branch_mlstm, w_out, ln1_g, ln1_b,
                           w_router, b_router):
    w_main = w_in[:, :COL_IF_END].astype(BF16)
    w_gates = w_in[:, COL_GATES:COL_GATES + 2 * D_MODEL].astype(BF16)
    bif = jnp.concatenate([b_if, jnp.zeros((LANES - 2 * N_HEADS,), F32)]).reshape(1, LANES)
    wqk = jnp.concatenate([w_q * (HEAD_DIM ** -0.5), w_k], axis=-1).astype(BF16)
    wpool_bd = jax.scipy.linalg.block_diag(*[w_pool[g] for g in range(len(POOL_WINDOWS))]).astype(BF16)
    wr_hi = w_router.astype(BF16)
    wr_lo = (w_router - wr_hi.astype(F32)).astype(BF16)
    lane_pad = jnp.zeros((D_MODEL, LANES - N_EXPERTS), BF16)
    wr_hi_lo = jnp.concatenate([wr_hi, lane_pad, wr_lo, lane_pad], axis=1)
    row = lambda v: v.reshape(1, -1)
    return (row(ln0_g), row(ln0_b), w_main, w_gates, bif, conv_w, row(conv_b), wqk, row(mh_gain), wpool_bd,
            row(b_pool), row(ls_pool), w_branch_pool.astype(BF16), w_branch_mlstm.astype(BF16),
            w_out.astype(BF16), row(ln1_g), row(ln1_b), wr_hi_lo,
            b_router.reshape(N_EXPERTS, 1))


def _plan(page_expert1, counts, pos, n_tokens, n_pages, n_rows_total):
    rows = EXPERT_ROWS
    n_assign = TOP_K * n_tokens
    i32 = jnp.int32
    slots = jnp.arange(n_assign, dtype=i32)
    hit = jnp.zeros((n_rows_total,), i32).at[pos.reshape(-1)].add(slots + 1)
    pad_slot = n_assign + jnp.arange(n_rows_total + rows, dtype=i32) % rows
    row_slot = jnp.where(jnp.concatenate([hit, jnp.zeros((rows,), i32)]) == 0, pad_slot,
                         jnp.concatenate([hit, jnp.zeros((rows,), i32)]) - 1)
    row_slot = row_slot.reshape(n_rows_total // rows + 1, 1, rows)
    filler = n_rows_total // rows

    pidx = jnp.arange(n_pages, dtype=i32)
    used = page_expert1 > 0
    n_used = jnp.sum(used.astype(i32))
    page_e = page_expert1 - 1
    key = jnp.where(used, page_e, N_EXPERTS) * n_pages + pidx
    place = jnp.sum((key[None, :] < key[:, None]).astype(i32), axis=1)
    at_block = place[None, :] == pidx[:, None]
    order = jnp.sum(jnp.where(at_block, pidx[None, :], 0), axis=1)
    block_e = jnp.clip(jnp.sum(jnp.where(at_block, page_e[None, :], 0), axis=1), 0, N_EXPERTS - 1)
    valid = pidx < n_used
    xpage = jnp.where(valid, order, 0)
    spage = jnp.where(valid, order, filler)
    prev_e = jnp.concatenate([jnp.full((1,), -1, i32), block_e[:-1]])
    first = jnp.logical_and(valid, block_e != prev_e).astype(i32)
    e_iota = jnp.arange(N_EXPERTS, dtype=i32)
    pages_per_e = (counts + rows - 1) // rows
    seg_end = jnp.cumsum(pages_per_e)
    mine = block_e[:, None] == e_iota[None, :]
    end_blk = jnp.sum(jnp.where(mine, seg_end[None, :], 0), axis=1)
    e_at_end = jnp.sum(jnp.where(pidx[None, :] == end_blk[:, None], block_e[None, :], 0), axis=1)
    next_e = jnp.where(end_blk < n_used, e_at_end, -1).astype(i32)
    plan = (xpage.astype(i32), spage.astype(i32), block_e.astype(i32), first, next_e,
            n_used.reshape(1).astype(i32))
    return plan, row_slot


def kernel(x, ln0_g, ln0_b, w_in, conv_w, conv_b, w_q, w_k, b_if, mh_gain, w_pool, b_pool, ls_pool,
           w_branch_pool, w_branch_mlstm, w_out, ln1_g, ln1_b, w_router, b_router, w_e1, b_e1,
           w_e2, b_e2, ln2_g, ln2_b):
    bsz, seq, _ = x.shape
    n_tokens = bsz * seq
    rows = EXPERT_ROWS
    assert w_in.shape[0] == 1, "single-layer trunk"
    assert seq % TOKEN_TILE == 0 and n_tokens % COMBINE_TILE == 0
    n_pages = (TOP_K * n_tokens + N_EXPERTS * (rows - 1) + rows - 1) // rows
    n_pages += n_pages % 2
    weights = _prepare_mixer_weights(
        ln0_g, ln0_b, w_in[0], conv_w[0], conv_b[0], w_q[0], w_k[0], b_if[0], mh_gain[0], w_pool[0],
        b_pool[0], ls_pool[0], w_branch_pool[0], w_branch_mlstm[0], w_out[0], ln1_g[0], ln1_b[0],
        w_router[0], b_router[0])
    h1_2d, x_pages, pos8, gcol, cnt, pe = _mixer_call(x, weights, n_pages)
    counts = cnt[:, 0].astype(jnp.int32)
    page_expert1 = pe[0, :n_pages].astype(jnp.int32)
    plan, row_slot = _plan(page_expert1, counts, pos8[:TOP_K], n_tokens, n_pages, x_pages.shape[0])
    y_slots = _experts_call(
        plan, row_slot, x_pages.reshape(-1, LANES), w_e1[0], b_e1[0], w_e2[0], b_e2[0], n_tokens)
    out = _combine_call(h1_2d, y_slots.reshape(-1, LANES), gcol, ln2_g[0].reshape(1, D_MODEL),
                        ln2_b[0].reshape(1, D_MODEL), n_tokens)
    return out.reshape(bsz, seq, D_MODEL)
```

```python
import functools

import jax
import jax.numpy as jnp
from jax import lax
from jax.experimental import pallas as pl
from jax.experimental.pallas import tpu as pltpu

F32 = jnp.float32
BF16 = jnp.bfloat16

D_MODEL = 1024
N_HEADS = 4
HEAD_DIM = 256
POOL_WIDTH = 512
POOL_GROUP = 128
POOL_WINDOWS = (2, 4, 8, 16)
CONV_WIDTH = 4
N_EXPERTS = 32
TOP_K = 4
D_FF = 1024
SWIGLU_ALPHA = 1.702
SWIGLU_LIMIT = 7.0
LN_EPS = 1e-5
DEEPNORM_ALPHA = 2.0 ** 0.25

SUBLANES = 8
LANES = 128
ROW_TILES = D_MODEL // LANES

COL_P = 0
COL_U = COL_P + POOL_WIDTH
COL_V = COL_U + D_MODEL
COL_O = COL_V + D_MODEL
COL_IF = COL_O + D_MODEL
COL_IF_END = COL_IF + LANES
COL_GATES = COL_IF + 2 * N_HEADS

TOKEN_TILE = 256
POOL_HALO = 24
CONV_HALO = 8
EXPERT_ROWS = 256
ROW_DMA_GROUPS = 12
SCATTER_STAGES = 3
EXPERT_CHUNK = 2 * LANES
COMBINE_TILE = 512
DMA_PRIORITIES = 2
VMEM_LIMIT = 56 * 1024 * 1024

assert TOKEN_TILE <= EXPERT_ROWS
assert (TOP_K * TOKEN_TILE) % EXPERT_ROWS == 0


def _layer_norm(x, g, b):
    mu = jnp.mean(x, axis=-1, keepdims=True)
    xc = x - mu
    var = jnp.mean(xc * xc, axis=-1, keepdims=True)
    return xc * lax.rsqrt(var + LN_EPS) * g + b


def _sigmoid(x):
    return 0.5 * jnp.tanh(0.5 * x) + 0.5


def _log_sigmoid(x):
    return -(jnp.maximum(-x, 0.0) + jnp.log(1.0 + jnp.exp(-jnp.abs(x))))


def _split3(x):
    hi = x.astype(BF16)
    r1 = x - hi.astype(F32)
    mid = r1.astype(BF16)
    lo = (r1 - mid.astype(F32)).astype(BF16)
    return hi, mid, lo


def _dot(a, b):
    return jnp.dot(a, b, preferred_element_type=F32)


def _dot_nt(a, b):
    return lax.dot_general(a, b, (((1,), (1,)), ((), ())), preferred_element_type=F32)


def _tile_lanes(row, n):
    return jnp.concatenate([row] * n, axis=1)


def _mixer_kernel(x_ref, ln0g_ref, ln0b_ref, win_ref, wgate_ref, bif_ref, convw_ref, convb_ref, wqk_ref,
                  gain_ref, wpool_ref, bpool_ref, lspool_ref, wbp_ref, wbm_ref, wout_ref,
                  ln1g_ref, ln1b_ref, wr_ref, br_ref,
                  h1_hbm, xp_hbm, pos_ref, gcol_ref, cnt_ref, pe_ref,
                  pext_ref, pw_ref, uext_ref, ct_ref, n_ref, m_ref, run_ref, page_ref, free_ref,
                  hx_ref, posv_ref, poss_ref, zb_ref, stv_ref, sts_ref, psem, rsem, hsem, zsem,
                  *, n_pages):
    tm = TOKEN_TILE
    tile_rows = tm * ROW_TILES
    b = pl.program_id(0)
    s = pl.program_id(1)
    n_s = pl.num_programs(1)
    g = b * n_s + s
    last = pl.num_programs(0) * n_s - 1
    par = g % 2
    q = 1 - par
    dyn_zero = lax.shift_right_arithmetic(g, 31)

    def hx_tile(p):
        return hx_ref.at[p, pl.ds(0, tile_rows)]

    def row_copy(p, t, k):
        return pltpu.make_async_copy(
            hx_ref.at[p, pl.ds(t * ROW_TILES, ROW_TILES)], xp_hbm.at[poss_ref[p, k, t]], rsem.at[p])

    def wait_row_copies(p):
        for _ in range(TOP_K):
            pltpu.make_async_copy(hx_tile(p), hx_tile(p), rsem.at[p]).wait()

    def h1_write(p, tile_index):
        off = pl.multiple_of(tile_index * tile_rows, tile_rows)
        return pltpu.make_async_copy(hx_tile(p), h1_hbm.at[pl.ds(off, tile_rows)], hsem.at[p])

    def pos_to_smem(p):
        return pltpu.make_async_copy(posv_ref.at[p], poss_ref.at[p], psem.at[p])

    @pl.when(s == 0)
    def _():
        pext_ref[0:POOL_HALO, :] = jnp.zeros((POOL_HALO, POOL_WIDTH), F32)
        uext_ref[0:CONV_HALO, :] = jnp.zeros((CONV_HALO, D_MODEL), F32)
        ct_ref[...] = jnp.zeros_like(ct_ref)
        n_ref[...] = jnp.zeros_like(n_ref)
        m_ref[...] = jnp.zeros_like(m_ref)
        pw_ref[:, 0:SUBLANES, :] = jnp.zeros((2, SUBLANES, POOL_GROUP), F32)

    @pl.when(g == 0)
    def _():
        run_ref[...] = jnp.zeros_like(run_ref)
        page_ref[...] = jnp.zeros_like(page_ref)
        free_ref[...] = jnp.zeros_like(free_ref)
        pe_ref[...] = jnp.zeros_like(pe_ref)
        zb_ref[...] = jnp.zeros_like(zb_ref)
        hx_ref[1] = jnp.zeros((tile_rows + SUBLANES, LANES), F32)
        for k in range(TOP_K):
            for t in range(tm):
                poss_ref[1, k, t] = n_pages * EXPERT_ROWS + k * tm + t

    @pl.when(g >= 1)
    def _():
        pos_to_smem(q).wait()
        wait_row_copies(par)

    @pl.when(g >= 2)
    def _():
        h1_write(par, 0).wait()

    def start_row_copies(grp):
        for t in range(tm * grp // ROW_DMA_GROUPS, tm * (grp + 1) // ROW_DMA_GROUPS):
            for k in range(TOP_K):
                row_copy(q, t, k).start(priority=k % DMA_PRIORITIES)
        spare = pl.multiple_of(tile_rows + dyn_zero * SUBLANES, SUBLANES)
        hx_ref[q, pl.ds(spare, SUBLANES), :] = jnp.full((SUBLANES, LANES), dyn_zero.astype(F32))
        return hx_ref[q, pl.ds(spare, SUBLANES), :][0:1, :]

    ln0b = ln0b_ref[...] + _tile_lanes(start_row_copies(0), ROW_TILES)
    h0 = _layer_norm(x_ref[...], ln0g_ref[...], ln0b)
    h0b = h0.astype(BF16)

    def proj(lo, hi):
        return _dot(h0b, win_ref[:, lo:hi])

    pext_ref[POOL_HALO:POOL_HALO + tm, :] = proj(COL_P, COL_U)
    tpos = s * tm + lax.broadcasted_iota(jnp.int32, (tm, 1), 0)
    groups = []
    end = POOL_HALO + tm
    for gi, w in enumerate(POOL_WINDOWS):
        cols = slice(gi * POOL_GROUP, (gi + 1) * POOL_GROUP)
        cur = pext_ref[POOL_HALO:end, cols]
        src, src_cols, k, nxt = pext_ref, cols, 1, 0
        while 2 * k < w:
            pw_ref[nxt, SUBLANES:end, :] = (src[SUBLANES:end, src_cols]
                                            + src[SUBLANES - k:end - k, src_cols])
            src, src_cols, k, nxt = pw_ref.at[nxt], slice(None), 2 * k, 1 - nxt
        acc = src[POOL_HALO:end, src_cols] + src[POOL_HALO - k:end - k, src_cols]
        inv_cnt = 1.0 / jnp.minimum(tpos + 1, w).astype(F32)
        groups.append(acc * inv_cnt - cur)
    pooled = jnp.concatenate(groups, axis=1)
    pext_ref[0:POOL_HALO, :] = pext_ref[tm:tm + POOL_HALO, :]
    bpool = bpool_ref[...] + _tile_lanes(start_row_copies(1), POOL_WIDTH // LANES)
    lspool = lspool_ref[...] + _tile_lanes(start_row_copies(8), POOL_WIDTH // LANES)
    mixed = (_dot(pooled.astype(BF16), wpool_ref[...]) + bpool) * lspool
    y_pool = _dot(mixed.astype(BF16), wbp_ref[...])

    uext_ref[CONV_HALO:CONV_HALO + tm, :] = proj(COL_U, COL_V)
    conv = convb_ref[...] + _tile_lanes(start_row_copies(2), ROW_TILES)
    for j in range(CONV_WIDTH):
        off = CONV_HALO - (CONV_WIDTH - 1) + j
        conv = conv + convw_ref[j:j + 1, :] * uext_ref[off:off + tm, :]
    uext_ref[0:CONV_HALO, :] = uext_ref[tm:tm + CONV_HALO, :]
    ucb = (conv * _sigmoid(conv)).astype(BF16)
    vb = proj(COL_V, COL_O).astype(BF16)

    slab = proj(COL_IF, COL_IF_END) + (bif_ref[...] + start_row_copies(3))
    lane = lax.broadcasted_iota(jnp.int32, (tm, LANES), 1)
    is_f = jnp.logical_and(lane >= N_HEADS, lane < 2 * N_HEADS)
    slab = jnp.where(is_f, _log_sigmoid(slab), slab)
    row_i = lax.broadcasted_iota(jnp.int32, (tm, tm), 0)
    col_i = lax.broadcasted_iota(jnp.int32, (tm, tm), 1)
    causal = row_i >= col_i
    tri = jnp.where(causal, 1.0, 0.0).astype(BF16)
    hi, mid, lo = _split3(slab)
    bcol = _dot(tri, hi) + _dot(tri, mid) + _dot(tri, lo)
    slab_t = slab.T
    bcol_t = bcol.T

    heads = []
    for h in range(N_HEADS):
        hs = slice(h * HEAD_DIM, (h + 1) * HEAD_DIM)
        qk = _dot(ucb[:, hs], wqk_ref[h])
        q_h = qk[:, :HEAD_DIM]
        k_h = qk[:, HEAD_DIM:]
        qb = q_h.astype(BF16)
        kb = k_h.astype(BF16)
        vh = vb[:, hs]

        i_c = slab[:, h:h + 1]
        b_c = bcol[:, N_HEADS + h:N_HEADS + h + 1]
        i_r = slab_t[h:h + 1, :]
        b_r = bcol_t[N_HEADS + h:N_HEADS + h + 1, :]
        m_prev = m_ref[:, h:h + 1] + start_row_copies(4 + h)[:, 0:1]

        d_log = jnp.where(causal, b_c - (b_r - i_r), -jnp.inf)
        m_inter = b_c + m_prev
        m_t = jnp.maximum(m_inter, jnp.max(d_log, axis=1, keepdims=True))
        w_intra = jnp.exp(d_log - m_t)
        sc = _dot_nt(qb, kb) * w_intra
        w_inter = jnp.exp(m_inter - m_t)
        ctb = ct_ref[h].astype(BF16)
        num = _dot(sc.astype(BF16), vh) + w_inter * _dot(qb, ctb)
        qn = jnp.sum(q_h * n_ref[h], axis=1, keepdims=True)
        den = jnp.sum(sc, axis=1, keepdims=True) + w_inter * qn
        hh = num * (1.0 / jnp.maximum(jnp.abs(den), jnp.exp(-m_t)))
        mu = jnp.mean(hh, axis=1, keepdims=True)
        hc = hh - mu
        var = jnp.mean(hc * hc, axis=1, keepdims=True)
        heads.append(hc * lax.rsqrt(var + LN_EPS))

        g_last = b_r[:, tm - 1:tm]
        m_new = jnp.maximum(g_last + m_prev, jnp.max(g_last - b_r + i_r, axis=1, keepdims=True))
        decay = jnp.exp(g_last + m_prev - m_new)
        w_state = jnp.exp(g_last - b_c + i_c - m_new)
        kw = k_h * w_state
        ct_ref[h] = decay * ct_ref[h] + _dot(kw.T.astype(BF16), vh)
        n_ref[h] = decay * n_ref[h] + jnp.sum(kw, axis=0, keepdims=True)
        m_ref[:, h:h + 1] = m_new

    hn = jnp.concatenate(heads, axis=1) * (gain_ref[...] + _tile_lanes(start_row_copies(9), ROW_TILES))
    h_out = _sigmoid(proj(COL_O, COL_IF)) * hn
    y_mlstm = _dot(h_out.astype(BF16), wbm_ref[...])

    merged = (_sigmoid(_dot(h0b, wgate_ref[:, 0:D_MODEL])) * y_pool
              + _sigmoid(_dot(h0b, wgate_ref[:, D_MODEL:2 * D_MODEL])) * y_mlstm)
    mix = _dot(merged.astype(BF16), wout_ref[...])
    ln1g = ln1g_ref[...] + _tile_lanes(start_row_copies(10), ROW_TILES)
    ln1b = ln1b_ref[...] + _tile_lanes(start_row_copies(11), ROW_TILES)
    h1 = _layer_norm(DEEPNORM_ALPHA * h0 + mix, ln1g, ln1b)
    for c in range(ROW_TILES):
        hx_ref[par, pl.ds(c, tm, stride=ROW_TILES), :] = h1[:, c * LANES:(c + 1) * LANES]
    h1_write(par, g).start()

    h1_hi = h1.astype(BF16)
    h1_lo = (h1 - h1_hi.astype(F32)).astype(BF16)
    la = _dot(h1_hi, wr_ref[...])
    lb = _dot(h1_lo, wr_ref[:, 0:LANES])
    logits = (la[:, 0:LANES] + la[:, LANES:2 * LANES] + lb).T[0:N_EXPERTS] + br_ref[...]
    e_iota = lax.broadcasted_iota(jnp.int32, (N_EXPERTS, tm), 0)
    vals, onehots = [], []
    lg = logits
    for _ in range(TOP_K):
        mx = jnp.max(lg, axis=0, keepdims=True)
        sel = jnp.min(jnp.where(lg == mx, e_iota, N_EXPERTS), axis=0, keepdims=True)
        oh = e_iota == sel
        lg = jnp.where(oh, -jnp.inf, lg)
        vals.append(mx)
        onehots.append(oh)
    exps = [jnp.exp(v - vals[0]) for v in vals]
    inv_den = 1.0 / (exps[0] + exps[1] + exps[2] + exps[3])
    gates = [e * inv_den for e in exps]

    oh_all = jnp.where(onehots[0], 1.0, 0.0)
    for oh in onehots[1:]:
        oh_all = oh_all + jnp.where(oh, 1.0, 0.0)
    strict = jnp.where(row_i < col_i, 1.0, 0.0).astype(BF16)
    run = run_ref[:, 0:1]
    rank = _dot(oh_all.astype(BF16), strict) + run
    count = jnp.sum(oh_all, axis=1, keepdims=True)
    inv_rows = 1.0 / EXPERT_ROWS
    pages_before = jnp.ceil(run * inv_rows)
    need = jnp.ceil((run + count) * inv_rows) - pages_before
    er = lax.broadcasted_iota(jnp.int32, (N_EXPERTS, N_EXPERTS), 0)
    ec = lax.broadcasted_iota(jnp.int32, (N_EXPERTS, N_EXPERTS), 1)
    earlier = jnp.where(er > ec, 1.0, 0.0).astype(BF16)
    need_b = jnp.broadcast_to(need, (N_EXPERTS, LANES)).astype(BF16)
    new_page = free_ref[0:1, 0:1] + _dot(earlier, need_b)[:, 0:1]
    page_seq = jnp.floor(rank * inv_rows)
    page = jnp.where(page_seq < pages_before, page_ref[:, 0:1], new_page)
    pos_all = page * EXPERT_ROWS + (rank - page_seq * EXPERT_ROWS)
    positions = [jnp.sum(jnp.where(oh, pos_all, 0.0), axis=0, keepdims=True) for oh in onehots]
    page_ref[...] = jnp.where(need > 0.0, new_page, page_ref[...])
    free_ref[...] = free_ref[...] + jnp.sum(need, axis=0, keepdims=True)
    run_ref[...] = run_ref[...] + count
    cnt_ref[...] = run_ref[...]
    p_lane = lax.broadcasted_iota(jnp.int32, (N_EXPERTS, pe_ref.shape[1]), 1).astype(F32)
    e_plus1 = (lax.broadcasted_iota(jnp.int32, (N_EXPERTS, 1), 0) + 1).astype(F32)
    taken = jnp.logical_and(p_lane == new_page, need > 0.0)
    pe_ref[...] = pe_ref[...] + jnp.sum(jnp.where(taken, e_plus1, 0.0), axis=0, keepdims=True)

    r8 = lax.broadcasted_iota(jnp.int32, (SUBLANES, tm), 0)
    pos_out = jnp.zeros((SUBLANES, tm), jnp.int32)
    r128 = lax.broadcasted_iota(jnp.int32, (LANES, tm), 0)
    gate_rows = jnp.zeros((LANES, tm), F32)
    for kk in range(TOP_K):
        pos_out = jnp.where(r8 == kk, positions[kk].astype(jnp.int32), pos_out)
        gate_rows = jnp.where(r128 == kk, gates[kk], gate_rows)
    pos_ref[...] = pos_out
    gcol_ref[...] = gate_rows.T
    posv_ref[par] = pos_out
    pos_to_smem(par).start()

    @pl.when(g == last)
    def _():
        pos_to_smem(par).wait()
        for t in range(tm):
            for k in range(TOP_K):
                row_copy(par, t, k).start(priority=k % DMA_PRIORITIES)
        wait_row_copies(q)
        wait_row_copies(par)
        h1_write(q, 0).wait()
        h1_write(par, 0).wait()

        st_lane = lax.broadcasted_iota(jnp.int32, (N_EXPERTS, LANES), 1)
        state = jnp.where(st_lane == 0, page_ref[...], jnp.where(st_lane == 1, run_ref[...], free_ref[...]))
        stv_ref[...] = state.astype(jnp.int32)
        state_copy = pltpu.make_async_copy(stv_ref, sts_ref, psem.at[par])
        state_copy.start()
        state_copy.wait()

        def zero_fill(wait):
            def act(rows_dst, n):
                cp = pltpu.make_async_copy(zb_ref.at[pl.ds(0, n)], xp_hbm.at[pl.ds(rows_dst, n)], zsem)
                if wait:
                    cp.wait()
                else:
                    cp.start()

            def unused_page(p, carry):
                act(pl.multiple_of(p * EXPERT_ROWS, EXPERT_ROWS), EXPERT_ROWS)
                return carry

            lax.fori_loop(sts_ref[0, 2], n_pages, unused_page, 0)

            def page_tail(e, carry):
                filled = sts_ref[e, 1] & (EXPERT_ROWS - 1)
                pad = jnp.where(filled == 0, 0, EXPERT_ROWS - filled)
                dst = sts_ref[e, 0] * EXPERT_ROWS + filled
                size = 1
                while size < EXPERT_ROWS:
                    @pl.when((pad & size) != 0)
                    def _(dst=dst, size=size):
                        act(dst, size)
                    dst = dst + (pad & size)
                    size *= 2
                return carry

            lax.fori_loop(0, N_EXPERTS, page_tail, 0)

        zero_fill(wait=False)
        zero_fill(wait=True)


def _const_spec(shape):
    zeros = (0,) * len(shape)
    return pl.BlockSpec(shape, lambda b, s: zeros, pipeline_mode=pl.Buffered(1))


def _mixer_call(x, weights, n_pages):
    bsz, seq, _ = x.shape
    tm = TOKEN_TILE
    n_s = seq // tm
    t_total = bsz * seq
    assert bsz * n_s >= 2
    tile = lambda b, s: (b * n_s + s)
    spare_pages = TOP_K * tm // EXPERT_ROWS
    pe_lanes = -(-n_pages // LANES) * LANES
    in_specs = [pl.BlockSpec((None, tm, D_MODEL), lambda b, s: (b, s, 0))]
    in_specs += [_const_spec(w.shape) for w in weights]
    out_shape = (
        jax.ShapeDtypeStruct((t_total * ROW_TILES, LANES), F32),
        jax.ShapeDtypeStruct(((n_pages + spare_pages) * EXPERT_ROWS, ROW_TILES, LANES), F32),
        jax.ShapeDtypeStruct((SUBLANES, t_total), jnp.int32),
        jax.ShapeDtypeStruct((t_total, LANES), F32),
        jax.ShapeDtypeStruct((N_EXPERTS, LANES), F32),
        jax.ShapeDtypeStruct((SUBLANES, pe_lanes), F32),
    )
    out_specs = (
        pl.BlockSpec(memory_space=pl.ANY),
        pl.BlockSpec(memory_space=pl.ANY),
        pl.BlockSpec((SUBLANES, tm), lambda b, s: (0, tile(b, s))),
        pl.BlockSpec((tm, LANES), lambda b, s: (tile(b, s), 0)),
        pl.BlockSpec((N_EXPERTS, LANES), lambda b, s: (0, 0)),
        pl.BlockSpec((SUBLANES, pe_lanes), lambda b, s: (0, 0)),
    )
    scratch = [
        pltpu.VMEM((POOL_HALO + tm, POOL_WIDTH), F32),
        pltpu.VMEM((2, POOL_HALO + tm, POOL_GROUP), F32),
        pltpu.VMEM((CONV_HALO + tm, D_MODEL), F32),
        pltpu.VMEM((N_HEADS, HEAD_DIM, HEAD_DIM), F32),
        pltpu.VMEM((N_HEADS, 1, HEAD_DIM), F32),
        pltpu.VMEM((1, LANES), F32),
        pltpu.VMEM((N_EXPERTS, LANES), F32),
        pltpu.VMEM((N_EXPERTS, LANES), F32),
        pltpu.VMEM((1, LANES), F32),
        pltpu.VMEM((2, tm * ROW_TILES + SUBLANES, LANES), F32),
        pltpu.VMEM((2, SUBLANES, tm), jnp.int32),
        pltpu.SMEM((2, SUBLANES, tm), jnp.int32),
        pltpu.VMEM((EXPERT_ROWS, ROW_TILES, LANES), F32),
        pltpu.VMEM((N_EXPERTS, LANES), jnp.int32),
        pltpu.SMEM((N_EXPERTS, LANES), jnp.int32),
        pltpu.SemaphoreType.DMA((2,)),
        pltpu.SemaphoreType.DMA((2,)),
        pltpu.SemaphoreType.DMA((2,)),
        pltpu.SemaphoreType.DMA(()),
    ]
    return pl.pallas_call(
        functools.partial(_mixer_kernel, n_pages=n_pages),
        grid=(bsz, n_s),
        in_specs=in_specs,
        out_specs=out_specs,
        out_shape=out_shape,
        scratch_shapes=scratch,
        compiler_params=pltpu.CompilerParams(
            dimension_semantics=("arbitrary", "arbitrary"), vmem_limit_bytes=VMEM_LIMIT),
        name="mixer",
    )(x, *weights)


def _experts_kernel(xpage_ref, spage_ref, be_ref, first_ref, nexte_ref, nused_ref,
                    slot_p_ref, slot_a_ref, slot_b_ref, xa_ref, xb_ref,
                    w1_hbm, b1_ref, w2_hbm, b2_ref, y_hbm,
                    yb0_ref, yb1_ref, w1s_ref, w2s_ref, w1b_ref, w2z_ref, w2b_ref, ssem, wsem):
    del xpage_ref, spage_ref
    rows = EXPERT_ROWS
    tile = rows * ROW_TILES
    i = pl.program_id(0)
    last = pl.num_programs(0) - 1
    n_used = nused_ref[0]
    yb = (yb0_ref, yb1_ref)
    x_in = (xa_ref, xb_ref)
    dyn_zero = lax.shift_right_arithmetic(n_used, 31)
    n_chunks = D_FF // EXPERT_CHUNK
    assert SCATTER_STAGES <= n_chunks + 1

    def scatter(slots, r, p):
        return pltpu.make_async_copy(
            yb[p].at[pl.ds(r * ROW_TILES, ROW_TILES)], y_hbm.at[slots[0, 0, r]], ssem.at[p])

    def start_scatters(stage, slots, p):
        if stage is None:
            r_range = range(rows)
        elif stage < SCATTER_STAGES:
            r_range = range(rows * stage // SCATTER_STAGES, rows * (stage + 1) // SCATTER_STAGES)
        else:
            return jnp.zeros((1, LANES), F32)
        for r in r_range:
            scatter(slots, r, p).start(priority=r % DMA_PRIORITIES)
        spare = pl.multiple_of(tile + dyn_zero * SUBLANES, SUBLANES)
        yb[p][pl.ds(spare, SUBLANES), :] = jnp.full((SUBLANES, LANES), dyn_zero.astype(F32))
        return yb[p][pl.ds(spare, SUBLANES), :][0:1, :]

    def wait_scatters(p):
        view = yb[p].at[pl.ds(0, tile)]
        pltpu.make_async_copy(view, view, ssem.at[p]).wait()

    def weight_copies(e):
        return (pltpu.make_async_copy(w1_hbm.at[e], w1s_ref, wsem.at[0]),
                pltpu.make_async_copy(w2_hbm.at[e], w2s_ref, wsem.at[1]))

    def switch_weights(blk):
        @pl.when(first_ref[blk] == 1)
        def _():
            for cp in weight_copies(0):
                cp.wait()
            step = 128
            for c in range(D_MODEL // step):
                w1b_ref[c * step:(c + 1) * step, :] = w1s_ref[c * step:(c + 1) * step, :].astype(BF16)
            half = D_FF // 2
            for c in range(ROW_TILES):
                cs = slice(c * LANES, (c + 1) * LANES)
                w2z_ref[c, pl.ds(0, half, stride=2), :] = w2s_ref[0:half, cs]
                w2z_ref[c, pl.ds(1, half, stride=2), :] = w2s_ref[half:D_FF, cs]
                w2b_ref[:, cs] = w2z_ref[c].astype(BF16)
            nxt = nexte_ref[blk]

            @pl.when(nxt >= 0)
            def _():
                for cp in weight_copies(nxt):
                    cp.start()

    def compute(p, blk, start_stage):
        e = be_ref[blk]
        x = jnp.concatenate(
            [x_in[p][pl.ds(c, rows, stride=ROW_TILES), :] for c in range(ROW_TILES)], axis=1).astype(BF16)
        b1 = b1_ref[pl.ds(e, 1), :]
        width = EXPERT_CHUNK
        even = (lax.broadcasted_iota(jnp.int32, (rows, width), 1) & 1) == 0
        zs = []
        for c in range(n_chunks):
            lo = c * width
            hi = D_FF + lo
            zero = _tile_lanes(start_stage(c), width // LANES)
            ha = _dot(x, w1b_ref[:, lo:lo + width]) + (b1[:, lo:lo + width] + zero)
            hb = _dot(x, w1b_ref[:, hi:hi + width]) + (b1[:, hi:hi + width] + zero)
            glu = jnp.where(even, ha, pltpu.roll(hb, 1, 1))
            lin = jnp.where(even, pltpu.roll(ha, width - 1, 1), hb)
            glu = jnp.minimum(glu, SWIGLU_LIMIT)
            lin = jnp.clip(lin, -SWIGLU_LIMIT, SWIGLU_LIMIT) + 1.0
            zs.append((glu * _sigmoid(SWIGLU_ALPHA * glu) * lin).astype(BF16))
        b2 = b2_ref[pl.ds(e, 1), :] + _tile_lanes(start_stage(n_chunks), ROW_TILES)
        y = _dot(jnp.concatenate(zs, axis=1), w2b_ref[...]) + b2
        for c in range(ROW_TILES):
            yb[p][pl.ds(c, rows, stride=ROW_TILES), :] = y[:, c * LANES:(c + 1) * LANES]

    def phase(blk, p, start_stage):
        switch_weights(blk)

        @pl.when(blk < n_used)
        def _():
            compute(p, blk, start_stage)

        @pl.when(blk >= n_used)
        def _():
            start_stage(None)

    @pl.when(i == 0)
    def _():
        yb1_ref[...] = jnp.zeros_like(yb1_ref)
        for cp in weight_copies(be_ref[0]):
            cp.start()

    @pl.when(i >= 1)
    def _():
        wait_scatters(0)

    phase(2 * i, 0, functools.partial(start_scatters, slots=slot_p_ref, p=1))

    wait_scatters(1)
    phase(2 * i + 1, 1, functools.partial(start_scatters, slots=slot_a_ref, p=0))

    @pl.when(i == last)
    def _():
        wait_scatters(0)
        for r in range(rows):
            scatter(slot_b_ref, r, 1).start(priority=r % DMA_PRIORITIES)
        wait_scatters(1)


def _experts_call(plan, row_slot, x_pages_2d, w_e1, b_e1, w_e2, b_e2, n_tokens):
    xpage, spage, block_e, first, next_e, n_used = plan
    rows = EXPERT_ROWS
    n_blocks = block_e.shape[0]
    assert n_blocks % 2 == 0
    n_slots = TOP_K * n_tokens + rows
    tile_rows = rows * ROW_TILES
    filler = row_slot.shape[0] - 1
    smem_rows = lambda index: pl.BlockSpec((1, 1, rows), index, memory_space=pltpu.SMEM)
    whole = lambda shape: pl.BlockSpec(shape, lambda i, *_: (0,) * len(shape))
    grid_spec = pltpu.PrefetchScalarGridSpec(
        num_scalar_prefetch=6,
        grid=(n_blocks // 2,),
        in_specs=[
            smem_rows(lambda i, xp, sp, *_: (jnp.where(i == 0, filler, sp[jnp.maximum(2 * i - 1, 0)]), 0, 0)),
            smem_rows(lambda i, xp, sp, *_: (sp[2 * i], 0, 0)),
            smem_rows(lambda i, xp, sp, *_: (sp[2 * i + 1], 0, 0)),
            pl.BlockSpec((tile_rows, LANES), lambda i, xp, *_: (xp[2 * i], 0)),
            pl.BlockSpec((tile_rows, LANES), lambda i, xp, *_: (xp[2 * i + 1], 0)),
            pl.BlockSpec(memory_space=pl.ANY),
            whole((N_EXPERTS, 2 * D_FF)),
            pl.BlockSpec(memory_space=pl.ANY),
            whole((N_EXPERTS, D_MODEL)),
        ],
        out_specs=pl.BlockSpec(memory_space=pl.ANY),
        scratch_shapes=[
            pltpu.VMEM((tile_rows + SUBLANES, LANES), F32),
            pltpu.VMEM((tile_rows + SUBLANES, LANES), F32),
            pltpu.VMEM((D_MODEL, 2 * D_FF), F32),
            pltpu.VMEM((D_FF, D_MODEL), F32),
            pltpu.VMEM((D_MODEL, 2 * D_FF), BF16),
            pltpu.VMEM((ROW_TILES, D_FF, LANES), F32),
            pltpu.VMEM((D_FF, D_MODEL), BF16),
            pltpu.SemaphoreType.DMA((2,)),
            pltpu.SemaphoreType.DMA((2,)),
        ],
    )
    return pl.pallas_call(
        _experts_kernel,
        grid_spec=grid_spec,
        out_shape=jax.ShapeDtypeStruct((n_slots, ROW_TILES, LANES), F32),
        compiler_params=pltpu.CompilerParams(
            dimension_semantics=("arbitrary",), vmem_limit_bytes=VMEM_LIMIT),
        name="experts",
    )(xpage, spage, block_e, first, next_e, n_used,
      row_slot, row_slot, row_slot, x_pages_2d, x_pages_2d, w_e1, b_e1, w_e2, b_e2)


def _combine_kernel(h1_ref, y0_ref, y1_ref, y2_ref, y3_ref, gcol_ref, g_ref, b_ref, out_ref):
    tc = COMBINE_TILE
    gcol = gcol_ref[...]
    y_refs = (y0_ref, y1_ref, y2_ref, y3_ref)
    chunks = []
    for c in range(ROW_TILES):
        z = DEEPNORM_ALPHA * h1_ref[pl.ds(c, tc, stride=ROW_TILES), :]
        for kk in range(TOP_K):
            z = z + gcol[:, kk:kk + 1] * y_refs[kk][pl.ds(c, tc, stride=ROW_TILES), :]
        chunks.append(z)
    total = chunks[0].sum(axis=1, keepdims=True)
    for z in chunks[1:]:
        total = total + z.sum(axis=1, keepdims=True)
    mu = total * (1.0 / D_MODEL)
    sq = None
    for z in chunks:
        zc = z - mu
        part = (zc * zc).sum(axis=1, keepdims=True)
        sq = part if sq is None else sq + part
    inv = lax.rsqrt(sq * (1.0 / D_MODEL) + LN_EPS)
    for c, z in enumerate(chunks):
        cs = slice(c * LANES, (c + 1) * LANES)
        out_ref[:, cs] = (z - mu) * inv * g_ref[:, cs] + b_ref[:, cs]


def _combine_call(h1_2d, y_2d, gcol, ln_g, ln_b, n_tokens):
    tc = COMBINE_TILE
    n_t = n_tokens // tc
    blk = tc * ROW_TILES
    y_spec = lambda kk: pl.BlockSpec((blk, LANES), lambda i: (kk * n_t + i, 0))
    return pl.pallas_call(
        _combine_kernel,
        grid=(n_t,),
        in_specs=[
            pl.BlockSpec((blk, LANES), lambda i: (i, 0)),
            y_spec(0), y_spec(1), y_spec(2), y_spec(3),
            pl.BlockSpec((tc, LANES), lambda i: (i, 0)),
            pl.BlockSpec((1, D_MODEL), lambda i: (0, 0)),
            pl.BlockSpec((1, D_MODEL), lambda i: (0, 0)),
        ],
        out_specs=pl.BlockSpec((tc, D_MODEL), lambda i: (i, 0)),
        out_shape=jax.ShapeDtypeStruct((n_tokens, D_MODEL), F32),
        compiler_params=pltpu.CompilerParams(
            dimension_semantics=("arbitrary",), vmem_limit_bytes=VMEM_LIMIT),
        name="combine",
    )(h1_2d, y_2d, y_2d, y_2d, y_2d, gcol, ln_g, ln_b)


def _prepare_mixer_weights(ln0_g, ln0_b, w_in, conv_w, conv_b, w_q, w_k, b_if, mh_gain, w_pool,
                           b_pool, ls_pool, w_branch_pool, w_branch_mlstm, w_out, ln1_g, ln1_b,
                           w_router, b_router):
    w_main = w_in[:, :COL_IF_END].astype(BF16)
    w_gates = w_in[:, COL_GATES:COL_GATES + 2 * D_MODEL].astype(BF16)
    bif = jnp.concatenate([b_if, jnp.zeros((LANES - 2 * N_HEADS,), F32)]).reshape(1, LANES)
    wqk = jnp.concatenate([w_q * (HEAD_DIM ** -0.5), w_k], axis=-1).astype(BF16)
    wpool_bd = jax.scipy.linalg.block_diag(*[w_pool[g] for g in range(len(POOL_WINDOWS))]).astype(BF16)
    wr_hi = w_router.astype(BF16)
    wr_lo = (w_router - wr_hi.astype(F32)).astype(BF16)
    lane_pad = jnp.zeros((D_MODEL, LANES - N_EXPERTS), BF16)
    wr_hi_lo = jnp.concatenate([wr_hi, lane_pad, wr_lo, lane_pad], axis=1)
    row = lambda v: v.reshape(1, -1)
    return (row(ln0_g), row(ln0_b), w_main, w_gates, bif, conv_w, row(conv_b), wqk, row(mh_gain), wpool_bd,
            row(b_pool), row(ls_pool), w_branch_pool.astype(BF16), w_branch_mlstm.astype(BF16),
            w_out.astype(BF16), row(ln1_g), row(ln1_b), wr_hi_lo,
            b_router.reshape(N_EXPERTS, 1))


def _plan(page_expert1, counts, pos, n_tokens, n_pages, n_rows_total):
    rows = EXPERT_ROWS
    n_assign = TOP_K * n_tokens
    i32 = jnp.int32
    slots = jnp.arange(n_assign, dtype=i32)
    hit = jnp.zeros((n_rows_total,), i32).at[pos.reshape(-1)].add(slots + 1)
    pad_slot = n_assign + jnp.arange(n_rows_total + rows, dtype=i32) % rows
    row_slot = jnp.where(jnp.concatenate([hit, jnp.zeros((rows,), i32)]) == 0, pad_slot,
                         jnp.concatenate([hit, jnp.zeros((rows,), i32)]) - 1)
    row_slot = row_slot.reshape(n_rows_total // rows + 1, 1, rows)
    filler = n_rows_total // rows

    pidx = jnp.arange(n_pages, dtype=i32)
    used = page_expert1 > 0
    n_used = jnp.sum(used.astype(i32))
    page_e = page_expert1 - 1
    key = jnp.where(used, page_e, N_EXPERTS) * n_pages + pidx
    place = jnp.sum((key[None, :] < key[:, None]).astype(i32), axis=1)
    at_block = place[None, :] == pidx[:, None]
    order = jnp.sum(jnp.where(at_block, pidx[None, :], 0), axis=1)
    block_e = jnp.clip(jnp.sum(jnp.where(at_block, page_e[None, :], 0), axis=1), 0, N_EXPERTS - 1)
    valid = pidx < n_used
    xpage = jnp.where(valid, order, 0)
    spage = jnp.where(valid, order, filler)
    prev_e = jnp.concatenate([jnp.full((1,), -1, i32), block_e[:-1]])
    first = jnp.logical_and(valid, block_e != prev_e).astype(i32)
    e_iota = jnp.arange(N_EXPERTS, dtype=i32)
    pages_per_e = (counts + rows - 1) // rows
    seg_end = jnp.cumsum(pages_per_e)
    mine = block_e[:, None] == e_iota[None, :]
    end_blk = jnp.sum(jnp.where(mine, seg_end[None, :], 0), axis=1)
    e_at_end = jnp.sum(jnp.where(pidx[None, :] == end_blk[:, None], block_e[None, :], 0), axis=1)
    next_e = jnp.where(end_blk < n_used, e_at_end, -1).astype(i32)
    plan = (xpage.astype(i32), spage.astype(i32), block_e.astype(i32), first, next_e,
            n_used.reshape(1).astype(i32))
    return plan, row_slot


def kernel(x, ln0_g, ln0_b, w_in, conv_w, conv_b, w_q, w_k, b_if, mh_gain, w_pool, b_pool, ls_pool,
           w_branch_pool, w_branch_mlstm, w_out, ln1_g, ln1_b, w_router, b_router, w_e1, b_e1,
           w_e2, b_e2, ln2_g, ln2_b):
    bsz, seq, _ = x.shape
    n_tokens = bsz * seq
    rows = EXPERT_ROWS
    assert w_in.shape[0] == 1, "single-layer trunk"
    assert seq % TOKEN_TILE == 0 and n_tokens % COMBINE_TILE == 0
    n_pages = (TOP_K * n_tokens + N_EXPERTS * (rows - 1) + rows - 1) // rows
    n_pages += n_pages % 2
    weights = _prepare_mixer_weights(
        ln0_g, ln0_b, w_in[0], conv_w[0], conv_b[0], w_q[0], w_k[0], b_if[0], mh_gain[0], w_pool[0],
        b_pool[0], ls_pool[0], w_branch_pool[0], w_branch_mlstm[0], w_out[0], ln1_g[0], ln1_b[0],
        w_router[0], b_router[0])
    h1_2d, x_pages, pos8, gcol, cnt, pe = _mixer_call(x, weights, n_pages)
    counts = cnt[:, 0].astype(jnp.int32)
    page_expert1 = pe[0, :n_pages].astype(jnp.int32)
    plan, row_slot = _plan(page_expert1, counts, pos8[:TOP_K], n_tokens, n_pages, x_pages.shape[0])
    y_slots = _experts_call(
        plan, row_slot, x_pages.reshape(-1, LANES), w_e1[0], b_e1[0], w_e2[0], b_e2[0], n_tokens)
    out = _combine_call(h1_2d, y_slots.reshape(-1, LANES), gcol, ln2_g[0].reshape(1, D_MODEL),
                        ln2_b[0].reshape(1, D_MODEL), n_tokens)
    return out.reshape(bsz, seq, D_MODEL)
```

```python
import functools

import jax
import jax.numpy as jnp
from jax import lax
from jax.experimental import pallas as pl
from jax.experimental.pallas import tpu as pltpu

F32 = jnp.float32
BF16 = jnp.bfloat16

D_MODEL = 1024
N_HEADS = 4
HEAD_DIM = 256
POOL_WIDTH = 512
POOL_GROUP = 128
POOL_WINDOWS = (2, 4, 8, 16)
CONV_WIDTH = 4
N_EXPERTS = 32
TOP_K = 4
D_FF = 1024
SWIGLU_ALPHA = 1.702
SWIGLU_LIMIT = 7.0
LN_EPS = 1e-5
DEEPNORM_ALPHA = 2.0 ** 0.25

SUBLANES = 8
LANES = 128
ROW_TILES = D_MODEL // LANES

COL_P = 0
COL_U = COL_P + POOL_WIDTH
COL_V = COL_U + D_MODEL
COL_O = COL_V + D_MODEL
COL_IF = COL_O + D_MODEL
COL_IF_END = COL_IF + LANES
COL_GATES = COL_IF + 2 * N_HEADS

TOKEN_TILE = 256
POOL_HALO = 24
CONV_HALO = 8
EXPERT_ROWS = 256
ROW_DMA_GROUPS = 12
SCATTER_STAGES = 3
EXPERT_CHUNK = 2 * LANES
COMBINE_TILE = 512
DMA_PRIORITIES = 2
VMEM_LIMIT = 56 * 1024 * 1024

assert TOKEN_TILE <= EXPERT_ROWS
assert (TOP_K * TOKEN_TILE) % EXPERT_ROWS == 0


def _layer_norm(x, g, b):
    mu = jnp.mean(x, axis=-1, keepdims=True)
    xc = x - mu
    var = jnp.mean(xc * xc, axis=-1, keepdims=True)
    return xc * lax.rsqrt(var + LN_EPS) * g + b


def _sigmoid(x):
    return 0.5 * jnp.tanh(0.5 * x) + 0.5


def _log_sigmoid(x):
    return -(jnp.maximum(-x, 0.0) + jnp.log(1.0 + jnp.exp(-jnp.abs(x))))


def _split3(x):
    hi = x.astype(BF16)
    r1 = x - hi.astype(F32)
    mid = r1.astype(BF16)
    lo = (r1 - mid.astype(F32)).astype(BF16)
    return hi, mid, lo


def _dot(a, b):
    return jnp.dot(a, b, preferred_element_type=F32)


def _dot_nt(a, b):
    return lax.dot_general(a, b, (((1,), (1,)), ((), ())), preferred_element_type=F32)


def _tile_lanes(row, n):
    return jnp.concatenate([row] * n, axis=1)


def _mixer_kernel(x_ref, xn_ref, ln0g_ref, ln0b_ref, win_ref, wgate_ref, bif_ref, convw_ref, convb_ref,
                  wqk_ref, gain_ref, wpool_ref, bpool_ref, lspool_ref, wbp_ref, wbm_ref, wout_ref,
                  ln1g_ref, ln1b_ref, wr_ref, br_ref,
                  h1_hbm, xp_hbm, pos_ref, gcol_ref, cnt_ref, pe_ref,
                  pext_ref, pw_ref, uext_ref, ct_ref, n_ref, m_ref, run_ref, page_ref, free_ref,
                  hx_ref, posv_ref, poss_ref, zb_ref, stv_ref, sts_ref, h0_ref, h0b_ref,
                  psem, rsem, hsem, zsem, *, n_pages):
    tm = TOKEN_TILE
    tile_rows = tm * ROW_TILES
    b = pl.program_id(0)
    s = pl.program_id(1)
    n_s = pl.num_programs(1)
    g = b * n_s + s
    last = pl.num_programs(0) * n_s - 1
    par = g % 2
    q = 1 - par
    dyn_zero = lax.shift_right_arithmetic(g, 31)

    def hx_tile(p):
        return hx_ref.at[p, pl.ds(0, tile_rows)]

    def row_copy(p, t, k):
        return pltpu.make_async_copy(
            hx_ref.at[p, pl.ds(t * ROW_TILES, ROW_TILES)], xp_hbm.at[poss_ref[p, k, t]], rsem.at[p])

    def wait_row_copies(p):
        for _ in range(TOP_K):
            pltpu.make_async_copy(hx_tile(p), hx_tile(p), rsem.at[p]).wait()

    def h1_write(p, tile_index):
        off = pl.multiple_of(tile_index * tile_rows, tile_rows)
        return pltpu.make_async_copy(hx_tile(p), h1_hbm.at[pl.ds(off, tile_rows)], hsem.at[p])

    def pos_to_smem(p):
        return pltpu.make_async_copy(posv_ref.at[p], poss_ref.at[p], psem.at[p])

    def entry_norm(x_tile_ref, bias, p):
        h0_tile = _layer_norm(x_tile_ref[...], ln0g_ref[...], bias)
        h0_ref[p] = h0_tile
        h0b_ref[p] = h0_tile.astype(BF16)

    @pl.when(s == 0)
    def _():
        pext_ref[0:POOL_HALO, :] = jnp.zeros((POOL_HALO, POOL_WIDTH), F32)
        uext_ref[0:CONV_HALO, :] = jnp.zeros((CONV_HALO, D_MODEL), F32)
        ct_ref[...] = jnp.zeros_like(ct_ref)
        n_ref[...] = jnp.zeros_like(n_ref)
        m_ref[...] = jnp.zeros_like(m_ref)
        pw_ref[:, 0:SUBLANES, :] = jnp.zeros((2, SUBLANES, POOL_GROUP), F32)

    @pl.when(g == 0)
    def _():
        run_ref[...] = jnp.zeros_like(run_ref)
        page_ref[...] = jnp.zeros_like(page_ref)
        free_ref[...] = jnp.zeros_like(free_ref)
        pe_ref[...] = jnp.zeros_like(pe_ref)
        zb_ref[...] = jnp.zeros_like(zb_ref)
        entry_norm(x_ref, ln0b_ref[...], 0)
        hx_ref[1] = jnp.zeros((tile_rows + SUBLANES, LANES), F32)
        for k in range(TOP_K):
            for t in range(tm):
                poss_ref[1, k, t] = n_pages * EXPERT_ROWS + k * tm + t

    @pl.when(g >= 1)
    def _():
        pos_to_smem(q).wait()
        wait_row_copies(par)

    @pl.when(g >= 2)
    def _():
        h1_write(par, 0).wait()

    def start_row_copies(grp):
        for t in range(tm * grp // ROW_DMA_GROUPS, tm * (grp + 1) // ROW_DMA_GROUPS):
            for k in range(TOP_K):
                row_copy(q, t, k).start(priority=k % DMA_PRIORITIES)
        spare = pl.multiple_of(tile_rows + dyn_zero * SUBLANES, SUBLANES)
        hx_ref[q, pl.ds(spare, SUBLANES), :] = jnp.full((SUBLANES, LANES), dyn_zero.astype(F32))
        return hx_ref[q, pl.ds(spare, SUBLANES), :][0:1, :]

    def proj(lo, hi):
        return _dot(h0b_ref[par], win_ref[:, lo:hi])

    pext_ref[POOL_HALO:POOL_HALO + tm, :] = proj(COL_P, COL_U)
    tpos = s * tm + lax.broadcasted_iota(jnp.int32, (tm, 1), 0)
    groups = []
    end = POOL_HALO + tm
    for gi, w in enumerate(POOL_WINDOWS):
        cols = slice(gi * POOL_GROUP, (gi + 1) * POOL_GROUP)
        cur = pext_ref[POOL_HALO:end, cols]
        src, src_cols, k, nxt = pext_ref, cols, 1, 0
        while 2 * k < w:
            pw_ref[nxt, SUBLANES:end, :] = (src[SUBLANES:end, src_cols]
                                            + src[SUBLANES - k:end - k, src_cols])
            src, src_cols, k, nxt = pw_ref.at[nxt], slice(None), 2 * k, 1 - nxt
        acc = src[POOL_HALO:end, src_cols] + src[POOL_HALO - k:end - k, src_cols]
        inv_cnt = 1.0 / jnp.minimum(tpos + 1, w).astype(F32)
        groups.append(acc * inv_cnt - cur)
    pooled = jnp.concatenate(groups, axis=1)
    pext_ref[0:POOL_HALO, :] = pext_ref[tm:tm + POOL_HALO, :]
    bpool = bpool_ref[...] + _tile_lanes(start_row_copies(1), POOL_WIDTH // LANES)
    lspool = lspool_ref[...] + _tile_lanes(start_row_copies(8), POOL_WIDTH // LANES)
    mixed = (_dot(pooled.astype(BF16), wpool_ref[...]) + bpool) * lspool
    y_pool = _dot(mixed.astype(BF16), wbp_ref[...])

    uext_ref[CONV_HALO:CONV_HALO + tm, :] = proj(COL_U, COL_V)
    entry_norm(xn_ref, ln0b_ref[...] + _tile_lanes(start_row_copies(0), ROW_TILES), q)
    back = h0_ref[q, pl.ds(pl.multiple_of(dyn_zero * SUBLANES, SUBLANES), SUBLANES), 0:LANES]
    normed = lax.bitcast_convert_type(lax.bitcast_convert_type(back, jnp.int32) & dyn_zero, F32)[0:1, :]
    conv = convb_ref[...] + _tile_lanes(start_row_copies(2) + normed, ROW_TILES)
    for j in range(CONV_WIDTH):
        off = CONV_HALO - (CONV_WIDTH - 1) + j
        conv = conv + convw_ref[j:j + 1, :] * uext_ref[off:off + tm, :]
    uext_ref[0:CONV_HALO, :] = uext_ref[tm:tm + CONV_HALO, :]
    ucb = (conv * _sigmoid(conv)).astype(BF16)
    vb = proj(COL_V, COL_O).astype(BF16)

    slab = proj(COL_IF, COL_IF_END) + (bif_ref[...] + start_row_copies(3))
    lane = lax.broadcasted_iota(jnp.int32, (tm, LANES), 1)
    is_f = jnp.logical_and(lane >= N_HEADS, lane < 2 * N_HEADS)
    slab = jnp.where(is_f, _log_sigmoid(slab), slab)
    row_i = lax.broadcasted_iota(jnp.int32, (tm, tm), 0)
    col_i = lax.broadcasted_iota(jnp.int32, (tm, tm), 1)
    causal = row_i >= col_i
    tri = jnp.where(causal, 1.0, 0.0).astype(BF16)
    hi, mid, lo = _split3(slab)
    bcol = _dot(tri, hi) + _dot(tri, mid) + _dot(tri, lo)
    slab_t = slab.T
    bcol_t = bcol.T

    heads = []
    for h in range(N_HEADS):
        hs = slice(h * HEAD_DIM, (h + 1) * HEAD_DIM)
        qk = _dot(ucb[:, hs], wqk_ref[h])
        q_h = qk[:, :HEAD_DIM]
        k_h = qk[:, HEAD_DIM:]
        qb = q_h.astype(BF16)
        kb = k_h.astype(BF16)
        vh = vb[:, hs]

        i_c = slab[:, h:h + 1]
        b_c = bcol[:, N_HEADS + h:N_HEADS + h + 1]
        i_r = slab_t[h:h + 1, :]
        b_r = bcol_t[N_HEADS + h:N_HEADS + h + 1, :]
        m_prev = m_ref[:, h:h + 1] + start_row_copies(4 + h)[:, 0:1]

        d_log = jnp.where(causal, b_c - (b_r - i_r), -jnp.inf)
        m_inter = b_c + m_prev
        m_t = jnp.maximum(m_inter, jnp.max(d_log, axis=1, keepdims=True))
        w_intra = jnp.exp(d_log - m_t)
        sc = _dot_nt(qb, kb) * w_intra
        w_inter = jnp.exp(m_inter - m_t)
        ctb = ct_ref[h].astype(BF16)
        num = _dot(sc.astype(BF16), vh) + w_inter * _dot(qb, ctb)
        qn = jnp.sum(q_h * n_ref[h], axis=1, keepdims=True)
        den = jnp.sum(sc, axis=1, keepdims=True) + w_inter * qn
        hh = num * (1.0 / jnp.maximum(jnp.abs(den), jnp.exp(-m_t)))
        mu = jnp.mean(hh, axis=1, keepdims=True)
        hc = hh - mu
        var = jnp.mean(hc * hc, axis=1, keepdims=True)
        heads.append(hc * lax.rsqrt(var + LN_EPS))

        g_last = b_r[:, tm - 1:tm]
        m_new = jnp.maximum(g_last + m_prev, jnp.max(g_last - b_r + i_r, axis=1, keepdims=True))
        decay = jnp.exp(g_last + m_prev - m_new)
        w_state = jnp.exp(g_last - b_c + i_c - m_new)
        kw = k_h * w_state
        ct_ref[h] = decay * ct_ref[h] + _dot(kw.T.astype(BF16), vh)
        n_ref[h] = decay * n_ref[h] + jnp.sum(kw, axis=0, keepdims=True)
        m_ref[:, h:h + 1] = m_new

    hn = jnp.concatenate(heads, axis=1) * (gain_ref[...] + _tile_lanes(start_row_copies(9), ROW_TILES))
    h_out = _sigmoid(proj(COL_O, COL_IF)) * hn
    y_mlstm = _dot(h_out.astype(BF16), wbm_ref[...])

    merged = (_sigmoid(_dot(h0b_ref[par], wgate_ref[:, 0:D_MODEL])) * y_pool
              + _sigmoid(_dot(h0b_ref[par], wgate_ref[:, D_MODEL:2 * D_MODEL])) * y_mlstm)
    mix = _dot(merged.astype(BF16), wout_ref[...])
    ln1g = ln1g_ref[...] + _tile_lanes(start_row_copies(10), ROW_TILES)
    ln1b = ln1b_ref[...] + _tile_lanes(start_row_copies(11), ROW_TILES)
    h1 = _layer_norm(DEEPNORM_ALPHA * h0_ref[par] + mix, ln1g, ln1b)
    for c in range(ROW_TILES):
        hx_ref[par, pl.ds(c, tm, stride=ROW_TILES), :] = h1[:, c * LANES:(c + 1) * LANES]
    h1_write(par, g).start()

    h1_hi = h1.astype(BF16)
    h1_lo = (h1 - h1_hi.astype(F32)).astype(BF16)
    la = _dot(h1_hi, wr_ref[...])
    lb = _dot(h1_lo, wr_ref[:, 0:LANES])
    logits = (la[:, 0:LANES] + la[:, LANES:2 * LANES] + lb).T[0:N_EXPERTS] + br_ref[...]
    e_iota = lax.broadcasted_iota(jnp.int32, (N_EXPERTS, tm), 0)
    vals, onehots = [], []
    lg = logits
    for _ in range(TOP_K):
        mx = jnp.max(lg, axis=0, keepdims=True)
        sel = jnp.min(jnp.where(lg == mx, e_iota, N_EXPERTS), axis=0, keepdims=True)
        oh = e_iota == sel
        lg = jnp.where(oh, -jnp.inf, lg)
        vals.append(mx)
        onehots.append(oh)
    exps = [jnp.exp(v - vals[0]) for v in vals]
    inv_den = 1.0 / (exps[0] + exps[1] + exps[2] + exps[3])
    gates = [e * inv_den for e in exps]

    oh_all = jnp.where(onehots[0], 1.0, 0.0)
    for oh in onehots[1:]:
        oh_all = oh_all + jnp.where(oh, 1.0, 0.0)
    strict = jnp.where(row_i < col_i, 1.0, 0.0).astype(BF16)
    run = run_ref[:, 0:1]
    rank = _dot(oh_all.astype(BF16), strict) + run
    count = jnp.sum(oh_all, axis=1, keepdims=True)
    inv_rows = 1.0 / EXPERT_ROWS
    pages_before = jnp.ceil(run * inv_rows)
    need = jnp.ceil((run + count) * inv_rows) - pages_before
    er = lax.broadcasted_iota(jnp.int32, (N_EXPERTS, N_EXPERTS), 0)
    ec = lax.broadcasted_iota(jnp.int32, (N_EXPERTS, N_EXPERTS), 1)
    earlier = jnp.where(er > ec, 1.0, 0.0).astype(BF16)
    need_b = jnp.broadcast_to(need, (N_EXPERTS, LANES)).astype(BF16)
    new_page = free_ref[0:1, 0:1] + _dot(earlier, need_b)[:, 0:1]
    page_seq = jnp.floor(rank * inv_rows)
    page = jnp.where(page_seq < pages_before, page_ref[:, 0:1], new_page)
    pos_all = page * EXPERT_ROWS + (rank - page_seq * EXPERT_ROWS)
    positions = [jnp.sum(jnp.where(oh, pos_all, 0.0), axis=0, keepdims=True) for oh in onehots]
    page_ref[...] = jnp.where(need > 0.0, new_page, page_ref[...])
    free_ref[...] = free_ref[...] + jnp.sum(need, axis=0, keepdims=True)
    run_ref[...] = run_ref[...] + count
    cnt_ref[...] = run_ref[...]
    p_lane = lax.broadcasted_iota(jnp.int32, (N_EXPERTS, pe_ref.shape[1]), 1).astype(F32)
    e_plus1 = (lax.broadcasted_iota(jnp.int32, (N_EXPERTS, 1), 0) + 1).astype(F32)
    taken = jnp.logical_and(p_lane == new_page, need > 0.0)
    pe_ref[...] = pe_ref[...] + jnp.sum(jnp.where(taken, e_plus1, 0.0), axis=0, keepdims=True)

    r8 = lax.broadcasted_iota(jnp.int32, (SUBLANES, tm), 0)
    pos_out = jnp.zeros((SUBLANES, tm), jnp.int32)
    r128 = lax.broadcasted_iota(jnp.int32, (LANES, tm), 0)
    gate_rows = jnp.zeros((LANES, tm), F32)
    for kk in range(TOP_K):
        pos_out = jnp.where(r8 == kk, positions[kk].astype(jnp.int32), pos_out)
        gate_rows = jnp.where(r128 == kk, gates[kk], gate_rows)
    pos_ref[...] = pos_out
    gcol_ref[...] = gate_rows.T
    posv_ref[par] = pos_out
    pos_to_smem(par).start()

    @pl.when(g == last)
    def _():
        pos_to_smem(par).wait()
        for t in range(tm):
            for k in range(TOP_K):
                row_copy(par, t, k).start(priority=k % DMA_PRIORITIES)
        wait_row_copies(q)
        wait_row_copies(par)
        h1_write(q, 0).wait()
        h1_write(par, 0).wait()

        st_lane = lax.broadcasted_iota(jnp.int32, (N_EXPERTS, LANES), 1)
        state = jnp.where(st_lane == 0, page_ref[...], jnp.where(st_lane == 1, run_ref[...], free_ref[...]))
        stv_ref[...] = state.astype(jnp.int32)
        state_copy = pltpu.make_async_copy(stv_ref, sts_ref, psem.at[par])
        state_copy.start()
        state_copy.wait()

        def zero_fill(wait):
            def act(rows_dst, n):
                cp = pltpu.make_async_copy(zb_ref.at[pl.ds(0, n)], xp_hbm.at[pl.ds(rows_dst, n)], zsem)
                if wait:
                    cp.wait()
                else:
                    cp.start()

            def unused_page(p, carry):
                act(pl.multiple_of(p * EXPERT_ROWS, EXPERT_ROWS), EXPERT_ROWS)
                return carry

            lax.fori_loop(sts_ref[0, 2], n_pages, unused_page, 0)

            def page_tail(e, carry):
                filled = sts_ref[e, 1] & (EXPERT_ROWS - 1)
                pad = jnp.where(filled == 0, 0, EXPERT_ROWS - filled)
                dst = sts_ref[e, 0] * EXPERT_ROWS + filled
                size = 1
                while size < EXPERT_ROWS:
                    @pl.when((pad & size) != 0)
                    def _(dst=dst, size=size):
                        act(dst, size)
                    dst = dst + (pad & size)
                    size *= 2
                return carry

            lax.fori_loop(0, N_EXPERTS, page_tail, 0)

        zero_fill(wait=False)
        zero_fill(wait=True)


def _const_spec(shape):
    zeros = (0,) * len(shape)
    return pl.BlockSpec(shape, lambda b, s: zeros, pipeline_mode=pl.Buffered(1))


def _mixer_call(x, weights, n_pages):
    bsz, seq, _ = x.shape
    tm = TOKEN_TILE
    n_s = seq // tm
    t_total = bsz * seq
    assert bsz * n_s >= 2
    tile = lambda b, s: (b * n_s + s)
    spare_pages = TOP_K * tm // EXPERT_ROWS
    pe_lanes = -(-n_pages // LANES) * LANES
    def next_tile(b, s):
        nxt = jnp.minimum(b * n_s + s + 1, bsz * n_s - 1)
        return nxt // n_s, nxt % n_s, 0

    in_specs = [pl.BlockSpec((None, tm, D_MODEL), lambda b, s: (b, s, 0)),
                pl.BlockSpec((None, tm, D_MODEL), next_tile)]
    in_specs += [_const_spec(w.shape) for w in weights]
    out_shape = (
        jax.ShapeDtypeStruct((t_total * ROW_TILES, LANES), F32),
        jax.ShapeDtypeStruct(((n_pages + spare_pages) * EXPERT_ROWS, ROW_TILES, LANES), F32),
        jax.ShapeDtypeStruct((SUBLANES, t_total), jnp.int32),
        jax.ShapeDtypeStruct((t_total, LANES), F32),
        jax.ShapeDtypeStruct((N_EXPERTS, LANES), F32),
        jax.ShapeDtypeStruct((SUBLANES, pe_lanes), F32),
    )
    out_specs = (
        pl.BlockSpec(memory_space=pl.ANY),
        pl.BlockSpec(memory_space=pl.ANY),
        pl.BlockSpec((SUBLANES, tm), lambda b, s: (0, tile(b, s))),
        pl.BlockSpec((tm, LANES), lambda b, s: (tile(b, s), 0)),
        pl.BlockSpec((N_EXPERTS, LANES), lambda b, s: (0, 0)),
        pl.BlockSpec((SUBLANES, pe_lanes), lambda b, s: (0, 0)),
    )
    scratch = [
        pltpu.VMEM((POOL_HALO + tm, POOL_WIDTH), F32),
        pltpu.VMEM((2, POOL_HALO + tm, POOL_GROUP), F32),
        pltpu.VMEM((CONV_HALO + tm, D_MODEL), F32),
        pltpu.VMEM((N_HEADS, HEAD_DIM, HEAD_DIM), F32),
        pltpu.VMEM((N_HEADS, 1, HEAD_DIM), F32),
        pltpu.VMEM((1, LANES), F32),
        pltpu.VMEM((N_EXPERTS, LANES), F32),
        pltpu.VMEM((N_EXPERTS, LANES), F32),
        pltpu.VMEM((1, LANES), F32),
        pltpu.VMEM((2, tm * ROW_TILES + SUBLANES, LANES), F32),
        pltpu.VMEM((2, SUBLANES, tm), jnp.int32),
        pltpu.SMEM((2, SUBLANES, tm), jnp.int32),
        pltpu.VMEM((EXPERT_ROWS, ROW_TILES, LANES), F32),
        pltpu.VMEM((N_EXPERTS, LANES), jnp.int32),
        pltpu.SMEM((N_EXPERTS, LANES), jnp.int32),
        pltpu.VMEM((2, tm, D_MODEL), F32),
        pltpu.VMEM((2, tm, D_MODEL), BF16),
        pltpu.SemaphoreType.DMA((2,)),
        pltpu.SemaphoreType.DMA((2,)),
        pltpu.SemaphoreType.DMA((2,)),
        pltpu.SemaphoreType.DMA(()),
    ]
    return pl.pallas_call(
        functools.partial(_mixer_kernel, n_pages=n_pages),
        grid=(bsz, n_s),
        in_specs=in_specs,
        out_specs=out_specs,
        out_shape=out_shape,
        scratch_shapes=scratch,
        compiler_params=pltpu.CompilerParams(
            dimension_semantics=("arbitrary", "arbitrary"), vmem_limit_bytes=VMEM_LIMIT),
        name="mixer",
    )(x, x, *weights)


def _experts_kernel(xpage_ref, spage_ref, be_ref, first_ref, nexte_ref, nused_ref,
                    slot_p_ref, slot_a_ref, slot_b_ref, xa_ref, xb_ref,
                    w1_hbm, b1_ref, w2_hbm, b2_ref, y_hbm,
                    yb0_ref, yb1_ref, w1s_ref, w2s_ref, w1b_ref, w2b_ref, ssem, wsem):
    del xpage_ref, spage_ref
    rows = EXPERT_ROWS
    tile = rows * ROW_TILES
    i = pl.program_id(0)
    last = pl.num_programs(0) - 1
    n_used = nused_ref[0]
    yb = (yb0_ref, yb1_ref)
    x_in = (xa_ref, xb_ref)
    dyn_zero = lax.shift_right_arithmetic(n_used, 31)
    n_chunks = D_FF // EXPERT_CHUNK
    assert SCATTER_STAGES <= n_chunks + 1

    def scatter(slots, r, p):
        return pltpu.make_async_copy(
            yb[p].at[pl.ds(r * ROW_TILES, ROW_TILES)], y_hbm.at[slots[0, 0, r]], ssem.at[p])

    def start_scatters(stage, slots, p):
        if stage is None:
            r_range = range(rows)
        elif stage < SCATTER_STAGES:
            r_range = range(rows * stage // SCATTER_STAGES, rows * (stage + 1) // SCATTER_STAGES)
        else:
            return jnp.zeros((1, LANES), F32)
        for r in r_range:
            scatter(slots, r, p).start(priority=r % DMA_PRIORITIES)
        spare = pl.multiple_of(tile + dyn_zero * SUBLANES, SUBLANES)
        yb[p][pl.ds(spare, SUBLANES), :] = jnp.full((SUBLANES, LANES), dyn_zero.astype(F32))
        return yb[p][pl.ds(spare, SUBLANES), :][0:1, :]

    def wait_scatters(p):
        view = yb[p].at[pl.ds(0, tile)]
        pltpu.make_async_copy(view, view, ssem.at[p]).wait()

    def weight_copies(e):
        return (pltpu.make_async_copy(w1_hbm.at[e], w1s_ref, wsem.at[0]),
                pltpu.make_async_copy(w2_hbm.at[e], w2s_ref, wsem.at[1]))

    def switch_weights(blk):
        @pl.when(first_ref[blk] == 1)
        def _():
            for cp in weight_copies(0):
                cp.wait()
            step = 128
            for c in range(D_MODEL // step):
                w1b_ref[c * step:(c + 1) * step, :] = w1s_ref[c * step:(c + 1) * step, :].astype(BF16)
            half = D_FF // 2
            step = 64
            for c in range(half // step):
                pair = pltpu.pack_elementwise(
                    [w2s_ref[c * step:(c + 1) * step, :], w2s_ref[half + c * step:half + (c + 1) * step, :]],
                    packed_dtype=BF16)
                w2b_ref[2 * c * step:2 * (c + 1) * step, :] = pltpu.bitcast(pair, BF16)
            nxt = nexte_ref[blk]

            @pl.when(nxt >= 0)
            def _():
                for cp in weight_copies(nxt):
                    cp.start()

    def compute(p, blk, start_stage):
        e = be_ref[blk]
        x = jnp.concatenate(
            [x_in[p][pl.ds(c, rows, stride=ROW_TILES), :] for c in range(ROW_TILES)], axis=1).astype(BF16)
        b1 = b1_ref[pl.ds(e, 1), :]
        width = EXPERT_CHUNK
        even = (lax.broadcasted_iota(jnp.int32, (rows, width), 1) & 1) == 0
        zs = []
        for c in range(n_chunks):
            lo = c * width
            hi = D_FF + lo
            zero = _tile_lanes(start_stage(c), width // LANES)
            ha = _dot(x, w1b_ref[:, lo:lo + width]) + (b1[:, lo:lo + width] + zero)
            hb = _dot(x, w1b_ref[:, hi:hi + width]) + (b1[:, hi:hi + width] + zero)
            glu = jnp.where(even, ha, pltpu.roll(hb, 1, 1))
            lin = jnp.where(even, pltpu.roll(ha, width - 1, 1), hb)
            glu = jnp.minimum(glu, SWIGLU_LIMIT)
            lin = jnp.clip(lin, -SWIGLU_LIMIT, SWIGLU_LIMIT) + 1.0
            zs.append((glu * _sigmoid(SWIGLU_ALPHA * glu) * lin).astype(BF16))
        b2 = b2_ref[pl.ds(e, 1), :] + _tile_lanes(start_stage(n_chunks), ROW_TILES)
        y = _dot(jnp.concatenate(zs, axis=1), w2b_ref[...]) + b2
        for c in range(ROW_TILES):
            yb[p][pl.ds(c, rows, stride=ROW_TILES), :] = y[:, c * LANES:(c + 1) * LANES]

    def phase(blk, p, start_stage):
        switch_weights(blk)

        @pl.when(blk < n_used)
        def _():
            compute(p, blk, start_stage)

        @pl.when(blk >= n_used)
        def _():
            start_stage(None)

    @pl.when(i == 0)
    def _():
        yb1_ref[...] = jnp.zeros_like(yb1_ref)
        for cp in weight_copies(be_ref[0]):
            cp.start()

    @pl.when(i >= 1)
    def _():
        wait_scatters(0)

    phase(2 * i, 0, functools.partial(start_scatters, slots=slot_p_ref, p=1))

    wait_scatters(1)
    phase(2 * i + 1, 1, functools.partial(start_scatters, slots=slot_a_ref, p=0))

    @pl.when(i == last)
    def _():
        wait_scatters(0)
        for r in range(rows):
            scatter(slot_b_ref, r, 1).start(priority=r % DMA_PRIORITIES)
        wait_scatters(1)


def _experts_call(plan, row_slot, x_pages_2d, w_e1, b_e1, w_e2, b_e2, n_tokens):
    xpage, spage, block_e, first, next_e, n_used = plan
    rows = EXPERT_ROWS
    n_blocks = block_e.shape[0]
    assert n_blocks % 2 == 0
    n_slots = TOP_K * n_tokens + rows
    tile_rows = rows * ROW_TILES
    filler = row_slot.shape[0] - 1
    smem_rows = lambda index: pl.BlockSpec((1, 1, rows), index, memory_space=pltpu.SMEM)
    whole = lambda shape: pl.BlockSpec(shape, lambda i, *_: (0,) * len(shape))
    grid_spec = pltpu.PrefetchScalarGridSpec(
        num_scalar_prefetch=6,
        grid=(n_blocks // 2,),
        in_specs=[
            smem_rows(lambda i, xp, sp, *_: (jnp.where(i == 0, filler, sp[jnp.maximum(2 * i - 1, 0)]), 0, 0)),
            smem_rows(lambda i, xp, sp, *_: (sp[2 * i], 0, 0)),
            smem_rows(lambda i, xp, sp, *_: (sp[2 * i + 1], 0, 0)),
            pl.BlockSpec((tile_rows, LANES), lambda i, xp, *_: (xp[2 * i], 0)),
            pl.BlockSpec((tile_rows, LANES), lambda i, xp, *_: (xp[2 * i + 1], 0)),
            pl.BlockSpec(memory_space=pl.ANY),
            whole((N_EXPERTS, 2 * D_FF)),
            pl.BlockSpec(memory_space=pl.ANY),
            whole((N_EXPERTS, D_MODEL)),
        ],
        out_specs=pl.BlockSpec(memory_space=pl.ANY),
        scratch_shapes=[
            pltpu.VMEM((tile_rows + SUBLANES, LANES), F32),
            pltpu.VMEM((tile_rows + SUBLANES, LANES), F32),
            pltpu.VMEM((D_MODEL, 2 * D_FF), F32),
            pltpu.VMEM((D_FF, D_MODEL), F32),
            pltpu.VMEM((D_MODEL, 2 * D_FF), BF16),
            pltpu.VMEM((D_FF, D_MODEL), BF16),
            pltpu.SemaphoreType.DMA((2,)),
            pltpu.SemaphoreType.DMA((2,)),
        ],
    )
    return pl.pallas_call(
        _experts_kernel,
        grid_spec=grid_spec,
        out_shape=jax.ShapeDtypeStruct((n_slots, ROW_TILES, LANES), F32),
        compiler_params=pltpu.CompilerParams(
            dimension_semantics=("arbitrary",), vmem_limit_bytes=VMEM_LIMIT),
        name="experts",
    )(xpage, spage, block_e, first, next_e, n_used,
      row_slot, row_slot, row_slot, x_pages_2d, x_pages_2d, w_e1, b_e1, w_e2, b_e2)


def _combine_kernel(h1_ref, y0_ref, y1_ref, y2_ref, y3_ref, gcol_ref, g_ref, b_ref, out_ref):
    tc = COMBINE_TILE
    gcol = gcol_ref[...]
    y_refs = (y0_ref, y1_ref, y2_ref, y3_ref)
    chunks = []
    for c in range(ROW_TILES):
        z = DEEPNORM_ALPHA * h1_ref[pl.ds(c, tc, stride=ROW_TILES), :]
        for kk in range(TOP_K):
            z = z + gcol[:, kk:kk + 1] * y_refs[kk][pl.ds(c, tc, stride=ROW_TILES), :]
        chunks.append(z)
    total = chunks[0].sum(axis=1, keepdims=True)
    for z in chunks[1:]:
        total = total + z.sum(axis=1, keepdims=True)
    mu = total * (1.0 / D_MODEL)
    sq = None
    for z in chunks:
        zc = z - mu
        part = (zc * zc).sum(axis=1, keepdims=True)
        sq = part if sq is None else sq + part
    inv = lax.rsqrt(sq * (1.0 / D_MODEL) + LN_EPS)
    for c, z in enumerate(chunks):
        cs = slice(c * LANES, (c + 1) * LANES)
        out_ref[:, cs] = (z - mu) * inv * g_ref[:, cs] + b_ref[:, cs]


def _combine_call(h1_2d, y_2d, gcol, ln_g, ln_b, n_tokens):
    tc = COMBINE_TILE
    n_t = n_tokens // tc
    blk = tc * ROW_TILES
    y_spec = lambda kk: pl.BlockSpec((blk, LANES), lambda i: (kk * n_t + i, 0))
    return pl.pallas_call(
        _combine_kernel,
        grid=(n_t,),
        in_specs=[
            pl.BlockSpec((blk, LANES), lambda i: (i, 0)),
            y_spec(0), y_spec(1), y_spec(2), y_spec(3),
            pl.BlockSpec((tc, LANES), lambda i: (i, 0)),
            pl.BlockSpec((1, D_MODEL), lambda i: (0, 0)),
            pl.BlockSpec((1, D_MODEL), lambda i: (0, 0)),
        ],
        out_specs=pl.BlockSpec((tc, D_MODEL), lambda i: (i, 0)),
        out_shape=jax.ShapeDtypeStruct((n_tokens, D_MODEL), F32),
        compiler_params=pltpu.CompilerParams(
            dimension_semantics=("arbitrary",), vmem_limit_bytes=VMEM_LIMIT),
        name="combine",
    )(h1_2d, y_2d, y_2d, y_2d, y_2d, gcol, ln_g, ln_b)


def _prepare_mixer_weights(ln0_g, ln0_b, w_in, conv_w, conv_b, w_q, w_k, b_if, mh_gain, w_pool,
                           b_pool, ls_pool, w_branch_pool, w_branch_mlstm, w_out, ln1_g, ln1_b,
                           w_router, b_router):
    w_main = w_in[:, :COL_IF_END].astype(BF16)
    w_gates = w_in[:, COL_GATES:COL_GATES + 2 * D_MODEL].astype(BF16)
    bif = jnp.concatenate([b_if, jnp.zeros((LANES - 2 * N_HEADS,), F32)]).reshape(1, LANES)
    wqk = jnp.concatenate([w_q * (HEAD_DIM ** -0.5), w_k], axis=-1).astype(BF16)
    wpool_bd = jax.scipy.linalg.block_diag(*[w_pool[g] for g in range(len(POOL_WINDOWS))]).astype(BF16)
    wr_hi = w_router.astype(BF16)
    wr_lo = (w_router - wr_hi.astype(F32)).astype(BF16)
    lane_pad = jnp.zeros((D_MODEL, LANES - N_EXPERTS), BF16)
    wr_hi_lo = jnp.concatenate([wr_hi, lane_pad, wr_lo, lane_pad], axis=1)
    row = lambda v: v.reshape(1, -1)
    return (row(ln0_g), row(ln0_b), w_main, w_gates, bif, conv_w, row(conv_b), wqk, row(mh_gain), wpool_bd,
            row(b_pool), row(ls_pool), w_branch_pool.astype(BF16), w_branch_mlstm.astype(BF16),
            w_out.astype(BF16), row(ln1_g), row(ln1_b), wr_hi_lo,
            b_router.reshape(N_EXPERTS, 1))


def _plan(page_expert1, counts, pos, n_tokens, n_pages, n_rows_total):
    rows = EXPERT_ROWS
    n_assign = TOP_K * n_tokens
    i32 = jnp.int32
    slots = jnp.arange(n_assign, dtype=i32)
    filler = n_rows_total // rows
    hit = jnp.zeros(((filler + 1) * rows,), i32).at[pos.reshape(-1)].add(slots + 1)
    pad_slot = n_assign + jnp.arange((filler + 1) * rows, dtype=i32) % rows
    row_slot = jnp.where(hit == 0, pad_slot, hit - 1).reshape(filler + 1, 1, rows)

    pidx = jnp.arange(n_pages, dtype=i32)
    used = page_expert1 > 0
    n_used = jnp.sum(used.astype(i32))
    page_e = page_expert1 - 1
    key = jnp.where(used, page_e, N_EXPERTS) * n_pages + pidx
    place = jnp.sum((key[None, :] < key[:, None]).astype(i32), axis=1)
    at_block = place[None, :] == pidx[:, None]
    order = jnp.sum(jnp.where(at_block, pidx[None, :], 0), axis=1)
    block_e = jnp.clip(jnp.sum(jnp.where(at_block, page_e[None, :], 0), axis=1), 0, N_EXPERTS - 1)
    valid = pidx < n_used
    xpage = jnp.where(valid, order, 0)
    spage = jnp.where(valid, order, filler)
    prev_e = jnp.concatenate([jnp.full((1,), -1, i32), block_e[:-1]])
    first = jnp.logical_and(valid, block_e != prev_e).astype(i32)
    e_iota = jnp.arange(N_EXPERTS, dtype=i32)
    pages_per_e = (counts + rows - 1) // rows
    seg_end = jnp.cumsum(pages_per_e)
    mine = block_e[:, None] == e_iota[None, :]
    end_blk = jnp.sum(jnp.where(mine, seg_end[None, :], 0), axis=1)
    e_at_end = jnp.sum(jnp.where(pidx[None, :] == end_blk[:, None], block_e[None, :], 0), axis=1)
    next_e = jnp.where(end_blk < n_used, e_at_end, -1).astype(i32)
    plan = (xpage.astype(i32), spage.astype(i32), block_e.astype(i32), first, next_e,
            n_used.reshape(1).astype(i32))
    return plan, row_slot


def kernel(x, ln0_g, ln0_b, w_in, conv_w, conv_b, w_q, w_k, b_if, mh_gain, w_pool, b_pool, ls_pool,
           w_branch_pool, w_branch_mlstm, w_out, ln1_g, ln1_b, w_router, b_router, w_e1, b_e1,
           w_e2, b_e2, ln2_g, ln2_b):
    bsz, seq, _ = x.shape
    n_tokens = bsz * seq
    rows = EXPERT_ROWS
    assert w_in.shape[0] == 1, "single-layer trunk"
    assert seq % TOKEN_TILE == 0 and n_tokens % COMBINE_TILE == 0
    n_pages = (TOP_K * n_tokens + N_EXPERTS * (rows - 1) + rows - 1) // rows
    n_pages += n_pages % 2
    weights = _prepare_mixer_weights(
        ln0_g, ln0_b, w_in[0], conv_w[0], conv_b[0], w_q[0], w_k[0], b_if[0], mh_gain[0], w_pool[0],
        b_pool[0], ls_pool[0], w_branch_pool[0], w_branch_mlstm[0], w_out[0], ln1_g[0], ln1_b[0],
        w_router[0], b_router[0])
    h1_2d, x_pages, pos8, gcol, cnt, pe = _mixer_call(x, weights, n_pages)
    counts = cnt[:, 0].astype(jnp.int32)
    page_expert1 = pe[0, :n_pages].astype(jnp.int32)
    plan, row_slot = _plan(page_expert1, counts, pos8[:TOP_K], n_tokens, n_pages, x_pages.shape[0])
    y_slots = _experts_call(
        plan, row_slot, x_pages.reshape(-1, LANES), w_e1[0], b_e1[0], w_e2[0], b_e2[0], n_tokens)
    out = _combine_call(h1_2d, y_slots.reshape(-1, LANES), gcol, ln2_g[0].reshape(1, D_MODEL),
                        ln2_b[0].reshape(1, D_MODEL), n_tokens)
    return out.reshape(bsz, seq, D_MODEL)
```

```python
import functools

import jax
import jax.numpy as jnp
from jax import lax
from jax.experimental import pallas as pl
from jax.experimental.pallas import tpu as pltpu

F32 = jnp.float32
BF16 = jnp.bfloat16

D_MODEL = 1024
N_HEADS = 4
HEAD_DIM = 256
POOL_WIDTH = 512
POOL_GROUP = 128
POOL_WINDOWS = (2, 4, 8, 16)
CONV_WIDTH = 4
N_EXPERTS = 32
TOP_K = 4
D_FF = 1024
SWIGLU_ALPHA = 1.702
SWIGLU_LIMIT = 7.0
LN_EPS = 1e-5
DEEPNORM_ALPHA = 2.0 ** 0.25

SUBLANES = 8
LANES = 128
ROW_TILES = D_MODEL // LANES

COL_P = 0
COL_U = COL_P + POOL_WIDTH
COL_V = COL_U + D_MODEL
COL_O = COL_V + D_MODEL
COL_IF = COL_O + D_MODEL
COL_IF_END = COL_IF + LANES
COL_GATES = COL_IF + 2 * N_HEADS

TOKEN_TILE = 256
POOL_HALO = 24
CONV_HALO = 8
EXPERT_ROWS = 256
ROW_DMA_GROUPS = 12
SCATTER_FIRST = 1
SCATTER_STAGES = 3
EXPERT_CHUNK = 2 * LANES
COMBINE_TILE = 512
DMA_PRIORITIES = 2
VMEM_LIMIT = 56 * 1024 * 1024

assert TOKEN_TILE <= EXPERT_ROWS
assert (TOP_K * TOKEN_TILE) % EXPERT_ROWS == 0


def _layer_norm(x, g, b):
    mu = jnp.mean(x, axis=-1, keepdims=True)
    xc = x - mu
    var = jnp.mean(xc * xc, axis=-1, keepdims=True)
    return xc * lax.rsqrt(var + LN_EPS) * g + b


def _sigmoid(x):
    return 0.5 * jnp.tanh(0.5 * x) + 0.5


def _log_sigmoid(x):
    return -(jnp.maximum(-x, 0.0) + jnp.log(1.0 + jnp.exp(-jnp.abs(x))))


def _split3(x):
    hi = x.astype(BF16)
    r1 = x - hi.astype(F32)
    mid = r1.astype(BF16)
    lo = (r1 - mid.astype(F32)).astype(BF16)
    return hi, mid, lo


def _dot(a, b):
    return jnp.dot(a, b, preferred_element_type=F32)


def _dot_nt(a, b):
    return lax.dot_general(a, b, (((1,), (1,)), ((), ())), preferred_element_type=F32)


def _tile_lanes(row, n):
    return jnp.concatenate([row] * n, axis=1)


def _mixer_kernel(x_ref, ln0g_ref, ln0b_ref, win_ref, wgate_ref, bif_ref, convw_ref, convb_ref, wqk_ref,
                  gain_ref, wpool_ref, bpool_ref, lspool_ref, wbp_ref, wbm_ref, wout_ref,
                  ln1g_ref, ln1b_ref, wr_ref, br_ref,
                  h1_hbm, xp_hbm, pos_ref, gcol_ref, cnt_ref, pe_ref,
                  pext_ref, pw_ref, uext_ref, ct_ref, n_ref, m_ref, run_ref, page_ref, free_ref,
                  hx_ref, posv_ref, poss_ref, zb_ref, stv_ref, sts_ref, psem, rsem, hsem, zsem,
                  *, n_pages):
    tm = TOKEN_TILE
    tile_rows = tm * ROW_TILES
    b = pl.program_id(0)
    s = pl.program_id(1)
    n_s = pl.num_programs(1)
    g = b * n_s + s
    last = pl.num_programs(0) * n_s - 1
    par = g % 2
    q = 1 - par
    dyn_zero = lax.shift_right_arithmetic(g, 31)

    def hx_tile(p):
        return hx_ref.at[p, pl.ds(0, tile_rows)]

    def row_copy(p, t, k):
        return pltpu.make_async_copy(
            hx_ref.at[p, pl.ds(t * ROW_TILES, ROW_TILES)], xp_hbm.at[poss_ref[p, k, t]], rsem.at[p])

    def wait_row_copies(p):
        for _ in range(TOP_K):
            pltpu.make_async_copy(hx_tile(p), hx_tile(p), rsem.at[p]).wait()

    def h1_write(p, tile_index):
        off = pl.multiple_of(tile_index * tile_rows, tile_rows)
        return pltpu.make_async_copy(hx_tile(p), h1_hbm.at[pl.ds(off, tile_rows)], hsem.at[p])

    def pos_to_smem(p):
        return pltpu.make_async_copy(posv_ref.at[p], poss_ref.at[p], psem.at[p])

    @pl.when(s == 0)
    def _():
        pext_ref[0:POOL_HALO, :] = jnp.zeros((POOL_HALO, POOL_WIDTH), F32)
        uext_ref[0:CONV_HALO, :] = jnp.zeros((CONV_HALO, D_MODEL), F32)
        ct_ref[...] = jnp.zeros_like(ct_ref)
        n_ref[...] = jnp.zeros_like(n_ref)
        m_ref[...] = jnp.zeros_like(m_ref)
        pw_ref[:, 0:SUBLANES, :] = jnp.zeros((2, SUBLANES, POOL_GROUP), F32)

    @pl.when(g == 0)
    def _():
        run_ref[...] = jnp.zeros_like(run_ref)
        page_ref[...] = jnp.zeros_like(page_ref)
        free_ref[...] = jnp.zeros_like(free_ref)
        pe_ref[...] = jnp.zeros_like(pe_ref)
        zb_ref[...] = jnp.zeros_like(zb_ref)
        hx_ref[1] = jnp.zeros((tile_rows + SUBLANES, LANES), F32)
        for k in range(TOP_K):
            for t in range(tm):
                poss_ref[1, k, t] = n_pages * EXPERT_ROWS + k * tm + t

    @pl.when(g >= 1)
    def _():
        pos_to_smem(q).wait()
        wait_row_copies(par)

    @pl.when(g >= 2)
    def _():
        h1_write(par, 0).wait()

    def start_row_copies(grp):
        for t in range(tm * grp // ROW_DMA_GROUPS, tm * (grp + 1) // ROW_DMA_GROUPS):
            for k in range(TOP_K):
                row_copy(q, t, k).start(priority=k % DMA_PRIORITIES)
        spare = pl.multiple_of(tile_rows + dyn_zero * SUBLANES, SUBLANES)
        hx_ref[q, pl.ds(spare, SUBLANES), :] = jnp.full((SUBLANES, LANES), dyn_zero.astype(F32))
        return hx_ref[q, pl.ds(spare, SUBLANES), :][0:1, :]

    ln0b = ln0b_ref[...] + _tile_lanes(start_row_copies(0), ROW_TILES)
    h0 = _layer_norm(x_ref[...], ln0g_ref[...], ln0b)
    h0b = h0.astype(BF16)

    def proj(lo, hi):
        return _dot(h0b, win_ref[:, lo:hi])

    pext_ref[POOL_HALO:POOL_HALO + tm, :] = proj(COL_P, COL_U)
    tpos = s * tm + lax.broadcasted_iota(jnp.int32, (tm, 1), 0)
    groups = []
    end = POOL_HALO + tm
    for gi, w in enumerate(POOL_WINDOWS):
        cols = slice(gi * POOL_GROUP, (gi + 1) * POOL_GROUP)
        cur = pext_ref[POOL_HALO:end, cols]
        src, src_cols, k, nxt = pext_ref, cols, 1, 0
        while 2 * k < w:
            pw_ref[nxt, SUBLANES:end, :] = (src[SUBLANES:end, src_cols]
                                            + src[SUBLANES - k:end - k, src_cols])
            src, src_cols, k, nxt = pw_ref.at[nxt], slice(None), 2 * k, 1 - nxt
        acc = src[POOL_HALO:end, src_cols] + src[POOL_HALO - k:end - k, src_cols]
        inv_cnt = 1.0 / jnp.minimum(tpos + 1, w).astype(F32)
        groups.append(acc * inv_cnt - cur)
    pooled = jnp.concatenate(groups, axis=1)
    pext_ref[0:POOL_HALO, :] = pext_ref[tm:tm + POOL_HALO, :]
    bpool = bpool_ref[...] + _tile_lanes(start_row_copies(1), POOL_WIDTH // LANES)
    lspool = lspool_ref[...] + _tile_lanes(start_row_copies(8), POOL_WIDTH // LANES)
    mixed = (_dot(pooled.astype(BF16), wpool_ref[...]) + bpool) * lspool
    y_pool = _dot(mixed.astype(BF16), wbp_ref[...])

    uext_ref[CONV_HALO:CONV_HALO + tm, :] = proj(COL_U, COL_V)
    conv = convb_ref[...] + _tile_lanes(start_row_copies(2), ROW_TILES)
    for j in range(CONV_WIDTH):
        off = CONV_HALO - (CONV_WIDTH - 1) + j
        conv = conv + convw_ref[j:j + 1, :] * uext_ref[off:off + tm, :]
    uext_ref[0:CONV_HALO, :] = uext_ref[tm:tm + CONV_HALO, :]
    ucb = (conv * _sigmoid(conv)).astype(BF16)
    vb = proj(COL_V, COL_O).astype(BF16)

    slab = proj(COL_IF, COL_IF_END) + (bif_ref[...] + start_row_copies(3))
    lane = lax.broadcasted_iota(jnp.int32, (tm, LANES), 1)
    is_f = jnp.logical_and(lane >= N_HEADS, lane < 2 * N_HEADS)
    slab = jnp.where(is_f, _log_sigmoid(slab), slab)
    row_i = lax.broadcasted_iota(jnp.int32, (tm, tm), 0)
    col_i = lax.broadcasted_iota(jnp.int32, (tm, tm), 1)
    causal = row_i >= col_i
    tri = jnp.where(causal, 1.0, 0.0).astype(BF16)
    hi, mid, lo = _split3(slab)
    bcol = _dot(tri, hi) + _dot(tri, mid) + _dot(tri, lo)
    slab_t = slab.T
    bcol_t = bcol.T

    heads = []
    for h in range(N_HEADS):
        hs = slice(h * HEAD_DIM, (h + 1) * HEAD_DIM)
        qk = _dot(ucb[:, hs], wqk_ref[h])
        q_h = qk[:, :HEAD_DIM]
        k_h = qk[:, HEAD_DIM:]
        qb = q_h.astype(BF16)
        kb = k_h.astype(BF16)
        vh = vb[:, hs]

        i_c = slab[:, h:h + 1]
        b_c = bcol[:, N_HEADS + h:N_HEADS + h + 1]
        i_r = slab_t[h:h + 1, :]
        b_r = bcol_t[N_HEADS + h:N_HEADS + h + 1, :]
        m_prev = m_ref[:, h:h + 1] + start_row_copies(4 + h)[:, 0:1]

        d_log = jnp.where(causal, b_c - (b_r - i_r), -jnp.inf)
        m_inter = b_c + m_prev
        m_t = jnp.maximum(m_inter, jnp.max(d_log, axis=1, keepdims=True))
        w_intra = jnp.exp(d_log - m_t)
        sc = _dot_nt(qb, kb) * w_intra
        w_inter = jnp.exp(m_inter - m_t)
        ctb = ct_ref[h].astype(BF16)
        num = _dot(sc.astype(BF16), vh) + w_inter * _dot(qb, ctb)
        qn = jnp.sum(q_h * n_ref[h], axis=1, keepdims=True)
        den = jnp.sum(sc, axis=1, keepdims=True) + w_inter * qn
        hh = num * (1.0 / jnp.maximum(jnp.abs(den), jnp.exp(-m_t)))
        mu = jnp.mean(hh, axis=1, keepdims=True)
        hc = hh - mu
        var = jnp.mean(hc * hc, axis=1, keepdims=True)
        heads.append(hc * lax.rsqrt(var + LN_EPS))

        g_last = b_r[:, tm - 1:tm]
        m_new = jnp.maximum(g_last + m_prev, jnp.max(g_last - b_r + i_r, axis=1, keepdims=True))
        decay = jnp.exp(g_last + m_prev - m_new)
        w_state = jnp.exp(g_last - b_c + i_c - m_new)
        kw = k_h * w_state
        ct_ref[h] = decay * ct_ref[h] + _dot(kw.T.astype(BF16), vh)
        n_ref[h] = decay * n_ref[h] + jnp.sum(kw, axis=0, keepdims=True)
        m_ref[:, h:h + 1] = m_new

    hn = jnp.concatenate(heads, axis=1) * (gain_ref[...] + _tile_lanes(start_row_copies(9), ROW_TILES))
    h_out = _sigmoid(proj(COL_O, COL_IF)) * hn
    y_mlstm = _dot(h_out.astype(BF16), wbm_ref[...])

    merged = (_sigmoid(_dot(h0b, wgate_ref[:, 0:D_MODEL])) * y_pool
              + _sigmoid(_dot(h0b, wgate_ref[:, D_MODEL:2 * D_MODEL])) * y_mlstm)
    mix = _dot(merged.astype(BF16), wout_ref[...])
    ln1g = ln1g_ref[...] + _tile_lanes(start_row_copies(10), ROW_TILES)
    ln1b = ln1b_ref[...] + _tile_lanes(start_row_copies(11), ROW_TILES)
    h1 = _layer_norm(DEEPNORM_ALPHA * h0 + mix, ln1g, ln1b)
    for c in range(ROW_TILES):
        hx_ref[par, pl.ds(c, tm, stride=ROW_TILES), :] = h1[:, c * LANES:(c + 1) * LANES]
    h1_write(par, g).start()

    h1_hi = h1.astype(BF16)
    h1_lo = (h1 - h1_hi.astype(F32)).astype(BF16)
    la = _dot(h1_hi, wr_ref[...])
    lb = _dot(h1_lo, wr_ref[:, 0:LANES])
    logits = (la[:, 0:LANES] + la[:, LANES:2 * LANES] + lb).T[0:N_EXPERTS] + br_ref[...]
    e_iota = lax.broadcasted_iota(jnp.int32, (N_EXPERTS, tm), 0)
    vals, onehots = [], []
    lg = logits
    for _ in range(TOP_K):
        mx = jnp.max(lg, axis=0, keepdims=True)
        sel = jnp.min(jnp.where(lg == mx, e_iota, N_EXPERTS), axis=0, keepdims=True)
        oh = e_iota == sel
        lg = jnp.where(oh, -jnp.inf, lg)
        vals.append(mx)
        onehots.append(oh)
    exps = [jnp.exp(v - vals[0]) for v in vals]
    inv_den = 1.0 / (exps[0] + exps[1] + exps[2] + exps[3])
    gates = [e * inv_den for e in exps]

    oh_all = jnp.where(onehots[0], 1.0, 0.0)
    for oh in onehots[1:]:
        oh_all = oh_all + jnp.where(oh, 1.0, 0.0)
    strict = jnp.where(row_i < col_i, 1.0, 0.0).astype(BF16)
    run = run_ref[:, 0:1]
    rank = _dot(oh_all.astype(BF16), strict) + run
    count = jnp.sum(oh_all, axis=1, keepdims=True)
    inv_rows = 1.0 / EXPERT_ROWS
    pages_before = jnp.ceil(run * inv_rows)
    need = jnp.ceil((run + count) * inv_rows) - pages_before
    er = lax.broadcasted_iota(jnp.int32, (N_EXPERTS, N_EXPERTS), 0)
    ec = lax.broadcasted_iota(jnp.int32, (N_EXPERTS, N_EXPERTS), 1)
    earlier = jnp.where(er > ec, 1.0, 0.0).astype(BF16)
    need_b = jnp.broadcast_to(need, (N_EXPERTS, LANES)).astype(BF16)
    new_page = free_ref[0:1, 0:1] + _dot(earlier, need_b)[:, 0:1]
    page_seq = jnp.floor(rank * inv_rows)
    page = jnp.where(page_seq < pages_before, page_ref[:, 0:1], new_page)
    pos_all = page * EXPERT_ROWS + (rank - page_seq * EXPERT_ROWS)
    positions = [jnp.sum(jnp.where(oh, pos_all, 0.0), axis=0, keepdims=True) for oh in onehots]
    page_ref[...] = jnp.where(need > 0.0, new_page, page_ref[...])
    free_ref[...] = free_ref[...] + jnp.sum(need, axis=0, keepdims=True)
    run_ref[...] = run_ref[...] + count
    cnt_ref[...] = run_ref[...]
    p_lane = lax.broadcasted_iota(jnp.int32, (N_EXPERTS, pe_ref.shape[1]), 1).astype(F32)
    e_plus1 = (lax.broadcasted_iota(jnp.int32, (N_EXPERTS, 1), 0) + 1).astype(F32)
    taken = jnp.logical_and(p_lane == new_page, need > 0.0)
    pe_ref[...] = pe_ref[...] + jnp.sum(jnp.where(taken, e_plus1, 0.0), axis=0, keepdims=True)

    r8 = lax.broadcasted_iota(jnp.int32, (SUBLANES, tm), 0)
    pos_out = jnp.zeros((SUBLANES, tm), jnp.int32)
    r128 = lax.broadcasted_iota(jnp.int32, (LANES, tm), 0)
    gate_rows = jnp.zeros((LANES, tm), F32)
    for kk in range(TOP_K):
        pos_out = jnp.where(r8 == kk, positions[kk].astype(jnp.int32), pos_out)
        gate_rows = jnp.where(r128 == kk, gates[kk], gate_rows)
    pos_ref[...] = pos_out
    gcol_ref[...] = gate_rows.T
    posv_ref[par] = pos_out
    pos_to_smem(par).start()

    @pl.when(g == last)
    def _():
        pos_to_smem(par).wait()
        for t in range(tm):
            for k in range(TOP_K):
                row_copy(par, t, k).start(priority=k % DMA_PRIORITIES)
        wait_row_copies(q)
        wait_row_copies(par)
        h1_write(q, 0).wait()
        h1_write(par, 0).wait()

        st_lane = lax.broadcasted_iota(jnp.int32, (N_EXPERTS, LANES), 1)
        state = jnp.where(st_lane == 0, page_ref[...], jnp.where(st_lane == 1, run_ref[...], free_ref[...]))
        stv_ref[...] = state.astype(jnp.int32)
        state_copy = pltpu.make_async_copy(stv_ref, sts_ref, psem.at[par])
        state_copy.start()
        state_copy.wait()

        def zero_fill(wait):
            def act(rows_dst, n):
                cp = pltpu.make_async_copy(zb_ref.at[pl.ds(0, n)], xp_hbm.at[pl.ds(rows_dst, n)], zsem)
                if wait:
                    cp.wait()
                else:
                    cp.start()

            def unused_page(p, carry):
                act(pl.multiple_of(p * EXPERT_ROWS, EXPERT_ROWS), EXPERT_ROWS)
                return carry

            lax.fori_loop(sts_ref[0, 2], n_pages, unused_page, 0)

            def page_tail(e, carry):
                filled = sts_ref[e, 1] & (EXPERT_ROWS - 1)
                pad = jnp.where(filled == 0, 0, EXPERT_ROWS - filled)
                dst = sts_ref[e, 0] * EXPERT_ROWS + filled
                size = 1
                while size < EXPERT_ROWS:
                    @pl.when((pad & size) != 0)
                    def _(dst=dst, size=size):
                        act(dst, size)
                    dst = dst + (pad & size)
                    size *= 2
                return carry

            lax.fori_loop(0, N_EXPERTS, page_tail, 0)

        zero_fill(wait=False)
        zero_fill(wait=True)


def _const_spec(shape):
    zeros = (0,) * len(shape)
    return pl.BlockSpec(shape, lambda b, s: zeros, pipeline_mode=pl.Buffered(1))


def _mixer_call(x, weights, n_pages):
    bsz, seq, _ = x.shape
    tm = TOKEN_TILE
    n_s = seq // tm
    t_total = bsz * seq
    assert bsz * n_s >= 2
    tile = lambda b, s: (b * n_s + s)
    spare_pages = TOP_K * tm // EXPERT_ROWS
    pe_lanes = -(-n_pages // LANES) * LANES
    in_specs = [pl.BlockSpec((None, tm, D_MODEL), lambda b, s: (b, s, 0))]
    in_specs += [_const_spec(w.shape) for w in weights]
    out_shape = (
        jax.ShapeDtypeStruct((t_total * ROW_TILES, LANES), F32),
        jax.ShapeDtypeStruct(((n_pages + spare_pages) * EXPERT_ROWS, ROW_TILES, LANES), F32),
        jax.ShapeDtypeStruct((SUBLANES, t_total), jnp.int32),
        jax.ShapeDtypeStruct((t_total, LANES), F32),
        jax.ShapeDtypeStruct((N_EXPERTS, LANES), F32),
        jax.ShapeDtypeStruct((SUBLANES, pe_lanes), F32),
    )
    out_specs = (
        pl.BlockSpec(memory_space=pl.ANY),
        pl.BlockSpec(memory_space=pl.ANY),
        pl.BlockSpec((SUBLANES, tm), lambda b, s: (0, tile(b, s))),
        pl.BlockSpec((tm, LANES), lambda b, s: (tile(b, s), 0)),
        pl.BlockSpec((N_EXPERTS, LANES), lambda b, s: (0, 0)),
        pl.BlockSpec((SUBLANES, pe_lanes), lambda b, s: (0, 0)),
    )
    scratch = [
        pltpu.VMEM((POOL_HALO + tm, POOL_WIDTH), F32),
        pltpu.VMEM((2, POOL_HALO + tm, POOL_GROUP), F32),
        pltpu.VMEM((CONV_HALO + tm, D_MODEL), F32),
        pltpu.VMEM((N_HEADS, HEAD_DIM, HEAD_DIM), F32),
        pltpu.VMEM((N_HEADS, 1, HEAD_DIM), F32),
        pltpu.VMEM((1, LANES), F32),
        pltpu.VMEM((N_EXPERTS, LANES), F32),
        pltpu.VMEM((N_EXPERTS, LANES), F32),
        pltpu.VMEM((1, LANES), F32),
        pltpu.VMEM((2, tm * ROW_TILES + SUBLANES, LANES), F32),
        pltpu.VMEM((2, SUBLANES, tm), jnp.int32),
        pltpu.SMEM((2, SUBLANES, tm), jnp.int32),
        pltpu.VMEM((EXPERT_ROWS, ROW_TILES, LANES), F32),
        pltpu.VMEM((N_EXPERTS, LANES), jnp.int32),
        pltpu.SMEM((N_EXPERTS, LANES), jnp.int32),
        pltpu.SemaphoreType.DMA((2,)),
        pltpu.SemaphoreType.DMA((2,)),
        pltpu.SemaphoreType.DMA((2,)),
        pltpu.SemaphoreType.DMA(()),
    ]
    return pl.pallas_call(
        functools.partial(_mixer_kernel, n_pages=n_pages),
        grid=(bsz, n_s),
        in_specs=in_specs,
        out_specs=out_specs,
        out_shape=out_shape,
        scratch_shapes=scratch,
        compiler_params=pltpu.CompilerParams(
            dimension_semantics=("arbitrary", "arbitrary"), vmem_limit_bytes=VMEM_LIMIT),
        name="mixer",
    )(x, *weights)


def _experts_kernel(xpage_ref, spage_ref, be_ref, first_ref, nexte_ref, nused_ref,
                    slot_p_ref, slot_a_ref, slot_b_ref, xa_ref, xb_ref,
                    w1_hbm, b1_ref, w2_hbm, b2_ref, y_hbm,
                    yb0_ref, yb1_ref, w1s_ref, w2s_ref, w1b_ref, w2b_ref, ssem, wsem):
    del xpage_ref, spage_ref
    rows = EXPERT_ROWS
    tile = rows * ROW_TILES
    i = pl.program_id(0)
    last = pl.num_programs(0) - 1
    n_used = nused_ref[0]
    yb = (yb0_ref, yb1_ref)
    x_in = (xa_ref, xb_ref)
    dyn_zero = lax.shift_right_arithmetic(n_used, 31)
    n_chunks = D_FF // EXPERT_CHUNK
    assert SCATTER_FIRST + SCATTER_STAGES <= n_chunks + 1

    def scatter(slots, r, p):
        return pltpu.make_async_copy(
            yb[p].at[pl.ds(r * ROW_TILES, ROW_TILES)], y_hbm.at[slots[0, 0, r]], ssem.at[p])

    def start_scatters(stage, slots, p):
        if stage is None:
            r_range = range(rows)
        elif SCATTER_FIRST <= stage < SCATTER_FIRST + SCATTER_STAGES:
            k = stage - SCATTER_FIRST
            r_range = range(rows * k // SCATTER_STAGES, rows * (k + 1) // SCATTER_STAGES)
        else:
            return jnp.zeros((1, LANES), F32)
        for r in r_range:
            scatter(slots, r, p).start(priority=r % DMA_PRIORITIES)
        spare = pl.multiple_of(tile + dyn_zero * SUBLANES, SUBLANES)
        yb[p][pl.ds(spare, SUBLANES), :] = jnp.full((SUBLANES, LANES), dyn_zero.astype(F32))
        return yb[p][pl.ds(spare, SUBLANES), :][0:1, :]

    def wait_scatters(p):
        view = yb[p].at[pl.ds(0, tile)]
        pltpu.make_async_copy(view, view, ssem.at[p]).wait()

    def weight_copies(e):
        return (pltpu.make_async_copy(w1_hbm.at[e], w1s_ref, wsem.at[0]),
                pltpu.make_async_copy(w2_hbm.at[e], w2s_ref, wsem.at[1]))

    def switch_weights(blk):
        @pl.when(first_ref[blk] == 1)
        def _():
            for cp in weight_copies(0):
                cp.wait()
            step = 128
            for c in range(D_MODEL // step):
                w1b_ref[c * step:(c + 1) * step, :] = w1s_ref[c * step:(c + 1) * step, :].astype(BF16)
            half = D_FF // 2
            step = 64
            for c in range(half // step):
                pair = pltpu.pack_elementwise(
                    [w2s_ref[c * step:(c + 1) * step, :], w2s_ref[half + c * step:half + (c + 1) * step, :]],
                    packed_dtype=BF16)
                w2b_ref[2 * c * step:2 * (c + 1) * step, :] = pltpu.bitcast(pair, BF16)
            nxt = nexte_ref[blk]

            @pl.when(nxt >= 0)
            def _():
                for cp in weight_copies(nxt):
                    cp.start()

    def compute(p, blk, start_stage):
        e = be_ref[blk]
        x = jnp.concatenate(
            [x_in[p][pl.ds(c, rows, stride=ROW_TILES), :] for c in range(ROW_TILES)], axis=1).astype(BF16)
        b1 = b1_ref[pl.ds(e, 1), :]
        width = EXPERT_CHUNK
        even = (lax.broadcasted_iota(jnp.int32, (rows, width), 1) & 1) == 0
        zs = []
        for c in range(n_chunks):
            lo = c * width
            hi = D_FF + lo
            zero = _tile_lanes(start_stage(c), width // LANES)
            ha = _dot(x, w1b_ref[:, lo:lo + width]) + (b1[:, lo:lo + width] + zero)
            hb = _dot(x, w1b_ref[:, hi:hi + width]) + (b1[:, hi:hi + width] + zero)
            glu = jnp.where(even, ha, pltpu.roll(hb, 1, 1))
            lin = jnp.where(even, pltpu.roll(ha, width - 1, 1), hb)
            glu = jnp.minimum(glu, SWIGLU_LIMIT)
            lin = jnp.clip(lin, -SWIGLU_LIMIT, SWIGLU_LIMIT) + 1.0
            zs.append((glu * _sigmoid(SWIGLU_ALPHA * glu) * lin).astype(BF16))
        b2 = b2_ref[pl.ds(e, 1), :] + _tile_lanes(start_stage(n_chunks), ROW_TILES)
        y = _dot(jnp.concatenate(zs, axis=1), w2b_ref[...]) + b2
        for c in range(ROW_TILES):
            yb[p][pl.ds(c, rows, stride=ROW_TILES), :] = y[:, c * LANES:(c + 1) * LANES]

    def phase(blk, p, start_stage):
        switch_weights(blk)

        @pl.when(blk < n_used)
        def _():
            compute(p, blk, start_stage)

        @pl.when(blk >= n_used)
        def _():
            start_stage(None)

    @pl.when(i == 0)
    def _():
        yb1_ref[...] = jnp.zeros_like(yb1_ref)
        for cp in weight_copies(be_ref[0]):
            cp.start()

    @pl.when(i >= 1)
    def _():
        wait_scatters(0)

    phase(2 * i, 0, functools.partial(start_scatters, slots=slot_p_ref, p=1))

    wait_scatters(1)
    phase(2 * i + 1, 1, functools.partial(start_scatters, slots=slot_a_ref, p=0))

    @pl.when(i == last)
    def _():
        wait_scatters(0)
        for r in range(rows):
            scatter(slot_b_ref, r, 1).start(priority=r % DMA_PRIORITIES)
        wait_scatters(1)


def _experts_call(plan, row_slot, x_pages_2d, w_e1, b_e1, w_e2, b_e2, n_tokens):
    xpage, spage, block_e, first, next_e, n_used = plan
    rows = EXPERT_ROWS
    n_blocks = block_e.shape[0]
    assert n_blocks % 2 == 0
    n_slots = TOP_K * n_tokens + rows
    tile_rows = rows * ROW_TILES
    filler = row_slot.shape[0] - 1
    smem_rows = lambda index: pl.BlockSpec((1, 1, rows), index, memory_space=pltpu.SMEM)
    whole = lambda shape: pl.BlockSpec(shape, lambda i, *_: (0,) * len(shape))
    grid_spec = pltpu.PrefetchScalarGridSpec(
        num_scalar_prefetch=6,
        grid=(n_blocks // 2,),
        in_specs=[
            smem_rows(lambda i, xp, sp, *_: (jnp.where(i == 0, filler, sp[jnp.maximum(2 * i - 1, 0)]), 0, 0)),
            smem_rows(lambda i, xp, sp, *_: (sp[2 * i], 0, 0)),
            smem_rows(lambda i, xp, sp, *_: (sp[2 * i + 1], 0, 0)),
            pl.BlockSpec((tile_rows, LANES), lambda i, xp, *_: (xp[2 * i], 0)),
            pl.BlockSpec((tile_rows, LANES), lambda i, xp, *_: (xp[2 * i + 1], 0)),
            pl.BlockSpec(memory_space=pl.ANY),
            whole((N_EXPERTS, 2 * D_FF)),
            pl.BlockSpec(memory_space=pl.ANY),
            whole((N_EXPERTS, D_MODEL)),
        ],
        out_specs=pl.BlockSpec(memory_space=pl.ANY),
        scratch_shapes=[
            pltpu.VMEM((tile_rows + SUBLANES, LANES), F32),
            pltpu.VMEM((tile_rows + SUBLANES, LANES), F32),
            pltpu.VMEM((D_MODEL, 2 * D_FF), F32),
            pltpu.VMEM((D_FF, D_MODEL), F32),
            pltpu.VMEM((D_MODEL, 2 * D_FF), BF16),
            pltpu.VMEM((D_FF, D_MODEL), BF16),
            pltpu.SemaphoreType.DMA((2,)),
            pltpu.SemaphoreType.DMA((2,)),
        ],
    )
    return pl.pallas_call(
        _experts_kernel,
        grid_spec=grid_spec,
        out_shape=jax.ShapeDtypeStruct((n_slots, ROW_TILES, LANES), F32),
        compiler_params=pltpu.CompilerParams(
            dimension_semantics=("arbitrary",), vmem_limit_bytes=VMEM_LIMIT),
        name="experts",
    )(xpage, spage, block_e, first, next_e, n_used,
      row_slot, row_slot, row_slot, x_pages_2d, x_pages_2d, w_e1, b_e1, w_e2, b_e2)


def _combine_kernel(h1_ref, y0_ref, y1_ref, y2_ref, y3_ref, gcol_ref, g_ref, b_ref, out_ref):
    tc = COMBINE_TILE
    gcol = gcol_ref[...]
    y_refs = (y0_ref, y1_ref, y2_ref, y3_ref)
    chunks = []
    for c in range(ROW_TILES):
        z = DEEPNORM_ALPHA * h1_ref[pl.ds(c, tc, stride=ROW_TILES), :]
        for kk in range(TOP_K):
            z = z + gcol[:, kk:kk + 1] * y_refs[kk][pl.ds(c, tc, stride=ROW_TILES), :]
        chunks.append(z)
    total = chunks[0].sum(axis=1, keepdims=True)
    for z in chunks[1:]:
        total = total + z.sum(axis=1, keepdims=True)
    mu = total * (1.0 / D_MODEL)
    sq = None
    for z in chunks:
        zc = z - mu
        part = (zc * zc).sum(axis=1, keepdims=True)
        sq = part if sq is None else sq + part
    inv = lax.rsqrt(sq * (1.0 / D_MODEL) + LN_EPS)
    for c, z in enumerate(chunks):
        cs = slice(c * LANES, (c + 1) * LANES)
        out_ref[:, cs] = (z - mu) * inv * g_ref[:, cs] + b_ref[:, cs]


def _combine_call(h1_2d, y_2d, gcol, ln_g, ln_b, n_tokens):
    tc = COMBINE_TILE
    n_t = n_tokens // tc
    blk = tc * ROW_TILES
    y_spec = lambda kk: pl.BlockSpec((blk, LANES), lambda i: (kk * n_t + i, 0))
    return pl.pallas_call(
        _combine_kernel,
        grid=(n_t,),
        in_specs=[
            pl.BlockSpec((blk, LANES), lambda i: (i, 0)),
            y_spec(0), y_spec(1), y_spec(2), y_spec(3),
            pl.BlockSpec((tc, LANES), lambda i: (i, 0)),
            pl.BlockSpec((1, D_MODEL), lambda i: (0, 0)),
            pl.BlockSpec((1, D_MODEL), lambda i: (0, 0)),
        ],
        out_specs=pl.BlockSpec((tc, D_MODEL), lambda i: (i, 0)),
        out_shape=jax.ShapeDtypeStruct((n_tokens, D_MODEL), F32),
        compiler_params=pltpu.CompilerParams(
            dimension_semantics=("arbitrary",), vmem_limit_bytes=VMEM_LIMIT),
        name="combine",
    )(h1_2d, y_2d, y_2d, y_2d, y_2d, gcol, ln_g, ln_b)


def _prepare_mixer_weights(ln0_g, ln0_b, w_in, conv_w, conv_b, w_q, w_k, b_if, mh_gain, w_pool,
                           b_pool, ls_pool, w_branch_pool, w_branch_mlstm, w_out, ln1_g, ln1_b,
                           w_router, b_router):
    w_main = w_in[:, :COL_IF_END].astype(BF16)
    w_gates = w_in[:, COL_GATES:COL_GATES + 2 * D_MODEL].astype(BF16)
    bif = jnp.concatenate([b_if, jnp.zeros((LANES - 2 * N_HEADS,), F32)]).reshape(1, LANES)
    wqk = jnp.concatenate([w_q * (HEAD_DIM ** -0.5), w_k], axis=-1).astype(BF16)
    wpool_bd = jax.scipy.linalg.block_diag(*[w_pool[g] for g in range(len(POOL_WINDOWS))]).astype(BF16)
    wr_hi = w_router.astype(BF16)
    wr_lo = (w_router - wr_hi.astype(F32)).astype(BF16)
    lane_pad = jnp.zeros((D_MODEL, LANES - N_EXPERTS), BF16)
    wr_hi_lo = jnp.concatenate([wr_hi, lane_pad, wr_lo, lane_pad], axis=1)
    row = lambda v: v.reshape(1, -1)
    return (row(ln0_g), row(ln0_b), w_main, w_gates, bif, conv_w, row(conv_b), wqk, row(mh_gain), wpool_bd,
            row(b_pool), row(ls_pool), w_branch_pool.astype(BF16), w_branch_mlstm.astype(BF16),
            w_out.astype(BF16), row(ln1_g), row(ln1_b), wr_hi_lo,
            b_router.reshape(N_EXPERTS, 1))


def _plan(page_expert1, counts, pos, n_tokens, n_pages, n_rows_total):
    rows = EXPERT_ROWS
    n_assign = TOP_K * n_tokens
    i32 = jnp.int32
    slots = jnp.arange(n_assign, dtype=i32)
    filler = n_rows_total // rows
    hit = jnp.zeros(((filler + 1) * rows,), i32).at[pos.reshape(-1)].add(slots + 1)
    pad_slot = n_assign + jnp.arange((filler + 1) * rows, dtype=i32) % rows
    row_slot = jnp.where(hit == 0, pad_slot, hit - 1).reshape(filler + 1, 1, rows)

    pidx = jnp.arange(n_pages, dtype=i32)
    used = page_expert1 > 0
    n_used = jnp.sum(used.astype(i32))
    page_e = page_expert1 - 1
    key = jnp.where(used, page_e, N_EXPERTS) * n_pages + pidx
    place = jnp.sum((key[None, :] < key[:, None]).astype(i32), axis=1)
    at_block = place[None, :] == pidx[:, None]
    order = jnp.sum(jnp.where(at_block, pidx[None, :], 0), axis=1)
    block_e = jnp.clip(jnp.sum(jnp.where(at_block, page_e[None, :], 0), axis=1), 0, N_EXPERTS - 1)
    valid = pidx < n_used
    xpage = jnp.where(valid, order, 0)
    spage = jnp.where(valid, order, filler)
    prev_e = jnp.concatenate([jnp.full((1,), -1, i32), block_e[:-1]])
    first = jnp.logical_and(valid, block_e != prev_e).astype(i32)
    e_iota = jnp.arange(N_EXPERTS, dtype=i32)
    pages_per_e = (counts + rows - 1) // rows
    seg_end = jnp.cumsum(pages_per_e)
    mine = block_e[:, None] == e_iota[None, :]
    end_blk = jnp.sum(jnp.where(mine, seg_end[None, :], 0), axis=1)
    e_at_end = jnp.sum(jnp.where(pidx[None, :] == end_blk[:, None], block_e[None, :], 0), axis=1)
    next_e = jnp.where(end_blk < n_used, e_at_end, -1).astype(i32)
    plan = (xpage.astype(i32), spage.astype(i32), block_e.astype(i32), first, next_e,
            n_used.reshape(1).astype(i32))
    return plan, row_slot


def kernel(x, ln0_g, ln0_b, w_in, conv_w, conv_b, w_q, w_k, b_if, mh_gain, w_pool, b_pool, ls_pool,
           w_branch_pool, w_branch_mlstm, w_out, ln1_g, ln1_b, w_router, b_router, w_e1, b_e1,
           w_e2, b_e2, ln2_g, ln2_b):
    bsz, seq, _ = x.shape
    n_tokens = bsz * seq
    rows = EXPERT_ROWS
    assert w_in.shape[0] == 1, "single-layer trunk"
    assert seq % TOKEN_TILE == 0 and n_tokens % COMBINE_TILE == 0
    n_pages = (TOP_K * n_tokens + N_EXPERTS * (rows - 1) + rows - 1) // rows
    n_pages += n_pages % 2
    weights = _prepare_mixer_weights(
        ln0_g, ln0_b, w_in[0], conv_w[0], conv_b[0], w_q[0], w_k[0], b_if[0], mh_gain[0], w_pool[0],
        b_pool[0], ls_pool[0], w_branch_pool[0], w_branch_mlstm[0], w_out[0], ln1_g[0], ln1_b[0],
        w_router[0], b_router[0])
    h1_2d, x_pages, pos8, gcol, cnt, pe = _mixer_call(x, weights, n_pages)
    counts = cnt[:, 0].astype(jnp.int32)
    page_expert1 = pe[0, :n_pages].astype(jnp.int32)
    plan, row_slot = _plan(page_expert1, counts, pos8[:TOP_K], n_tokens, n_pages, x_pages.shape[0])
    y_slots = _experts_call(
        plan, row_slot, x_pages.reshape(-1, LANES), w_e1[0], b_e1[0], w_e2[0], b_e2[0], n_tokens)
    out = _combine_call(h1_2d, y_slots.reshape(-1, LANES), gcol, ln2_g[0].reshape(1, D_MODEL),
                        ln2_b[0].reshape(1, D_MODEL), n_tokens)
    return out.reshape(bsz, seq, D_MODEL)
```

```python
import functools

import jax
import jax.numpy as jnp
from jax import lax
from jax.experimental import pallas as pl
from jax.experimental.pallas import tpu as pltpu

F32 = jnp.float32
BF16 = jnp.bfloat16

D_MODEL = 1024
N_HEADS = 4
HEAD_DIM = 256
POOL_WIDTH = 512
POOL_GROUP = 128
POOL_WINDOWS = (2, 4, 8, 16)
CONV_WIDTH = 4
N_EXPERTS = 32
TOP_K = 4
D_FF = 1024
SWIGLU_ALPHA = 1.702
SWIGLU_LIMIT = 7.0
LN_EPS = 1e-5
DEEPNORM_ALPHA = 2.0 ** 0.25

SUBLANES = 8
LANES = 128
ROW_TILES = D_MODEL // LANES

COL_P = 0
COL_U = COL_P + POOL_WIDTH
COL_V = COL_U + D_MODEL
COL_O = COL_V + D_MODEL
COL_IF = COL_O + D_MODEL
COL_IF_END = COL_IF + LANES
COL_GATES = COL_IF + 2 * N_HEADS

TOKEN_TILE = 256
POOL_HALO = 24
CONV_HALO = 8
EXPERT_ROWS = 256
ROW_DMA_GROUPS = 12
SCATTER_STAGES = 4
EXPERT_CHUNK = 2 * LANES
COMBINE_TILE = 512
DMA_PRIORITIES = 1
VMEM_LIMIT = 56 * 1024 * 1024

assert TOKEN_TILE <= EXPERT_ROWS
assert (TOP_K * TOKEN_TILE) % EXPERT_ROWS == 0


def _layer_norm(x, g, b):
    mu = jnp.mean(x, axis=-1, keepdims=True)
    xc = x - mu
    var = jnp.mean(xc * xc, axis=-1, keepdims=True)
    return xc * lax.rsqrt(var + LN_EPS) * g + b


def _sigmoid(x):
    return 0.5 * jnp.tanh(0.5 * x) + 0.5


def _log_sigmoid(x):
    return -(jnp.maximum(-x, 0.0) + jnp.log(1.0 + jnp.exp(-jnp.abs(x))))


def _split3(x):
    hi = x.astype(BF16)
    r1 = x - hi.astype(F32)
    mid = r1.astype(BF16)
    lo = (r1 - mid.astype(F32)).astype(BF16)
    return hi, mid, lo


def _dot(a, b):
    return jnp.dot(a, b, preferred_element_type=F32)


def _dot_nt(a, b):
    return lax.dot_general(a, b, (((1,), (1,)), ((), ())), preferred_element_type=F32)


def _tile_lanes(row, n):
    return jnp.concatenate([row] * n, axis=1)


def _mixer_kernel(x_ref, ln0g_ref, ln0b_ref, win_ref, wgate_ref, bif_ref, convw_ref, convb_ref, wqk_ref,
                  gain_ref, wpool_ref, bpool_ref, lspool_ref, wbp_ref, wbm_ref, wout_ref,
                  ln1g_ref, ln1b_ref, wr_ref, br_ref,
                  h1_hbm, xp_hbm, pos_ref, gcol_ref, cnt_ref, pe_ref,
                  pext_ref, pw_ref, uext_ref, ct_ref, n_ref, m_ref, run_ref, page_ref, free_ref,
                  hx_ref, posv_ref, poss_ref, zb_ref, stv_ref, sts_ref, psem, rsem, hsem, zsem,
                  *, n_pages):
    tm = TOKEN_TILE
    tile_rows = tm * ROW_TILES
    b = pl.program_id(0)
    s = pl.program_id(1)
    n_s = pl.num_programs(1)
    g = b * n_s + s
    last = pl.num_programs(0) * n_s - 1
    par = g % 2
    q = 1 - par
    dyn_zero = lax.shift_right_arithmetic(g, 31)

    def hx_tile(p):
        return hx_ref.at[p, pl.ds(0, tile_rows)]

    def row_copy(p, t, k):
        return pltpu.make_async_copy(
            hx_ref.at[p, pl.ds(t * ROW_TILES, ROW_TILES)], xp_hbm.at[poss_ref[p, k, t]], rsem.at[p])

    def wait_row_copies(p):
        for _ in range(TOP_K):
            pltpu.make_async_copy(hx_tile(p), hx_tile(p), rsem.at[p]).wait()

    def h1_write(p, tile_index):
        off = pl.multiple_of(tile_index * tile_rows, tile_rows)
        return pltpu.make_async_copy(hx_tile(p), h1_hbm.at[pl.ds(off, tile_rows)], hsem.at[p])

    def pos_to_smem(p):
        return pltpu.make_async_copy(posv_ref.at[p], poss_ref.at[p], psem.at[p])

    @pl.when(s == 0)
    def _():
        pext_ref[0:POOL_HALO, :] = jnp.zeros((POOL_HALO, POOL_WIDTH), F32)
        uext_ref[0:CONV_HALO, :] = jnp.zeros((CONV_HALO, D_MODEL), F32)
        ct_ref[...] = jnp.zeros_like(ct_ref)
        n_ref[...] = jnp.zeros_like(n_ref)
        m_ref[...] = jnp.zeros_like(m_ref)
        pw_ref[:, 0:SUBLANES, :] = jnp.zeros((2, SUBLANES, POOL_GROUP), F32)

    @pl.when(g == 0)
    def _():
        run_ref[...] = jnp.zeros_like(run_ref)
        page_ref[...] = jnp.zeros_like(page_ref)
        free_ref[...] = jnp.zeros_like(free_ref)
        pe_ref[...] = jnp.zeros_like(pe_ref)
        zb_ref[...] = jnp.zeros_like(zb_ref)
        hx_ref[1] = jnp.zeros((tile_rows + SUBLANES, LANES), F32)
        for k in range(TOP_K):
            for t in range(tm):
                poss_ref[1, k, t] = n_pages * EXPERT_ROWS + k * tm + t

    @pl.when(g >= 1)
    def _():
        pos_to_smem(q).wait()
        wait_row_copies(par)

    @pl.when(g >= 2)
    def _():
        h1_write(par, 0).wait()

    def start_row_copies(grp):
        for t in range(tm * grp // ROW_DMA_GROUPS, tm * (grp + 1) // ROW_DMA_GROUPS):
            for k in range(TOP_K):
                row_copy(q, t, k).start(priority=k % DMA_PRIORITIES)
        spare = pl.multiple_of(tile_rows + dyn_zero * SUBLANES, SUBLANES)
        hx_ref[q, pl.ds(spare, SUBLANES), :] = jnp.full((SUBLANES, LANES), dyn_zero.astype(F32))
        return hx_ref[q, pl.ds(spare, SUBLANES), :][0:1, :]

    ln0b = ln0b_ref[...] + _tile_lanes(start_row_copies(0), ROW_TILES)
    h0 = _layer_norm(x_ref[...], ln0g_ref[...], ln0b)
    h0b = h0.astype(BF16)

    def proj(lo, hi):
        return _dot(h0b, win_ref[:, lo:hi])

    pext_ref[POOL_HALO:POOL_HALO + tm, :] = proj(COL_P, COL_U)
    tpos = s * tm + lax.broadcasted_iota(jnp.int32, (tm, 1), 0)
    groups = []
    end = POOL_HALO + tm
    for gi, w in enumerate(POOL_WINDOWS):
        cols = slice(gi * POOL_GROUP, (gi + 1) * POOL_GROUP)
        cur = pext_ref[POOL_HALO:end, cols]
        src, src_cols, k, nxt = pext_ref, cols, 1, 0
        while 2 * k < w:
            pw_ref[nxt, SUBLANES:end, :] = (src[SUBLANES:end, src_cols]
                                            + src[SUBLANES - k:end - k, src_cols])
            src, src_cols, k, nxt = pw_ref.at[nxt], slice(None), 2 * k, 1 - nxt
        acc = src[POOL_HALO:end, src_cols] + src[POOL_HALO - k:end - k, src_cols]
        inv_cnt = 1.0 / jnp.minimum(tpos + 1, w).astype(F32)
        groups.append(acc * inv_cnt - cur)
    pooled = jnp.concatenate(groups, axis=1)
    pext_ref[0:POOL_HALO, :] = pext_ref[tm:tm + POOL_HALO, :]
    bpool = bpool_ref[...] + _tile_lanes(start_row_copies(1), POOL_WIDTH // LANES)
    lspool = lspool_ref[...] + _tile_lanes(start_row_copies(8), POOL_WIDTH // LANES)
    mixed = (_dot(pooled.astype(BF16), wpool_ref[...]) + bpool) * lspool
    y_pool = _dot(mixed.astype(BF16), wbp_ref[...])

    uext_ref[CONV_HALO:CONV_HALO + tm, :] = proj(COL_U, COL_V)
    conv = convb_ref[...] + _tile_lanes(start_row_copies(2), ROW_TILES)
    for j in range(CONV_WIDTH):
        off = CONV_HALO - (CONV_WIDTH - 1) + j
        conv = conv + convw_ref[j:j + 1, :] * uext_ref[off:off + tm, :]
    uext_ref[0:CONV_HALO, :] = uext_ref[tm:tm + CONV_HALO, :]
    ucb = (conv * _sigmoid(conv)).astype(BF16)
    vb = proj(COL_V, COL_O).astype(BF16)

    slab = proj(COL_IF, COL_IF_END) + (bif_ref[...] + start_row_copies(3))
    lane = lax.broadcasted_iota(jnp.int32, (tm, LANES), 1)
    is_f = jnp.logical_and(lane >= N_HEADS, lane < 2 * N_HEADS)
    slab = jnp.where(is_f, _log_sigmoid(slab), slab)
    row_i = lax.broadcasted_iota(jnp.int32, (tm, tm), 0)
    col_i = lax.broadcasted_iota(jnp.int32, (tm, tm), 1)
    causal = row_i >= col_i
    tri = jnp.where(causal, 1.0, 0.0).astype(BF16)
    hi, mid, lo = _split3(slab)
    bcol = _dot(tri, hi) + _dot(tri, mid) + _dot(tri, lo)
    slab_t = slab.T
    bcol_t = bcol.T

    heads = []
    for h in range(N_HEADS):
        hs = slice(h * HEAD_DIM, (h + 1) * HEAD_DIM)
        qk = _dot(ucb[:, hs], wqk_ref[h])
        q_h = qk[:, :HEAD_DIM]
        k_h = qk[:, HEAD_DIM:]
        qb = q_h.astype(BF16)
        kb = k_h.astype(BF16)
        vh = vb[:, hs]

        i_c = slab[:, h:h + 1]
        b_c = bcol[:, N_HEADS + h:N_HEADS + h + 1]
        i_r = slab_t[h:h + 1, :]
        b_r = bcol_t[N_HEADS + h:N_HEADS + h + 1, :]
        m_prev = m_ref[:, h:h + 1] + start_row_copies(4 + h)[:, 0:1]

        d_log = jnp.where(causal, b_c - (b_r - i_r), -jnp.inf)
        m_inter = b_c + m_prev
        m_t = jnp.maximum(m_inter, jnp.max(d_log, axis=1, keepdims=True))
        w_intra = jnp.exp(d_log - m_t)
        sc = _dot_nt(qb, kb) * w_intra
        w_inter = jnp.exp(m_inter - m_t)
        ctb = ct_ref[h].astype(BF16)
        num = _dot(sc.astype(BF16), vh) + w_inter * _dot(qb, ctb)
        qn = jnp.sum(q_h * n_ref[h], axis=1, keepdims=True)
        den = jnp.sum(sc, axis=1, keepdims=True) + w_inter * qn
        hh = num * (1.0 / jnp.maximum(jnp.abs(den), jnp.exp(-m_t)))
        mu = jnp.mean(hh, axis=1, keepdims=True)
        hc = hh - mu
        var = jnp.mean(hc * hc, axis=1, keepdims=True)
        heads.append(hc * lax.rsqrt(var + LN_EPS))

        g_last = b_r[:, tm - 1:tm]
        m_new = jnp.maximum(g_last + m_prev, jnp.max(g_last - b_r + i_r, axis=1, keepdims=True))
        decay = jnp.exp(g_last + m_prev - m_new)
        w_state = jnp.exp(g_last - b_c + i_c - m_new)
        kw = k_h * w_state
        ct_ref[h] = decay * ct_ref[h] + _dot(kw.T.astype(BF16), vh)
        n_ref[h] = decay * n_ref[h] + jnp.sum(kw, axis=0, keepdims=True)
        m_ref[:, h:h + 1] = m_new

    hn = jnp.concatenate(heads, axis=1) * (gain_ref[...] + _tile_lanes(start_row_copies(9), ROW_TILES))
    h_out = _sigmoid(proj(COL_O, COL_IF)) * hn
    y_mlstm = _dot(h_out.astype(BF16), wbm_ref[...])

    merged = (_sigmoid(_dot(h0b, wgate_ref[:, 0:D_MODEL])) * y_pool
              + _sigmoid(_dot(h0b, wgate_ref[:, D_MODEL:2 * D_MODEL])) * y_mlstm)
    mix = _dot(merged.astype(BF16), wout_ref[...])
    ln1g = ln1g_ref[...] + _tile_lanes(start_row_copies(10), ROW_TILES)
    ln1b = ln1b_ref[...] + _tile_lanes(start_row_copies(11), ROW_TILES)
    h1 = _layer_norm(DEEPNORM_ALPHA * h0 + mix, ln1g, ln1b)
    for c in range(ROW_TILES):
        hx_ref[par, pl.ds(c, tm, stride=ROW_TILES), :] = h1[:, c * LANES:(c + 1) * LANES]
    h1_write(par, g).start()

    h1_hi = h1.astype(BF16)
    h1_lo = (h1 - h1_hi.astype(F32)).astype(BF16)
    la = _dot(h1_hi, wr_ref[...])
    lb = _dot(h1_lo, wr_ref[:, 0:LANES])
    logits = (la[:, 0:LANES] + la[:, LANES:2 * LANES] + lb).T[0:N_EXPERTS] + br_ref[...]
    e_iota = lax.broadcasted_iota(jnp.int32, (N_EXPERTS, tm), 0)
    vals, onehots = [], []
    lg = logits
    for _ in range(TOP_K):
        mx = jnp.max(lg, axis=0, keepdims=True)
        sel = jnp.min(jnp.where(lg == mx, e_iota, N_EXPERTS), axis=0, keepdims=True)
        oh = e_iota == sel
        lg = jnp.where(oh, -jnp.inf, lg)
        vals.append(mx)
        onehots.append(oh)
    exps = [jnp.exp(v - vals[0]) for v in vals]
    inv_den = 1.0 / (exps[0] + exps[1] + exps[2] + exps[3])
    gates = [e * inv_den for e in exps]

    oh_all = jnp.where(onehots[0], 1.0, 0.0)
    for oh in onehots[1:]:
        oh_all = oh_all + jnp.where(oh, 1.0, 0.0)
    strict = jnp.where(row_i < col_i, 1.0, 0.0).astype(BF16)
    run = run_ref[:, 0:1]
    rank = _dot(oh_all.astype(BF16), strict) + run
    count = jnp.sum(oh_all, axis=1, keepdims=True)
    inv_rows = 1.0 / EXPERT_ROWS
    pages_before = jnp.ceil(run * inv_rows)
    need = jnp.ceil((run + count) * inv_rows) - pages_before
    er = lax.broadcasted_iota(jnp.int32, (N_EXPERTS, N_EXPERTS), 0)
    ec = lax.broadcasted_iota(jnp.int32, (N_EXPERTS, N_EXPERTS), 1)
    earlier = jnp.where(er > ec, 1.0, 0.0).astype(BF16)
    need_b = jnp.broadcast_to(need, (N_EXPERTS, LANES)).astype(BF16)
    new_page = free_ref[0:1, 0:1] + _dot(earlier, need_b)[:, 0:1]
    page_seq = jnp.floor(rank * inv_rows)
    page = jnp.where(page_seq < pages_before, page_ref[:, 0:1], new_page)
    pos_all = page * EXPERT_ROWS + (rank - page_seq * EXPERT_ROWS)
    positions = [jnp.sum(jnp.where(oh, pos_all, 0.0), axis=0, keepdims=True) for oh in onehots]
    page_ref[...] = jnp.where(need > 0.0, new_page, page_ref[...])
    free_ref[...] = free_ref[...] + jnp.sum(need, axis=0, keepdims=True)
    run_ref[...] = run_ref[...] + count
    cnt_ref[...] = run_ref[...]
    p_lane = lax.broadcasted_iota(jnp.int32, (N_EXPERTS, pe_ref.shape[1]), 1).astype(F32)
    e_plus1 = (lax.broadcasted_iota(jnp.int32, (N_EXPERTS, 1), 0) + 1).astype(F32)
    taken = jnp.logical_and(p_lane == new_page, need > 0.0)
    pe_ref[...] = pe_ref[...] + jnp.sum(jnp.where(taken, e_plus1, 0.0), axis=0, keepdims=True)

    r8 = lax.broadcasted_iota(jnp.int32, (SUBLANES, tm), 0)
    pos_out = jnp.zeros((SUBLANES, tm), jnp.int32)
    r128 = lax.broadcasted_iota(jnp.int32, (LANES, tm), 0)
    gate_rows = jnp.zeros((LANES, tm), F32)
    for kk in range(TOP_K):
        pos_out = jnp.where(r8 == kk, positions[kk].astype(jnp.int32), pos_out)
        gate_rows = jnp.where(r128 == kk, gates[kk], gate_rows)
    pos_ref[...] = pos_out
    gcol_ref[...] = gate_rows.T
    posv_ref[par] = pos_out
    pos_to_smem(par).start()

    @pl.when(g == last)
    def _():
        pos_to_smem(par).wait()
        for t in range(tm):
            for k in range(TOP_K):
                row_copy(par, t, k).start(priority=k % DMA_PRIORITIES)
        wait_row_copies(q)
        wait_row_copies(par)
        h1_write(q, 0).wait()
        h1_write(par, 0).wait()

        st_lane = lax.broadcasted_iota(jnp.int32, (N_EXPERTS, LANES), 1)
        state = jnp.where(st_lane == 0, page_ref[...], jnp.where(st_lane == 1, run_ref[...], free_ref[...]))
        stv_ref[...] = state.astype(jnp.int32)
        state_copy = pltpu.make_async_copy(stv_ref, sts_ref, psem.at[par])
        state_copy.start()
        state_copy.wait()

        def zero_fill(wait):
            def act(rows_dst, n):
                cp = pltpu.make_async_copy(zb_ref.at[pl.ds(0, n)], xp_hbm.at[pl.ds(rows_dst, n)], zsem)
                if wait:
                    cp.wait()
                else:
                    cp.start()

            def unused_page(p, carry):
                act(pl.multiple_of(p * EXPERT_ROWS, EXPERT_ROWS), EXPERT_ROWS)
                return carry

            lax.fori_loop(sts_ref[0, 2], n_pages, unused_page, 0)

            def page_tail(e, carry):
                filled = sts_ref[e, 1] & (EXPERT_ROWS - 1)
                pad = jnp.where(filled == 0, 0, EXPERT_ROWS - filled)
                dst = sts_ref[e, 0] * EXPERT_ROWS + filled
                size = 1
                while size < EXPERT_ROWS:
                    @pl.when((pad & size) != 0)
                    def _(dst=dst, size=size):
                        act(dst, size)
                    dst = dst + (pad & size)
                    size *= 2
                return carry

            lax.fori_loop(0, N_EXPERTS, page_tail, 0)

        zero_fill(wait=False)
        zero_fill(wait=True)


def _const_spec(shape):
    zeros = (0,) * len(shape)
    return pl.BlockSpec(shape, lambda b, s: zeros, pipeline_mode=pl.Buffered(1))


def _mixer_call(x, weights, n_pages):
    bsz, seq, _ = x.shape
    tm = TOKEN_TILE
    n_s = seq // tm
    t_total = bsz * seq
    assert bsz * n_s >= 2
    tile = lambda b, s: (b * n_s + s)
    spare_pages = TOP_K * tm // EXPERT_ROWS
    pe_lanes = -(-n_pages // LANES) * LANES
    in_specs = [pl.BlockSpec((None, tm, D_MODEL), lambda b, s: (b, s, 0))]
    in_specs += [_const_spec(w.shape) for w in weights]
    out_shape = (
        jax.ShapeDtypeStruct((t_total * ROW_TILES, LANES), F32),
        jax.ShapeDtypeStruct(((n_pages + spare_pages) * EXPERT_ROWS, ROW_TILES, LANES), F32),
        jax.ShapeDtypeStruct((SUBLANES, t_total), jnp.int32),
        jax.ShapeDtypeStruct((t_total, LANES), F32),
        jax.ShapeDtypeStruct((N_EXPERTS, LANES), F32),
        jax.ShapeDtypeStruct((SUBLANES, pe_lanes), F32),
    )
    out_specs = (
        pl.BlockSpec(memory_space=pl.ANY),
        pl.BlockSpec(memory_space=pl.ANY),
        pl.BlockSpec((SUBLANES, tm), lambda b, s: (0, tile(b, s))),
        pl.BlockSpec((tm, LANES), lambda b, s: (tile(b, s), 0)),
        pl.BlockSpec((N_EXPERTS, LANES), lambda b, s: (0, 0)),
        pl.BlockSpec((SUBLANES, pe_lanes), lambda b, s: (0, 0)),
    )
    scratch = [
        pltpu.VMEM((POOL_HALO + tm, POOL_WIDTH), F32),
        pltpu.VMEM((2, POOL_HALO + tm, POOL_GROUP), F32),
        pltpu.VMEM((CONV_HALO + tm, D_MODEL), F32),
        pltpu.VMEM((N_HEADS, HEAD_DIM, HEAD_DIM), F32),
        pltpu.VMEM((N_HEADS, 1, HEAD_DIM), F32),
        pltpu.VMEM((1, LANES), F32),
        pltpu.VMEM((N_EXPERTS, LANES), F32),
        pltpu.VMEM((N_EXPERTS, LANES), F32),
        pltpu.VMEM((1, LANES), F32),
        pltpu.VMEM((2, tm * ROW_TILES + SUBLANES, LANES), F32),
        pltpu.VMEM((2, SUBLANES, tm), jnp.int32),
        pltpu.SMEM((2, SUBLANES, tm), jnp.int32),
        pltpu.VMEM((EXPERT_ROWS, ROW_TILES, LANES), F32),
        pltpu.VMEM((N_EXPERTS, LANES), jnp.int32),
        pltpu.SMEM((N_EXPERTS, LANES), jnp.int32),
        pltpu.SemaphoreType.DMA((2,)),
        pltpu.SemaphoreType.DMA((2,)),
        pltpu.SemaphoreType.DMA((2,)),
        pltpu.SemaphoreType.DMA(()),
    ]
    return pl.pallas_call(
        functools.partial(_mixer_kernel, n_pages=n_pages),
        grid=(bsz, n_s),
        in_specs=in_specs,
        out_specs=out_specs,
        out_shape=out_shape,
        scratch_shapes=scratch,
        compiler_params=pltpu.CompilerParams(
            dimension_semantics=("arbitrary", "arbitrary"), vmem_limit_bytes=VMEM_LIMIT),
        name="mixer",
    )(x, *weights)


def _experts_kernel(xpage_ref, spage_ref, be_ref, first_ref, nexte_ref, nused_ref,
                    slot_p_ref, slot_a_ref, slot_b_ref, xa_ref, xb_ref,
                    w1_hbm, b1_ref, w2_hbm, b2_ref, y_hbm,
                    yb0_ref, yb1_ref, w1s_ref, w2s_ref, w1b_ref, w2b_ref, ssem, wsem):
    del xpage_ref, spage_ref
    rows = EXPERT_ROWS
    tile = rows * ROW_TILES
    i = pl.program_id(0)
    last = pl.num_programs(0) - 1
    n_used = nused_ref[0]
    yb = (yb0_ref, yb1_ref)
    x_in = (xa_ref, xb_ref)
    dyn_zero = lax.shift_right_arithmetic(n_used, 31)
    n_chunks = D_FF // EXPERT_CHUNK
    assert SCATTER_STAGES <= n_chunks + 1

    def scatter(slots, r, p):
        return pltpu.make_async_copy(
            yb[p].at[pl.ds(r * ROW_TILES, ROW_TILES)], y_hbm.at[slots[0, 0, r]], ssem.at[p])

    def start_scatters(stage, slots, p):
        if stage is None:
            r_range = range(rows)
        elif stage < SCATTER_STAGES:
            r_range = range(rows * stage // SCATTER_STAGES, rows * (stage + 1) // SCATTER_STAGES)
        else:
            return jnp.zeros((1, LANES), F32)
        for r in r_range:
            scatter(slots, r, p).start(priority=r % DMA_PRIORITIES)
        spare = pl.multiple_of(tile + dyn_zero * SUBLANES, SUBLANES)
        yb[p][pl.ds(spare, SUBLANES), :] = jnp.full((SUBLANES, LANES), dyn_zero.astype(F32))
        return yb[p][pl.ds(spare, SUBLANES), :][0:1, :]

    def wait_scatters(p):
        view = yb[p].at[pl.ds(0, tile)]
        pltpu.make_async_copy(view, view, ssem.at[p]).wait()

    def weight_copies(e):
        return (pltpu.make_async_copy(w1_hbm.at[e], w1s_ref, wsem.at[0]),
                pltpu.make_async_copy(w2_hbm.at[e], w2s_ref, wsem.at[1]))

    def switch_weights(blk):
        @pl.when(first_ref[blk] == 1)
        def _():
            for cp in weight_copies(0):
                cp.wait()
            step = 128
            for c in range(D_MODEL // step):
                w1b_ref[c * step:(c + 1) * step, :] = w1s_ref[c * step:(c + 1) * step, :].astype(BF16)
            half = D_FF // 2
            step = 64
            for c in range(half // step):
                pair = pltpu.pack_elementwise(
                    [w2s_ref[c * step:(c + 1) * step, :], w2s_ref[half + c * step:half + (c + 1) * step, :]],
                    packed_dtype=BF16)
                w2b_ref[2 * c * step:2 * (c + 1) * step, :] = pltpu.bitcast(pair, BF16)
            nxt = nexte_ref[blk]

            @pl.when(nxt >= 0)
            def _():
                for cp in weight_copies(nxt):
                    cp.start()

    def compute(p, blk, start_stage):
        e = be_ref[blk]
        x = jnp.concatenate(
            [x_in[p][pl.ds(c, rows, stride=ROW_TILES), :] for c in range(ROW_TILES)], axis=1).astype(BF16)
        b1 = b1_ref[pl.ds(e, 1), :]
        width = EXPERT_CHUNK
        even = (lax.broadcasted_iota(jnp.int32, (rows, width), 1) & 1) == 0
        zs = []
        for c in range(n_chunks):
            lo = c * width
            hi = D_FF + lo
            zero = _tile_lanes(start_stage(c), width // LANES)
            ha = _dot(x, w1b_ref[:, lo:lo + width]) + (b1[:, lo:lo + width] + zero)
            hb = _dot(x, w1b_ref[:, hi:hi + width]) + (b1[:, hi:hi + width] + zero)
            glu = jnp.where(even, ha, pltpu.roll(hb, 1, 1))
            lin = jnp.where(even, pltpu.roll(ha, width - 1, 1), hb)
            glu = jnp.minimum(glu, SWIGLU_LIMIT)
            lin = jnp.clip(lin, -SWIGLU_LIMIT, SWIGLU_LIMIT) + 1.0
            zs.append((glu * _sigmoid(SWIGLU_ALPHA * glu) * lin).astype(BF16))
        b2 = b2_ref[pl.ds(e, 1), :] + _tile_lanes(start_stage(n_chunks), ROW_TILES)
        y = _dot(jnp.concatenate(zs, axis=1), w2b_ref[...]) + b2
        for c in range(ROW_TILES):
            yb[p][pl.ds(c, rows, stride=ROW_TILES), :] = y[:, c * LANES:(c + 1) * LANES]

    def phase(blk, p, start_stage):
        switch_weights(blk)

        @pl.when(blk < n_used)
        def _():
            compute(p, blk, start_stage)

        @pl.when(blk >= n_used)
        def _():
            start_stage(None)

    @pl.when(i == 0)
    def _():
        yb1_ref[...] = jnp.zeros_like(yb1_ref)
        for cp in weight_copies(be_ref[0]):
            cp.start()

    @pl.when(i >= 1)
    def _():
        wait_scatters(0)

    phase(2 * i, 0, functools.partial(start_scatters, slots=slot_p_ref, p=1))

    wait_scatters(1)
    phase(2 * i + 1, 1, functools.partial(start_scatters, slots=slot_a_ref, p=0))

    @pl.when(i == last)
    def _():
        wait_scatters(0)
        for r in range(rows):
            scatter(slot_b_ref, r, 1).start(priority=r % DMA_PRIORITIES)
        wait_scatters(1)


def _experts_call(plan, row_slot, x_pages_2d, w_e1, b_e1, w_e2, b_e2, n_tokens):
    xpage, spage, block_e, first, next_e, n_used = plan
    rows = EXPERT_ROWS
    n_blocks = block_e.shape[0]
    assert n_blocks % 2 == 0
    n_slots = TOP_K * n_tokens + rows
    tile_rows = rows * ROW_TILES
    filler = row_slot.shape[0] - 1
    smem_rows = lambda index: pl.BlockSpec((1, 1, rows), index, memory_space=pltpu.SMEM)
    whole = lambda shape: pl.BlockSpec(shape, lambda i, *_: (0,) * len(shape))
    grid_spec = pltpu.PrefetchScalarGridSpec(
        num_scalar_prefetch=6,
        grid=(n_blocks // 2,),
        in_specs=[
            smem_rows(lambda i, xp, sp, *_: (jnp.where(i == 0, filler, sp[jnp.maximum(2 * i - 1, 0)]), 0, 0)),
            smem_rows(lambda i, xp, sp, *_: (sp[2 * i], 0, 0)),
            smem_rows(lambda i, xp, sp, *_: (sp[2 * i + 1], 0, 0)),
            pl.BlockSpec((tile_rows, LANES), lambda i, xp, *_: (xp[2 * i], 0)),
            pl.BlockSpec((tile_rows, LANES), lambda i, xp, *_: (xp[2 * i + 1], 0)),
            pl.BlockSpec(memory_space=pl.ANY),
            whole((N_EXPERTS, 2 * D_FF)),
            pl.BlockSpec(memory_space=pl.ANY),
            whole((N_EXPERTS, D_MODEL)),
        ],
        out_specs=pl.BlockSpec(memory_space=pl.ANY),
        scratch_shapes=[
            pltpu.VMEM((tile_rows + SUBLANES, LANES), F32),
            pltpu.VMEM((tile_rows + SUBLANES, LANES), F32),
            pltpu.VMEM((D_MODEL, 2 * D_FF), F32),
            pltpu.VMEM((D_FF, D_MODEL), F32),
            pltpu.VMEM((D_MODEL, 2 * D_FF), BF16),
            pltpu.VMEM((D_FF, D_MODEL), BF16),
            pltpu.SemaphoreType.DMA((2,)),
            pltpu.SemaphoreType.DMA((2,)),
        ],
    )
    return pl.pallas_call(
        _experts_kernel,
        grid_spec=grid_spec,
        out_shape=jax.ShapeDtypeStruct((n_slots, ROW_TILES, LANES), F32),
        compiler_params=pltpu.CompilerParams(
            dimension_semantics=("arbitrary",), vmem_limit_bytes=VMEM_LIMIT),
        name="experts",
    )(xpage, spage, block_e, first, next_e, n_used,
      row_slot, row_slot, row_slot, x_pages_2d, x_pages_2d, w_e1, b_e1, w_e2, b_e2)


def _combine_kernel(h1_ref, y0_ref, y1_ref, y2_ref, y3_ref, gcol_ref, g_ref, b_ref, out_ref):
    tc = COMBINE_TILE
    gcol = gcol_ref[...]
    y_refs = (y0_ref, y1_ref, y2_ref, y3_ref)
    chunks = []
    for c in range(ROW_TILES):
        z = DEEPNORM_ALPHA * h1_ref[pl.ds(c, tc, stride=ROW_TILES), :]
        for kk in range(TOP_K):
            z = z + gcol[:, kk:kk + 1] * y_refs[kk][pl.ds(c, tc, stride=ROW_TILES), :]
        chunks.append(z)
    total = chunks[0].sum(axis=1, keepdims=True)
    for z in chunks[1:]:
        total = total + z.sum(axis=1, keepdims=True)
    mu = total * (1.0 / D_MODEL)
    sq = None
    for z in chunks:
        zc = z - mu
        part = (zc * zc).sum(axis=1, keepdims=True)
        sq = part if sq is None else sq + part
    inv = lax.rsqrt(sq * (1.0 / D_MODEL) + LN_EPS)
    for c, z in enumerate(chunks):
        cs = slice(c * LANES, (c + 1) * LANES)
        out_ref[:, cs] = (z - mu) * inv * g_ref[:, cs] + b_ref[:, cs]


def _combine_call(h1_2d, y_2d, gcol, ln_g, ln_b, n_tokens):
    tc = COMBINE_TILE
    n_t = n_tokens // tc
    blk = tc * ROW_TILES
    y_spec = lambda kk: pl.BlockSpec((blk, LANES), lambda i: (kk * n_t + i, 0))
    return pl.pallas_call(
        _combine_kernel,
        grid=(n_t,),
        in_specs=[
            pl.BlockSpec((blk, LANES), lambda i: (i, 0)),
            y_spec(0), y_spec(1), y_spec(2), y_spec(3),
            pl.BlockSpec((tc, LANES), lambda i: (i, 0)),
            pl.BlockSpec((1, D_MODEL), lambda i: (0, 0)),
            pl.BlockSpec((1, D_MODEL), lambda i: (0, 0)),
        ],
        out_specs=pl.BlockSpec((tc, D_MODEL), lambda i: (i, 0)),
        out_shape=jax.ShapeDtypeStruct((n_tokens, D_MODEL), F32),
        compiler_params=pltpu.CompilerParams(
            dimension_semantics=("arbitrary",), vmem_limit_bytes=VMEM_LIMIT),
        name="combine",
    )(h1_2d, y_2d, y_2d, y_2d, y_2d, gcol, ln_g, ln_b)


def _prepare_mixer_weights(ln0_g, ln0_b, w_in, conv_w, conv_b, w_q, w_k, b_if, mh_gain, w_pool,
                           b_pool, ls_pool, w_branch_pool, w_branch_mlstm, w_out, ln1_g, ln1_b,
                           w_router, b_router):
    w_main = w_in[:, :COL_IF_END].astype(BF16)
    w_gates = w_in[:, COL_GATES:COL_GATES + 2 * D_MODEL].astype(BF16)
    bif = jnp.concatenate([b_if, jnp.zeros((LANES - 2 * N_HEADS,), F32)]).reshape(1, LANES)
    wqk = jnp.concatenate([w_q * (HEAD_DIM ** -0.5), w_k], axis=-1).astype(BF16)
    wpool_bd = jax.scipy.linalg.block_diag(*[w_pool[g] for g in range(len(POOL_WINDOWS))]).astype(BF16)
    wr_hi = w_router.astype(BF16)
    wr_lo = (w_router - wr_hi.astype(F32)).astype(BF16)
    lane_pad = jnp.zeros((D_MODEL, LANES - N_EXPERTS), BF16)
    wr_hi_lo = jnp.concatenate([wr_hi, lane_pad, wr_lo, lane_pad], axis=1)
    row = lambda v: v.reshape(1, -1)
    return (row(ln0_g), row(ln0_b), w_main, w_gates, bif, conv_w, row(conv_b), wqk, row(mh_gain), wpool_bd,
            row(b_pool), row(ls_pool), w_branch_pool.astype(BF16), w_branch_mlstm.astype(BF16),
            w_out.astype(BF16), row(ln1_g), row(ln1_b), wr_hi_lo,
            b_router.reshape(N_EXPERTS, 1))


def _plan(page_expert1, counts, pos, n_tokens, n_pages, n_rows_total):
    rows = EXPERT_ROWS
    n_assign = TOP_K * n_tokens
    i32 = jnp.int32
    slots = jnp.arange(n_assign, dtype=i32)
    filler = n_rows_total // rows
    hit = jnp.zeros(((filler + 1) * rows,), i32).at[pos.reshape(-1)].add(slots + 1)
    pad_slot = n_assign + jnp.arange((filler + 1) * rows, dtype=i32) % rows
    row_slot = jnp.where(hit == 0, pad_slot, hit - 1).reshape(filler + 1, 1, rows)

    pidx = jnp.arange(n_pages, dtype=i32)
    used = page_expert1 > 0
    n_used = jnp.sum(used.astype(i32))
    page_e = page_expert1 - 1
    key = jnp.where(used, page_e, N_EXPERTS) * n_pages + pidx
    place = jnp.sum((key[None, :] < key[:, None]).astype(i32), axis=1)
    at_block = place[None, :] == pidx[:, None]
    order = jnp.sum(jnp.where(at_block, pidx[None, :], 0), axis=1)
    block_e = jnp.clip(jnp.sum(jnp.where(at_block, page_e[None, :], 0), axis=1), 0, N_EXPERTS - 1)
    valid = pidx < n_used
    xpage = jnp.where(valid, order, 0)
    spage = jnp.where(valid, order, filler)
    prev_e = jnp.concatenate([jnp.full((1,), -1, i32), block_e[:-1]])
    first = jnp.logical_and(valid, block_e != prev_e).astype(i32)
    e_iota = jnp.arange(N_EXPERTS, dtype=i32)
    pages_per_e = (counts + rows - 1) // rows
    seg_end = jnp.cumsum(pages_per_e)
    mine = block_e[:, None] == e_iota[None, :]
    end_blk = jnp.sum(jnp.where(mine, seg_end[None, :], 0), axis=1)
    e_at_end = jnp.sum(jnp.where(pidx[None, :] == end_blk[:, None], block_e[None, :], 0), axis=1)
    next_e = jnp.where(end_blk < n_used, e_at_end, -1).astype(i32)
    plan = (xpage.astype(i32), spage.astype(i32), block_e.astype(i32), first, next_e,
            n_used.reshape(1).astype(i32))
    return plan, row_slot


def kernel(x, ln0_g, ln0_b, w_in, conv_w, conv_b, w_q, w_k, b_if, mh_gain, w_pool, b_pool, ls_pool,
           w_branch_pool, w_branch_mlstm, w_out, ln1_g, ln1_b, w_router, b_router, w_e1, b_e1,
           w_e2, b_e2, ln2_g, ln2_b):
    bsz, seq, _ = x.shape
    n_tokens = bsz * seq
    rows = EXPERT_ROWS
    assert w_in.shape[0] == 1, "single-layer trunk"
    assert seq % TOKEN_TILE == 0 and n_tokens % COMBINE_TILE == 0
    n_pages = (TOP_K * n_tokens + N_EXPERTS * (rows - 1) + rows - 1) // rows
    n_pages += n_pages % 2
    weights = _prepare_mixer_weights(
        ln0_g, ln0_b, w_in[0], conv_w[0], conv_b[0], w_q[0], w_k[0], b_if[0], mh_gain[0], w_pool[0],
        b_pool[0], ls_pool[0], w_branch_pool[0], w_branch_mlstm[0], w_out[0], ln1_g[0], ln1_b[0],
        w_router[0], b_router[0])
    h1_2d, x_pages, pos8, gcol, cnt, pe = _mixer_call(x, weights, n_pages)
    counts = cnt[:, 0].astype(jnp.int32)
    page_expert1 = pe[0, :n_pages].astype(jnp.int32)
    plan, row_slot = _plan(page_expert1, counts, pos8[:TOP_K], n_tokens, n_pages, x_pages.shape[0])
    y_slots = _experts_call(
        plan, row_slot, x_pages.reshape(-1, LANES), w_e1[0], b_e1[0], w_e2[0], b_e2[0], n_tokens)
    out = _combine_call(h1_2d, y_slots.reshape(-1, LANES), gcol, ln2_g[0].reshape(1, D_MODEL),
                        ln2_b[0].reshape(1, D_MODEL), n_tokens)
    return out.reshape(bsz, seq, D_MODEL)
```

```python
import functools

import jax
import jax.numpy as jnp
from jax import lax
from jax.experimental import pallas as pl
from jax.experimental.pallas import tpu as pltpu

F32 = jnp.float32
BF16 = jnp.bfloat16

D_MODEL = 1024
N_HEADS = 4
HEAD_DIM = 256
POOL_WIDTH = 512
POOL_GROUP = 128
POOL_WINDOWS = (2, 4, 8, 16)
CONV_WIDTH = 4
N_EXPERTS = 32
TOP_K = 4
D_FF = 1024
SWIGLU_ALPHA = 1.702
SWIGLU_LIMIT = 7.0
LN_EPS = 1e-5
DEEPNORM_ALPHA = 2.0 ** 0.25

SUBLANES = 8
LANES = 128
ROW_TILES = D_MODEL // LANES

COL_P = 0
COL_U = COL_P + POOL_WIDTH
COL_V = COL_U + D_MODEL
COL_O = COL_V + D_MODEL
COL_IF = COL_O + D_MODEL
COL_IF_END = COL_IF + LANES
COL_GATES = COL_IF + 2 * N_HEADS

TOKEN_TILE = 256
POOL_HALO = 24
CONV_HALO = 8
EXPERT_ROWS = 256
ROW_DMA_GROUPS = 12
SCATTER_SHARES = (80, 72, 64, 40)
assert sum(SCATTER_SHARES) == EXPERT_ROWS
EXPERT_CHUNK = 2 * LANES
COMBINE_TILE = 512
DMA_PRIORITIES = 2
VMEM_LIMIT = 56 * 1024 * 1024

assert TOKEN_TILE <= EXPERT_ROWS
assert (TOP_K * TOKEN_TILE) % EXPERT_ROWS == 0


def _layer_norm(x, g, b):
    mu = jnp.mean(x, axis=-1, keepdims=True)
    xc = x - mu
    var = jnp.mean(xc * xc, axis=-1, keepdims=True)
    return xc * lax.rsqrt(var + LN_EPS) * g + b


def _sigmoid(x):
    return 0.5 * jnp.tanh(0.5 * x) + 0.5


def _log_sigmoid(x):
    return -(jnp.maximum(-x, 0.0) + jnp.log(1.0 + jnp.exp(-jnp.abs(x))))


def _split3(x):
    hi = x.astype(BF16)
    r1 = x - hi.astype(F32)
    mid = r1.astype(BF16)
    lo = (r1 - mid.astype(F32)).astype(BF16)
    return hi, mid, lo


def _dot(a, b):
    return jnp.dot(a, b, preferred_element_type=F32)


def _dot_nt(a, b):
    return lax.dot_general(a, b, (((1,), (1,)), ((), ())), preferred_element_type=F32)


def _tile_lanes(row, n):
    return jnp.concatenate([row] * n, axis=1)


def _mixer_kernel(x_ref, ln0g_ref, ln0b_ref, win_ref, wgate_ref, bif_ref, convw_ref, convb_ref, wqk_ref,
                  gain_ref, wpool_ref, bpool_ref, lspool_ref, wbp_ref, wbm_ref, wout_ref,
                  ln1g_ref, ln1b_ref, wr_ref, br_ref,
                  h1_hbm, xp_hbm, pos_ref, gcol_ref, cnt_ref, pe_ref,
                  pext_ref, pw_ref, uext_ref, ct_ref, n_ref, m_ref, run_ref, page_ref, free_ref,
                  hx_ref, posv_ref, poss_ref, zb_ref, stv_ref, sts_ref, psem, rsem, hsem, zsem,
                  *, n_pages):
    tm = TOKEN_TILE
    tile_rows = tm * ROW_TILES
    b = pl.program_id(0)
    s = pl.program_id(1)
    n_s = pl.num_programs(1)
    g = b * n_s + s
    last = pl.num_programs(0) * n_s - 1
    par = g % 2
    q = 1 - par
    dyn_zero = lax.shift_right_arithmetic(g, 31)

    def hx_tile(p):
        return hx_ref.at[p, pl.ds(0, tile_rows)]

    def row_copy(p, t, k):
        return pltpu.make_async_copy(
            hx_ref.at[p, pl.ds(t * ROW_TILES, ROW_TILES)], xp_hbm.at[poss_ref[p, k, t]], rsem.at[p])

    def wait_row_copies(p):
        for _ in range(TOP_K):
            pltpu.make_async_copy(hx_tile(p), hx_tile(p), rsem.at[p]).wait()

    def h1_write(p, tile_index):
        off = pl.multiple_of(tile_index * tile_rows, tile_rows)
        return pltpu.make_async_copy(hx_tile(p), h1_hbm.at[pl.ds(off, tile_rows)], hsem.at[p])

    def pos_to_smem(p):
        return pltpu.make_async_copy(posv_ref.at[p], poss_ref.at[p], psem.at[p])

    @pl.when(s == 0)
    def _():
        pext_ref[0:POOL_HALO, :] = jnp.zeros((POOL_HALO, POOL_WIDTH), F32)
        uext_ref[0:CONV_HALO, :] = jnp.zeros((CONV_HALO, D_MODEL), F32)
        ct_ref[...] = jnp.zeros_like(ct_ref)
        n_ref[...] = jnp.zeros_like(n_ref)
        m_ref[...] = jnp.zeros_like(m_ref)
        pw_ref[:, 0:SUBLANES, :] = jnp.zeros((2, SUBLANES, POOL_GROUP), F32)

    @pl.when(g == 0)
    def _():
        run_ref[...] = jnp.zeros_like(run_ref)
        page_ref[...] = jnp.zeros_like(page_ref)
        free_ref[...] = jnp.zeros_like(free_ref)
        pe_ref[...] = jnp.zeros_like(pe_ref)
        zb_ref[...] = jnp.zeros_like(zb_ref)
        hx_ref[1] = jnp.zeros((tile_rows + SUBLANES, LANES), F32)
        for k in range(TOP_K):
            for t in range(tm):
                poss_ref[1, k, t] = n_pages * EXPERT_ROWS + k * tm + t

    @pl.when(g >= 1)
    def _():
        pos_to_smem(q).wait()
        wait_row_copies(par)

    @pl.when(g >= 2)
    def _():
        h1_write(par, 0).wait()

    def start_row_copies(grp):
        for t in range(tm * grp // ROW_DMA_GROUPS, tm * (grp + 1) // ROW_DMA_GROUPS):
            for k in range(TOP_K):
                row_copy(q, t, k).start(priority=k % DMA_PRIORITIES)
        spare = pl.multiple_of(tile_rows + dyn_zero * SUBLANES, SUBLANES)
        hx_ref[q, pl.ds(spare, SUBLANES), :] = jnp.full((SUBLANES, LANES), dyn_zero.astype(F32))
        return hx_ref[q, pl.ds(spare, SUBLANES), :][0:1, :]

    ln0b = ln0b_ref[...] + _tile_lanes(start_row_copies(0), ROW_TILES)
    h0 = _layer_norm(x_ref[...], ln0g_ref[...], ln0b)
    h0b = h0.astype(BF16)

    def proj(lo, hi):
        return _dot(h0b, win_ref[:, lo:hi])

    pext_ref[POOL_HALO:POOL_HALO + tm, :] = proj(COL_P, COL_U)
    tpos = s * tm + lax.broadcasted_iota(jnp.int32, (tm, 1), 0)
    groups = []
    end = POOL_HALO + tm
    for gi, w in enumerate(POOL_WINDOWS):
        cols = slice(gi * POOL_GROUP, (gi + 1) * POOL_GROUP)
        cur = pext_ref[POOL_HALO:end, cols]
        src, src_cols, k, nxt = pext_ref, cols, 1, 0
        while 2 * k < w:
            pw_ref[nxt, SUBLANES:end, :] = (src[SUBLANES:end, src_cols]
                                            + src[SUBLANES - k:end - k, src_cols])
            src, src_cols, k, nxt = pw_ref.at[nxt], slice(None), 2 * k, 1 - nxt
        acc = src[POOL_HALO:end, src_cols] + src[POOL_HALO - k:end - k, src_cols]
        inv_cnt = 1.0 / jnp.minimum(tpos + 1, w).astype(F32)
        groups.append(acc * inv_cnt - cur)
    pooled = jnp.concatenate(groups, axis=1)
    pext_ref[0:POOL_HALO, :] = pext_ref[tm:tm + POOL_HALO, :]
    bpool = bpool_ref[...] + _tile_lanes(start_row_copies(1), POOL_WIDTH // LANES)
    lspool = lspool_ref[...] + _tile_lanes(start_row_copies(8), POOL_WIDTH // LANES)
    mixed = (_dot(pooled.astype(BF16), wpool_ref[...]) + bpool) * lspool
    y_pool = _dot(mixed.astype(BF16), wbp_ref[...])

    uext_ref[CONV_HALO:CONV_HALO + tm, :] = proj(COL_U, COL_V)
    conv = convb_ref[...] + _tile_lanes(start_row_copies(2), ROW_TILES)
    for j in range(CONV_WIDTH):
        off = CONV_HALO - (CONV_WIDTH - 1) + j
        conv = conv + convw_ref[j:j + 1, :] * uext_ref[off:off + tm, :]
    uext_ref[0:CONV_HALO, :] = uext_ref[tm:tm + CONV_HALO, :]
    ucb = (conv * _sigmoid(conv)).astype(BF16)
    vb = proj(COL_V, COL_O).astype(BF16)

    slab = proj(COL_IF, COL_IF_END) + (bif_ref[...] + start_row_copies(3))
    lane = lax.broadcasted_iota(jnp.int32, (tm, LANES), 1)
    is_f = jnp.logical_and(lane >= N_HEADS, lane < 2 * N_HEADS)
    slab = jnp.where(is_f, _log_sigmoid(slab), slab)
    row_i = lax.broadcasted_iota(jnp.int32, (tm, tm), 0)
    col_i = lax.broadcasted_iota(jnp.int32, (tm, tm), 1)
    causal = row_i >= col_i
    tri = jnp.where(causal, 1.0, 0.0).astype(BF16)
    hi, mid, lo = _split3(slab)
    bcol = _dot(tri, hi) + _dot(tri, mid) + _dot(tri, lo)
    slab_t = slab.T
    bcol_t = bcol.T

    heads = []
    for h in range(N_HEADS):
        hs = slice(h * HEAD_DIM, (h + 1) * HEAD_DIM)
        qk = _dot(ucb[:, hs], wqk_ref[h])
        q_h = qk[:, :HEAD_DIM]
        k_h = qk[:, HEAD_DIM:]
        qb = q_h.astype(BF16)
        kb = k_h.astype(BF16)
        vh = vb[:, hs]

        i_c = slab[:, h:h + 1]
        b_c = bcol[:, N_HEADS + h:N_HEADS + h + 1]
        i_r = slab_t[h:h + 1, :]
        b_r = bcol_t[N_HEADS + h:N_HEADS + h + 1, :]
        m_prev = m_ref[:, h:h + 1] + start_row_copies(4 + h)[:, 0:1]

        d_log = jnp.where(causal, b_c - (b_r - i_r), -jnp.inf)
        m_inter = b_c + m_prev
        m_t = jnp.maximum(m_inter, jnp.max(d_log, axis=1, keepdims=True))
        w_intra = jnp.exp(d_log - m_t)
        sc = _dot_nt(qb, kb) * w_intra
        w_inter = jnp.exp(m_inter - m_t)
        ctb = ct_ref[h].astype(BF16)
        num = _dot(sc.astype(BF16), vh) + w_inter * _dot(qb, ctb)
        qn = jnp.sum(q_h * n_ref[h], axis=1, keepdims=True)
        den = jnp.sum(sc, axis=1, keepdims=True) + w_inter * qn
        hh = num * (1.0 / jnp.maximum(jnp.abs(den), jnp.exp(-m_t)))
        mu = jnp.mean(hh, axis=1, keepdims=True)
        hc = hh - mu
        var = jnp.mean(hc * hc, axis=1, keepdims=True)
        heads.append(hc * lax.rsqrt(var + LN_EPS))

        g_last = b_r[:, tm - 1:tm]
        m_new = jnp.maximum(g_last + m_prev, jnp.max(g_last - b_r + i_r, axis=1, keepdims=True))
        decay = jnp.exp(g_last + m_prev - m_new)
        w_state = jnp.exp(g_last - b_c + i_c - m_new)
        kw = k_h * w_state
        ct_ref[h] = decay * ct_ref[h] + _dot(kw.T.astype(BF16), vh)
        n_ref[h] = decay * n_ref[h] + jnp.sum(kw, axis=0, keepdims=True)
        m_ref[:, h:h + 1] = m_new

    hn = jnp.concatenate(heads, axis=1) * (gain_ref[...] + _tile_lanes(start_row_copies(9), ROW_TILES))
    h_out = _sigmoid(proj(COL_O, COL_IF)) * hn
    y_mlstm = _dot(h_out.astype(BF16), wbm_ref[...])

    merged = (_sigmoid(_dot(h0b, wgate_ref[:, 0:D_MODEL])) * y_pool
              + _sigmoid(_dot(h0b, wgate_ref[:, D_MODEL:2 * D_MODEL])) * y_mlstm)
    mix = _dot(merged.astype(BF16), wout_ref[...])
    ln1g = ln1g_ref[...] + _tile_lanes(start_row_copies(10), ROW_TILES)
    ln1b = ln1b_ref[...] + _tile_lanes(start_row_copies(11), ROW_TILES)
    h1 = _layer_norm(DEEPNORM_ALPHA * h0 + mix, ln1g, ln1b)
    for c in range(ROW_TILES):
        hx_ref[par, pl.ds(c, tm, stride=ROW_TILES), :] = h1[:, c * LANES:(c + 1) * LANES]
    h1_write(par, g).start()

    h1_hi = h1.astype(BF16)
    h1_lo = (h1 - h1_hi.astype(F32)).astype(BF16)
    la = _dot(h1_hi, wr_ref[...])
    lb = _dot(h1_lo, wr_ref[:, 0:LANES])
    logits = (la[:, 0:LANES] + la[:, LANES:2 * LANES] + lb).T[0:N_EXPERTS] + br_ref[...]
    e_iota = lax.broadcasted_iota(jnp.int32, (N_EXPERTS, tm), 0)
    vals, onehots = [], []
    lg = logits
    for _ in range(TOP_K):
        mx = jnp.max(lg, axis=0, keepdims=True)
        sel = jnp.min(jnp.where(lg == mx, e_iota, N_EXPERTS), axis=0, keepdims=True)
        oh = e_iota == sel
        lg = jnp.where(oh, -jnp.inf, lg)
        vals.append(mx)
        onehots.append(oh)
    exps = [jnp.exp(v - vals[0]) for v in vals]
    inv_den = 1.0 / (exps[0] + exps[1] + exps[2] + exps[3])
    gates = [e * inv_den for e in exps]

    oh_all = jnp.where(onehots[0], 1.0, 0.0)
    for oh in onehots[1:]:
        oh_all = oh_all + jnp.where(oh, 1.0, 0.0)
    strict = jnp.where(row_i < col_i, 1.0, 0.0).astype(BF16)
    run = run_ref[:, 0:1]
    rank = _dot(oh_all.astype(BF16), strict) + run
    count = jnp.sum(oh_all, axis=1, keepdims=True)
    inv_rows = 1.0 / EXPERT_ROWS
    pages_before = jnp.ceil(run * inv_rows)
    need = jnp.ceil((run + count) * inv_rows) - pages_before
    er = lax.broadcasted_iota(jnp.int32, (N_EXPERTS, N_EXPERTS), 0)
    ec = lax.broadcasted_iota(jnp.int32, (N_EXPERTS, N_EXPERTS), 1)
    earlier = jnp.where(er > ec, 1.0, 0.0).astype(BF16)
    need_b = jnp.broadcast_to(need, (N_EXPERTS, LANES)).astype(BF16)
    new_page = free_ref[0:1, 0:1] + _dot(earlier, need_b)[:, 0:1]
    page_seq = jnp.floor(rank * inv_rows)
    page = jnp.where(page_seq < pages_before, page_ref[:, 0:1], new_page)
    pos_all = page * EXPERT_ROWS + (rank - page_seq * EXPERT_ROWS)
    positions = [jnp.sum(jnp.where(oh, pos_all, 0.0), axis=0, keepdims=True) for oh in onehots]
    page_ref[...] = jnp.where(need > 0.0, new_page, page_ref[...])
    free_ref[...] = free_ref[...] + jnp.sum(need, axis=0, keepdims=True)
    run_ref[...] = run_ref[...] + count
    cnt_ref[...] = run_ref[...]
    p_lane = lax.broadcasted_iota(jnp.int32, (N_EXPERTS, pe_ref.shape[1]), 1).astype(F32)
    e_plus1 = (lax.broadcasted_iota(jnp.int32, (N_EXPERTS, 1), 0) + 1).astype(F32)
    taken = jnp.logical_and(p_lane == new_page, need > 0.0)
    pe_ref[...] = pe_ref[...] + jnp.sum(jnp.where(taken, e_plus1, 0.0), axis=0, keepdims=True)

    r8 = lax.broadcasted_iota(jnp.int32, (SUBLANES, tm), 0)
    pos_out = jnp.zeros((SUBLANES, tm), jnp.int32)
    r128 = lax.broadcasted_iota(jnp.int32, (LANES, tm), 0)
    gate_rows = jnp.zeros((LANES, tm), F32)
    for kk in range(TOP_K):
        pos_out = jnp.where(r8 == kk, positions[kk].astype(jnp.int32), pos_out)
        gate_rows = jnp.where(r128 == kk, gates[kk], gate_rows)
    pos_ref[...] = pos_out
    gcol_ref[...] = gate_rows.T
    posv_ref[par] = pos_out
    pos_to_smem(par).start()

    @pl.when(g == last)
    def _():
        pos_to_smem(par).wait()
        for t in range(tm):
            for k in range(TOP_K):
                row_copy(par, t, k).start(priority=k % DMA_PRIORITIES)
        wait_row_copies(q)
        wait_row_copies(par)
        h1_write(q, 0).wait()
        h1_write(par, 0).wait()

        st_lane = lax.broadcasted_iota(jnp.int32, (N_EXPERTS, LANES), 1)
        state = jnp.where(st_lane == 0, page_ref[...], jnp.where(st_lane == 1, run_ref[...], free_ref[...]))
        stv_ref[...] = state.astype(jnp.int32)
        state_copy = pltpu.make_async_copy(stv_ref, sts_ref, psem.at[par])
        state_copy.start()
        state_copy.wait()

        def zero_fill(wait):
            def act(rows_dst, n):
                cp = pltpu.make_async_copy(zb_ref.at[pl.ds(0, n)], xp_hbm.at[pl.ds(rows_dst, n)], zsem)
                if wait:
                    cp.wait()
                else:
                    cp.start()

            def unused_page(p, carry):
                act(pl.multiple_of(p * EXPERT_ROWS, EXPERT_ROWS), EXPERT_ROWS)
                return carry

            lax.fori_loop(sts_ref[0, 2], n_pages, unused_page, 0)

            def page_tail(e, carry):
                filled = sts_ref[e, 1] & (EXPERT_ROWS - 1)
                pad = jnp.where(filled == 0, 0, EXPERT_ROWS - filled)
                dst = sts_ref[e, 0] * EXPERT_ROWS + filled
                size = 1
                while size < EXPERT_ROWS:
                    @pl.when((pad & size) != 0)
                    def _(dst=dst, size=size):
                        act(dst, size)
                    dst = dst + (pad & size)
                    size *= 2
                return carry

            lax.fori_loop(0, N_EXPERTS, page_tail, 0)

        zero_fill(wait=False)
        zero_fill(wait=True)


def _const_spec(shape):
    zeros = (0,) * len(shape)
    return pl.BlockSpec(shape, lambda b, s: zeros, pipeline_mode=pl.Buffered(1))


def _mixer_call(x, weights, n_pages):
    bsz, seq, _ = x.shape
    tm = TOKEN_TILE
    n_s = seq // tm
    t_total = bsz * seq
    assert bsz * n_s >= 2
    tile = lambda b, s: (b * n_s + s)
    spare_pages = TOP_K * tm // EXPERT_ROWS
    pe_lanes = -(-n_pages // LANES) * LANES
    in_specs = [pl.BlockSpec((None, tm, D_MODEL), lambda b, s: (b, s, 0))]
    in_specs += [_const_spec(w.shape) for w in weights]
    out_shape = (
        jax.ShapeDtypeStruct((t_total * ROW_TILES, LANES), F32),
        jax.ShapeDtypeStruct(((n_pages + spare_pages) * EXPERT_ROWS, ROW_TILES, LANES), F32),
        jax.ShapeDtypeStruct((SUBLANES, t_total), jnp.int32),
        jax.ShapeDtypeStruct((t_total, LANES), F32),
        jax.ShapeDtypeStruct((N_EXPERTS, LANES), F32),
        jax.ShapeDtypeStruct((SUBLANES, pe_lanes), F32),
    )
    out_specs = (
        pl.BlockSpec(memory_space=pl.ANY),
        pl.BlockSpec(memory_space=pl.ANY),
        pl.BlockSpec((SUBLANES, tm), lambda b, s: (0, tile(b, s))),
        pl.BlockSpec((tm, LANES), lambda b, s: (tile(b, s), 0)),
        pl.BlockSpec((N_EXPERTS, LANES), lambda b, s: (0, 0)),
        pl.BlockSpec((SUBLANES, pe_lanes), lambda b, s: (0, 0)),
    )
    scratch = [
        pltpu.VMEM((POOL_HALO + tm, POOL_WIDTH), F32),
        pltpu.VMEM((2, POOL_HALO + tm, POOL_GROUP), F32),
        pltpu.VMEM((CONV_HALO + tm, D_MODEL), F32),
        pltpu.VMEM((N_HEADS, HEAD_DIM, HEAD_DIM), F32),
        pltpu.VMEM((N_HEADS, 1, HEAD_DIM), F32),
        pltpu.VMEM((1, LANES), F32),
        pltpu.VMEM((N_EXPERTS, LANES), F32),
        pltpu.VMEM((N_EXPERTS, LANES), F32),
        pltpu.VMEM((1, LANES), F32),
        pltpu.VMEM((2, tm * ROW_TILES + SUBLANES, LANES), F32),
        pltpu.VMEM((2, SUBLANES, tm), jnp.int32),
        pltpu.SMEM((2, SUBLANES, tm), jnp.int32),
        pltpu.VMEM((EXPERT_ROWS, ROW_TILES, LANES), F32),
        pltpu.VMEM((N_EXPERTS, LANES), jnp.int32),
        pltpu.SMEM((N_EXPERTS, LANES), jnp.int32),
        pltpu.SemaphoreType.DMA((2,)),
        pltpu.SemaphoreType.DMA((2,)),
        pltpu.SemaphoreType.DMA((2,)),
        pltpu.SemaphoreType.DMA(()),
    ]
    return pl.pallas_call(
        functools.partial(_mixer_kernel, n_pages=n_pages),
        grid=(bsz, n_s),
        in_specs=in_specs,
        out_specs=out_specs,
        out_shape=out_shape,
        scratch_shapes=scratch,
        compiler_params=pltpu.CompilerParams(
            dimension_semantics=("arbitrary", "arbitrary"), vmem_limit_bytes=VMEM_LIMIT),
        name="mixer",
    )(x, *weights)


def _experts_kernel(xpage_ref, spage_ref, be_ref, first_ref, nexte_ref, nused_ref,
                    slot_p_ref, slot_a_ref, slot_b_ref, xa_ref, xb_ref,
                    w1_hbm, b1_ref, w2_hbm, b2_ref, y_hbm,
                    yb0_ref, yb1_ref, w1s_ref, w2s_ref, w1b_ref, w2b_ref, ssem, wsem):
    del xpage_ref, spage_ref
    rows = EXPERT_ROWS
    tile = rows * ROW_TILES
    i = pl.program_id(0)
    last = pl.num_programs(0) - 1
    n_used = nused_ref[0]
    yb = (yb0_ref, yb1_ref)
    x_in = (xa_ref, xb_ref)
    dyn_zero = lax.shift_right_arithmetic(n_used, 31)
    n_chunks = D_FF // EXPERT_CHUNK
    assert len(SCATTER_SHARES) <= n_chunks + 1

    def scatter(slots, r, p):
        return pltpu.make_async_copy(
            yb[p].at[pl.ds(r * ROW_TILES, ROW_TILES)], y_hbm.at[slots[0, 0, r]], ssem.at[p])

    def start_scatters(stage, slots, p):
        if stage is None:
            r_range = range(rows)
        elif stage < len(SCATTER_SHARES):
            r_range = range(sum(SCATTER_SHARES[:stage]), sum(SCATTER_SHARES[:stage + 1]))
        else:
            return jnp.zeros((1, LANES), F32)
        for r in r_range:
            scatter(slots, r, p).start(priority=r % DMA_PRIORITIES)
        spare = pl.multiple_of(tile + dyn_zero * SUBLANES, SUBLANES)
        yb[p][pl.ds(spare, SUBLANES), :] = jnp.full((SUBLANES, LANES), dyn_zero.astype(F32))
        return yb[p][pl.ds(spare, SUBLANES), :][0:1, :]

    def wait_scatters(p):
        view = yb[p].at[pl.ds(0, tile)]
        pltpu.make_async_copy(view, view, ssem.at[p]).wait()

    def weight_copies(e):
        return (pltpu.make_async_copy(w1_hbm.at[e], w1s_ref, wsem.at[0]),
                pltpu.make_async_copy(w2_hbm.at[e], w2s_ref, wsem.at[1]))

    def switch_weights(blk):
        @pl.when(first_ref[blk] == 1)
        def _():
            for cp in weight_copies(0):
                cp.wait()
            step = 128
            for c in range(D_MODEL // step):
                w1b_ref[c * step:(c + 1) * step, :] = w1s_ref[c * step:(c + 1) * step, :].astype(BF16)
            half = D_FF // 2
            step = 64
            for c in range(half // step):
                pair = pltpu.pack_elementwise(
                    [w2s_ref[c * step:(c + 1) * step, :], w2s_ref[half + c * step:half + (c + 1) * step, :]],
                    packed_dtype=BF16)
                w2b_ref[2 * c * step:2 * (c + 1) * step, :] = pltpu.bitcast(pair, BF16)
            nxt = nexte_ref[blk]

            @pl.when(nxt >= 0)
            def _():
                for cp in weight_copies(nxt):
                    cp.start()

    def compute(p, blk, start_stage):
        e = be_ref[blk]
        x = jnp.concatenate(
            [x_in[p][pl.ds(c, rows, stride=ROW_TILES), :] for c in range(ROW_TILES)], axis=1).astype(BF16)
        b1 = b1_ref[pl.ds(e, 1), :]
        width = EXPERT_CHUNK
        even = (lax.broadcasted_iota(jnp.int32, (rows, width), 1) & 1) == 0
        zs = []
        for c in range(n_chunks):
            lo = c * width
            hi = D_FF + lo
            zero = _tile_lanes(start_stage(c), width // LANES)
            ha = _dot(x, w1b_ref[:, lo:lo + width]) + (b1[:, lo:lo + width] + zero)
            hb = _dot(x, w1b_ref[:, hi:hi + width]) + (b1[:, hi:hi + width] + zero)
            glu = jnp.where(even, ha, pltpu.roll(hb, 1, 1))
            lin = jnp.where(even, pltpu.roll(ha, width - 1, 1), hb)
            glu = jnp.minimum(glu, SWIGLU_LIMIT)
            lin = jnp.clip(lin, -SWIGLU_LIMIT, SWIGLU_LIMIT) + 1.0
            zs.append((glu * _sigmoid(SWIGLU_ALPHA * glu) * lin).astype(BF16))
        b2 = b2_ref[pl.ds(e, 1), :] + _tile_lanes(start_stage(n_chunks), ROW_TILES)
        y = _dot(jnp.concatenate(zs, axis=1), w2b_ref[...]) + b2
        for c in range(ROW_TILES):
            yb[p][pl.ds(c, rows, stride=ROW_TILES), :] = y[:, c * LANES:(c + 1) * LANES]

    def phase(blk, p, start_stage):
        switch_weights(blk)

        @pl.when(blk < n_used)
        def _():
            compute(p, blk, start_stage)

        @pl.when(blk >= n_used)
        def _():
            start_stage(None)

    @pl.when(i == 0)
    def _():
        yb1_ref[...] = jnp.zeros_like(yb1_ref)
        for cp in weight_copies(be_ref[0]):
            cp.start()

    @pl.when(i >= 1)
    def _():
        wait_scatters(0)

    phase(2 * i, 0, functools.partial(start_scatters, slots=slot_p_ref, p=1))

    wait_scatters(1)
    phase(2 * i + 1, 1, functools.partial(start_scatters, slots=slot_a_ref, p=0))

    @pl.when(i == last)
    def _():
        wait_scatters(0)
        for r in range(rows):
            scatter(slot_b_ref, r, 1).start(priority=r % DMA_PRIORITIES)
        wait_scatters(1)


def _experts_call(plan, row_slot, x_pages_2d, w_e1, b_e1, w_e2, b_e2, n_tokens):
    xpage, spage, block_e, first, next_e, n_used = plan
    rows = EXPERT_ROWS
    n_blocks = block_e.shape[0]
    assert n_blocks % 2 == 0
    n_slots = TOP_K * n_tokens + rows
    tile_rows = rows * ROW_TILES
    filler = row_slot.shape[0] - 1
    smem_rows = lambda index: pl.BlockSpec((1, 1, rows), index, memory_space=pltpu.SMEM)
    whole = lambda shape: pl.BlockSpec(shape, lambda i, *_: (0,) * len(shape))
    grid_spec = pltpu.PrefetchScalarGridSpec(
        num_scalar_prefetch=6,
        grid=(n_blocks // 2,),
        in_specs=[
            smem_rows(lambda i, xp, sp, *_: (jnp.where(i == 0, filler, sp[jnp.maximum(2 * i - 1, 0)]), 0, 0)),
            smem_rows(lambda i, xp, sp, *_: (sp[2 * i], 0, 0)),
            smem_rows(lambda i, xp, sp, *_: (sp[2 * i + 1], 0, 0)),
            pl.BlockSpec((tile_rows, LANES), lambda i, xp, *_: (xp[2 * i], 0)),
            pl.BlockSpec((tile_rows, LANES), lambda i, xp, *_: (xp[2 * i + 1], 0)),
            pl.BlockSpec(memory_space=pl.ANY),
            whole((N_EXPERTS, 2 * D_FF)),
            pl.BlockSpec(memory_space=pl.ANY),
            whole((N_EXPERTS, D_MODEL)),
        ],
        out_specs=pl.BlockSpec(memory_space=pl.ANY),
        scratch_shapes=[
            pltpu.VMEM((tile_rows + SUBLANES, LANES), F32),
            pltpu.VMEM((tile_rows + SUBLANES, LANES), F32),
            pltpu.VMEM((D_MODEL, 2 * D_FF), F32),
            pltpu.VMEM((D_FF, D_MODEL), F32),
            pltpu.VMEM((D_MODEL, 2 * D_FF), BF16),
            pltpu.VMEM((D_FF, D_MODEL), BF16),
            pltpu.SemaphoreType.DMA((2,)),
            pltpu.SemaphoreType.DMA((2,)),
        ],
    )
    return pl.pallas_call(
        _experts_kernel,
        grid_spec=grid_spec,
        out_shape=jax.ShapeDtypeStruct((n_slots, ROW_TILES, LANES), F32),
        compiler_params=pltpu.CompilerParams(
            dimension_semantics=("arbitrary",), vmem_limit_bytes=VMEM_LIMIT),
        name="experts",
    )(xpage, spage, block_e, first, next_e, n_used,
      row_slot, row_slot, row_slot, x_pages_2d, x_pages_2d, w_e1, b_e1, w_e2, b_e2)


def _combine_kernel(h1_ref, y0_ref, y1_ref, y2_ref, y3_ref, gcol_ref, g_ref, b_ref, out_ref):
    tc = COMBINE_TILE
    gcol = gcol_ref[...]
    y_refs = (y0_ref, y1_ref, y2_ref, y3_ref)
    chunks = []
    for c in range(ROW_TILES):
        z = DEEPNORM_ALPHA * h1_ref[pl.ds(c, tc, stride=ROW_TILES), :]
        for kk in range(TOP_K):
            z = z + gcol[:, kk:kk + 1] * y_refs[kk][pl.ds(c, tc, stride=ROW_TILES), :]
        chunks.append(z)
    total = chunks[0].sum(axis=1, keepdims=True)
    for z in chunks[1:]:
        total = total + z.sum(axis=1, keepdims=True)
    mu = total * (1.0 / D_MODEL)
    sq = None
    for z in chunks:
        zc = z - mu
        part = (zc * zc).sum(axis=1, keepdims=True)
        sq = part if sq is None else sq + part
    inv = lax.rsqrt(sq * (1.0 / D_MODEL) + LN_EPS)
    for c, z in enumerate(chunks):
        cs = slice(c * LANES, (c + 1) * LANES)
        out_ref[:, cs] = (z - mu) * inv * g_ref[:, cs] + b_ref[:, cs]


def _combine_call(h1_2d, y_2d, gcol, ln_g, ln_b, n_tokens):
    tc = COMBINE_TILE
    n_t = n_tokens // tc
    blk = tc * ROW_TILES
    y_spec = lambda kk: pl.BlockSpec((blk, LANES), lambda i: (kk * n_t + i, 0))
    return pl.pallas_call(
        _combine_kernel,
        grid=(n_t,),
        in_specs=[
            pl.BlockSpec((blk, LANES), lambda i: (i, 0)),
            y_spec(0), y_spec(1), y_spec(2), y_spec(3),
            pl.BlockSpec((tc, LANES), lambda i: (i, 0)),
            pl.BlockSpec((1, D_MODEL), lambda i: (0, 0)),
            pl.BlockSpec((1, D_MODEL), lambda i: (0, 0)),
        ],
        out_specs=pl.BlockSpec((tc, D_MODEL), lambda i: (i, 0)),
        out_shape=jax.ShapeDtypeStruct((n_tokens, D_MODEL), F32),
        compiler_params=pltpu.CompilerParams(
            dimension_semantics=("arbitrary",), vmem_limit_bytes=VMEM_LIMIT),
        name="combine",
    )(h1_2d, y_2d, y_2d, y_2d, y_2d, gcol, ln_g, ln_b)


def _prepare_mixer_weights(ln0_g, ln0_b, w_in, conv_w, conv_b, w_q, w_k, b_if, mh_gain, w_pool,
                           b_pool, ls_pool, w_branch_pool, w_branch_mlstm, w_out, ln1_g, ln1_b,
                           w_router, b_router):
    w_main = w_in[:, :COL_IF_END].astype(BF16)
    w_gates = w_in[:, COL_GATES:COL_GATES + 2 * D_MODEL].astype(BF16)
    bif = jnp.concatenate([b_if, jnp.zeros((LANES - 2 * N_HEADS,), F32)]).reshape(1, LANES)
    wqk = jnp.concatenate([w_q * (HEAD_DIM ** -0.5), w_k], axis=-1).astype(BF16)
    wpool_bd = jax.scipy.linalg.block_diag(*[w_pool[g] for g in range(len(POOL_WINDOWS))]).astype(BF16)
    wr_hi = w_router.astype(BF16)
    wr_lo = (w_router - wr_hi.astype(F32)).astype(BF16)
    lane_pad = jnp.zeros((D_MODEL, LANES - N_EXPERTS), BF16)
    wr_hi_lo = jnp.concatenate([wr_hi, lane_pad, wr_lo, lane_pad], axis=1)
    row = lambda v: v.reshape(1, -1)
    return (row(ln0_g), row(ln0_b), w_main, w_gates, bif, conv_w, row(conv_b), wqk, row(mh_gain), wpool_bd,
            row(b_pool), row(ls_pool), w_branch_pool.astype(BF16), w_branch_mlstm.astype(BF16),
            w_out.astype(BF16), row(ln1_g), row(ln1_b), wr_hi_lo,
            b_router.reshape(N_EXPERTS, 1))


def _plan(page_expert1, counts, pos, n_tokens, n_pages, n_rows_total):
    rows = EXPERT_ROWS
    n_assign = TOP_K * n_tokens
    i32 = jnp.int32
    slots = jnp.arange(n_assign, dtype=i32)
    filler = n_rows_total // rows
    hit = jnp.zeros(((filler + 1) * rows,), i32).at[pos.reshape(-1)].add(slots + 1)
    pad_slot = n_assign + jnp.arange((filler + 1) * rows, dtype=i32) % rows
    row_slot = jnp.where(hit == 0, pad_slot, hit - 1).reshape(filler + 1, 1, rows)

    pidx = jnp.arange(n_pages, dtype=i32)
    used = page_expert1 > 0
    n_used = jnp.sum(used.astype(i32))
    page_e = page_expert1 - 1
    key = jnp.where(used, page_e, N_EXPERTS) * n_pages + pidx
    place = jnp.sum((key[None, :] < key[:, None]).astype(i32), axis=1)
    at_block = place[None, :] == pidx[:, None]
    order = jnp.sum(jnp.where(at_block, pidx[None, :], 0), axis=1)
    block_e = jnp.clip(jnp.sum(jnp.where(at_block, page_e[None, :], 0), axis=1), 0, N_EXPERTS - 1)
    valid = pidx < n_used
    xpage = jnp.where(valid, order, 0)
    spage = jnp.where(valid, order, filler)
    prev_e = jnp.concatenate([jnp.full((1,), -1, i32), block_e[:-1]])
    first = jnp.logical_and(valid, block_e != prev_e).astype(i32)
    e_iota = jnp.arange(N_EXPERTS, dtype=i32)
    pages_per_e = (counts + rows - 1) // rows
    seg_end = jnp.cumsum(pages_per_e)
    mine = block_e[:, None] == e_iota[None, :]
    end_blk = jnp.sum(jnp.where(mine, seg_end[None, :], 0), axis=1)
    e_at_end = jnp.sum(jnp.where(pidx[None, :] == end_blk[:, None], block_e[None, :], 0), axis=1)
    next_e = jnp.where(end_blk < n_used, e_at_end, -1).astype(i32)
    plan = (xpage.astype(i32), spage.astype(i32), block_e.astype(i32), first, next_e,
            n_used.reshape(1).astype(i32))
    return plan, row_slot


def kernel(x, ln0_g, ln0_b, w_in, conv_w, conv_b, w_q, w_k, b_if, mh_gain, w_pool, b_pool, ls_pool,
           w_branch_pool, w_branch_mlstm, w_out, ln1_g, ln1_b, w_router, b_router, w_e1, b_e1,
           w_e2, b_e2, ln2_g, ln2_b):
    bsz, seq, _ = x.shape
    n_tokens = bsz * seq
    rows = EXPERT_ROWS
    assert w_in.shape[0] == 1, "single-layer trunk"
    assert seq % TOKEN_TILE == 0 and n_tokens % COMBINE_TILE == 0
    n_pages = (TOP_K * n_tokens + N_EXPERTS * (rows - 1) + rows - 1) // rows
    n_pages += n_pages % 2
    weights = _prepare_mixer_weights(
        ln0_g, ln0_b, w_in[0], conv_w[0], conv_b[0], w_q[0], w_k[0], b_if[0], mh_gain[0], w_pool[0],
        b_pool[0], ls_pool[0], w_branch_pool[0], w_branch_mlstm[0], w_out[0], ln1_g[0], ln1_b[0],
        w_router[0], b_router[0])
    h1_2d, x_pages, pos8, gcol, cnt, pe = _mixer_call(x, weights, n_pages)
    counts = cnt[:, 0].astype(jnp.int32)
    page_expert1 = pe[0, :n_pages].astype(jnp.int32)
    plan, row_slot = _plan(page_expert1, counts, pos8[:TOP_K], n_tokens, n_pages, x_pages.shape[0])
    y_slots = _experts_call(
        plan, row_slot, x_pages.reshape(-1, LANES), w_e1[0], b_e1[0], w_e2[0], b_e2[0], n_tokens)
    out = _combine_call(h1_2d, y_slots.reshape(-1, LANES), gcol, ln2_g[0].reshape(1, D_MODEL),
                        ln2_b[0].reshape(1, D_MODEL), n_tokens)
    return out.reshape(bsz, seq, D_MODEL)
```

```python
import functools

import jax
import jax.numpy as jnp
from jax import lax
from jax.experimental import pallas as pl
from jax.experimental.pallas import tpu as pltpu

F32 = jnp.float32
BF16 = jnp.bfloat16

D_MODEL = 1024
N_HEADS = 4
HEAD_DIM = 256
POOL_WIDTH = 512
POOL_GROUP = 128
POOL_WINDOWS = (2, 4, 8, 16)
CONV_WIDTH = 4
N_EXPERTS = 32
TOP_K = 4
D_FF = 1024
SWIGLU_ALPHA = 1.702
SWIGLU_LIMIT = 7.0
LN_EPS = 1e-5
DEEPNORM_ALPHA = 2.0 ** 0.25

SUBLANES = 8
LANES = 128
ROW_TILES = D_MODEL // LANES

COL_P = 0
COL_U = COL_P + POOL_WIDTH
COL_V = COL_U + D_MODEL
COL_O = COL_V + D_MODEL
COL_IF = COL_O + D_MODEL
COL_IF_END = COL_IF + LANES
COL_GATES = COL_IF + 2 * N_HEADS

TOKEN_TILE = 256
POOL_HALO = 24
CONV_HALO = 8
EXPERT_ROWS = 256
ROW_DMA_GROUPS = 12
SCATTER_STAGES = 4
EXPERT_CHUNK = 2 * LANES
COMBINE_TILE = 512
DMA_PRIORITIES = 2
VMEM_LIMIT = 56 * 1024 * 1024

assert TOKEN_TILE <= EXPERT_ROWS
assert (TOP_K * TOKEN_TILE) % EXPERT_ROWS == 0


def _layer_norm(x, g, b):
    mu = jnp.mean(x, axis=-1, keepdims=True)
    xc = x - mu
    var = jnp.mean(xc * xc, axis=-1, keepdims=True)
    return xc * lax.rsqrt(var + LN_EPS) * g + b


def _sigmoid(x):
    return 0.5 * jnp.tanh(0.5 * x) + 0.5


def _log_sigmoid(x):
    return -(jnp.maximum(-x, 0.0) + jnp.log(1.0 + jnp.exp(-jnp.abs(x))))


def _split3(x):
    hi = x.astype(BF16)
    r1 = x - hi.astype(F32)
    mid = r1.astype(BF16)
    lo = (r1 - mid.astype(F32)).astype(BF16)
    return hi, mid, lo


def _dot(a, b):
    return jnp.dot(a, b, preferred_element_type=F32)


def _dot_nt(a, b):
    return lax.dot_general(a, b, (((1,), (1,)), ((), ())), preferred_element_type=F32)


def _tile_lanes(row, n):
    return jnp.concatenate([row] * n, axis=1)


def _mixer_kernel(x_ref, ln0g_ref, ln0b_ref, win_ref, wgate_ref, bif_ref, convw_ref, convb_ref, wqk_ref,
                  gain_ref, wpool_ref, bpool_ref, lspool_ref, wbp_ref, wbm_ref, wout_ref,
                  ln1g_ref, ln1b_ref, wr_ref, br_ref,
                  h1_hbm, xp_hbm, pos_ref, gcol_ref, cnt_ref, pe_ref,
                  pext_ref, pw_ref, uext_ref, ct_ref, n_ref, m_ref, run_ref, page_ref, free_ref,
                  hx_ref, posv_ref, poss_ref, zb_ref, stv_ref, sts_ref, psem, rsem, hsem, zsem,
                  *, n_pages):
    tm = TOKEN_TILE
    tile_rows = tm * ROW_TILES
    b = pl.program_id(0)
    s = pl.program_id(1)
    n_s = pl.num_programs(1)
    g = b * n_s + s
    last = pl.num_programs(0) * n_s - 1
    par = g % 2
    q = 1 - par
    dyn_zero = lax.shift_right_arithmetic(g, 31)

    def hx_tile(p):
        return hx_ref.at[p, pl.ds(0, tile_rows)]

    def row_copy(p, t, k):
        return pltpu.make_async_copy(
            hx_ref.at[p, pl.ds(t * ROW_TILES, ROW_TILES)], xp_hbm.at[poss_ref[p, k, t]], rsem.at[p])

    def wait_row_copies(p):
        for _ in range(TOP_K):
            pltpu.make_async_copy(hx_tile(p), hx_tile(p), rsem.at[p]).wait()

    def h1_write(p, tile_index):
        off = pl.multiple_of(tile_index * tile_rows, tile_rows)
        return pltpu.make_async_copy(hx_tile(p), h1_hbm.at[pl.ds(off, tile_rows)], hsem.at[p])

    def pos_to_smem(p):
        return pltpu.make_async_copy(posv_ref.at[p], poss_ref.at[p], psem.at[p])

    @pl.when(s == 0)
    def _():
        pext_ref[0:POOL_HALO, :] = jnp.zeros((POOL_HALO, POOL_WIDTH), F32)
        uext_ref[0:CONV_HALO, :] = jnp.zeros((CONV_HALO, D_MODEL), F32)
        ct_ref[...] = jnp.zeros_like(ct_ref)
        n_ref[...] = jnp.zeros_like(n_ref)
        m_ref[...] = jnp.zeros_like(m_ref)
        pw_ref[:, 0:SUBLANES, :] = jnp.zeros((2, SUBLANES, POOL_GROUP), F32)

    @pl.when(g == 0)
    def _():
        run_ref[...] = jnp.zeros_like(run_ref)
        page_ref[...] = jnp.zeros_like(page_ref)
        free_ref[...] = jnp.zeros_like(free_ref)
        pe_ref[...] = jnp.zeros_like(pe_ref)
        zb_ref[...] = jnp.zeros_like(zb_ref)
        hx_ref[1] = jnp.zeros((tile_rows + SUBLANES, LANES), F32)
        for k in range(TOP_K):
            for t in range(tm):
                poss_ref[1, k, t] = n_pages * EXPERT_ROWS + k * tm + t

    @pl.when(g >= 1)
    def _():
        pos_to_smem(q).wait()
        wait_row_copies(par)

    @pl.when(g >= 2)
    def _():
        h1_write(par, 0).wait()

    def start_row_copies(grp):
        for t in range(tm * grp // ROW_DMA_GROUPS, tm * (grp + 1) // ROW_DMA_GROUPS):
            for k in range(TOP_K):
                row_copy(q, t, k).start(priority=k % DMA_PRIORITIES)
        spare = pl.multiple_of(tile_rows + dyn_zero * SUBLANES, SUBLANES)
        hx_ref[q, pl.ds(spare, SUBLANES), :] = jnp.full((SUBLANES, LANES), dyn_zero.astype(F32))
        return hx_ref[q, pl.ds(spare, SUBLANES), :][0:1, :]

    ln0b = ln0b_ref[...] + _tile_lanes(start_row_copies(0), ROW_TILES)
    h0 = _layer_norm(x_ref[...], ln0g_ref[...], ln0b)
    h0b = h0.astype(BF16)

    def proj(lo, hi):
        return _dot(h0b, win_ref[:, lo:hi])

    pext_ref[POOL_HALO:POOL_HALO + tm, :] = proj(COL_P, COL_U)
    tpos = s * tm + lax.broadcasted_iota(jnp.int32, (tm, 1), 0)
    groups = []
    end = POOL_HALO + tm
    for gi, w in enumerate(POOL_WINDOWS):
        cols = slice(gi * POOL_GROUP, (gi + 1) * POOL_GROUP)
        cur = pext_ref[POOL_HALO:end, cols]
        src, src_cols, k, nxt = pext_ref, cols, 1, 0
        while 2 * k < w:
            pw_ref[nxt, SUBLANES:end, :] = (src[SUBLANES:end, src_cols]
                                            + src[SUBLANES - k:end - k, src_cols])
            src, src_cols, k, nxt = pw_ref.at[nxt], slice(None), 2 * k, 1 - nxt
        acc = src[POOL_HALO:end, src_cols] + src[POOL_HALO - k:end - k, src_cols]
        inv_cnt = 1.0 / jnp.minimum(tpos + 1, w).astype(F32)
        groups.append(acc * inv_cnt - cur)
    pooled = jnp.concatenate(groups, axis=1)
    pext_ref[0:POOL_HALO, :] = pext_ref[tm:tm + POOL_HALO, :]
    bpool = bpool_ref[...] + _tile_lanes(start_row_copies(1), POOL_WIDTH // LANES)
    lspool = lspool_ref[...] + _tile_lanes(start_row_copies(8), POOL_WIDTH // LANES)
    mixed = (_dot(pooled.astype(BF16), wpool_ref[...]) + bpool) * lspool
    y_pool = _dot(mixed.astype(BF16), wbp_ref[...])

    uext_ref[CONV_HALO:CONV_HALO + tm, :] = proj(COL_U, COL_V)
    conv = convb_ref[...] + _tile_lanes(start_row_copies(2), ROW_TILES)
    for j in range(CONV_WIDTH):
        off = CONV_HALO - (CONV_WIDTH - 1) + j
        conv = conv + convw_ref[j:j + 1, :] * uext_ref[off:off + tm, :]
    uext_ref[0:CONV_HALO, :] = uext_ref[tm:tm + CONV_HALO, :]
    ucb = (conv * _sigmoid(conv)).astype(BF16)
    vb = proj(COL_V, COL_O).astype(BF16)

    slab = proj(COL_IF, COL_IF_END) + (bif_ref[...] + start_row_copies(3))
    lane = lax.broadcasted_iota(jnp.int32, (tm, LANES), 1)
    is_f = jnp.logical_and(lane >= N_HEADS, lane < 2 * N_HEADS)
    slab = jnp.where(is_f, _log_sigmoid(slab), slab)
    row_i = lax.broadcasted_iota(jnp.int32, (tm, tm), 0)
    col_i = lax.broadcasted_iota(jnp.int32, (tm, tm), 1)
    causal = row_i >= col_i
    tri = jnp.where(causal, 1.0, 0.0).astype(BF16)
    hi, mid, lo = _split3(slab)
    bcol = _dot(tri, hi) + _dot(tri, mid) + _dot(tri, lo)
    slab_t = slab.T
    bcol_t = bcol.T

    heads = []
    for h in range(N_HEADS):
        hs = slice(h * HEAD_DIM, (h + 1) * HEAD_DIM)
        qk = _dot(ucb[:, hs], wqk_ref[h])
        q_h = qk[:, :HEAD_DIM]
        k_h = qk[:, HEAD_DIM:]
        qb = q_h.astype(BF16)
        kb = k_h.astype(BF16)
        vh = vb[:, hs]

        i_c = slab[:, h:h + 1]
        b_c = bcol[:, N_HEADS + h:N_HEADS + h + 1]
        i_r = slab_t[h:h + 1, :]
        b_r = bcol_t[N_HEADS + h:N_HEADS + h + 1, :]
        m_prev = m_ref[:, h:h + 1] + start_row_copies(4 + h)[:, 0:1]

        d_log = jnp.where(causal, b_c - (b_r - i_r), -jnp.inf)
        m_inter = b_c + m_prev
        m_t = jnp.maximum(m_inter, jnp.max(d_log, axis=1, keepdims=True))
        w_intra = jnp.exp(d_log - m_t)
        sc = _dot_nt(qb, kb) * w_intra
        w_inter = jnp.exp(m_inter - m_t)
        ctb = ct_ref[h].astype(BF16)
        num = _dot(sc.astype(BF16), vh) + w_inter * _dot(qb, ctb)
        qn = jnp.sum(q_h * n_ref[h], axis=1, keepdims=True)
        den = jnp.sum(sc, axis=1, keepdims=True) + w_inter * qn
        hh = num * (1.0 / jnp.maximum(jnp.abs(den), jnp.exp(-m_t)))
        mu = jnp.mean(hh, axis=1, keepdims=True)
        hc = hh - mu
        var = jnp.mean(hc * hc, axis=1, keepdims=True)
        heads.append(hc * lax.rsqrt(var + LN_EPS))

        g_last = b_r[:, tm - 1:tm]
        m_new = jnp.maximum(g_last + m_prev, jnp.max(g_last - b_r + i_r, axis=1, keepdims=True))
        decay = jnp.exp(g_last + m_prev - m_new)
        w_state = jnp.exp(g_last - b_c + i_c - m_new)
        kw = k_h * w_state
        ct_ref[h] = decay * ct_ref[h] + _dot(kw.T.astype(BF16), vh)
        n_ref[h] = decay * n_ref[h] + jnp.sum(kw, axis=0, keepdims=True)
        m_ref[:, h:h + 1] = m_new

    hn = jnp.concatenate(heads, axis=1) * (gain_ref[...] + _tile_lanes(start_row_copies(9), ROW_TILES))
    h_out = _sigmoid(proj(COL_O, COL_IF)) * hn
    y_mlstm = _dot(h_out.astype(BF16), wbm_ref[...])

    merged = (_sigmoid(_dot(h0b, wgate_ref[:, 0:D_MODEL])) * y_pool
              + _sigmoid(_dot(h0b, wgate_ref[:, D_MODEL:2 * D_MODEL])) * y_mlstm)
    mix = _dot(merged.astype(BF16), wout_ref[...])
    ln1g = ln1g_ref[...] + _tile_lanes(start_row_copies(10), ROW_TILES)
    ln1b = ln1b_ref[...] + _tile_lanes(start_row_copies(11), ROW_TILES)
    h1 = _layer_norm(DEEPNORM_ALPHA * h0 + mix, ln1g, ln1b)
    for c in range(ROW_TILES):
        hx_ref[par, pl.ds(c, tm, stride=ROW_TILES), :] = h1[:, c * LANES:(c + 1) * LANES]
    h1_write(par, g).start()

    h1_hi = h1.astype(BF16)
    h1_lo = (h1 - h1_hi.astype(F32)).astype(BF16)
    la = _dot(h1_hi, wr_ref[...])
    lb = _dot(h1_lo, wr_ref[:, 0:LANES])
    logits = (la[:, 0:LANES] + la[:, LANES:2 * LANES] + lb).T[0:N_EXPERTS] + br_ref[...]
    e_iota = lax.broadcasted_iota(jnp.int32, (N_EXPERTS, tm), 0)
    vals, onehots = [], []
    lg = logits
    for _ in range(TOP_K):
        mx = jnp.max(lg, axis=0, keepdims=True)
        sel = jnp.min(jnp.where(lg == mx, e_iota, N_EXPERTS), axis=0, keepdims=True)
        oh = e_iota == sel
        lg = jnp.where(oh, -jnp.inf, lg)
        vals.append(mx)
        onehots.append(oh)
    exps = [jnp.exp(v - vals[0]) for v in vals]
    inv_den = 1.0 / (exps[0] + exps[1] + exps[2] + exps[3])
    gates = [e * inv_den for e in exps]

    oh_all = jnp.where(onehots[0], 1.0, 0.0)
    for oh in onehots[1:]:
        oh_all = oh_all + jnp.where(oh, 1.0, 0.0)
    strict = jnp.where(row_i < col_i, 1.0, 0.0).astype(BF16)
    run = run_ref[:, 0:1]
    rank = _dot(oh_all.astype(BF16), strict) + run
    count = jnp.sum(oh_all, axis=1, keepdims=True)
    inv_rows = 1.0 / EXPERT_ROWS
    pages_before = jnp.ceil(run * inv_rows)
    need = jnp.ceil((run + count) * inv_rows) - pages_before
    er = lax.broadcasted_iota(jnp.int32, (N_EXPERTS, N_EXPERTS), 0)
    ec = lax.broadcasted_iota(jnp.int32, (N_EXPERTS, N_EXPERTS), 1)
    earlier = jnp.where(er > ec, 1.0, 0.0).astype(BF16)
    need_b = jnp.broadcast_to(need, (N_EXPERTS, LANES)).astype(BF16)
    new_page = free_ref[0:1, 0:1] + _dot(earlier, need_b)[:, 0:1]
    page_seq = jnp.floor(rank * inv_rows)
    page = jnp.where(page_seq < pages_before, page_ref[:, 0:1], new_page)
    pos_all = page * EXPERT_ROWS + (rank - page_seq * EXPERT_ROWS)
    positions = [jnp.sum(jnp.where(oh, pos_all, 0.0), axis=0, keepdims=True) for oh in onehots]
    page_ref[...] = jnp.where(need > 0.0, new_page, page_ref[...])
    free_ref[...] = free_ref[...] + jnp.sum(need, axis=0, keepdims=True)
    run_ref[...] = run_ref[...] + count
    cnt_ref[...] = run_ref[...]
    p_lane = lax.broadcasted_iota(jnp.int32, (N_EXPERTS, pe_ref.shape[1]), 1).astype(F32)
    e_plus1 = (lax.broadcasted_iota(jnp.int32, (N_EXPERTS, 1), 0) + 1).astype(F32)
    taken = jnp.logical_and(p_lane == new_page, need > 0.0)
    pe_ref[...] = pe_ref[...] + jnp.sum(jnp.where(taken, e_plus1, 0.0), axis=0, keepdims=True)

    r8 = lax.broadcasted_iota(jnp.int32, (SUBLANES, tm), 0)
    pos_out = jnp.zeros((SUBLANES, tm), jnp.int32)
    r128 = lax.broadcasted_iota(jnp.int32, (LANES, tm), 0)
    gate_rows = jnp.zeros((LANES, tm), F32)
    for kk in range(TOP_K):
        pos_out = jnp.where(r8 == kk, positions[kk].astype(jnp.int32), pos_out)
        gate_rows = jnp.where(r128 == kk, gates[kk], gate_rows)
    pos_ref[...] = pos_out
    gcol_ref[...] = gate_rows.T
    posv_ref[par] = pos_out
    pos_to_smem(par).start()

    @pl.when(g == last)
    def _():
        pos_to_smem(par).wait()
        for t in range(tm):
            for k in range(TOP_K):
                row_copy(par, t, k).start(priority=k % DMA_PRIORITIES)
        wait_row_copies(q)
        wait_row_copies(par)
        h1_write(q, 0).wait()
        h1_write(par, 0).wait()

        st_lane = lax.broadcasted_iota(jnp.int32, (N_EXPERTS, LANES), 1)
        state = jnp.where(st_lane == 0, page_ref[...], jnp.where(st_lane == 1, run_ref[...], free_ref[...]))
        stv_ref[...] = state.astype(jnp.int32)
        state_copy = pltpu.make_async_copy(stv_ref, sts_ref, psem.at[par])
        state_copy.start()
        state_copy.wait()

        def zero_fill(wait):
            def act(rows_dst, n):
                cp = pltpu.make_async_copy(zb_ref.at[pl.ds(0, n)], xp_hbm.at[pl.ds(rows_dst, n)], zsem)
                if wait:
                    cp.wait()
                else:
                    cp.start()

            def unused_page(p, carry):
                act(pl.multiple_of(p * EXPERT_ROWS, EXPERT_ROWS), EXPERT_ROWS)
                return carry

            lax.fori_loop(sts_ref[0, 2], n_pages, unused_page, 0)

            def page_tail(e, carry):
                filled = sts_ref[e, 1] & (EXPERT_ROWS - 1)
                pad = jnp.where(filled == 0, 0, EXPERT_ROWS - filled)
                dst = sts_ref[e, 0] * EXPERT_ROWS + filled
                size = 1
                while size < EXPERT_ROWS:
                    @pl.when((pad & size) != 0)
                    def _(dst=dst, size=size):
                        act(dst, size)
                    dst = dst + (pad & size)
                    size *= 2
                return carry

            lax.fori_loop(0, N_EXPERTS, page_tail, 0)

        zero_fill(wait=False)
        zero_fill(wait=True)


def _const_spec(shape):
    zeros = (0,) * len(shape)
    return pl.BlockSpec(shape, lambda b, s: zeros, pipeline_mode=pl.Buffered(1))


def _mixer_call(x, weights, n_pages):
    bsz, seq, _ = x.shape
    tm = TOKEN_TILE
    n_s = seq // tm
    t_total = bsz * seq
    assert bsz * n_s >= 2
    tile = lambda b, s: (b * n_s + s)
    spare_pages = TOP_K * tm // EXPERT_ROWS
    pe_lanes = -(-n_pages // LANES) * LANES
    in_specs = [pl.BlockSpec((None, tm, D_MODEL), lambda b, s: (b, s, 0))]
    in_specs += [_const_spec(w.shape) for w in weights]
    out_shape = (
        jax.ShapeDtypeStruct((t_total * ROW_TILES, LANES), F32),
        jax.ShapeDtypeStruct(((n_pages + spare_pages) * EXPERT_ROWS, ROW_TILES, LANES), F32),
        jax.ShapeDtypeStruct((SUBLANES, t_total), jnp.int32),
        jax.ShapeDtypeStruct((t_total, LANES), F32),
        jax.ShapeDtypeStruct((N_EXPERTS, LANES), F32),
        jax.ShapeDtypeStruct((SUBLANES, pe_lanes), F32),
    )
    out_specs = (
        pl.BlockSpec(memory_space=pl.ANY),
        pl.BlockSpec(memory_space=pl.ANY),
        pl.BlockSpec((SUBLANES, tm), lambda b, s: (0, tile(b, s))),
        pl.BlockSpec((tm, LANES), lambda b, s: (tile(b, s), 0)),
        pl.BlockSpec((N_EXPERTS, LANES), lambda b, s: (0, 0)),
        pl.BlockSpec((SUBLANES, pe_lanes), lambda b, s: (0, 0)),
    )
    scratch = [
        pltpu.VMEM((POOL_HALO + tm, POOL_WIDTH), F32),
        pltpu.VMEM((2, POOL_HALO + tm, POOL_GROUP), F32),
        pltpu.VMEM((CONV_HALO + tm, D_MODEL), F32),
        pltpu.VMEM((N_HEADS, HEAD_DIM, HEAD_DIM), F32),
        pltpu.VMEM((N_HEADS, 1, HEAD_DIM), F32),
        pltpu.VMEM((1, LANES), F32),
        pltpu.VMEM((N_EXPERTS, LANES), F32),
        pltpu.VMEM((N_EXPERTS, LANES), F32),
        pltpu.VMEM((1, LANES), F32),
        pltpu.VMEM((2, tm * ROW_TILES + SUBLANES, LANES), F32),
        pltpu.VMEM((2, SUBLANES, tm), jnp.int32),
        pltpu.SMEM((2, SUBLANES, tm), jnp.int32),
        pltpu.VMEM((EXPERT_ROWS, ROW_TILES, LANES), F32),
        pltpu.VMEM((N_EXPERTS, LANES), jnp.int32),
        pltpu.SMEM((N_EXPERTS, LANES), jnp.int32),
        pltpu.SemaphoreType.DMA((2,)),
        pltpu.SemaphoreType.DMA((2,)),
        pltpu.SemaphoreType.DMA((2,)),
        pltpu.SemaphoreType.DMA(()),
    ]
    return pl.pallas_call(
        functools.partial(_mixer_kernel, n_pages=n_pages),
        grid=(bsz, n_s),
        in_specs=in_specs,
        out_specs=out_specs,
        out_shape=out_shape,
        scratch_shapes=scratch,
        compiler_params=pltpu.CompilerParams(
            dimension_semantics=("arbitrary", "arbitrary"), vmem_limit_bytes=VMEM_LIMIT),
        name="mixer",
    )(x, *weights)


def _experts_kernel(xpage_ref, spage_ref, be_ref, first_ref, nexte_ref, nused_ref,
                    slot_p_ref, slot_a_ref, slot_b_ref, xa_ref, xb_ref,
                    w1_hbm, b1_ref, w2_hbm, b2_ref, y_hbm,
                    yb0_ref, yb1_ref, w1s_ref, w2s_ref, w1b_ref, w2b_ref, ssem, wsem):
    del xpage_ref, spage_ref
    rows = EXPERT_ROWS
    tile = rows * ROW_TILES
    i = pl.program_id(0)
    last = pl.num_programs(0) - 1
    n_used = nused_ref[0]
    yb = (yb0_ref, yb1_ref)
    x_in = (xa_ref, xb_ref)
    dyn_zero = lax.shift_right_arithmetic(n_used, 31)
    n_chunks = D_FF // EXPERT_CHUNK
    assert SCATTER_STAGES <= n_chunks + 1

    def scatter(slots, r, p):
        return pltpu.make_async_copy(
            yb[p].at[pl.ds(r * ROW_TILES, ROW_TILES)], y_hbm.at[slots[0, 0, r]], ssem.at[p])

    def start_scatters(stage, slots, p):
        if stage is None:
            r_range = range(rows)
        elif stage < SCATTER_STAGES:
            r_range = range(rows * stage // SCATTER_STAGES, rows * (stage + 1) // SCATTER_STAGES)
        else:
            return jnp.zeros((1, LANES), F32)
        for r in r_range:
            scatter(slots, r, p).start(priority=r % DMA_PRIORITIES)
        spare = pl.multiple_of(tile + dyn_zero * SUBLANES, SUBLANES)
        yb[p][pl.ds(spare, SUBLANES), :] = jnp.full((SUBLANES, LANES), dyn_zero.astype(F32))
        return yb[p][pl.ds(spare, SUBLANES), :][0:1, :]

    def wait_scatters(p):
        view = yb[p].at[pl.ds(0, tile)]
        pltpu.make_async_copy(view, view, ssem.at[p]).wait()

    def weight_copies(e):
        return (pltpu.make_async_copy(w1_hbm.at[e], w1s_ref, wsem.at[0]),
                pltpu.make_async_copy(w2_hbm.at[e], w2s_ref, wsem.at[1]))

    def switch_weights(blk):
        @pl.when(first_ref[blk] == 1)
        def _():
            for cp in weight_copies(0):
                cp.wait()
            step = 128
            for c in range(D_MODEL // step):
                w1b_ref[c * step:(c + 1) * step, :] = w1s_ref[c * step:(c + 1) * step, :].astype(BF16)
            half = D_FF // 2
            step = 64
            for c in range(half // step):
                pair = pltpu.pack_elementwise(
                    [w2s_ref[c * step:(c + 1) * step, :], w2s_ref[half + c * step:half + (c + 1) * step, :]],
                    packed_dtype=BF16)
                w2b_ref[2 * c * step:2 * (c + 1) * step, :] = pltpu.bitcast(pair, BF16)
            nxt = nexte_ref[blk]

            @pl.when(nxt >= 0)
            def _():
                for cp in weight_copies(nxt):
                    cp.start()

    def compute(p, blk, start_stage):
        e = be_ref[blk]
        x = jnp.concatenate(
            [x_in[p][pl.ds(c, rows, stride=ROW_TILES), :] for c in range(ROW_TILES)], axis=1).astype(BF16)
        b1 = b1_ref[pl.ds(e, 1), :]
        width = EXPERT_CHUNK
        even = (lax.broadcasted_iota(jnp.int32, (rows, width), 1) & 1) == 0
        zs = []
        for c in range(n_chunks):
            lo = c * width
            hi = D_FF + lo
            zero = _tile_lanes(start_stage(c), width // LANES)
            ha = _dot(x, w1b_ref[:, lo:lo + width]) + (b1[:, lo:lo + width] + zero)
            hb = _dot(x, w1b_ref[:, hi:hi + width]) + (b1[:, hi:hi + width] + zero)
            glu = jnp.where(even, ha, pltpu.roll(hb, 1, 1))
            lin = jnp.where(even, pltpu.roll(ha, width - 1, 1), hb)
            glu = jnp.minimum(glu, SWIGLU_LIMIT)
            lin = jnp.clip(lin, -SWIGLU_LIMIT, SWIGLU_LIMIT) + 1.0
            zs.append((glu * _sigmoid(SWIGLU_ALPHA * glu) * lin).astype(BF16))
        b2 = b2_ref[pl.ds(e, 1), :] + _tile_lanes(start_stage(n_chunks), ROW_TILES)
        y = _dot(jnp.concatenate(zs, axis=1), w2b_ref[...]) + b2
        for c in range(ROW_TILES):
            yb[p][pl.ds(c, rows, stride=ROW_TILES), :] = y[:, c * LANES:(c + 1) * LANES]

    def phase(blk, p, start_stage):
        switch_weights(blk)

        @pl.when(blk < n_used)
        def _():
            compute(p, blk, start_stage)

        @pl.when(blk == n_used)
        def _():
            start_stage(None)

    @pl.when(i == 0)
    def _():
        yb1_ref[...] = jnp.zeros_like(yb1_ref)
        for cp in weight_copies(be_ref[0]):
            cp.start()


    @pl.when(jnp.logical_and(i >= 1, 2 * i - 1 <= n_used))
    def _():
        wait_scatters(0)

    phase(2 * i, 0, functools.partial(start_scatters, slots=slot_p_ref, p=1))

    @pl.when(2 * i <= n_used)
    def _():
        wait_scatters(1)

    phase(2 * i + 1, 1, functools.partial(start_scatters, slots=slot_a_ref, p=0))

    @pl.when(jnp.logical_and(i == last, 2 * i + 1 <= n_used))
    def _():
        wait_scatters(0)

    @pl.when(jnp.logical_and(i == last, 2 * i + 1 < n_used))
    def _():
        for r in range(rows):
            scatter(slot_b_ref, r, 1).start(priority=r % DMA_PRIORITIES)
        wait_scatters(1)


def _experts_call(plan, row_slot, x_pages_2d, w_e1, b_e1, w_e2, b_e2, n_tokens):
    xpage, spage, block_e, first, next_e, n_used = plan
    rows = EXPERT_ROWS
    n_blocks = block_e.shape[0]
    assert n_blocks % 2 == 0
    n_slots = TOP_K * n_tokens + rows
    tile_rows = rows * ROW_TILES
    filler = row_slot.shape[0] - 1
    smem_rows = lambda index: pl.BlockSpec((1, 1, rows), index, memory_space=pltpu.SMEM)
    whole = lambda shape: pl.BlockSpec(shape, lambda i, *_: (0,) * len(shape))
    grid_spec = pltpu.PrefetchScalarGridSpec(
        num_scalar_prefetch=6,
        grid=(n_blocks // 2,),
        in_specs=[
            smem_rows(lambda i, xp, sp, *_: (jnp.where(i == 0, filler, sp[jnp.maximum(2 * i - 1, 0)]), 0, 0)),
            smem_rows(lambda i, xp, sp, *_: (sp[2 * i], 0, 0)),
            smem_rows(lambda i, xp, sp, *_: (sp[2 * i + 1], 0, 0)),
            pl.BlockSpec((tile_rows, LANES), lambda i, xp, *_: (xp[2 * i], 0)),
            pl.BlockSpec((tile_rows, LANES), lambda i, xp, *_: (xp[2 * i + 1], 0)),
            pl.BlockSpec(memory_space=pl.ANY),
            whole((N_EXPERTS, 2 * D_FF)),
            pl.BlockSpec(memory_space=pl.ANY),
            whole((N_EXPERTS, D_MODEL)),
        ],
        out_specs=pl.BlockSpec(memory_space=pl.ANY),
        scratch_shapes=[
            pltpu.VMEM((tile_rows + SUBLANES, LANES), F32),
            pltpu.VMEM((tile_rows + SUBLANES, LANES), F32),
            pltpu.VMEM((D_MODEL, 2 * D_FF), F32),
            pltpu.VMEM((D_FF, D_MODEL), F32),
            pltpu.VMEM((D_MODEL, 2 * D_FF), BF16),
            pltpu.VMEM((D_FF, D_MODEL), BF16),
            pltpu.SemaphoreType.DMA((2,)),
            pltpu.SemaphoreType.DMA((2,)),
        ],
    )
    return pl.pallas_call(
        _experts_kernel,
        grid_spec=grid_spec,
        out_shape=jax.ShapeDtypeStruct((n_slots, ROW_TILES, LANES), F32),
        compiler_params=pltpu.CompilerParams(
            dimension_semantics=("arbitrary",), vmem_limit_bytes=VMEM_LIMIT),
        name="experts",
    )(xpage, spage, block_e, first, next_e, n_used,
      row_slot, row_slot, row_slot, x_pages_2d, x_pages_2d, w_e1, b_e1, w_e2, b_e2)


def _combine_kernel(h1_ref, y0_ref, y1_ref, y2_ref, y3_ref, gcol_ref, g_ref, b_ref, out_ref):
    tc = COMBINE_TILE
    gcol = gcol_ref[...]
    y_refs = (y0_ref, y1_ref, y2_ref, y3_ref)
    chunks = []
    for c in range(ROW_TILES):
        z = DEEPNORM_ALPHA * h1_ref[pl.ds(c, tc, stride=ROW_TILES), :]
        for kk in range(TOP_K):
            z = z + gcol[:, kk:kk + 1] * y_refs[kk][pl.ds(c, tc, stride=ROW_TILES), :]
        chunks.append(z)
    total = chunks[0].sum(axis=1, keepdims=True)
    for z in chunks[1:]:
        total = total + z.sum(axis=1, keepdims=True)
    mu = total * (1.0 / D_MODEL)
    sq = None
    for z in chunks:
        zc = z - mu
        part = (zc * zc).sum(axis=1, keepdims=True)
        sq = part if sq is None else sq + part
    inv = lax.rsqrt(sq * (1.0 / D_MODEL) + LN_EPS)
    for c, z in enumerate(chunks):
        cs = slice(c * LANES, (c + 1) * LANES)
        out_ref[:, cs] = (z - mu) * inv * g_ref[:, cs] + b_ref[:, cs]


def _combine_call(h1_2d, y_2d, gcol, ln_g, ln_b, n_tokens):
    tc = COMBINE_TILE
    n_t = n_tokens // tc
    blk = tc * ROW_TILES
    y_spec = lambda kk: pl.BlockSpec((blk, LANES), lambda i: (kk * n_t + i, 0))
    return pl.pallas_call(
        _combine_kernel,
        grid=(n_t,),
        in_specs=[
            pl.BlockSpec((blk, LANES), lambda i: (i, 0)),
            y_spec(0), y_spec(1), y_spec(2), y_spec(3),
            pl.BlockSpec((tc, LANES), lambda i: (i, 0)),
            pl.BlockSpec((1, D_MODEL), lambda i: (0, 0)),
            pl.BlockSpec((1, D_MODEL), lambda i: (0, 0)),
        ],
        out_specs=pl.BlockSpec((tc, D_MODEL), lambda i: (i, 0)),
        out_shape=jax.ShapeDtypeStruct((n_tokens, D_MODEL), F32),
        compiler_params=pltpu.CompilerParams(
            dimension_semantics=("arbitrary",), vmem_limit_bytes=VMEM_LIMIT),
        name="combine",
    )(h1_2d, y_2d, y_2d, y_2d, y_2d, gcol, ln_g, ln_b)


def _prepare_mixer_weights(ln0_g, ln0_b, w_in, conv_w, conv_b, w_q, w_k, b_if, mh_gain, w_pool,
                           b_pool, ls_pool, w_branch_pool, w_branch_mlstm, w_out, ln1_g, ln1_b,
                           w_router, b_router):
    w_main = w_in[:, :COL_IF_END].astype(BF16)
    w_gates = w_in[:, COL_GATES:COL_GATES + 2 * D_MODEL].astype(BF16)
    bif = jnp.concatenate([b_if, jnp.zeros((LANES - 2 * N_HEADS,), F32)]).reshape(1, LANES)
    wqk = jnp.concatenate([w_q * (HEAD_DIM ** -0.5), w_k], axis=-1).astype(BF16)
    wpool_bd = jax.scipy.linalg.block_diag(*[w_pool[g] for g in range(len(POOL_WINDOWS))]).astype(BF16)
    wr_hi = w_router.astype(BF16)
    wr_lo = (w_router - wr_hi.astype(F32)).astype(BF16)
    lane_pad = jnp.zeros((D_MODEL, LANES - N_EXPERTS), BF16)
    wr_hi_lo = jnp.concatenate([wr_hi, lane_pad, wr_lo, lane_pad], axis=1)
    row = lambda v: v.reshape(1, -1)
    return (row(ln0_g), row(ln0_b), w_main, w_gates, bif, conv_w, row(conv_b), wqk, row(mh_gain), wpool_bd,
            row(b_pool), row(ls_pool), w_branch_pool.astype(BF16), w_branch_mlstm.astype(BF16),
            w_out.astype(BF16), row(ln1_g), row(ln1_b), wr_hi_lo,
            b_router.reshape(N_EXPERTS, 1))


def _plan(page_expert1, counts, pos, n_tokens, n_pages, n_rows_total):
    rows = EXPERT_ROWS
    n_assign = TOP_K * n_tokens
    i32 = jnp.int32
    slots = jnp.arange(n_assign, dtype=i32)
    filler = n_rows_total // rows
    hit = jnp.zeros(((filler + 1) * rows,), i32).at[pos.reshape(-1)].add(slots + 1)
    pad_slot = n_assign + jnp.arange((filler + 1) * rows, dtype=i32) % rows
    row_slot = jnp.where(hit == 0, pad_slot, hit - 1).reshape(filler + 1, 1, rows)

    pidx = jnp.arange(n_pages, dtype=i32)
    used = page_expert1 > 0
    n_used = jnp.sum(used.astype(i32))
    page_e = page_expert1 - 1
    key = jnp.where(used, page_e, N_EXPERTS) * n_pages + pidx
    place = jnp.sum((key[None, :] < key[:, None]).astype(i32), axis=1)
    at_block = place[None, :] == pidx[:, None]
    order = jnp.sum(jnp.where(at_block, pidx[None, :], 0), axis=1)
    block_e = jnp.clip(jnp.sum(jnp.where(at_block, page_e[None, :], 0), axis=1), 0, N_EXPERTS - 1)
    valid = pidx < n_used
    xpage = jnp.where(valid, order, 0)
    spage = jnp.where(valid, order, filler)
    prev_e = jnp.concatenate([jnp.full((1,), -1, i32), block_e[:-1]])
    first = jnp.logical_and(valid, block_e != prev_e).astype(i32)
    e_iota = jnp.arange(N_EXPERTS, dtype=i32)
    pages_per_e = (counts + rows - 1) // rows
    seg_end = jnp.cumsum(pages_per_e)
    mine = block_e[:, None] == e_iota[None, :]
    end_blk = jnp.sum(jnp.where(mine, seg_end[None, :], 0), axis=1)
    e_at_end = jnp.sum(jnp.where(pidx[None, :] == end_blk[:, None], block_e[None, :], 0), axis=1)
    next_e = jnp.where(end_blk < n_used, e_at_end, -1).astype(i32)
    plan = (xpage.astype(i32), spage.astype(i32), block_e.astype(i32), first, next_e,
            n_used.reshape(1).astype(i32))
    return plan, row_slot


def kernel(x, ln0_g, ln0_b, w_in, conv_w, conv_b, w_q, w_k, b_if, mh_gain, w_pool, b_pool, ls_pool,
           w_branch_pool, w_branch_mlstm, w_out, ln1_g, ln1_b, w_router, b_router, w_e1, b_e1,
           w_e2, b_e2, ln2_g, ln2_b):
    bsz, seq, _ = x.shape
    n_tokens = bsz * seq
    rows = EXPERT_ROWS
    assert w_in.shape[0] == 1, "single-layer trunk"
    assert seq % TOKEN_TILE == 0 and n_tokens % COMBINE_TILE == 0
    n_pages = (TOP_K * n_tokens + N_EXPERTS * (rows - 1) + rows - 1) // rows
    n_pages += n_pages % 2
    weights = _prepare_mixer_weights(
        ln0_g, ln0_b, w_in[0], conv_w[0], conv_b[0], w_q[0], w_k[0], b_if[0], mh_gain[0], w_pool[0],
        b_pool[0], ls_pool[0], w_branch_pool[0], w_branch_mlstm[0], w_out[0], ln1_g[0], ln1_b[0],
        w_router[0], b_router[0])
    h1_2d, x_pages, pos8, gcol, cnt, pe = _mixer_call(x, weights, n_pages)
    counts = cnt[:, 0].astype(jnp.int32)
    page_expert1 = pe[0, :n_pages].astype(jnp.int32)
    plan, row_slot = _plan(page_expert1, counts, pos8[:TOP_K], n_tokens, n_pages, x_pages.shape[0])
    y_slots = _experts_call(
        plan, row_slot, x_pages.reshape(-1, LANES), w_e1[0], b_e1[0], w_e2[0], b_e2[0], n_tokens)
    out = _combine_call(h1_2d, y_slots.reshape(-1, LANES), gcol, ln2_g[0].reshape(1, D_MODEL),
                        ln2_b[0].reshape(1, D_MODEL), n_tokens)
    return out.reshape(bsz, seq, D_MODEL)
```

```python
import functools

import jax
import jax.numpy as jnp
from jax import lax
from jax.experimental import pallas as pl
from jax.experimental.pallas import tpu as pltpu

F32 = jnp.float32
BF16 = jnp.bfloat16

D_MODEL = 1024
N_HEADS = 4
HEAD_DIM = 256
POOL_WIDTH = 512
POOL_GROUP = 128
POOL_WINDOWS = (2, 4, 8, 16)
CONV_WIDTH = 4
N_EXPERTS = 32
TOP_K = 4
D_FF = 1024
SWIGLU_ALPHA = 1.702
SWIGLU_LIMIT = 7.0
LN_EPS = 1e-5
DEEPNORM_ALPHA = 2.0 ** 0.25

SUBLANES = 8
LANES = 128
ROW_TILES = D_MODEL // LANES

COL_P = 0
COL_U = COL_P + POOL_WIDTH
COL_V = COL_U + D_MODEL
COL_O = COL_V + D_MODEL
COL_IF = COL_O + D_MODEL
COL_IF_END = COL_IF + LANES
COL_GATES = COL_IF + 2 * N_HEADS

TOKEN_TILE = 256
POOL_HALO = 24
CONV_HALO = 8
EXPERT_ROWS = 256
ROW_DMA_GROUPS = 12
SCATTER_STAGES = 4
EXPERT_CHUNK = 2 * LANES
COMBINE_TILE = 512
DMA_PRIORITIES = 2
VMEM_LIMIT = 56 * 1024 * 1024

assert TOKEN_TILE <= EXPERT_ROWS
assert (TOP_K * TOKEN_TILE) % EXPERT_ROWS == 0


def _layer_norm(x, g, b):
    mu = jnp.mean(x, axis=-1, keepdims=True)
    xc = x - mu
    var = jnp.mean(xc * xc, axis=-1, keepdims=True)
    return xc * lax.rsqrt(var + LN_EPS) * g + b


def _sigmoid(x):
    return 0.5 * jnp.tanh(0.5 * x) + 0.5


def _log_sigmoid(x):
    return -(jnp.maximum(-x, 0.0) + jnp.log(1.0 + jnp.exp(-jnp.abs(x))))


def _split3(x):
    hi = x.astype(BF16)
    r1 = x - hi.astype(F32)
    mid = r1.astype(BF16)
    lo = (r1 - mid.astype(F32)).astype(BF16)
    return hi, mid, lo


def _dot(a, b):
    return jnp.dot(a, b, preferred_element_type=F32)


def _dot_nt(a, b):
    return lax.dot_general(a, b, (((1,), (1,)), ((), ())), preferred_element_type=F32)


def _tile_lanes(row, n):
    return jnp.concatenate([row] * n, axis=1)


def _mixer_kernel(x_ref, ln0g_ref, ln0b_ref, win_ref, wgate_ref, bif_ref, convw_ref, convb_ref, wqk_ref,
                  gain_ref, wpool_ref, bpool_ref, lspool_ref, wbp_ref, wbm_ref, wout_ref,
                  ln1g_ref, ln1b_ref, wr_ref, br_ref,
                  h1_hbm, xp_hbm, pos_ref, gcol_ref, cnt_ref, pe_ref,
                  pext_ref, pw_ref, uext_ref, ct_ref, n_ref, m_ref, run_ref, page_ref, free_ref,
                  hx_ref, posv_ref, poss_ref, zb_ref, stv_ref, sts_ref, psem, rsem, hsem, zsem,
                  *, n_pages):
    tm = TOKEN_TILE
    tile_rows = tm * ROW_TILES
    b = pl.program_id(0)
    s = pl.program_id(1)
    n_s = pl.num_programs(1)
    g = b * n_s + s
    last = pl.num_programs(0) * n_s - 1
    par = g % 2
    q = 1 - par
    dyn_zero = lax.shift_right_arithmetic(g, 31)

    def hx_tile(p):
        return hx_ref.at[p, pl.ds(0, tile_rows)]

    def row_copy(p, t, k):
        return pltpu.make_async_copy(
            hx_ref.at[p, pl.ds(t * ROW_TILES, ROW_TILES)], xp_hbm.at[poss_ref[p, k, t]], rsem.at[p])

    def wait_row_copies(p):
        for _ in range(TOP_K):
            pltpu.make_async_copy(hx_tile(p), hx_tile(p), rsem.at[p]).wait()

    def h1_write(p, tile_index):
        off = pl.multiple_of(tile_index * tile_rows, tile_rows)
        return pltpu.make_async_copy(hx_tile(p), h1_hbm.at[pl.ds(off, tile_rows)], hsem.at[p])

    def pos_to_smem(p):
        return pltpu.make_async_copy(posv_ref.at[p], poss_ref.at[p], psem.at[p])

    @pl.when(s == 0)
    def _():
        pext_ref[0:POOL_HALO, :] = jnp.zeros((POOL_HALO, POOL_WIDTH), F32)
        uext_ref[0:CONV_HALO, :] = jnp.zeros((CONV_HALO, D_MODEL), F32)
        ct_ref[...] = jnp.zeros_like(ct_ref)
        n_ref[...] = jnp.zeros_like(n_ref)
        m_ref[...] = jnp.zeros_like(m_ref)
        pw_ref[:, 0:SUBLANES, :] = jnp.zeros((2, SUBLANES, POOL_GROUP), F32)

    @pl.when(g == 0)
    def _():
        run_ref[...] = jnp.zeros_like(run_ref)
        page_ref[...] = jnp.zeros_like(page_ref)
        free_ref[...] = jnp.zeros_like(free_ref)
        pe_ref[...] = jnp.zeros_like(pe_ref)
        zb_ref[...] = jnp.zeros_like(zb_ref)
        hx_ref[1] = jnp.zeros((tile_rows + SUBLANES, LANES), F32)
        for k in range(TOP_K):
            for t in range(tm):
                poss_ref[1, k, t] = n_pages * EXPERT_ROWS + k * tm + t

    @pl.when(g >= 1)
    def _():
        pos_to_smem(q).wait()
        wait_row_copies(par)

    @pl.when(g >= 2)
    def _():
        h1_write(par, 0).wait()

    def start_row_copies(grp):
        for t in range(tm * grp // ROW_DMA_GROUPS, tm * (grp + 1) // ROW_DMA_GROUPS):
            for k in range(TOP_K):
                row_copy(q, t, k).start(priority=k % DMA_PRIORITIES)
        spare = pl.multiple_of(tile_rows + dyn_zero * SUBLANES, SUBLANES)
        hx_ref[q, pl.ds(spare, SUBLANES), :] = jnp.full((SUBLANES, LANES), dyn_zero.astype(F32))
        return hx_ref[q, pl.ds(spare, SUBLANES), :][0:1, :]

    ln0b = ln0b_ref[...] + _tile_lanes(start_row_copies(0), ROW_TILES)
    h0 = _layer_norm(x_ref[...], ln0g_ref[...], ln0b)
    h0b = h0.astype(BF16)

    def proj(lo, hi):
        return _dot(h0b, win_ref[:, lo:hi])

    pext_ref[POOL_HALO:POOL_HALO + tm, :] = proj(COL_P, COL_U)
    tpos = s * tm + lax.broadcasted_iota(jnp.int32, (tm, 1), 0)
    groups = []
    end = POOL_HALO + tm
    for gi, w in enumerate(POOL_WINDOWS):
        cols = slice(gi * POOL_GROUP, (gi + 1) * POOL_GROUP)
        cur = pext_ref[POOL_HALO:end, cols]
        src, src_cols, k, nxt = pext_ref, cols, 1, 0
        while 2 * k < w:
            pw_ref[nxt, SUBLANES:end, :] = (src[SUBLANES:end, src_cols]
                                            + src[SUBLANES - k:end - k, src_cols])
            src, src_cols, k, nxt = pw_ref.at[nxt], slice(None), 2 * k, 1 - nxt
        acc = src[POOL_HALO:end, src_cols] + src[POOL_HALO - k:end - k, src_cols]
        inv_cnt = 1.0 / jnp.minimum(tpos + 1, w).astype(F32)
        groups.append(acc * inv_cnt - cur)
    pooled = jnp.concatenate(groups, axis=1)
    pext_ref[0:POOL_HALO, :] = pext_ref[tm:tm + POOL_HALO, :]
    bpool = bpool_ref[...] + _tile_lanes(start_row_copies(1), POOL_WIDTH // LANES)
    lspool = lspool_ref[...] + _tile_lanes(start_row_copies(8), POOL_WIDTH // LANES)
    mixed = (_dot(pooled.astype(BF16), wpool_ref[...]) + bpool) * lspool
    y_pool = _dot(mixed.astype(BF16), wbp_ref[...])

    uext_ref[CONV_HALO:CONV_HALO + tm, :] = proj(COL_U, COL_V)
    conv = convb_ref[...] + _tile_lanes(start_row_copies(2), ROW_TILES)
    for j in range(CONV_WIDTH):
        off = CONV_HALO - (CONV_WIDTH - 1) + j
        conv = conv + convw_ref[j:j + 1, :] * uext_ref[off:off + tm, :]
    uext_ref[0:CONV_HALO, :] = uext_ref[tm:tm + CONV_HALO, :]
    ucb = (conv * _sigmoid(conv)).astype(BF16)
    vb = proj(COL_V, COL_O).astype(BF16)

    slab = proj(COL_IF, COL_IF_END) + (bif_ref[...] + start_row_copies(3))
    lane = lax.broadcasted_iota(jnp.int32, (tm, LANES), 1)
    is_f = jnp.logical_and(lane >= N_HEADS, lane < 2 * N_HEADS)
    slab = jnp.where(is_f, _log_sigmoid(slab), slab)
    row_i = lax.broadcasted_iota(jnp.int32, (tm, tm), 0)
    col_i = lax.broadcasted_iota(jnp.int32, (tm, tm), 1)
    causal = row_i >= col_i
    tri = jnp.where(causal, 1.0, 0.0).astype(BF16)
    hi, mid, lo = _split3(slab)
    bcol = _dot(tri, hi) + _dot(tri, mid) + _dot(tri, lo)
    slab_t = slab.T
    bcol_t = bcol.T

    heads = []
    for h in range(N_HEADS):
        hs = slice(h * HEAD_DIM, (h + 1) * HEAD_DIM)
        qk = _dot(ucb[:, hs], wqk_ref[h])
        q_h = qk[:, :HEAD_DIM]
        k_h = qk[:, HEAD_DIM:]
        qb = q_h.astype(BF16)
        kb = k_h.astype(BF16)
        vh = vb[:, hs]

        i_c = slab[:, h:h + 1]
        b_c = bcol[:, N_HEADS + h:N_HEADS + h + 1]
        i_r = slab_t[h:h + 1, :]
        b_r = bcol_t[N_HEADS + h:N_HEADS + h + 1, :]
        m_prev = m_ref[:, h:h + 1] + start_row_copies(4 + h)[:, 0:1]

        d_log = jnp.where(causal, b_c - (b_r - i_r), -jnp.inf)
        m_inter = b_c + m_prev
        m_t = jnp.maximum(m_inter, jnp.max(d_log, axis=1, keepdims=True))
        w_intra = jnp.exp(d_log - m_t)
        sc = _dot_nt(qb, kb) * w_intra
        w_inter = jnp.exp(m_inter - m_t)
        ctb = ct_ref[h].astype(BF16)
        num = _dot(sc.astype(BF16), vh) + w_inter * _dot(qb, ctb)
        qn = jnp.sum(q_h * n_ref[h], axis=1, keepdims=True)
        den = jnp.sum(sc, axis=1, keepdims=True) + w_inter * qn
        hh = num * (1.0 / jnp.maximum(jnp.abs(den), jnp.exp(-m_t)))
        mu = jnp.mean(hh, axis=1, keepdims=True)
        hc = hh - mu
        var = jnp.mean(hc * hc, axis=1, keepdims=True)
        heads.append(hc * lax.rsqrt(var + LN_EPS))

        g_last = b_r[:, tm - 1:tm]
        m_new = jnp.maximum(g_last + m_prev, jnp.max(g_last - b_r + i_r, axis=1, keepdims=True))
        decay = jnp.exp(g_last + m_prev - m_new)
        w_state = jnp.exp(g_last - b_c + i_c - m_new)
        kw = k_h * w_state
        ct_ref[h] = decay * ct_ref[h] + _dot(kw.T.astype(BF16), vh)
        n_ref[h] = decay * n_ref[h] + jnp.sum(kw, axis=0, keepdims=True)
        m_ref[:, h:h + 1] = m_new

    hn = jnp.concatenate(heads, axis=1) * (gain_ref[...] + _tile_lanes(start_row_copies(9), ROW_TILES))
    h_out = _sigmoid(proj(COL_O, COL_IF)) * hn
    y_mlstm = _dot(h_out.astype(BF16), wbm_ref[...])

    merged = (_sigmoid(_dot(h0b, wgate_ref[:, 0:D_MODEL])) * y_pool
              + _sigmoid(_dot(h0b, wgate_ref[:, D_MODEL:2 * D_MODEL])) * y_mlstm)
    mix = _dot(merged.astype(BF16), wout_ref[...])
    ln1g = ln1g_ref[...] + _tile_lanes(start_row_copies(10), ROW_TILES)
    ln1b = ln1b_ref[...] + _tile_lanes(start_row_copies(11), ROW_TILES)
    h1 = _layer_norm(DEEPNORM_ALPHA * h0 + mix, ln1g, ln1b)
    for c in range(ROW_TILES):
        hx_ref[par, pl.ds(c, tm, stride=ROW_TILES), :] = h1[:, c * LANES:(c + 1) * LANES]
    h1_write(par, g).start()

    h1_hi = h1.astype(BF16)
    h1_lo = (h1 - h1_hi.astype(F32)).astype(BF16)
    la = _dot(h1_hi, wr_ref[...])
    lb = _dot(h1_lo, wr_ref[:, 0:LANES])
    logits = (la[:, 0:LANES] + la[:, LANES:2 * LANES] + lb).T[0:N_EXPERTS] + br_ref[...]
    e_iota = lax.broadcasted_iota(jnp.int32, (N_EXPERTS, tm), 0)
    vals, onehots = [], []
    lg = logits
    for _ in range(TOP_K):
        mx = jnp.max(lg, axis=0, keepdims=True)
        sel = jnp.min(jnp.where(lg == mx, e_iota, N_EXPERTS), axis=0, keepdims=True)
        oh = e_iota == sel
        lg = jnp.where(oh, -jnp.inf, lg)
        vals.append(mx)
        onehots.append(oh)
    exps = [jnp.exp(v - vals[0]) for v in vals]
    inv_den = 1.0 / (exps[0] + exps[1] + exps[2] + exps[3])
    gates = [e * inv_den for e in exps]

    oh_all = jnp.where(onehots[0], 1.0, 0.0)
    for oh in onehots[1:]:
        oh_all = oh_all + jnp.where(oh, 1.0, 0.0)
    strict = jnp.where(row_i < col_i, 1.0, 0.0).astype(BF16)
    run = run_ref[:, 0:1]
    rank = _dot(oh_all.astype(BF16), strict) + run
    count = jnp.sum(oh_all, axis=1, keepdims=True)
    inv_rows = 1.0 / EXPERT_ROWS
    pages_before = jnp.ceil(run * inv_rows)
    need = jnp.ceil((run + count) * inv_rows) - pages_before
    er = lax.broadcasted_iota(jnp.int32, (N_EXPERTS, N_EXPERTS), 0)
    ec = lax.broadcasted_iota(jnp.int32, (N_EXPERTS, N_EXPERTS), 1)
    earlier = jnp.where(er > ec, 1.0, 0.0).astype(BF16)
    need_b = jnp.broadcast_to(need, (N_EXPERTS, LANES)).astype(BF16)
    new_page = free_ref[0:1, 0:1] + _dot(earlier, need_b)[:, 0:1]
    page_seq = jnp.floor(rank * inv_rows)
    page = jnp.where(page_seq < pages_before, page_ref[:, 0:1], new_page)
    pos_all = page * EXPERT_ROWS + (rank - page_seq * EXPERT_ROWS)
    positions = [jnp.sum(jnp.where(oh, pos_all, 0.0), axis=0, keepdims=True) for oh in onehots]
    page_ref[...] = jnp.where(need > 0.0, new_page, page_ref[...])
    free_ref[...] = free_ref[...] + jnp.sum(need, axis=0, keepdims=True)
    run_ref[...] = run_ref[...] + count
    cnt_ref[...] = run_ref[...]
    p_lane = lax.broadcasted_iota(jnp.int32, (N_EXPERTS, pe_ref.shape[1]), 1).astype(F32)
    e_plus1 = (lax.broadcasted_iota(jnp.int32, (N_EXPERTS, 1), 0) + 1).astype(F32)
    taken = jnp.logical_and(p_lane == new_page, need > 0.0)
    pe_ref[...] = pe_ref[...] + jnp.sum(jnp.where(taken, e_plus1, 0.0), axis=0, keepdims=True)

    r8 = lax.broadcasted_iota(jnp.int32, (SUBLANES, tm), 0)
    pos_out = jnp.zeros((SUBLANES, tm), jnp.int32)
    r128 = lax.broadcasted_iota(jnp.int32, (LANES, tm), 0)
    gate_rows = jnp.zeros((LANES, tm), F32)
    for kk in range(TOP_K):
        pos_out = jnp.where(r8 == kk, positions[kk].astype(jnp.int32), pos_out)
        gate_rows = jnp.where(r128 == kk, gates[kk], gate_rows)
    pos_ref[...] = pos_out
    gcol_ref[...] = gate_rows.T
    posv_ref[par] = pos_out
    pos_to_smem(par).start()

    @pl.when(g == last)
    def _():
        pos_to_smem(par).wait()
        for t in range(tm):
            for k in range(TOP_K):
                row_copy(par, t, k).start(priority=k % DMA_PRIORITIES)
        wait_row_copies(q)
        wait_row_copies(par)
        h1_write(q, 0).wait()
        h1_write(par, 0).wait()

        st_lane = lax.broadcasted_iota(jnp.int32, (N_EXPERTS, LANES), 1)
        state = jnp.where(st_lane == 0, page_ref[...], jnp.where(st_lane == 1, run_ref[...], free_ref[...]))
        stv_ref[...] = state.astype(jnp.int32)
        state_copy = pltpu.make_async_copy(stv_ref, sts_ref, psem.at[par])
        state_copy.start()
        state_copy.wait()

        def zero_fill(wait):
            def act(rows_dst, n):
                cp = pltpu.make_async_copy(zb_ref.at[pl.ds(0, n)], xp_hbm.at[pl.ds(rows_dst, n)], zsem)
                if wait:
                    cp.wait()
                else:
                    cp.start()

            def unused_page(p, carry):
                act(pl.multiple_of(p * EXPERT_ROWS, EXPERT_ROWS), EXPERT_ROWS)
                return carry

            lax.fori_loop(sts_ref[0, 2], n_pages, unused_page, 0)

            def page_tail(e, carry):
                filled = sts_ref[e, 1] & (EXPERT_ROWS - 1)
                pad = jnp.where(filled == 0, 0, EXPERT_ROWS - filled)
                dst = sts_ref[e, 0] * EXPERT_ROWS + filled
                size = 1
                while size < EXPERT_ROWS:
                    @pl.when((pad & size) != 0)
                    def _(dst=dst, size=size):
                        act(dst, size)
                    dst = dst + (pad & size)
                    size *= 2
                return carry

            lax.fori_loop(0, N_EXPERTS, page_tail, 0)

        zero_fill(wait=False)
        zero_fill(wait=True)


def _const_spec(shape):
    zeros = (0,) * len(shape)
    return pl.BlockSpec(shape, lambda b, s: zeros, pipeline_mode=pl.Buffered(1))


def _mixer_call(x, weights, n_pages):
    bsz, seq, _ = x.shape
    tm = TOKEN_TILE
    n_s = seq // tm
    t_total = bsz * seq
    assert bsz * n_s >= 2
    tile = lambda b, s: (b * n_s + s)
    spare_pages = TOP_K * tm // EXPERT_ROWS
    pe_lanes = -(-n_pages // LANES) * LANES
    in_specs = [pl.BlockSpec((None, tm, D_MODEL), lambda b, s: (b, s, 0))]
    in_specs += [_const_spec(w.shape) for w in weights]
    out_shape = (
        jax.ShapeDtypeStruct((t_total * ROW_TILES, LANES), F32),
        jax.ShapeDtypeStruct(((n_pages + spare_pages) * EXPERT_ROWS, ROW_TILES, LANES), F32),
        jax.ShapeDtypeStruct((SUBLANES, t_total), jnp.int32),
        jax.ShapeDtypeStruct((t_total, LANES), F32),
        jax.ShapeDtypeStruct((N_EXPERTS, LANES), F32),
        jax.ShapeDtypeStruct((SUBLANES, pe_lanes), F32),
    )
    out_specs = (
        pl.BlockSpec(memory_space=pl.ANY),
        pl.BlockSpec(memory_space=pl.ANY),
        pl.BlockSpec((SUBLANES, tm), lambda b, s: (0, tile(b, s))),
        pl.BlockSpec((tm, LANES), lambda b, s: (tile(b, s), 0)),
        pl.BlockSpec((N_EXPERTS, LANES), lambda b, s: (0, 0)),
        pl.BlockSpec((SUBLANES, pe_lanes), lambda b, s: (0, 0)),
    )
    scratch = [
        pltpu.VMEM((POOL_HALO + tm, POOL_WIDTH), F32),
        pltpu.VMEM((2, POOL_HALO + tm, POOL_GROUP), F32),
        pltpu.VMEM((CONV_HALO + tm, D_MODEL), F32),
        pltpu.VMEM((N_HEADS, HEAD_DIM, HEAD_DIM), F32),
        pltpu.VMEM((N_HEADS, 1, HEAD_DIM), F32),
        pltpu.VMEM((1, LANES), F32),
        pltpu.VMEM((N_EXPERTS, LANES), F32),
        pltpu.VMEM((N_EXPERTS, LANES), F32),
        pltpu.VMEM((1, LANES), F32),
        pltpu.VMEM((2, tm * ROW_TILES + SUBLANES, LANES), F32),
        pltpu.VMEM((2, SUBLANES, tm), jnp.int32),
        pltpu.SMEM((2, SUBLANES, tm), jnp.int32),
        pltpu.VMEM((EXPERT_ROWS, ROW_TILES, LANES), F32),
        pltpu.VMEM((N_EXPERTS, LANES), jnp.int32),
        pltpu.SMEM((N_EXPERTS, LANES), jnp.int32),
        pltpu.SemaphoreType.DMA((2,)),
        pltpu.SemaphoreType.DMA((2,)),
        pltpu.SemaphoreType.DMA((2,)),
        pltpu.SemaphoreType.DMA(()),
    ]
    return pl.pallas_call(
        functools.partial(_mixer_kernel, n_pages=n_pages),
        grid=(bsz, n_s),
        in_specs=in_specs,
        out_specs=out_specs,
        out_shape=out_shape,
        scratch_shapes=scratch,
        compiler_params=pltpu.CompilerParams(
            dimension_semantics=("arbitrary", "arbitrary"), vmem_limit_bytes=VMEM_LIMIT),
        name="mixer",
    )(x, *weights)


def _experts_kernel(xpage_ref, spage_ref, be_ref, first_ref, nexte_ref, nused_ref,
                    slot_p_ref, slot_a_ref, slot_b_ref, xa_ref, xb_ref,
                    w1_hbm, b1_ref, w2_hbm, b2_ref, y_hbm,
                    yb0_ref, yb1_ref, w1s_ref, w2s_ref, w1b_ref, w2b_ref, ssem, wsem):
    del xpage_ref, spage_ref
    rows = EXPERT_ROWS
    tile = rows * ROW_TILES
    i = pl.program_id(0)
    last = pl.num_programs(0) - 1
    n_used = nused_ref[0]
    yb = (yb0_ref, yb1_ref)
    x_in = (xa_ref, xb_ref)
    dyn_zero = lax.shift_right_arithmetic(n_used, 31)
    n_chunks = D_FF // EXPERT_CHUNK
    assert SCATTER_STAGES <= n_chunks + 1

    def scatter(slots, r, p):
        return pltpu.make_async_copy(
            yb[p].at[pl.ds(r * ROW_TILES, ROW_TILES)], y_hbm.at[slots[0, 0, r]], ssem.at[p])

    def start_scatters(stage, slots, p):
        if stage is None:
            r_range = range(rows)
        elif stage < SCATTER_STAGES:
            r_range = range(rows * stage // SCATTER_STAGES, rows * (stage + 1) // SCATTER_STAGES)
        else:
            return jnp.zeros((1, LANES), F32)
        for r in r_range:
            scatter(slots, r, p).start(priority=r % DMA_PRIORITIES)
        spare = pl.multiple_of(tile + dyn_zero * SUBLANES, SUBLANES)
        yb[p][pl.ds(spare, SUBLANES), :] = jnp.full((SUBLANES, LANES), dyn_zero.astype(F32))
        return yb[p][pl.ds(spare, SUBLANES), :][0:1, :]

    def wait_scatters(p):
        view = yb[p].at[pl.ds(0, tile)]
        pltpu.make_async_copy(view, view, ssem.at[p]).wait()

    def weight_copies(e):
        return (pltpu.make_async_copy(w1_hbm.at[e], w1s_ref, wsem.at[0]),
                pltpu.make_async_copy(w2_hbm.at[e], w2s_ref, wsem.at[1]))

    def switch_weights(blk):
        @pl.when(first_ref[blk] == 1)
        def _():
            for cp in weight_copies(0):
                cp.wait()
            step = 128
            for c in range(D_MODEL // step):
                w1b_ref[c * step:(c + 1) * step, :] = w1s_ref[c * step:(c + 1) * step, :].astype(BF16)
            half = D_FF // 2
            step = 64
            for c in range(half // step):
                pair = pltpu.pack_elementwise(
                    [w2s_ref[c * step:(c + 1) * step, :], w2s_ref[half + c * step:half + (c + 1) * step, :]],
                    packed_dtype=BF16)
                w2b_ref[2 * c * step:2 * (c + 1) * step, :] = pltpu.bitcast(pair, BF16)
            nxt = nexte_ref[blk]

            @pl.when(nxt >= 0)
            def _():
                for cp in weight_copies(nxt):
                    cp.start()

    def compute(p, blk, start_stage):
        e = be_ref[blk]
        x = jnp.concatenate(
            [x_in[p][pl.ds(c, rows, stride=ROW_TILES), :] for c in range(ROW_TILES)], axis=1).astype(BF16)
        b1 = b1_ref[pl.ds(e, 1), :]
        width = EXPERT_CHUNK
        even = (lax.broadcasted_iota(jnp.int32, (rows, width), 1) & 1) == 0
        zs = []
        for c in range(n_chunks):
            lo = c * width
            hi = D_FF + lo
            zero = _tile_lanes(start_stage(c), width // LANES)
            ha = _dot(x, w1b_ref[:, lo:lo + width]) + (b1[:, lo:lo + width] + zero)
            hb = _dot(x, w1b_ref[:, hi:hi + width]) + (b1[:, hi:hi + width] + zero)
            glu = jnp.where(even, ha, pltpu.roll(hb, 1, 1))
            lin = jnp.where(even, pltpu.roll(ha, width - 1, 1), hb)
            glu = jnp.minimum(glu, SWIGLU_LIMIT)
            lin = jnp.clip(lin, -SWIGLU_LIMIT, SWIGLU_LIMIT) + 1.0
            zs.append((glu * _sigmoid(SWIGLU_ALPHA * glu) * lin).astype(BF16))
        b2 = b2_ref[pl.ds(e, 1), :] + _tile_lanes(start_stage(n_chunks), ROW_TILES)
        y = _dot(jnp.concatenate(zs, axis=1), w2b_ref[...]) + b2
        for c in range(ROW_TILES):
            yb[p][pl.ds(c, rows, stride=ROW_TILES), :] = y[:, c * LANES:(c + 1) * LANES]

    def phase(blk, p, start_stage):
        switch_weights(blk)

        @pl.when(blk < n_used)
        def _():
            compute(p, blk, start_stage)

        @pl.when(blk == n_used)
        def _():
            start_stage(None)

    @pl.when(i == 0)
    def _():
        yb1_ref[...] = jnp.zeros_like(yb1_ref)
        for cp in weight_copies(be_ref[0]):
            cp.start()


    @pl.when(jnp.logical_and(i >= 1, 2 * i - 1 <= n_used))
    def _():
        wait_scatters(0)

    phase(2 * i, 0, functools.partial(start_scatters, slots=slot_p_ref, p=1))

    @pl.when(2 * i <= n_used)
    def _():
        wait_scatters(1)

    phase(2 * i + 1, 1, functools.partial(start_scatters, slots=slot_a_ref, p=0))

    @pl.when(jnp.logical_and(i == last, 2 * i + 1 <= n_used))
    def _():
        wait_scatters(0)

    @pl.when(jnp.logical_and(i == last, 2 * i + 1 < n_used))
    def _():
        for r in range(rows):
            scatter(slot_b_ref, r, 1).start(priority=r % DMA_PRIORITIES)
        wait_scatters(1)


def _experts_call(plan, row_slot, x_pages_2d, w_e1, b_e1, w_e2, b_e2, n_tokens):
    xpage, spage, block_e, first, next_e, n_used = plan
    rows = EXPERT_ROWS
    n_blocks = block_e.shape[0]
    assert n_blocks % 2 == 0
    n_slots = TOP_K * n_tokens + rows
    tile_rows = rows * ROW_TILES
    filler = row_slot.shape[0] - 1
    smem_rows = lambda index: pl.BlockSpec((1, 1, rows), index, memory_space=pltpu.SMEM)
    whole = lambda shape: pl.BlockSpec(shape, lambda i, *_: (0,) * len(shape))
    grid_spec = pltpu.PrefetchScalarGridSpec(
        num_scalar_prefetch=6,
        grid=(n_blocks // 2,),
        in_specs=[
            smem_rows(lambda i, xp, sp, *_: (jnp.where(i == 0, filler, sp[jnp.maximum(2 * i - 1, 0)]), 0, 0)),
            smem_rows(lambda i, xp, sp, *_: (sp[2 * i], 0, 0)),
            smem_rows(lambda i, xp, sp, *_: (sp[2 * i + 1], 0, 0)),
            pl.BlockSpec((tile_rows, LANES), lambda i, xp, *_: (xp[2 * i], 0)),
            pl.BlockSpec((tile_rows, LANES), lambda i, xp, *_: (xp[2 * i + 1], 0)),
            pl.BlockSpec(memory_space=pl.ANY),
            whole((N_EXPERTS, 2 * D_FF)),
            pl.BlockSpec(memory_space=pl.ANY),
            whole((N_EXPERTS, D_MODEL)),
        ],
        out_specs=pl.BlockSpec(memory_space=pl.ANY),
        scratch_shapes=[
            pltpu.VMEM((tile_rows + SUBLANES, LANES), F32),
            pltpu.VMEM((tile_rows + SUBLANES, LANES), F32),
            pltpu.VMEM((D_MODEL, 2 * D_FF), F32),
            pltpu.VMEM((D_FF, D_MODEL), F32),
            pltpu.VMEM((D_MODEL, 2 * D_FF), BF16),
            pltpu.VMEM((D_FF, D_MODEL), BF16),
            pltpu.SemaphoreType.DMA((2,)),
            pltpu.SemaphoreType.DMA((2,)),
        ],
    )
    return pl.pallas_call(
        _experts_kernel,
        grid_spec=grid_spec,
        out_shape=jax.ShapeDtypeStruct((n_slots, ROW_TILES, LANES), F32),
        compiler_params=pltpu.CompilerParams(
            dimension_semantics=("arbitrary",), vmem_limit_bytes=VMEM_LIMIT),
        name="experts",
    )(xpage, spage, block_e, first, next_e, n_used,
      row_slot, row_slot, row_slot, x_pages_2d, x_pages_2d, w_e1, b_e1, w_e2, b_e2)


def _combine_kernel(h1_ref, y0_ref, y1_ref, y2_ref, y3_ref, gcol_ref, g_ref, b_ref, out_ref):
    tc = COMBINE_TILE
    gcol = gcol_ref[...]
    y_refs = (y0_ref, y1_ref, y2_ref, y3_ref)
    chunks = []
    for c in range(ROW_TILES):
        z = DEEPNORM_ALPHA * h1_ref[pl.ds(c, tc, stride=ROW_TILES), :]
        for kk in range(TOP_K):
            z = z + gcol[:, kk:kk + 1] * y_refs[kk][pl.ds(c, tc, stride=ROW_TILES), :]
        chunks.append(z)
    total = chunks[0].sum(axis=1, keepdims=True)
    for z in chunks[1:]:
        total = total + z.sum(axis=1, keepdims=True)
    mu = total * (1.0 / D_MODEL)
    sq = None
    for z in chunks:
        zc = z - mu
        part = (zc * zc).sum(axis=1, keepdims=True)
        sq = part if sq is None else sq + part
    inv = lax.rsqrt(sq * (1.0 / D_MODEL) + LN_EPS)
    for c, z in enumerate(chunks):
        cs = slice(c * LANES, (c + 1) * LANES)
        out_ref[:, cs] = (z - mu) * inv * g_ref[:, cs] + b_ref[:, cs]


def _combine_call(h1_2d, y_2d, gcol, ln_g, ln_b, n_tokens):
    tc = COMBINE_TILE
    n_t = n_tokens // tc
    blk = tc * ROW_TILES
    y_spec = lambda kk: pl.BlockSpec((blk, LANES), lambda i: (kk * n_t + i, 0))
    return pl.pallas_call(
        _combine_kernel,
        grid=(n_t,),
        in_specs=[
            pl.BlockSpec((blk, LANES), lambda i: (i, 0)),
            y_spec(0), y_spec(1), y_spec(2), y_spec(3),
            pl.BlockSpec((tc, LANES), lambda i: (i, 0)),
            pl.BlockSpec((1, D_MODEL), lambda i: (0, 0)),
            pl.BlockSpec((1, D_MODEL), lambda i: (0, 0)),
        ],
        out_specs=pl.BlockSpec((tc, D_MODEL), lambda i: (i, 0)),
        out_shape=jax.ShapeDtypeStruct((n_tokens, D_MODEL), F32),
        compiler_params=pltpu.CompilerParams(
            dimension_semantics=("arbitrary",), vmem_limit_bytes=VMEM_LIMIT),
        name="combine",
    )(h1_2d, y_2d, y_2d, y_2d, y_2d, gcol, ln_g, ln_b)


def _prepare_mixer_weights(ln0_g, ln0_b, w_in, conv_w, conv_b, w_q, w_k, b_if, mh_gain, w_pool,
                           b_pool, ls_pool, w_branch_pool, w_branch_mlstm, w_out, ln1_g, ln1_b,
                           w_router, b_router):
    w_main = w_in[:, :COL_IF_END].astype(BF16)
    w_gates = w_in[:, COL_GATES:COL_GATES + 2 * D_MODEL].astype(BF16)
    bif = jnp.concatenate([b_if, jnp.zeros((LANES - 2 * N_HEADS,), F32)]).reshape(1, LANES)
    wqk = jnp.concatenate([w_q * (HEAD_DIM ** -0.5), w_k], axis=-1).astype(BF16)
    wpool_bd = jax.scipy.linalg.block_diag(*[w_pool[g] for g in range(len(POOL_WINDOWS))]).astype(BF16)
    wr_hi = w_router.astype(BF16)
    wr_lo = (w_router - wr_hi.astype(F32)).astype(BF16)
    lane_pad = jnp.zeros((D_MODEL, LANES - N_EXPERTS), BF16)
    wr_hi_lo = jnp.concatenate([wr_hi, lane_pad, wr_lo, lane_pad], axis=1)
    row = lambda v: v.reshape(1, -1)
    return (row(ln0_g), row(ln0_b), w_main, w_gates, bif, conv_w, row(conv_b), wqk, row(mh_gain), wpool_bd,
            row(b_pool), row(ls_pool), w_branch_pool.astype(BF16), w_branch_mlstm.astype(BF16),
            w_out.astype(BF16), row(ln1_g), row(ln1_b), wr_hi_lo,
            b_router.reshape(N_EXPERTS, 1))


def _plan(page_expert1, counts, pos, n_tokens, n_pages, n_rows_total):
    rows = EXPERT_ROWS
    n_assign = TOP_K * n_tokens
    i32 = jnp.int32
    slots = jnp.arange(n_assign, dtype=i32)
    filler = n_rows_total // rows
    hit = jnp.zeros(((filler + 1) * rows,), i32).at[pos.reshape(-1)].add(slots + 1)
    pad_slot = n_assign + jnp.arange((filler + 1) * rows, dtype=i32) % rows
    row_slot = jnp.where(hit == 0, pad_slot, hit - 1).reshape(filler + 1, 1, rows)

    pidx = jnp.arange(n_pages, dtype=i32)
    used = page_expert1 > 0
    n_used = jnp.sum(used.astype(i32))
    page_e = page_expert1 - 1
    key = jnp.where(used, page_e, N_EXPERTS) * n_pages + pidx
    place = jnp.sum((key[None, :] < key[:, None]).astype(i32), axis=1)
    at_block = place[None, :] == pidx[:, None]
    order = jnp.sum(jnp.where(at_block, pidx[None, :], 0), axis=1)
    block_e = jnp.clip(jnp.sum(jnp.where(at_block, page_e[None, :], 0), axis=1), 0, N_EXPERTS - 1)
    valid = pidx < n_used
    last_page = jnp.sum(jnp.where(pidx == n_used - 1, order, 0))
    xpage = jnp.where(valid, order, last_page)
    spage = jnp.where(valid, order, filler)
    prev_e = jnp.concatenate([jnp.full((1,), -1, i32), block_e[:-1]])
    first = jnp.logical_and(valid, block_e != prev_e).astype(i32)
    e_iota = jnp.arange(N_EXPERTS, dtype=i32)
    pages_per_e = (counts + rows - 1) // rows
    seg_end = jnp.cumsum(pages_per_e)
    mine = block_e[:, None] == e_iota[None, :]
    end_blk = jnp.sum(jnp.where(mine, seg_end[None, :], 0), axis=1)
    e_at_end = jnp.sum(jnp.where(pidx[None, :] == end_blk[:, None], block_e[None, :], 0), axis=1)
    next_e = jnp.where(end_blk < n_used, e_at_end, -1).astype(i32)
    plan = (xpage.astype(i32), spage.astype(i32), block_e.astype(i32), first, next_e,
            n_used.reshape(1).astype(i32))
    return plan, row_slot


def kernel(x, ln0_g, ln0_b, w_in, conv_w, conv_b, w_q, w_k, b_if, mh_gain, w_pool, b_pool, ls_pool,
           w_branch_pool, w_branch_mlstm, w_out, ln1_g, ln1_b, w_router, b_router, w_e1, b_e1,
           w_e2, b_e2, ln2_g, ln2_b):
    bsz, seq, _ = x.shape
    n_tokens = bsz * seq
    rows = EXPERT_ROWS
    assert w_in.shape[0] == 1, "single-layer trunk"
    assert seq % TOKEN_TILE == 0 and n_tokens % COMBINE_TILE == 0
    n_pages = (TOP_K * n_tokens + N_EXPERTS * (rows - 1) + rows - 1) // rows
    n_pages += n_pages % 2
    weights = _prepare_mixer_weights(
        ln0_g, ln0_b, w_in[0], conv_w[0], conv_b[0], w_q[0], w_k[0], b_if[0], mh_gain[0], w_pool[0],
        b_pool[0], ls_pool[0], w_branch_pool[0], w_branch_mlstm[0], w_out[0], ln1_g[0], ln1_b[0],
        w_router[0], b_router[0])
    h1_2d, x_pages, pos8, gcol, cnt, pe = _mixer_call(x, weights, n_pages)
    counts = cnt[:, 0].astype(jnp.int32)
    page_expert1 = pe[0, :n_pages].astype(jnp.int32)
    plan, row_slot = _plan(page_expert1, counts, pos8[:TOP_K], n_tokens, n_pages, x_pages.shape[0])
    y_slots = _experts_call(
        plan, row_slot, x_pages.reshape(-1, LANES), w_e1[0], b_e1[0], w_e2[0], b_e2[0], n_tokens)
    out = _combine_call(h1_2d, y_slots.reshape(-1, LANES), gcol, ln2_g[0].reshape(1, D_MODEL),
                        ln2_b[0].reshape(1, D_MODEL), n_tokens)
    return out.reshape(bsz, seq, D_MODEL)
```

```python
import functools

import jax
import jax.numpy as jnp
from jax import lax
from jax.experimental import pallas as pl
from jax.experimental.pallas import tpu as pltpu

F32 = jnp.float32
BF16 = jnp.bfloat16

D_MODEL = 1024
N_HEADS = 4
HEAD_DIM = 256
POOL_WIDTH = 512
POOL_GROUP = 128
POOL_WINDOWS = (2, 4, 8, 16)
CONV_WIDTH = 4
N_EXPERTS = 32
TOP_K = 4
D_FF = 1024
SWIGLU_ALPHA = 1.702
SWIGLU_LIMIT = 7.0
LN_EPS = 1e-5
DEEPNORM_ALPHA = 2.0 ** 0.25

SUBLANES = 8
LANES = 128
ROW_TILES = D_MODEL // LANES

COL_P = 0
COL_U = COL_P + POOL_WIDTH
COL_V = COL_U + D_MODEL
COL_O = COL_V + D_MODEL
COL_IF = COL_O + D_MODEL
COL_IF_END = COL_IF + LANES
COL_GATES = COL_IF + 2 * N_HEADS

TOKEN_TILE = 256
POOL_HALO = 24
CONV_HALO = 8
EXPERT_ROWS = 256
ROW_DMA_GROUPS = 12
SCATTER_STAGES = 4
EXPERT_CHUNK = 2 * LANES
COMBINE_TILE = 512
DMA_PRIORITIES = 2
VMEM_LIMIT = 56 * 1024 * 1024

assert TOKEN_TILE <= EXPERT_ROWS
assert (TOP_K * TOKEN_TILE) % EXPERT_ROWS == 0


def _layer_norm(x, g, b):
    mu = jnp.mean(x, axis=-1, keepdims=True)
    xc = x - mu
    var = jnp.mean(xc * xc, axis=-1, keepdims=True)
    return xc * lax.rsqrt(var + LN_EPS) * g + b


def _sigmoid(x):
    return 0.5 * jnp.tanh(0.5 * x) + 0.5


def _log_sigmoid(x):
    return -(jnp.maximum(-x, 0.0) + jnp.log(1.0 + jnp.exp(-jnp.abs(x))))


def _split3(x):
    hi = x.astype(BF16)
    r1 = x - hi.astype(F32)
    mid = r1.astype(BF16)
    lo = (r1 - mid.astype(F32)).astype(BF16)
    return hi, mid, lo


def _dot(a, b):
    return jnp.dot(a, b, preferred_element_type=F32)


def _dot_nt(a, b):
    return lax.dot_general(a, b, (((1,), (1,)), ((), ())), preferred_element_type=F32)


def _tile_lanes(row, n):
    return jnp.concatenate([row] * n, axis=1)


def _mixer_kernel(x_ref, ln0g_ref, ln0b_ref, win_ref, wgate_ref, bif_ref, convw_ref, convb_ref, wqk_ref,
                  gain_ref, wpool_ref, bpool_ref, lspool_ref, wbp_ref, wbm_ref, wout_ref,
                  ln1g_ref, ln1b_ref, wr_ref, br_ref,
                  h1_hbm, xp_hbm, pos_ref, gcol_ref, cnt_ref, pe_ref, h1d_ref,
                  pext_ref, pw_ref, uext_ref, ct_ref, n_ref, m_ref, run_ref, page_ref, free_ref,
                  hx_ref, posv_ref, poss_ref, zb_ref, stv_ref, sts_ref, psem, rsem, hsem, zsem,
                  *, n_pages):
    tm = TOKEN_TILE
    tile_rows = tm * ROW_TILES
    b = pl.program_id(0)
    s = pl.program_id(1)
    n_s = pl.num_programs(1)
    g = b * n_s + s
    last = pl.num_programs(0) * n_s - 1
    par = g % 2
    q = 1 - par
    dyn_zero = lax.shift_right_arithmetic(g, 31)

    def hx_tile(p):
        return hx_ref.at[p, pl.ds(0, tile_rows)]

    def row_copy(p, t, k):
        return pltpu.make_async_copy(
            hx_ref.at[p, pl.ds(t * ROW_TILES, ROW_TILES)], xp_hbm.at[poss_ref[p, k, t]], rsem.at[p])

    def wait_row_copies(p):
        for _ in range(TOP_K):
            pltpu.make_async_copy(hx_tile(p), hx_tile(p), rsem.at[p]).wait()

    def h1_write(p, tile_index):
        off = pl.multiple_of(tile_index * tile_rows, tile_rows)
        return pltpu.make_async_copy(hx_tile(p), h1_hbm.at[pl.ds(off, tile_rows)], hsem.at[p])

    def pos_to_smem(p):
        return pltpu.make_async_copy(posv_ref.at[p], poss_ref.at[p], psem.at[p])

    @pl.when(s == 0)
    def _():
        pext_ref[0:POOL_HALO, :] = jnp.zeros((POOL_HALO, POOL_WIDTH), F32)
        uext_ref[0:CONV_HALO, :] = jnp.zeros((CONV_HALO, D_MODEL), F32)
        ct_ref[...] = jnp.zeros_like(ct_ref)
        n_ref[...] = jnp.zeros_like(n_ref)
        m_ref[...] = jnp.zeros_like(m_ref)
        pw_ref[:, 0:SUBLANES, :] = jnp.zeros((2, SUBLANES, POOL_GROUP), F32)

    @pl.when(g == 0)
    def _():
        run_ref[...] = jnp.zeros_like(run_ref)
        page_ref[...] = jnp.zeros_like(page_ref)
        free_ref[...] = jnp.zeros_like(free_ref)
        pe_ref[...] = jnp.zeros_like(pe_ref)
        zb_ref[...] = jnp.zeros_like(zb_ref)
        hx_ref[1] = jnp.zeros((tile_rows + SUBLANES, LANES), F32)
        for k in range(TOP_K):
            for t in range(tm):
                poss_ref[1, k, t] = n_pages * EXPERT_ROWS + k * tm + t

    @pl.when(g >= 1)
    def _():
        pos_to_smem(q).wait()
        wait_row_copies(par)

    @pl.when(g >= 2)
    def _():
        h1_write(par, 0).wait()

    def start_row_copies(grp):
        for t in range(tm * grp // ROW_DMA_GROUPS, tm * (grp + 1) // ROW_DMA_GROUPS):
            for k in range(TOP_K):
                row_copy(q, t, k).start(priority=k % DMA_PRIORITIES)
        spare = pl.multiple_of(tile_rows + dyn_zero * SUBLANES, SUBLANES)
        hx_ref[q, pl.ds(spare, SUBLANES), :] = jnp.full((SUBLANES, LANES), dyn_zero.astype(F32))
        return hx_ref[q, pl.ds(spare, SUBLANES), :][0:1, :]

    ln0b = ln0b_ref[...] + _tile_lanes(start_row_copies(0), ROW_TILES)
    h0 = _layer_norm(x_ref[...], ln0g_ref[...], ln0b)
    h0b = h0.astype(BF16)

    def proj(lo, hi):
        return _dot(h0b, win_ref[:, lo:hi])

    pext_ref[POOL_HALO:POOL_HALO + tm, :] = proj(COL_P, COL_U)
    tpos = s * tm + lax.broadcasted_iota(jnp.int32, (tm, 1), 0)
    groups = []
    end = POOL_HALO + tm
    for gi, w in enumerate(POOL_WINDOWS):
        cols = slice(gi * POOL_GROUP, (gi + 1) * POOL_GROUP)
        cur = pext_ref[POOL_HALO:end, cols]
        src, src_cols, k, nxt = pext_ref, cols, 1, 0
        while 2 * k < w:
            pw_ref[nxt, SUBLANES:end, :] = (src[SUBLANES:end, src_cols]
                                            + src[SUBLANES - k:end - k, src_cols])
            src, src_cols, k, nxt = pw_ref.at[nxt], slice(None), 2 * k, 1 - nxt
        acc = src[POOL_HALO:end, src_cols] + src[POOL_HALO - k:end - k, src_cols]
        inv_cnt = 1.0 / jnp.minimum(tpos + 1, w).astype(F32)
        groups.append(acc * inv_cnt - cur)
    pooled = jnp.concatenate(groups, axis=1)
    pext_ref[0:POOL_HALO, :] = pext_ref[tm:tm + POOL_HALO, :]
    bpool = bpool_ref[...] + _tile_lanes(start_row_copies(1), POOL_WIDTH // LANES)
    lspool = lspool_ref[...] + _tile_lanes(start_row_copies(8), POOL_WIDTH // LANES)
    mixed = (_dot(pooled.astype(BF16), wpool_ref[...]) + bpool) * lspool
    y_pool = _dot(mixed.astype(BF16), wbp_ref[...])

    uext_ref[CONV_HALO:CONV_HALO + tm, :] = proj(COL_U, COL_V)
    conv = convb_ref[...] + _tile_lanes(start_row_copies(2), ROW_TILES)
    for j in range(CONV_WIDTH):
        off = CONV_HALO - (CONV_WIDTH - 1) + j
        conv = conv + convw_ref[j:j + 1, :] * uext_ref[off:off + tm, :]
    uext_ref[0:CONV_HALO, :] = uext_ref[tm:tm + CONV_HALO, :]
    ucb = (conv * _sigmoid(conv)).astype(BF16)
    vb = proj(COL_V, COL_O).astype(BF16)

    slab = proj(COL_IF, COL_IF_END) + (bif_ref[...] + start_row_copies(3))
    lane = lax.broadcasted_iota(jnp.int32, (tm, LANES), 1)
    is_f = jnp.logical_and(lane >= N_HEADS, lane < 2 * N_HEADS)
    slab = jnp.where(is_f, _log_sigmoid(slab), slab)
    row_i = lax.broadcasted_iota(jnp.int32, (tm, tm), 0)
    col_i = lax.broadcasted_iota(jnp.int32, (tm, tm), 1)
    causal = row_i >= col_i
    tri = jnp.where(causal, 1.0, 0.0).astype(BF16)
    hi, mid, lo = _split3(slab)
    bcol = _dot(tri, hi) + _dot(tri, mid) + _dot(tri, lo)
    slab_t = slab.T
    bcol_t = bcol.T

    heads = []
    for h in range(N_HEADS):
        hs = slice(h * HEAD_DIM, (h + 1) * HEAD_DIM)
        qk = _dot(ucb[:, hs], wqk_ref[h])
        q_h = qk[:, :HEAD_DIM]
        k_h = qk[:, HEAD_DIM:]
        qb = q_h.astype(BF16)
        kb = k_h.astype(BF16)
        vh = vb[:, hs]

        i_c = slab[:, h:h + 1]
        b_c = bcol[:, N_HEADS + h:N_HEADS + h + 1]
        i_r = slab_t[h:h + 1, :]
        b_r = bcol_t[N_HEADS + h:N_HEADS + h + 1, :]
        m_prev = m_ref[:, h:h + 1] + start_row_copies(4 + h)[:, 0:1]

        d_log = jnp.where(causal, b_c - (b_r - i_r), -jnp.inf)
        m_inter = b_c + m_prev
        m_t = jnp.maximum(m_inter, jnp.max(d_log, axis=1, keepdims=True))
        w_intra = jnp.exp(d_log - m_t)
        sc = _dot_nt(qb, kb) * w_intra
        w_inter = jnp.exp(m_inter - m_t)
        ctb = ct_ref[h].astype(BF16)
        num = _dot(sc.astype(BF16), vh) + w_inter * _dot(qb, ctb)
        qn = jnp.sum(q_h * n_ref[h], axis=1, keepdims=True)
        den = jnp.sum(sc, axis=1, keepdims=True) + w_inter * qn
        hh = num * (1.0 / jnp.maximum(jnp.abs(den), jnp.exp(-m_t)))
        mu = jnp.mean(hh, axis=1, keepdims=True)
        hc = hh - mu
        var = jnp.mean(hc * hc, axis=1, keepdims=True)
        heads.append(hc * lax.rsqrt(var + LN_EPS))

        g_last = b_r[:, tm - 1:tm]
        m_new = jnp.maximum(g_last + m_prev, jnp.max(g_last - b_r + i_r, axis=1, keepdims=True))
        decay = jnp.exp(g_last + m_prev - m_new)
        w_state = jnp.exp(g_last - b_c + i_c - m_new)
        kw = k_h * w_state
        ct_ref[h] = decay * ct_ref[h] + _dot(kw.T.astype(BF16), vh)
        n_ref[h] = decay * n_ref[h] + jnp.sum(kw, axis=0, keepdims=True)
        m_ref[:, h:h + 1] = m_new

    hn = jnp.concatenate(heads, axis=1) * (gain_ref[...] + _tile_lanes(start_row_copies(9), ROW_TILES))
    h_out = _sigmoid(proj(COL_O, COL_IF)) * hn
    y_mlstm = _dot(h_out.astype(BF16), wbm_ref[...])

    merged = (_sigmoid(_dot(h0b, wgate_ref[:, 0:D_MODEL])) * y_pool
              + _sigmoid(_dot(h0b, wgate_ref[:, D_MODEL:2 * D_MODEL])) * y_mlstm)
    mix = _dot(merged.astype(BF16), wout_ref[...])
    ln1g = ln1g_ref[...] + _tile_lanes(start_row_copies(10), ROW_TILES)
    ln1b = ln1b_ref[...] + _tile_lanes(start_row_copies(11), ROW_TILES)
    h1 = _layer_norm(DEEPNORM_ALPHA * h0 + mix, ln1g, ln1b)
    for c in range(ROW_TILES):
        hx_ref[par, pl.ds(c, tm, stride=ROW_TILES), :] = h1[:, c * LANES:(c + 1) * LANES]
    h1_write(par, g).start()
    h1d_ref[...] = h1

    h1_hi = h1.astype(BF16)
    h1_lo = (h1 - h1_hi.astype(F32)).astype(BF16)
    la = _dot(h1_hi, wr_ref[...])
    lb = _dot(h1_lo, wr_ref[:, 0:LANES])
    logits = (la[:, 0:LANES] + la[:, LANES:2 * LANES] + lb).T[0:N_EXPERTS] + br_ref[...]
    e_iota = lax.broadcasted_iota(jnp.int32, (N_EXPERTS, tm), 0)
    vals, onehots = [], []
    lg = logits
    for _ in range(TOP_K):
        mx = jnp.max(lg, axis=0, keepdims=True)
        sel = jnp.min(jnp.where(lg == mx, e_iota, N_EXPERTS), axis=0, keepdims=True)
        oh = e_iota == sel
        lg = jnp.where(oh, -jnp.inf, lg)
        vals.append(mx)
        onehots.append(oh)
    exps = [jnp.exp(v - vals[0]) for v in vals]
    inv_den = 1.0 / (exps[0] + exps[1] + exps[2] + exps[3])
    gates = [e * inv_den for e in exps]

    oh_all = jnp.where(onehots[0], 1.0, 0.0)
    for oh in onehots[1:]:
        oh_all = oh_all + jnp.where(oh, 1.0, 0.0)
    strict = jnp.where(row_i < col_i, 1.0, 0.0).astype(BF16)
    run = run_ref[:, 0:1]
    rank = _dot(oh_all.astype(BF16), strict) + run
    count = jnp.sum(oh_all, axis=1, keepdims=True)
    inv_rows = 1.0 / EXPERT_ROWS
    pages_before = jnp.ceil(run * inv_rows)
    need = jnp.ceil((run + count) * inv_rows) - pages_before
    er = lax.broadcasted_iota(jnp.int32, (N_EXPERTS, N_EXPERTS), 0)
    ec = lax.broadcasted_iota(jnp.int32, (N_EXPERTS, N_EXPERTS), 1)
    earlier = jnp.where(er > ec, 1.0, 0.0).astype(BF16)
    need_b = jnp.broadcast_to(need, (N_EXPERTS, LANES)).astype(BF16)
    new_page = free_ref[0:1, 0:1] + _dot(earlier, need_b)[:, 0:1]
    page_seq = jnp.floor(rank * inv_rows)
    page = jnp.where(page_seq < pages_before, page_ref[:, 0:1], new_page)
    pos_all = page * EXPERT_ROWS + (rank - page_seq * EXPERT_ROWS)
    positions = [jnp.sum(jnp.where(oh, pos_all, 0.0), axis=0, keepdims=True) for oh in onehots]
    page_ref[...] = jnp.where(need > 0.0, new_page, page_ref[...])
    free_ref[...] = free_ref[...] + jnp.sum(need, axis=0, keepdims=True)
    run_ref[...] = run_ref[...] + count
    cnt_ref[...] = run_ref[...]
    p_lane = lax.broadcasted_iota(jnp.int32, (N_EXPERTS, pe_ref.shape[1]), 1).astype(F32)
    e_plus1 = (lax.broadcasted_iota(jnp.int32, (N_EXPERTS, 1), 0) + 1).astype(F32)
    taken = jnp.logical_and(p_lane == new_page, need > 0.0)
    pe_ref[...] = pe_ref[...] + jnp.sum(jnp.where(taken, e_plus1, 0.0), axis=0, keepdims=True)

    r8 = lax.broadcasted_iota(jnp.int32, (SUBLANES, tm), 0)
    pos_out = jnp.zeros((SUBLANES, tm), jnp.int32)
    r128 = lax.broadcasted_iota(jnp.int32, (LANES, tm), 0)
    gate_rows = jnp.zeros((LANES, tm), F32)
    for kk in range(TOP_K):
        pos_out = jnp.where(r8 == kk, positions[kk].astype(jnp.int32), pos_out)
        gate_rows = jnp.where(r128 == kk, gates[kk], gate_rows)
    pos_ref[...] = pos_out
    gcol_ref[...] = gate_rows.T
    posv_ref[par] = pos_out
    pos_to_smem(par).start()

    @pl.when(g == last)
    def _():
        pos_to_smem(par).wait()
        for t in range(tm):
            for k in range(TOP_K):
                row_copy(par, t, k).start(priority=k % DMA_PRIORITIES)
        wait_row_copies(q)
        wait_row_copies(par)
        h1_write(q, 0).wait()
        h1_write(par, 0).wait()

        st_lane = lax.broadcasted_iota(jnp.int32, (N_EXPERTS, LANES), 1)
        state = jnp.where(st_lane == 0, page_ref[...], jnp.where(st_lane == 1, run_ref[...], free_ref[...]))
        stv_ref[...] = state.astype(jnp.int32)
        state_copy = pltpu.make_async_copy(stv_ref, sts_ref, psem.at[par])
        state_copy.start()
        state_copy.wait()

        def zero_fill(wait):
            def act(rows_dst, n):
                cp = pltpu.make_async_copy(zb_ref.at[pl.ds(0, n)], xp_hbm.at[pl.ds(rows_dst, n)], zsem)
                if wait:
                    cp.wait()
                else:
                    cp.start()

            def unused_page(p, carry):
                act(pl.multiple_of(p * EXPERT_ROWS, EXPERT_ROWS), EXPERT_ROWS)
                return carry

            lax.fori_loop(sts_ref[0, 2], n_pages, unused_page, 0)

            def page_tail(e, carry):
                filled = sts_ref[e, 1] & (EXPERT_ROWS - 1)
                pad = jnp.where(filled == 0, 0, EXPERT_ROWS - filled)
                dst = sts_ref[e, 0] * EXPERT_ROWS + filled
                size = 1
                while size < EXPERT_ROWS:
                    @pl.when((pad & size) != 0)
                    def _(dst=dst, size=size):
                        act(dst, size)
                    dst = dst + (pad & size)
                    size *= 2
                return carry

            lax.fori_loop(0, N_EXPERTS, page_tail, 0)

        zero_fill(wait=False)
        zero_fill(wait=True)


def _const_spec(shape):
    zeros = (0,) * len(shape)
    return pl.BlockSpec(shape, lambda b, s: zeros, pipeline_mode=pl.Buffered(1))


def _mixer_call(x, weights, n_pages):
    bsz, seq, _ = x.shape
    tm = TOKEN_TILE
    n_s = seq // tm
    t_total = bsz * seq
    assert bsz * n_s >= 2
    tile = lambda b, s: (b * n_s + s)
    spare_pages = TOP_K * tm // EXPERT_ROWS
    pe_lanes = -(-n_pages // LANES) * LANES
    in_specs = [pl.BlockSpec((None, tm, D_MODEL), lambda b, s: (b, s, 0))]
    in_specs += [_const_spec(w.shape) for w in weights]
    out_shape = (
        jax.ShapeDtypeStruct((t_total * ROW_TILES, LANES), F32),
        jax.ShapeDtypeStruct(((n_pages + spare_pages) * EXPERT_ROWS, ROW_TILES, LANES), F32),
        jax.ShapeDtypeStruct((SUBLANES, t_total), jnp.int32),
        jax.ShapeDtypeStruct((t_total, LANES), F32),
        jax.ShapeDtypeStruct((N_EXPERTS, LANES), F32),
        jax.ShapeDtypeStruct((SUBLANES, pe_lanes), F32),
        jax.ShapeDtypeStruct((t_total, D_MODEL), F32),
    )
    out_specs = (
        pl.BlockSpec(memory_space=pl.ANY),
        pl.BlockSpec(memory_space=pl.ANY),
        pl.BlockSpec((SUBLANES, tm), lambda b, s: (0, tile(b, s))),
        pl.BlockSpec((tm, LANES), lambda b, s: (tile(b, s), 0)),
        pl.BlockSpec((N_EXPERTS, LANES), lambda b, s: (0, 0)),
        pl.BlockSpec((SUBLANES, pe_lanes), lambda b, s: (0, 0)),
        pl.BlockSpec((tm, D_MODEL), lambda b, s: (tile(b, s), 0)),
    )
    scratch = [
        pltpu.VMEM((POOL_HALO + tm, POOL_WIDTH), F32),
        pltpu.VMEM((2, POOL_HALO + tm, POOL_GROUP), F32),
        pltpu.VMEM((CONV_HALO + tm, D_MODEL), F32),
        pltpu.VMEM((N_HEADS, HEAD_DIM, HEAD_DIM), F32),
        pltpu.VMEM((N_HEADS, 1, HEAD_DIM), F32),
        pltpu.VMEM((1, LANES), F32),
        pltpu.VMEM((N_EXPERTS, LANES), F32),
        pltpu.VMEM((N_EXPERTS, LANES), F32),
        pltpu.VMEM((1, LANES), F32),
        pltpu.VMEM((2, tm * ROW_TILES + SUBLANES, LANES), F32),
        pltpu.VMEM((2, SUBLANES, tm), jnp.int32),
        pltpu.SMEM((2, SUBLANES, tm), jnp.int32),
        pltpu.VMEM((EXPERT_ROWS, ROW_TILES, LANES), F32),
        pltpu.VMEM((N_EXPERTS, LANES), jnp.int32),
        pltpu.SMEM((N_EXPERTS, LANES), jnp.int32),
        pltpu.SemaphoreType.DMA((2,)),
        pltpu.SemaphoreType.DMA((2,)),
        pltpu.SemaphoreType.DMA((2,)),
        pltpu.SemaphoreType.DMA(()),
    ]
    return pl.pallas_call(
        functools.partial(_mixer_kernel, n_pages=n_pages),
        grid=(bsz, n_s),
        in_specs=in_specs,
        out_specs=out_specs,
        out_shape=out_shape,
        scratch_shapes=scratch,
        compiler_params=pltpu.CompilerParams(
            dimension_semantics=("arbitrary", "arbitrary"), vmem_limit_bytes=VMEM_LIMIT),
        name="mixer",
    )(x, *weights)


def _experts_kernel(xpage_ref, spage_ref, be_ref, first_ref, nexte_ref, nused_ref,
                    slot_p_ref, slot_a_ref, slot_b_ref, xa_ref, xb_ref,
                    w1_hbm, b1_ref, w2_hbm, b2_ref, y_hbm,
                    yb0_ref, yb1_ref, w1s_ref, w2s_ref, w1b_ref, w2b_ref, ssem, wsem):
    del xpage_ref, spage_ref
    rows = EXPERT_ROWS
    tile = rows * ROW_TILES
    i = pl.program_id(0)
    last = pl.num_programs(0) - 1
    n_used = nused_ref[0]
    yb = (yb0_ref, yb1_ref)
    x_in = (xa_ref, xb_ref)
    dyn_zero = lax.shift_right_arithmetic(n_used, 31)
    n_chunks = D_FF // EXPERT_CHUNK
    assert SCATTER_STAGES <= n_chunks + 1

    def scatter(slots, r, p):
        return pltpu.make_async_copy(
            yb[p].at[pl.ds(r * ROW_TILES, ROW_TILES)], y_hbm.at[slots[0, 0, r]], ssem.at[p])

    def start_scatters(stage, slots, p):
        if stage is None:
            r_range = range(rows)
        elif stage < SCATTER_STAGES:
            r_range = range(rows * stage // SCATTER_STAGES, rows * (stage + 1) // SCATTER_STAGES)
        else:
            return jnp.zeros((1, LANES), F32)
        for r in r_range:
            scatter(slots, r, p).start(priority=r % DMA_PRIORITIES)
        spare = pl.multiple_of(tile + dyn_zero * SUBLANES, SUBLANES)
        yb[p][pl.ds(spare, SUBLANES), :] = jnp.full((SUBLANES, LANES), dyn_zero.astype(F32))
        return yb[p][pl.ds(spare, SUBLANES), :][0:1, :]

    def wait_scatters(p):
        view = yb[p].at[pl.ds(0, tile)]
        pltpu.make_async_copy(view, view, ssem.at[p]).wait()

    def weight_copies(e):
        return (pltpu.make_async_copy(w1_hbm.at[e], w1s_ref, wsem.at[0]),
                pltpu.make_async_copy(w2_hbm.at[e], w2s_ref, wsem.at[1]))

    def switch_weights(blk):
        @pl.when(first_ref[blk] == 1)
        def _():
            for cp in weight_copies(0):
                cp.wait()
            step = 128
            for c in range(D_MODEL // step):
                w1b_ref[c * step:(c + 1) * step, :] = w1s_ref[c * step:(c + 1) * step, :].astype(BF16)
            half = D_FF // 2
            step = 64
            for c in range(half // step):
                pair = pltpu.pack_elementwise(
                    [w2s_ref[c * step:(c + 1) * step, :], w2s_ref[half + c * step:half + (c + 1) * step, :]],
                    packed_dtype=BF16)
                w2b_ref[2 * c * step:2 * (c + 1) * step, :] = pltpu.bitcast(pair, BF16)
            nxt = nexte_ref[blk]

            @pl.when(nxt >= 0)
            def _():
                for cp in weight_copies(nxt):
                    cp.start()

    def compute(p, blk, start_stage):
        e = be_ref[blk]
        x = jnp.concatenate(
            [x_in[p][pl.ds(c, rows, stride=ROW_TILES), :] for c in range(ROW_TILES)], axis=1).astype(BF16)
        b1 = b1_ref[pl.ds(e, 1), :]
        width = EXPERT_CHUNK
        even = (lax.broadcasted_iota(jnp.int32, (rows, width), 1) & 1) == 0
        zs = []
        for c in range(n_chunks):
            lo = c * width
            hi = D_FF + lo
            zero = _tile_lanes(start_stage(c), width // LANES)
            ha = _dot(x, w1b_ref[:, lo:lo + width]) + (b1[:, lo:lo + width] + zero)
            hb = _dot(x, w1b_ref[:, hi:hi + width]) + (b1[:, hi:hi + width] + zero)
            glu = jnp.where(even, ha, pltpu.roll(hb, 1, 1))
            lin = jnp.where(even, pltpu.roll(ha, width - 1, 1), hb)
            glu = jnp.minimum(glu, SWIGLU_LIMIT)
            lin = jnp.clip(lin, -SWIGLU_LIMIT, SWIGLU_LIMIT) + 1.0
            zs.append((glu * _sigmoid(SWIGLU_ALPHA * glu) * lin).astype(BF16))
        b2 = b2_ref[pl.ds(e, 1), :] + _tile_lanes(start_stage(n_chunks), ROW_TILES)
        y = _dot(jnp.concatenate(zs, axis=1), w2b_ref[...]) + b2
        for c in range(ROW_TILES):
            yb[p][pl.ds(c, rows, stride=ROW_TILES), :] = y[:, c * LANES:(c + 1) * LANES]

    def phase(blk, p, start_stage):
        switch_weights(blk)

        @pl.when(blk < n_used)
        def _():
            compute(p, blk, start_stage)

        @pl.when(blk == n_used)
        def _():
            start_stage(None)

    @pl.when(i == 0)
    def _():
        yb1_ref[...] = jnp.zeros_like(yb1_ref)
        for cp in weight_copies(be_ref[0]):
            cp.start()


    @pl.when(jnp.logical_and(i >= 1, 2 * i - 1 <= n_used))
    def _():
        wait_scatters(0)

    phase(2 * i, 0, functools.partial(start_scatters, slots=slot_p_ref, p=1))

    @pl.when(2 * i <= n_used)
    def _():
        wait_scatters(1)

    phase(2 * i + 1, 1, functools.partial(start_scatters, slots=slot_a_ref, p=0))

    @pl.when(jnp.logical_and(i == last, 2 * i + 1 <= n_used))
    def _():
        wait_scatters(0)

    @pl.when(jnp.logical_and(i == last, 2 * i + 1 < n_used))
    def _():
        for r in range(rows):
            scatter(slot_b_ref, r, 1).start(priority=r % DMA_PRIORITIES)
        wait_scatters(1)


def _experts_call(plan, row_slot, x_pages_2d, w_e1, b_e1, w_e2, b_e2, n_tokens):
    xpage, spage, block_e, first, next_e, n_used = plan
    rows = EXPERT_ROWS
    n_blocks = block_e.shape[0]
    assert n_blocks % 2 == 0
    n_slots = TOP_K * n_tokens + rows
    tile_rows = rows * ROW_TILES
    filler = row_slot.shape[0] - 1
    smem_rows = lambda index: pl.BlockSpec((1, 1, rows), index, memory_space=pltpu.SMEM)
    whole = lambda shape: pl.BlockSpec(shape, lambda i, *_: (0,) * len(shape))
    grid_spec = pltpu.PrefetchScalarGridSpec(
        num_scalar_prefetch=6,
        grid=(n_blocks // 2,),
        in_specs=[
            smem_rows(lambda i, xp, sp, *_: (jnp.where(i == 0, filler, sp[jnp.maximum(2 * i - 1, 0)]), 0, 0)),
            smem_rows(lambda i, xp, sp, *_: (sp[2 * i], 0, 0)),
            smem_rows(lambda i, xp, sp, *_: (sp[2 * i + 1], 0, 0)),
            pl.BlockSpec((tile_rows, LANES), lambda i, xp, *_: (xp[2 * i], 0)),
            pl.BlockSpec((tile_rows, LANES), lambda i, xp, *_: (xp[2 * i + 1], 0)),
            pl.BlockSpec(memory_space=pl.ANY),
            whole((N_EXPERTS, 2 * D_FF)),
            pl.BlockSpec(memory_space=pl.ANY),
            whole((N_EXPERTS, D_MODEL)),
        ],
        out_specs=pl.BlockSpec(memory_space=pl.ANY),
        scratch_shapes=[
            pltpu.VMEM((tile_rows + SUBLANES, LANES), F32),
            pltpu.VMEM((tile_rows + SUBLANES, LANES), F32),
            pltpu.VMEM((D_MODEL, 2 * D_FF), F32),
            pltpu.VMEM((D_FF, D_MODEL), F32),
            pltpu.VMEM((D_MODEL, 2 * D_FF), BF16),
            pltpu.VMEM((D_FF, D_MODEL), BF16),
            pltpu.SemaphoreType.DMA((2,)),
            pltpu.SemaphoreType.DMA((2,)),
        ],
    )
    return pl.pallas_call(
        _experts_kernel,
        grid_spec=grid_spec,
        out_shape=jax.ShapeDtypeStruct((n_slots, ROW_TILES, LANES), F32),
        compiler_params=pltpu.CompilerParams(
            dimension_semantics=("arbitrary",), vmem_limit_bytes=VMEM_LIMIT),
        name="experts",
    )(xpage, spage, block_e, first, next_e, n_used,
      row_slot, row_slot, row_slot, x_pages_2d, x_pages_2d, w_e1, b_e1, w_e2, b_e2)


def _combine_kernel(h1_ref, y0_ref, y1_ref, y2_ref, y3_ref, gcol_ref, g_ref, b_ref, out_ref):
    tc = COMBINE_TILE
    gcol = gcol_ref[...]
    y_refs = (y0_ref, y1_ref, y2_ref, y3_ref)
    chunks = []
    for c in range(ROW_TILES):
        z = DEEPNORM_ALPHA * h1_ref[:, c * LANES:(c + 1) * LANES]
        for kk in range(TOP_K):
            z = z + gcol[:, kk:kk + 1] * y_refs[kk][pl.ds(c, tc, stride=ROW_TILES), :]
        chunks.append(z)
    total = chunks[0].sum(axis=1, keepdims=True)
    for z in chunks[1:]:
        total = total + z.sum(axis=1, keepdims=True)
    mu = total * (1.0 / D_MODEL)
    sq = None
    for z in chunks:
        zc = z - mu
        part = (zc * zc).sum(axis=1, keepdims=True)
        sq = part if sq is None else sq + part
    inv = lax.rsqrt(sq * (1.0 / D_MODEL) + LN_EPS)
    for c, z in enumerate(chunks):
        cs = slice(c * LANES, (c + 1) * LANES)
        out_ref[:, cs] = (z - mu) * inv * g_ref[:, cs] + b_ref[:, cs]


def _combine_call(h1_2d, y_2d, gcol, ln_g, ln_b, n_tokens):
    tc = COMBINE_TILE
    n_t = n_tokens // tc
    blk = tc * ROW_TILES
    y_spec = lambda kk: pl.BlockSpec((blk, LANES), lambda i: (kk * n_t + i, 0))
    return pl.pallas_call(
        _combine_kernel,
        grid=(n_t,),
        in_specs=[
            pl.BlockSpec((tc, D_MODEL), lambda i: (i, 0)),
            y_spec(0), y_spec(1), y_spec(2), y_spec(3),
            pl.BlockSpec((tc, LANES), lambda i: (i, 0)),
            pl.BlockSpec((1, D_MODEL), lambda i: (0, 0)),
            pl.BlockSpec((1, D_MODEL), lambda i: (0, 0)),
        ],
        out_specs=pl.BlockSpec((tc, D_MODEL), lambda i: (i, 0)),
        out_shape=jax.ShapeDtypeStruct((n_tokens, D_MODEL), F32),
        compiler_params=pltpu.CompilerParams(
            dimension_semantics=("arbitrary",), vmem_limit_bytes=VMEM_LIMIT),
        name="combine",
    )(h1_2d, y_2d, y_2d, y_2d, y_2d, gcol, ln_g, ln_b)


def _prepare_mixer_weights(ln0_g, ln0_b, w_in, conv_w, conv_b, w_q, w_k, b_if, mh_gain, w_pool,
                           b_pool, ls_pool, w_branch_pool, w_branch_mlstm, w_out, ln1_g, ln1_b,
                           w_router, b_router):
    w_main = w_in[:, :COL_IF_END].astype(BF16)
    w_gates = w_in[:, COL_GATES:COL_GATES + 2 * D_MODEL].astype(BF16)
    bif = jnp.concatenate([b_if, jnp.zeros((LANES - 2 * N_HEADS,), F32)]).reshape(1, LANES)
    wqk = jnp.concatenate([w_q * (HEAD_DIM ** -0.5), w_k], axis=-1).astype(BF16)
    wpool_bd = jax.scipy.linalg.block_diag(*[w_pool[g] for g in range(len(POOL_WINDOWS))]).astype(BF16)
    wr_hi = w_router.astype(BF16)
    wr_lo = (w_router - wr_hi.astype(F32)).astype(BF16)
    lane_pad = jnp.zeros((D_MODEL, LANES - N_EXPERTS), BF16)
    wr_hi_lo = jnp.concatenate([wr_hi, lane_pad, wr_lo, lane_pad], axis=1)
    row = lambda v: v.reshape(1, -1)
    return (row(ln0_g), row(ln0_b), w_main, w_gates, bif, conv_w, row(conv_b), wqk, row(mh_gain), wpool_bd,
            row(b_pool), row(ls_pool), w_branch_pool.astype(BF16), w_branch_mlstm.astype(BF16),
            w_out.astype(BF16), row(ln1_g), row(ln1_b), wr_hi_lo,
            b_router.reshape(N_EXPERTS, 1))


def _plan(page_expert1, counts, pos, n_tokens, n_pages, n_rows_total):
    rows = EXPERT_ROWS
    n_assign = TOP_K * n_tokens
    i32 = jnp.int32
    slots = jnp.arange(n_assign, dtype=i32)
    filler = n_rows_total // rows
    hit = jnp.zeros(((filler + 1) * rows,), i32).at[pos.reshape(-1)].add(slots + 1)
    pad_slot = n_assign + jnp.arange((filler + 1) * rows, dtype=i32) % rows
    row_slot = jnp.where(hit == 0, pad_slot, hit - 1).reshape(filler + 1, 1, rows)

    pidx = jnp.arange(n_pages, dtype=i32)
    used = page_expert1 > 0
    n_used = jnp.sum(used.astype(i32))
    page_e = page_expert1 - 1
    key = jnp.where(used, page_e, N_EXPERTS) * n_pages + pidx
    place = jnp.sum((key[None, :] < key[:, None]).astype(i32), axis=1)
    at_block = place[None, :] == pidx[:, None]
    order = jnp.sum(jnp.where(at_block, pidx[None, :], 0), axis=1)
    block_e = jnp.clip(jnp.sum(jnp.where(at_block, page_e[None, :], 0), axis=1), 0, N_EXPERTS - 1)
    valid = pidx < n_used
    xpage = jnp.where(valid, order, 0)
    spage = jnp.where(valid, order, filler)
    prev_e = jnp.concatenate([jnp.full((1,), -1, i32), block_e[:-1]])
    first = jnp.logical_and(valid, block_e != prev_e).astype(i32)
    e_iota = jnp.arange(N_EXPERTS, dtype=i32)
    pages_per_e = (counts + rows - 1) // rows
    seg_end = jnp.cumsum(pages_per_e)
    mine = block_e[:, None] == e_iota[None, :]
    end_blk = jnp.sum(jnp.where(mine, seg_end[None, :], 0), axis=1)
    e_at_end = jnp.sum(jnp.where(pidx[None, :] == end_blk[:, None], block_e[None, :], 0), axis=1)
    next_e = jnp.where(end_blk < n_used, e_at_end, -1).astype(i32)
    plan = (xpage.astype(i32), spage.astype(i32), block_e.astype(i32), first, next_e,
            n_used.reshape(1).astype(i32))
    return plan, row_slot


def kernel(x, ln0_g, ln0_b, w_in, conv_w, conv_b, w_q, w_k, b_if, mh_gain, w_pool, b_pool, ls_pool,
           w_branch_pool, w_branch_mlstm, w_out, ln1_g, ln1_b, w_router, b_router, w_e1, b_e1,
           w_e2, b_e2, ln2_g, ln2_b):
    bsz, seq, _ = x.shape
    n_tokens = bsz * seq
    rows = EXPERT_ROWS
    assert w_in.shape[0] == 1, "single-layer trunk"
    assert seq % TOKEN_TILE == 0 and n_tokens % COMBINE_TILE == 0
    n_pages = (TOP_K * n_tokens + N_EXPERTS * (rows - 1) + rows - 1) // rows
    n_pages += n_pages % 2
    weights = _prepare_mixer_weights(
        ln0_g, ln0_b, w_in[0], conv_w[0], conv_b[0], w_q[0], w_k[0], b_if[0], mh_gain[0], w_pool[0],
        b_pool[0], ls_pool[0], w_branch_pool[0], w_branch_mlstm[0], w_out[0], ln1_g[0], ln1_b[0],
        w_router[0], b_router[0])
    _, x_pages, pos8, gcol, cnt, pe, h1_2d = _mixer_call(x, weights, n_pages)
    counts = cnt[:, 0].astype(jnp.int32)
    page_expert1 = pe[0, :n_pages].astype(jnp.int32)
    plan, row_slot = _plan(page_expert1, counts, pos8[:TOP_K], n_tokens, n_pages, x_pages.shape[0])
    y_slots = _experts_call(
        plan, row_slot, x_pages.reshape(-1, LANES), w_e1[0], b_e1[0], w_e2[0], b_e2[0], n_tokens)
    out = _combine_call(h1_2d, y_slots.reshape(-1, LANES), gcol, ln2_g[0].reshape(1, D_MODEL),
                        ln2_b[0].reshape(1, D_MODEL), n_tokens)
    return out.reshape(bsz, seq, D_MODEL)
```

```python
import functools

import jax
import jax.numpy as jnp
from jax import lax
from jax.experimental import pallas as pl
from jax.experimental.pallas import tpu as pltpu

F32 = jnp.float32
BF16 = jnp.bfloat16

D_MODEL = 1024
N_HEADS = 4
HEAD_DIM = 256
POOL_WIDTH = 512
POOL_GROUP = 128
POOL_WINDOWS = (2, 4, 8, 16)
CONV_WIDTH = 4
N_EXPERTS = 32
TOP_K = 4
D_FF = 1024
SWIGLU_ALPHA = 1.702
SWIGLU_LIMIT = 7.0
LN_EPS = 1e-5
DEEPNORM_ALPHA = 2.0 ** 0.25

SUBLANES = 8
LANES = 128
ROW_TILES = D_MODEL // LANES

COL_P = 0
COL_U = COL_P + POOL_WIDTH
COL_V = COL_U + D_MODEL
COL_O = COL_V + D_MODEL
COL_IF = COL_O + D_MODEL
COL_IF_END = COL_IF + LANES
COL_GATES = COL_IF + 2 * N_HEADS

TOKEN_TILE = 256
POOL_HALO = 24
CONV_HALO = 8
EXPERT_ROWS = 256
ROW_DMA_GROUPS = 12
SCATTER_STAGES = 4
EXPERT_CHUNK = 2 * LANES
COMBINE_TILE = 512
DMA_PRIORITIES = 2
VMEM_LIMIT = 56 * 1024 * 1024

assert TOKEN_TILE <= EXPERT_ROWS
assert (TOP_K * TOKEN_TILE) % EXPERT_ROWS == 0


def _layer_norm(x, g, b):
    mu = jnp.mean(x, axis=-1, keepdims=True)
    xc = x - mu
    var = jnp.mean(xc * xc, axis=-1, keepdims=True)
    return xc * lax.rsqrt(var + LN_EPS) * g + b


def _sigmoid(x):
    return 0.5 * jnp.tanh(0.5 * x) + 0.5


def _log_sigmoid(x):
    return -(jnp.maximum(-x, 0.0) + jnp.log(1.0 + jnp.exp(-jnp.abs(x))))


def _split3(x):
    hi = x.astype(BF16)
    r1 = x - hi.astype(F32)
    mid = r1.astype(BF16)
    lo = (r1 - mid.astype(F32)).astype(BF16)
    return hi, mid, lo


def _dot(a, b):
    return jnp.dot(a, b, preferred_element_type=F32)


def _dot_nt(a, b):
    return lax.dot_general(a, b, (((1,), (1,)), ((), ())), preferred_element_type=F32)


def _tile_lanes(row, n):
    return jnp.concatenate([row] * n, axis=1)


def _mixer_kernel(x_ref, ln0g_ref, ln0b_ref, win_ref, wgate_ref, bif_ref, convw_ref, convb_ref, wqk_ref,
                  gain_ref, wpool_ref, bpool_ref, lspool_ref, wbp_ref, wbm_ref, wout_ref,
                  ln1g_ref, ln1b_ref, wr_ref, br_ref,
                  xp_hbm, pos_ref, gcol_ref, cnt_ref, pe_ref, h1d_ref,
                  pext_ref, pw_ref, uext_ref, ct_ref, n_ref, m_ref, run_ref, page_ref, free_ref,
                  hx_ref, posv_ref, poss_ref, zb_ref, stv_ref, sts_ref, psem, rsem, zsem,
                  *, n_pages):
    tm = TOKEN_TILE
    tile_rows = tm * ROW_TILES
    b = pl.program_id(0)
    s = pl.program_id(1)
    n_s = pl.num_programs(1)
    g = b * n_s + s
    last = pl.num_programs(0) * n_s - 1
    par = g % 2
    q = 1 - par
    dyn_zero = lax.shift_right_arithmetic(g, 31)

    def hx_tile(p):
        return hx_ref.at[p, pl.ds(0, tile_rows)]

    def row_copy(p, t, k):
        return pltpu.make_async_copy(
            hx_ref.at[p, pl.ds(t * ROW_TILES, ROW_TILES)], xp_hbm.at[poss_ref[p, k, t]], rsem.at[p])

    def wait_row_copies(p):
        for _ in range(TOP_K):
            pltpu.make_async_copy(hx_tile(p), hx_tile(p), rsem.at[p]).wait()

    def pos_to_smem(p):
        return pltpu.make_async_copy(posv_ref.at[p], poss_ref.at[p], psem.at[p])

    @pl.when(s == 0)
    def _():
        pext_ref[0:POOL_HALO, :] = jnp.zeros((POOL_HALO, POOL_WIDTH), F32)
        uext_ref[0:CONV_HALO, :] = jnp.zeros((CONV_HALO, D_MODEL), F32)
        ct_ref[...] = jnp.zeros_like(ct_ref)
        n_ref[...] = jnp.zeros_like(n_ref)
        m_ref[...] = jnp.zeros_like(m_ref)
        pw_ref[:, 0:SUBLANES, :] = jnp.zeros((2, SUBLANES, POOL_GROUP), F32)

    @pl.when(g == 0)
    def _():
        run_ref[...] = jnp.zeros_like(run_ref)
        page_ref[...] = jnp.zeros_like(page_ref)
        free_ref[...] = jnp.zeros_like(free_ref)
        pe_ref[...] = jnp.zeros_like(pe_ref)
        zb_ref[...] = jnp.zeros_like(zb_ref)
        hx_ref[1] = jnp.zeros((tile_rows + SUBLANES, LANES), F32)
        for k in range(TOP_K):
            for t in range(tm):
                poss_ref[1, k, t] = n_pages * EXPERT_ROWS + k * tm + t

    @pl.when(g >= 1)
    def _():
        pos_to_smem(q).wait()
        wait_row_copies(par)

    def start_row_copies(grp):
        for t in range(tm * grp // ROW_DMA_GROUPS, tm * (grp + 1) // ROW_DMA_GROUPS):
            for k in range(TOP_K):
                row_copy(q, t, k).start(priority=k % DMA_PRIORITIES)
        spare = pl.multiple_of(tile_rows + dyn_zero * SUBLANES, SUBLANES)
        hx_ref[q, pl.ds(spare, SUBLANES), :] = jnp.full((SUBLANES, LANES), dyn_zero.astype(F32))
        return hx_ref[q, pl.ds(spare, SUBLANES), :][0:1, :]

    ln0b = ln0b_ref[...] + _tile_lanes(start_row_copies(0), ROW_TILES)
    h0 = _layer_norm(x_ref[...], ln0g_ref[...], ln0b)
    h0b = h0.astype(BF16)

    def proj(lo, hi):
        return _dot(h0b, win_ref[:, lo:hi])

    pext_ref[POOL_HALO:POOL_HALO + tm, :] = proj(COL_P, COL_U)
    tpos = s * tm + lax.broadcasted_iota(jnp.int32, (tm, 1), 0)
    groups = []
    end = POOL_HALO + tm
    for gi, w in enumerate(POOL_WINDOWS):
        cols = slice(gi * POOL_GROUP, (gi + 1) * POOL_GROUP)
        cur = pext_ref[POOL_HALO:end, cols]
        src, src_cols, k, nxt = pext_ref, cols, 1, 0
        while 2 * k < w:
            pw_ref[nxt, SUBLANES:end, :] = (src[SUBLANES:end, src_cols]
                                            + src[SUBLANES - k:end - k, src_cols])
            src, src_cols, k, nxt = pw_ref.at[nxt], slice(None), 2 * k, 1 - nxt
        acc = src[POOL_HALO:end, src_cols] + src[POOL_HALO - k:end - k, src_cols]
        inv_cnt = 1.0 / jnp.minimum(tpos + 1, w).astype(F32)
        groups.append(acc * inv_cnt - cur)
    pooled = jnp.concatenate(groups, axis=1)
    pext_ref[0:POOL_HALO, :] = pext_ref[tm:tm + POOL_HALO, :]
    bpool = bpool_ref[...] + _tile_lanes(start_row_copies(1), POOL_WIDTH // LANES)
    lspool = lspool_ref[...] + _tile_lanes(start_row_copies(8), POOL_WIDTH // LANES)
    mixed = (_dot(pooled.astype(BF16), wpool_ref[...]) + bpool) * lspool
    y_pool = _dot(mixed.astype(BF16), wbp_ref[...])

    uext_ref[CONV_HALO:CONV_HALO + tm, :] = proj(COL_U, COL_V)
    conv = convb_ref[...] + _tile_lanes(start_row_copies(2), ROW_TILES)
    for j in range(CONV_WIDTH):
        off = CONV_HALO - (CONV_WIDTH - 1) + j
        conv = conv + convw_ref[j:j + 1, :] * uext_ref[off:off + tm, :]
    uext_ref[0:CONV_HALO, :] = uext_ref[tm:tm + CONV_HALO, :]
    ucb = (conv * _sigmoid(conv)).astype(BF16)
    vb = proj(COL_V, COL_O).astype(BF16)

    slab = proj(COL_IF, COL_IF_END) + (bif_ref[...] + start_row_copies(3))
    lane = lax.broadcasted_iota(jnp.int32, (tm, LANES), 1)
    is_f = jnp.logical_and(lane >= N_HEADS, lane < 2 * N_HEADS)
    slab = jnp.where(is_f, _log_sigmoid(slab), slab)
    row_i = lax.broadcasted_iota(jnp.int32, (tm, tm), 0)
    col_i = lax.broadcasted_iota(jnp.int32, (tm, tm), 1)
    causal = row_i >= col_i
    tri = jnp.where(causal, 1.0, 0.0).astype(BF16)
    hi, mid, lo = _split3(slab)
    bcol = _dot(tri, hi) + _dot(tri, mid) + _dot(tri, lo)
    slab_t = slab.T
    bcol_t = bcol.T

    heads = []
    for h in range(N_HEADS):
        hs = slice(h * HEAD_DIM, (h + 1) * HEAD_DIM)
        qk = _dot(ucb[:, hs], wqk_ref[h])
        q_h = qk[:, :HEAD_DIM]
        k_h = qk[:, HEAD_DIM:]
        qb = q_h.astype(BF16)
        kb = k_h.astype(BF16)
        vh = vb[:, hs]

        i_c = slab[:, h:h + 1]
        b_c = bcol[:, N_HEADS + h:N_HEADS + h + 1]
        i_r = slab_t[h:h + 1, :]
        b_r = bcol_t[N_HEADS + h:N_HEADS + h + 1, :]
        m_prev = m_ref[:, h:h + 1] + start_row_copies(4 + h)[:, 0:1]

        d_log = jnp.where(causal, b_c - (b_r - i_r), -jnp.inf)
        m_inter = b_c + m_prev
        m_t = jnp.maximum(m_inter, jnp.max(d_log, axis=1, keepdims=True))
        w_intra = jnp.exp(d_log - m_t)
        sc = _dot_nt(qb, kb) * w_intra
        w_inter = jnp.exp(m_inter - m_t)
        ctb = ct_ref[h].astype(BF16)
        num = _dot(sc.astype(BF16), vh) + w_inter * _dot(qb, ctb)
        qn = jnp.sum(q_h * n_ref[h], axis=1, keepdims=True)
        den = jnp.sum(sc, axis=1, keepdims=True) + w_inter * qn
        hh = num * (1.0 / jnp.maximum(jnp.abs(den), jnp.exp(-m_t)))
        mu = jnp.mean(hh, axis=1, keepdims=True)
        hc = hh - mu
        var = jnp.mean(hc * hc, axis=1, keepdims=True)
        heads.append(hc * lax.rsqrt(var + LN_EPS))

        g_last = b_r[:, tm - 1:tm]
        m_new = jnp.maximum(g_last + m_prev, jnp.max(g_last - b_r + i_r, axis=1, keepdims=True))
        decay = jnp.exp(g_last + m_prev - m_new)
        w_state = jnp.exp(g_last - b_c + i_c - m_new)
        kw = k_h * w_state
        ct_ref[h] = decay * ct_ref[h] + _dot(kw.T.astype(BF16), vh)
        n_ref[h] = decay * n_ref[h] + jnp.sum(kw, axis=0, keepdims=True)
        m_ref[:, h:h + 1] = m_new

    hn = jnp.concatenate(heads, axis=1) * (gain_ref[...] + _tile_lanes(start_row_copies(9), ROW_TILES))
    h_out = _sigmoid(proj(COL_O, COL_IF)) * hn
    y_mlstm = _dot(h_out.astype(BF16), wbm_ref[...])

    merged = (_sigmoid(_dot(h0b, wgate_ref[:, 0:D_MODEL])) * y_pool
              + _sigmoid(_dot(h0b, wgate_ref[:, D_MODEL:2 * D_MODEL])) * y_mlstm)
    mix = _dot(merged.astype(BF16), wout_ref[...])
    ln1g = ln1g_ref[...] + _tile_lanes(start_row_copies(10), ROW_TILES)
    ln1b = ln1b_ref[...] + _tile_lanes(start_row_copies(11), ROW_TILES)
    h1 = _layer_norm(DEEPNORM_ALPHA * h0 + mix, ln1g, ln1b)
    for c in range(ROW_TILES):
        hx_ref[par, pl.ds(c, tm, stride=ROW_TILES), :] = h1[:, c * LANES:(c + 1) * LANES]
    h1d_ref[...] = h1

    h1_hi = h1.astype(BF16)
    h1_lo = (h1 - h1_hi.astype(F32)).astype(BF16)
    la = _dot(h1_hi, wr_ref[...])
    lb = _dot(h1_lo, wr_ref[:, 0:LANES])
    logits = (la[:, 0:LANES] + la[:, LANES:2 * LANES] + lb).T[0:N_EXPERTS] + br_ref[...]
    e_iota = lax.broadcasted_iota(jnp.int32, (N_EXPERTS, tm), 0)
    vals, onehots = [], []
    lg = logits
    for _ in range(TOP_K):
        mx = jnp.max(lg, axis=0, keepdims=True)
        sel = jnp.min(jnp.where(lg == mx, e_iota, N_EXPERTS), axis=0, keepdims=True)
        oh = e_iota == sel
        lg = jnp.where(oh, -jnp.inf, lg)
        vals.append(mx)
        onehots.append(oh)
    exps = [jnp.exp(v - vals[0]) for v in vals]
    inv_den = 1.0 / (exps[0] + exps[1] + exps[2] + exps[3])
    gates = [e * inv_den for e in exps]

    oh_all = jnp.where(onehots[0], 1.0, 0.0)
    for oh in onehots[1:]:
        oh_all = oh_all + jnp.where(oh, 1.0, 0.0)
    strict = jnp.where(row_i < col_i, 1.0, 0.0).astype(BF16)
    run = run_ref[:, 0:1]
    rank = _dot(oh_all.astype(BF16), strict) + run
    count = jnp.sum(oh_all, axis=1, keepdims=True)
    inv_rows = 1.0 / EXPERT_ROWS
    pages_before = jnp.ceil(run * inv_rows)
    need = jnp.ceil((run + count) * inv_rows) - pages_before
    er = lax.broadcasted_iota(jnp.int32, (N_EXPERTS, N_EXPERTS), 0)
    ec = lax.broadcasted_iota(jnp.int32, (N_EXPERTS, N_EXPERTS), 1)
    earlier = jnp.where(er > ec, 1.0, 0.0).astype(BF16)
    need_b = jnp.broadcast_to(need, (N_EXPERTS, LANES)).astype(BF16)
    new_page = free_ref[0:1, 0:1] + _dot(earlier, need_b)[:, 0:1]
    page_seq = jnp.floor(rank * inv_rows)
    page = jnp.where(page_seq < pages_before, page_ref[:, 0:1], new_page)
    pos_all = page * EXPERT_ROWS + (rank - page_seq * EXPERT_ROWS)
    positions = [jnp.sum(jnp.where(oh, pos_all, 0.0), axis=0, keepdims=True) for oh in onehots]
    page_ref[...] = jnp.where(need > 0.0, new_page, page_ref[...])
    free_ref[...] = free_ref[...] + jnp.sum(need, axis=0, keepdims=True)
    run_ref[...] = run_ref[...] + count
    cnt_ref[...] = run_ref[...]
    p_lane = lax.broadcasted_iota(jnp.int32, (N_EXPERTS, pe_ref.shape[1]), 1).astype(F32)
    e_plus1 = (lax.broadcasted_iota(jnp.int32, (N_EXPERTS, 1), 0) + 1).astype(F32)
    taken = jnp.logical_and(p_lane == new_page, need > 0.0)
    pe_ref[...] = pe_ref[...] + jnp.sum(jnp.where(taken, e_plus1, 0.0), axis=0, keepdims=True)

    r8 = lax.broadcasted_iota(jnp.int32, (SUBLANES, tm), 0)
    pos_out = jnp.zeros((SUBLANES, tm), jnp.int32)
    r128 = lax.broadcasted_iota(jnp.int32, (LANES, tm), 0)
    gate_rows = jnp.zeros((LANES, tm), F32)
    for kk in range(TOP_K):
        pos_out = jnp.where(r8 == kk, positions[kk].astype(jnp.int32), pos_out)
        gate_rows = jnp.where(r128 == kk, gates[kk], gate_rows)
    pos_ref[...] = pos_out
    gcol_ref[...] = gate_rows.T
    posv_ref[par] = pos_out
    pos_to_smem(par).start()

    @pl.when(g == last)
    def _():
        pos_to_smem(par).wait()
        for t in range(tm):
            for k in range(TOP_K):
                row_copy(par, t, k).start(priority=k % DMA_PRIORITIES)
        wait_row_copies(q)
        wait_row_copies(par)

        st_lane = lax.broadcasted_iota(jnp.int32, (N_EXPERTS, LANES), 1)
        state = jnp.where(st_lane == 0, page_ref[...], jnp.where(st_lane == 1, run_ref[...], free_ref[...]))
        stv_ref[...] = state.astype(jnp.int32)
        state_copy = pltpu.make_async_copy(stv_ref, sts_ref, psem.at[par])
        state_copy.start()
        state_copy.wait()

        def zero_fill(wait):
            def act(rows_dst, n):
                cp = pltpu.make_async_copy(zb_ref.at[pl.ds(0, n)], xp_hbm.at[pl.ds(rows_dst, n)], zsem)
                if wait:
                    cp.wait()
                else:
                    cp.start()

            def unused_page(p, carry):
                act(pl.multiple_of(p * EXPERT_ROWS, EXPERT_ROWS), EXPERT_ROWS)
                return carry

            lax.fori_loop(sts_ref[0, 2], n_pages, unused_page, 0)

            def page_tail(e, carry):
                filled = sts_ref[e, 1] & (EXPERT_ROWS - 1)
                pad = jnp.where(filled == 0, 0, EXPERT_ROWS - filled)
                dst = sts_ref[e, 0] * EXPERT_ROWS + filled
                size = 1
                while size < EXPERT_ROWS:
                    @pl.when((pad & size) != 0)
                    def _(dst=dst, size=size):
                        act(dst, size)
                    dst = dst + (pad & size)
                    size *= 2
                return carry

            lax.fori_loop(0, N_EXPERTS, page_tail, 0)

        zero_fill(wait=False)
        zero_fill(wait=True)


def _const_spec(shape):
    zeros = (0,) * len(shape)
    return pl.BlockSpec(shape, lambda b, s: zeros, pipeline_mode=pl.Buffered(1))


def _mixer_call(x, weights, n_pages):
    bsz, seq, _ = x.shape
    tm = TOKEN_TILE
    n_s = seq // tm
    t_total = bsz * seq
    assert bsz * n_s >= 2
    tile = lambda b, s: (b * n_s + s)
    spare_pages = TOP_K * tm // EXPERT_ROWS
    pe_lanes = -(-n_pages // LANES) * LANES
    in_specs = [pl.BlockSpec((None, tm, D_MODEL), lambda b, s: (b, s, 0))]
    in_specs += [_const_spec(w.shape) for w in weights]
    out_shape = (
        jax.ShapeDtypeStruct(((n_pages + spare_pages) * EXPERT_ROWS, ROW_TILES, LANES), F32),
        jax.ShapeDtypeStruct((SUBLANES, t_total), jnp.int32),
        jax.ShapeDtypeStruct((t_total, LANES), F32),
        jax.ShapeDtypeStruct((N_EXPERTS, LANES), F32),
        jax.ShapeDtypeStruct((SUBLANES, pe_lanes), F32),
        jax.ShapeDtypeStruct((t_total, D_MODEL), F32),
    )
    out_specs = (
        pl.BlockSpec(memory_space=pl.ANY),
        pl.BlockSpec((SUBLANES, tm), lambda b, s: (0, tile(b, s))),
        pl.BlockSpec((tm, LANES), lambda b, s: (tile(b, s), 0)),
        pl.BlockSpec((N_EXPERTS, LANES), lambda b, s: (0, 0)),
        pl.BlockSpec((SUBLANES, pe_lanes), lambda b, s: (0, 0)),
        pl.BlockSpec((tm, D_MODEL), lambda b, s: (tile(b, s), 0)),
    )
    scratch = [
        pltpu.VMEM((POOL_HALO + tm, POOL_WIDTH), F32),
        pltpu.VMEM((2, POOL_HALO + tm, POOL_GROUP), F32),
        pltpu.VMEM((CONV_HALO + tm, D_MODEL), F32),
        pltpu.VMEM((N_HEADS, HEAD_DIM, HEAD_DIM), F32),
        pltpu.VMEM((N_HEADS, 1, HEAD_DIM), F32),
        pltpu.VMEM((1, LANES), F32),
        pltpu.VMEM((N_EXPERTS, LANES), F32),
        pltpu.VMEM((N_EXPERTS, LANES), F32),
        pltpu.VMEM((1, LANES), F32),
        pltpu.VMEM((2, tm * ROW_TILES + SUBLANES, LANES), F32),
        pltpu.VMEM((2, SUBLANES, tm), jnp.int32),
        pltpu.SMEM((2, SUBLANES, tm), jnp.int32),
        pltpu.VMEM((EXPERT_ROWS, ROW_TILES, LANES), F32),
        pltpu.VMEM((N_EXPERTS, LANES), jnp.int32),
        pltpu.SMEM((N_EXPERTS, LANES), jnp.int32),
        pltpu.SemaphoreType.DMA((2,)),
        pltpu.SemaphoreType.DMA((2,)),
        pltpu.SemaphoreType.DMA(()),
    ]
    return pl.pallas_call(
        functools.partial(_mixer_kernel, n_pages=n_pages),
        grid=(bsz, n_s),
        in_specs=in_specs,
        out_specs=out_specs,
        out_shape=out_shape,
        scratch_shapes=scratch,
        compiler_params=pltpu.CompilerParams(
            dimension_semantics=("arbitrary", "arbitrary"), vmem_limit_bytes=VMEM_LIMIT),
        name="mixer",
    )(x, *weights)


def _experts_kernel(xpage_ref, spage_ref, be_ref, first_ref, nexte_ref, nused_ref,
                    slot_p_ref, slot_a_ref, slot_b_ref, xa_ref, xb_ref,
                    w1_hbm, b1_ref, w2_hbm, b2_ref, y_hbm,
                    yb0_ref, yb1_ref, w1s_ref, w2s_ref, w1b_ref, w2b_ref, ssem, wsem):
    del xpage_ref, spage_ref
    rows = EXPERT_ROWS
    tile = rows * ROW_TILES
    i = pl.program_id(0)
    last = pl.num_programs(0) - 1
    n_used = nused_ref[0]
    yb = (yb0_ref, yb1_ref)
    x_in = (xa_ref, xb_ref)
    dyn_zero = lax.shift_right_arithmetic(n_used, 31)
    n_chunks = D_FF // EXPERT_CHUNK
    assert SCATTER_STAGES <= n_chunks + 1

    def scatter(slots, r, p):
        return pltpu.make_async_copy(
            yb[p].at[pl.ds(r * ROW_TILES, ROW_TILES)], y_hbm.at[slots[0, 0, r]], ssem.at[p])

    def start_scatters(stage, slots, p):
        if stage is None:
            r_range = range(rows)
        elif stage < SCATTER_STAGES:
            r_range = range(rows * stage // SCATTER_STAGES, rows * (stage + 1) // SCATTER_STAGES)
        else:
            return jnp.zeros((1, LANES), F32)
        for r in r_range:
            scatter(slots, r, p).start(priority=r % DMA_PRIORITIES)
        spare = pl.multiple_of(tile + dyn_zero * SUBLANES, SUBLANES)
        yb[p][pl.ds(spare, SUBLANES), :] = jnp.full((SUBLANES, LANES), dyn_zero.astype(F32))
        return yb[p][pl.ds(spare, SUBLANES), :][0:1, :]

    def wait_scatters(p):
        view = yb[p].at[pl.ds(0, tile)]
        pltpu.make_async_copy(view, view, ssem.at[p]).wait()

    def weight_copies(e):
        return (pltpu.make_async_copy(w1_hbm.at[e], w1s_ref, wsem.at[0]),
                pltpu.make_async_copy(w2_hbm.at[e], w2s_ref, wsem.at[1]))

    def switch_weights(blk):
        @pl.when(first_ref[blk] == 1)
        def _():
            for cp in weight_copies(0):
                cp.wait()
            step = 128
            for c in range(D_MODEL // step):
                w1b_ref[c * step:(c + 1) * step, :] = w1s_ref[c * step:(c + 1) * step, :].astype(BF16)
            half = D_FF // 2
            step = 64
            for c in range(half // step):
                pair = pltpu.pack_elementwise(
                    [w2s_ref[c * step:(c + 1) * step, :], w2s_ref[half + c * step:half + (c + 1) * step, :]],
                    packed_dtype=BF16)
                w2b_ref[2 * c * step:2 * (c + 1) * step, :] = pltpu.bitcast(pair, BF16)
            nxt = nexte_ref[blk]

            @pl.when(nxt >= 0)
            def _():
                for cp in weight_copies(nxt):
                    cp.start()

    def compute(p, blk, start_stage):
        e = be_ref[blk]
        x = jnp.concatenate(
            [x_in[p][pl.ds(c, rows, stride=ROW_TILES), :] for c in range(ROW_TILES)], axis=1).astype(BF16)
        b1 = b1_ref[pl.ds(e, 1), :]
        width = EXPERT_CHUNK
        even = (lax.broadcasted_iota(jnp.int32, (rows, width), 1) & 1) == 0
        zs = []
        for c in range(n_chunks):
            lo = c * width
            hi = D_FF + lo
            zero = _tile_lanes(start_stage(c), width // LANES)
            ha = _dot(x, w1b_ref[:, lo:lo + width]) + (b1[:, lo:lo + width] + zero)
            hb = _dot(x, w1b_ref[:, hi:hi + width]) + (b1[:, hi:hi + width] + zero)
            glu = jnp.where(even, ha, pltpu.roll(hb, 1, 1))
            lin = jnp.where(even, pltpu.roll(ha, width - 1, 1), hb)
            glu = jnp.minimum(glu, SWIGLU_LIMIT)
            lin = jnp.clip(lin, -SWIGLU_LIMIT, SWIGLU_LIMIT) + 1.0
            zs.append((glu * _sigmoid(SWIGLU_ALPHA * glu) * lin).astype(BF16))
        b2 = b2_ref[pl.ds(e, 1), :] + _tile_lanes(start_stage(n_chunks), ROW_TILES)
        y = _dot(jnp.concatenate(zs, axis=1), w2b_ref[...]) + b2
        for c in range(ROW_TILES):
            yb[p][pl.ds(c, rows, stride=ROW_TILES), :] = y[:, c * LANES:(c + 1) * LANES]

    def phase(blk, p, start_stage):
        switch_weights(blk)

        @pl.when(blk < n_used)
        def _():
            compute(p, blk, start_stage)

        @pl.when(blk == n_used)
        def _():
            start_stage(None)

    @pl.when(i == 0)
    def _():
        yb1_ref[...] = jnp.zeros_like(yb1_ref)
        for cp in weight_copies(be_ref[0]):
            cp.start()


    @pl.when(jnp.logical_and(i >= 1, 2 * i - 1 <= n_used))
    def _():
        wait_scatters(0)

    phase(2 * i, 0, functools.partial(start_scatters, slots=slot_p_ref, p=1))

    @pl.when(2 * i <= n_used)
    def _():
        wait_scatters(1)

    phase(2 * i + 1, 1, functools.partial(start_scatters, slots=slot_a_ref, p=0))

    @pl.when(jnp.logical_and(i == last, 2 * i + 1 <= n_used))
    def _():
        wait_scatters(0)

    @pl.when(jnp.logical_and(i == last, 2 * i + 1 < n_used))
    def _():
        for r in range(rows):
            scatter(slot_b_ref, r, 1).start(priority=r % DMA_PRIORITIES)
        wait_scatters(1)


def _experts_call(plan, row_slot, x_pages_2d, w_e1, b_e1, w_e2, b_e2, n_tokens):
    xpage, spage, block_e, first, next_e, n_used = plan
    rows = EXPERT_ROWS
    n_blocks = block_e.shape[0]
    assert n_blocks % 2 == 0
    n_slots = TOP_K * n_tokens + rows
    tile_rows = rows * ROW_TILES
    filler = row_slot.shape[0] - 1
    smem_rows = lambda index: pl.BlockSpec((1, 1, rows), index, memory_space=pltpu.SMEM)
    whole = lambda shape: pl.BlockSpec(shape, lambda i, *_: (0,) * len(shape))
    grid_spec = pltpu.PrefetchScalarGridSpec(
        num_scalar_prefetch=6,
        grid=(n_blocks // 2,),
        in_specs=[
            smem_rows(lambda i, xp, sp, *_: (jnp.where(i == 0, filler, sp[jnp.maximum(2 * i - 1, 0)]), 0, 0)),
            smem_rows(lambda i, xp, sp, *_: (sp[2 * i], 0, 0)),
            smem_rows(lambda i, xp, sp, *_: (sp[2 * i + 1], 0, 0)),
            pl.BlockSpec((tile_rows, LANES), lambda i, xp, *_: (xp[2 * i], 0)),
            pl.BlockSpec((tile_rows, LANES), lambda i, xp, *_: (xp[2 * i + 1], 0)),
            pl.BlockSpec(memory_space=pl.ANY),
            whole((N_EXPERTS, 2 * D_FF)),
            pl.BlockSpec(memory_space=pl.ANY),
            whole((N_EXPERTS, D_MODEL)),
        ],
        out_specs=pl.BlockSpec(memory_space=pl.ANY),
        scratch_shapes=[
            pltpu.VMEM((tile_rows + SUBLANES, LANES), F32),
            pltpu.VMEM((tile_rows + SUBLANES, LANES), F32),
            pltpu.VMEM((D_MODEL, 2 * D_FF), F32),
            pltpu.VMEM((D_FF, D_MODEL), F32),
            pltpu.VMEM((D_MODEL, 2 * D_FF), BF16),
            pltpu.VMEM((D_FF, D_MODEL), BF16),
            pltpu.SemaphoreType.DMA((2,)),
            pltpu.SemaphoreType.DMA((2,)),
        ],
    )
    return pl.pallas_call(
        _experts_kernel,
        grid_spec=grid_spec,
        out_shape=jax.ShapeDtypeStruct((n_slots, ROW_TILES, LANES), F32),
        compiler_params=pltpu.CompilerParams(
            dimension_semantics=("arbitrary",), vmem_limit_bytes=VMEM_LIMIT),
        name="experts",
    )(xpage, spage, block_e, first, next_e, n_used,
      row_slot, row_slot, row_slot, x_pages_2d, x_pages_2d, w_e1, b_e1, w_e2, b_e2)


def _combine_kernel(h1_ref, y0_ref, y1_ref, y2_ref, y3_ref, gcol_ref, g_ref, b_ref, out_ref):
    tc = COMBINE_TILE
    gcol = gcol_ref[...]
    y_refs = (y0_ref, y1_ref, y2_ref, y3_ref)
    chunks = []
    for c in range(ROW_TILES):
        z = DEEPNORM_ALPHA * h1_ref[:, c * LANES:(c + 1) * LANES]
        for kk in range(TOP_K):
            z = z + gcol[:, kk:kk + 1] * y_refs[kk][pl.ds(c, tc, stride=ROW_TILES), :]
        chunks.append(z)
    total = chunks[0].sum(axis=1, keepdims=True)
    for z in chunks[1:]:
        total = total + z.sum(axis=1, keepdims=True)
    mu = total * (1.0 / D_MODEL)
    sq = None
    for z in chunks:
        zc = z - mu
        part = (zc * zc).sum(axis=1, keepdims=True)
        sq = part if sq is None else sq + part
    inv = lax.rsqrt(sq * (1.0 / D_MODEL) + LN_EPS)
    for c, z in enumerate(chunks):
        cs = slice(c * LANES, (c + 1) * LANES)
        out_ref[:, cs] = (z - mu) * inv * g_ref[:, cs] + b_ref[:, cs]


def _combine_call(h1_2d, y_2d, gcol, ln_g, ln_b, n_tokens):
    tc = COMBINE_TILE
    n_t = n_tokens // tc
    blk = tc * ROW_TILES
    y_spec = lambda kk: pl.BlockSpec((blk, LANES), lambda i: (kk * n_t + i, 0))
    return pl.pallas_call(
        _combine_kernel,
        grid=(n_t,),
        in_specs=[
            pl.BlockSpec((tc, D_MODEL), lambda i: (i, 0)),
            y_spec(0), y_spec(1), y_spec(2), y_spec(3),
            pl.BlockSpec((tc, LANES), lambda i: (i, 0)),
            pl.BlockSpec((1, D_MODEL), lambda i: (0, 0)),
            pl.BlockSpec((1, D_MODEL), lambda i: (0, 0)),
        ],
        out_specs=pl.BlockSpec((tc, D_MODEL), lambda i: (i, 0)),
        out_shape=jax.ShapeDtypeStruct((n_tokens, D_MODEL), F32),
        compiler_params=pltpu.CompilerParams(
            dimension_semantics=("arbitrary",), vmem_limit_bytes=VMEM_LIMIT),
        name="combine",
    )(h1_2d, y_2d, y_2d, y_2d, y_2d, gcol, ln_g, ln_b)


def _prepare_mixer_weights(ln0_g, ln0_b, w_in, conv_w, conv_b, w_q, w_k, b_if, mh_gain, w_pool,
                           b_pool, ls_pool, w_branch_pool, w_branch_mlstm, w_out, ln1_g, ln1_b,
                           w_router, b_router):
    w_main = w_in[:, :COL_IF_END].astype(BF16)
    w_gates = w_in[:, COL_GATES:COL_GATES + 2 * D_MODEL].astype(BF16)
    bif = jnp.concatenate([b_if, jnp.zeros((LANES - 2 * N_HEADS,), F32)]).reshape(1, LANES)
    wqk = jnp.concatenate([w_q * (HEAD_DIM ** -0.5), w_k], axis=-1).astype(BF16)
    wpool_bd = jax.scipy.linalg.block_diag(*[w_pool[g] for g in range(len(POOL_WINDOWS))]).astype(BF16)
    wr_hi = w_router.astype(BF16)
    wr_lo = (w_router - wr_hi.astype(F32)).astype(BF16)
    lane_pad = jnp.zeros((D_MODEL, LANES - N_EXPERTS), BF16)
    wr_hi_lo = jnp.concatenate([wr_hi, lane_pad, wr_lo, lane_pad], axis=1)
    row = lambda v: v.reshape(1, -1)
    return (row(ln0_g), row(ln0_b), w_main, w_gates, bif, conv_w, row(conv_b), wqk, row(mh_gain), wpool_bd,
            row(b_pool), row(ls_pool), w_branch_pool.astype(BF16), w_branch_mlstm.astype(BF16),
            w_out.astype(BF16), row(ln1_g), row(ln1_b), wr_hi_lo,
            b_router.reshape(N_EXPERTS, 1))


def _plan(page_expert1, counts, pos, n_tokens, n_pages, n_rows_total):
    rows = EXPERT_ROWS
    n_assign = TOP_K * n_tokens
    i32 = jnp.int32
    slots = jnp.arange(n_assign, dtype=i32)
    filler = n_rows_total // rows
    hit = jnp.zeros(((filler + 1) * rows,), i32).at[pos.reshape(-1)].add(slots + 1)
    pad_slot = n_assign + jnp.arange((filler + 1) * rows, dtype=i32) % rows
    row_slot = jnp.where(hit == 0, pad_slot, hit - 1).reshape(filler + 1, 1, rows)

    pidx = jnp.arange(n_pages, dtype=i32)
    used = page_expert1 > 0
    n_used = jnp.sum(used.astype(i32))
    page_e = page_expert1 - 1
    key = jnp.where(used, page_e, N_EXPERTS) * n_pages + pidx
    place = jnp.sum((key[None, :] < key[:, None]).astype(i32), axis=1)
    at_block = place[None, :] == pidx[:, None]
    order = jnp.sum(jnp.where(at_block, pidx[None, :], 0), axis=1)
    block_e = jnp.clip(jnp.sum(jnp.where(at_block, page_e[None, :], 0), axis=1), 0, N_EXPERTS - 1)
    valid = pidx < n_used
    xpage = jnp.where(valid, order, 0)
    spage = jnp.where(valid, order, filler)
    prev_e = jnp.concatenate([jnp.full((1,), -1, i32), block_e[:-1]])
    first = jnp.logical_and(valid, block_e != prev_e).astype(i32)
    e_iota = jnp.arange(N_EXPERTS, dtype=i32)
    pages_per_e = (counts + rows - 1) // rows
    seg_end = jnp.cumsum(pages_per_e)
    mine = block_e[:, None] == e_iota[None, :]
    end_blk = jnp.sum(jnp.where(mine, seg_end[None, :], 0), axis=1)
    e_at_end = jnp.sum(jnp.where(pidx[None, :] == end_blk[:, None], block_e[None, :], 0), axis=1)
    next_e = jnp.where(end_blk < n_used, e_at_end, -1).astype(i32)
    plan = (xpage.astype(i32), spage.astype(i32), block_e.astype(i32), first, next_e,
            n_used.reshape(1).astype(i32))
    return plan, row_slot


def kernel(x, ln0_g, ln0_b, w_in, conv_w, conv_b, w_q, w_k, b_if, mh_gain, w_pool, b_pool, ls_pool,
           w_branch_pool, w_branch_mlstm, w_out, ln1_g, ln1_b, w_router, b_router, w_e1, b_e1,
           w_e2, b_e2, ln2_g, ln2_b):
    bsz, seq, _ = x.shape
    n_tokens = bsz * seq
    rows = EXPERT_ROWS
    assert w_in.shape[0] == 1, "single-layer trunk"
    assert seq % TOKEN_TILE == 0 and n_tokens % COMBINE_TILE == 0
    n_pages = (TOP_K * n_tokens + N_EXPERTS * (rows - 1) + rows - 1) // rows
    n_pages += n_pages % 2
    weights = _prepare_mixer_weights(
        ln0_g, ln0_b, w_in[0], conv_w[0], conv_b[0], w_q[0], w_k[0], b_if[0], mh_gain[0], w_pool[0],
        b_pool[0], ls_pool[0], w_branch_pool[0], w_branch_mlstm[0], w_out[0], ln1_g[0], ln1_b[0],
        w_router[0], b_router[0])
    x_pages, pos8, gcol, cnt, pe, h1_2d = _mixer_call(x, weights, n_pages)
    counts = cnt[:, 0].astype(jnp.int32)
    page_expert1 = pe[0, :n_pages].astype(jnp.int32)
    plan, row_slot = _plan(page_expert1, counts, pos8[:TOP_K], n_tokens, n_pages, x_pages.shape[0])
    y_slots = _experts_call(
        plan, row_slot, x_pages.reshape(-1, LANES), w_e1[0], b_e1[0], w_e2[0], b_e2[0], n_tokens)
    out = _combine_call(h1_2d, y_slots.reshape(-1, LANES), gcol, ln2_g[0].reshape(1, D_MODEL),
                        ln2_b[0].reshape(1, D_MODEL), n_tokens)
    return out.reshape(bsz, seq, D_MODEL)
```
